```python
import jax, jax.numpy as jnp
from jax import lax
import numpy as np

D_MODEL = 1024
BATCH = 8
SEQ = 2048
DEPTH = 2
DEC_BATCH = 128
DEC_SEQ = 1
PAST_LEN = 16384
PAGE_SIZE = 128

N_EVEN = (DEPTH + 1) // 2
N_ODD = DEPTH // 2
MIX_WIDTH = D_MODEL
POOL_WIDTH = MIX_WIDTH // 2
POOL_GROUPS = 4
POOL_GC = POOL_WIDTH // POOL_GROUPS
POOL_WINDOWS = (2, 4, 8, 16)
POOL_BUF = max(POOL_WINDOWS) - 1
RWKV_WIDTH = MIX_WIDTH - POOL_WIDTH
RWKV_HEAD = 64
RWKV_HEADS = RWKV_WIDTH // RWKV_HEAD
W_RANK = 64
A_RANK = 64
G_RANK = 128
SHIFT_WIDTH = 3 * RWKV_WIDTH + W_RANK + A_RANK + G_RANK
IN_AB_WIDTH = POOL_WIDTH + SHIFT_WIDTH
LNX_EPS = 64e-5
CONV_WIDTH = 3
CONV_DIM = D_MODEL
D_FF = 2816
N_EXPERTS = 8
TOP_K = 2
E_FF = 3584
N_MEM = 256
X_HEADS = 4
X_HEAD_DIM = D_MODEL // X_HEADS
RMS_EPS = 1e-6
F32 = jnp.float32

kernel_name = 'pool_rwkv7_shortconv_moe_memxattn_step'


def rmsnorm(x, g):
    xf = x.astype(F32)
    y = xf * lax.rsqrt(jnp.mean(xf * xf, axis=-1, keepdims=True) + RMS_EPS)
    return (y * g.astype(F32)).astype(x.dtype)


def pool_mixer(u, u_prev, pos0, pool_w, pool_scale):
    B, T, _ = u.shape
    ext = jnp.concatenate([u_prev.astype(F32), u.astype(F32)], axis=1)
    csum = jnp.concatenate([jnp.zeros_like(ext[:, :1]), lax.cumsum(ext, axis=1)], axis=1)
    end = csum[:, POOL_BUF + 1:POOL_BUF + 1 + T]
    pos = pos0 + jnp.arange(T)
    means = []
    for g, w in enumerate(POOL_WINDOWS):
        lo, hi = g * POOL_GC, (g + 1) * POOL_GC
        start = csum[:, POOL_BUF + 1 - w:POOL_BUF + 1 - w + T, lo:hi]
        cnt = jnp.minimum(w, pos + 1).astype(F32)[None, :, None]
        means.append((end[..., lo:hi] - start) / cnt)
    d = jnp.concatenate(means, axis=-1) - ext[:, POOL_BUF:]
    d = d.reshape(B, T, POOL_GROUPS, POOL_GC)
    y = jnp.einsum('btgc,gcd->btgd', d, pool_w.astype(F32)).reshape(B, T, POOL_WIDTH)
    return y * pool_scale.astype(F32), ext[:, -POOL_BUF:]


def wkv_scan(S0, r, decay, k, v, kk, a):
    def step(S, inp):
        r_t, w_t, k_t, v_t, kk_t, a_t = inp
        s_kk = jnp.einsum('bhvk,bhk->bhv', S, kk_t)
        S = (S * w_t[:, :, None, :] - s_kk[..., None] * (kk_t * a_t)[:, :, None, :]
             + v_t[..., None] * k_t[:, :, None, :])
        return S, jnp.einsum('bhvk,bhk->bhv', S, r_t)
    xs = tuple(jnp.moveaxis(t, 1, 0) for t in (r, decay, k, v, kk, a))
    S, ys = lax.scan(step, S0.astype(F32), xs)
    return S, jnp.moveaxis(ys, 0, 1)


def rwkv7_mixer(p, prev_row, S0, e, P):
    B, T, _ = p.shape
    pf = p.astype(F32)
    shifted = jnp.concatenate([prev_row[:, None].astype(F32), pf[:, :-1]], axis=1)
    m = pf + (shifted - pf) * P['mu_shift'][e].astype(F32)
    C = RWKV_WIDTH
    r, k, v = m[..., :C], m[..., C:2 * C], m[..., 2 * C:3 * C]
    o = 3 * C
    dw = m[..., o:o + W_RANK]
    o += W_RANK
    da = m[..., o:o + A_RANK]
    o += A_RANK
    dg = m[..., o:o + G_RANK]
    w_log = -jax.nn.softplus(-(P['rw_w0'][e].astype(F32) + jnp.tanh(dw) @ P['rw_w2'][e].astype(F32))) - 0.5
    decay = jnp.exp(-jnp.exp(w_log))
    a = jax.nn.sigmoid(P['rw_a0'][e].astype(F32) + da @ P['rw_a2'][e].astype(F32))
    g = jax.nn.sigmoid(dg) @ P['rw_g2'][e].astype(F32)
    hs = lambda t: t.reshape(B, T, RWKV_HEADS, RWKV_HEAD)
    hp = lambda t: t.astype(F32).reshape(RWKV_HEADS, RWKV_HEAD)
    r, k, v, decay, a, g = hs(r), hs(k), hs(v), hs(decay), hs(a), hs(g)
    kk = k * hp(P['rw_kk'][e])
    kk = kk / jnp.maximum(jnp.sqrt(jnp.sum(kk * kk, axis=-1, keepdims=True)), 1e-12)
    k = k * (1.0 + (a - 1.0) * hp(P['rw_ka'][e]))
    S, y = wkv_scan(S0, r, decay, k, v, kk, a)
    mu = jnp.mean(y, axis=-1, keepdims=True)
    var = jnp.mean(jnp.square(y - mu), axis=-1, keepdims=True)
    yn = (y - mu) * lax.rsqrt(var + LNX_EPS)
    yn = yn * hp(P['rw_lnx_w'][e]) + hp(P['rw_lnx_b'][e])
    bonus = jnp.sum(r * k * P['rw_rk'][e].astype(F32), axis=-1, keepdims=True) * v
    out = ((yn + bonus) * g).reshape(B, T, C)
    return out, S, pf[:, -1]


def shortconv_mixer(xn, prev, o, P):
    T = xn.shape[1]
    h = (xn @ P['w_in_c'][o]).astype(F32)
    bg, cg, hv = h[..., :CONV_DIM], h[..., CONV_DIM:2 * CONV_DIM], h[..., 2 * CONV_DIM:]
    ext = jnp.concatenate([prev.astype(F32), cg * hv], axis=1)
    cw = P['conv_w'][o].astype(F32)
    z = sum(cw[j] * ext[:, j:j + T] for j in range(CONV_WIDTH))
    y = (bg * z).astype(xn.dtype) @ P['w_out_c'][o]
    return y, ext[:, -(CONV_WIDTH - 1):]


def mem_attention(xn, mem_k, mem_v, wq, wo):
    B, T, _ = xn.shape
    q = (xn @ wq).reshape(B, T, X_HEADS, X_HEAD_DIM)
    s = jnp.einsum('bthd,bmhd->bhtm', q, mem_k).astype(F32) * (X_HEAD_DIM ** -0.5)
    pr = jax.nn.softmax(s, axis=-1).astype(mem_v.dtype)
    out = jnp.einsum('bhtm,bmhd->bthd', pr, mem_v).reshape(B, T, D_MODEL)
    return out @ wo


def dense_swiglu(xn, e, P):
    h = jax.nn.silu(xn @ P['ffn_gate'][e]) * (xn @ P['ffn_up'][e])
    return h @ P['ffn_down'][e]


def moe_swiglu(xn, o, P):
    B, T, D = xn.shape
    x2 = xn.reshape(B * T, D)
    logits = (x2 @ P['router_w'][o]).astype(F32) + P['router_b'][o].astype(F32)
    top_v, top_i = lax.top_k(logits, TOP_K)
    gw = jax.nn.softmax(top_v, axis=-1)
    gates = jnp.sum(jax.nn.one_hot(top_i, N_EXPERTS, dtype=F32) * gw[..., None], axis=1)
    out = jnp.zeros((B * T, D), F32)
    for ex in range(N_EXPERTS):
        h = jax.nn.silu(x2 @ P['moe_gate'][o, ex]) * (x2 @ P['moe_up'][o, ex])
        out = out + gates[:, ex:ex + 1] * (h @ P['moe_down'][o, ex]).astype(F32)
    return out.astype(xn.dtype).reshape(B, T, D)


def trunk(x, pos0, mem_k, mem_v, pool_prev, shift_prev, wkv_prev, conv_prev, P):
    dt = x.dtype
    new_pool, new_shift, new_wkv, new_conv = [], [], [], []
    for i in range(DEPTH):
        xn = rmsnorm(x, P['norm_mix'][i])
        if i % 2 == 0:
            e = i // 2
            h = xn @ P['w_in_ab'][e]
            y_a, buf = pool_mixer(h[..., :POOL_WIDTH], pool_prev[e], pos0, P['pool_w'][e], P['pool_scale'][e])
            y_b, S, row = rwkv7_mixer(h[..., POOL_WIDTH:], shift_prev[e], wkv_prev[e], e, P)
            mix = jnp.concatenate([y_a, y_b], axis=-1).astype(dt) @ P['w_out_ab'][e]
            new_pool.append(buf.astype(dt))
            new_shift.append(row.astype(dt))
            new_wkv.append(S.astype(dt))
        else:
            o = i // 2
            mix, cbuf = shortconv_mixer(xn, conv_prev[o], o, P)
            new_conv.append(cbuf.astype(dt))
        x = x + mix
        x = x + mem_attention(rmsnorm(x, P['norm_xattn'][i]), mem_k[i], mem_v[i], P['w_xq'][i], P['w_xo'][i])
        xn = rmsnorm(x, P['norm_ffn'][i])
        x = x + (dense_swiglu(xn, i // 2, P) if i % 2 == 0 else moe_swiglu(xn, i // 2, P))
    y = rmsnorm(x, P['norm_final'])
    return y, jnp.stack(new_pool), jnp.stack(new_shift), jnp.stack(new_wkv), jnp.stack(new_conv)


def setup_inputs(seed: int = 0) -> dict:
    key = jax.random.key(seed)
    keys = jax.random.split(key, 64)
    counter = [0]
    def nk():
        kk = keys[counter[0]]
        counter[0] += 1
        return kk
    def nrm(shape, scale=1.0):
        return scale * jax.random.normal(nk(), shape, jnp.float32)
    def uni(shape, lo, hi):
        return jax.random.uniform(nk(), shape, jnp.float32, lo, hi)
    D = D_MODEL
    return {
        'x_prompt': nrm((BATCH, SEQ, D)),
        'x_sample': nrm((DEC_BATCH, DEC_SEQ, D)),
        'mem_prompt': nrm((BATCH, N_MEM, D)),
        'cache_mem_k': nrm((DEPTH, DEC_BATCH, N_MEM, X_HEADS, X_HEAD_DIM)),
        'cache_mem_v': nrm((DEPTH, DEC_BATCH, N_MEM, X_HEADS, X_HEAD_DIM)),
        'state_pool': nrm((N_EVEN, DEC_BATCH, POOL_BUF, POOL_WIDTH)),
        'state_shift': nrm((N_EVEN, DEC_BATCH, SHIFT_WIDTH)),
        'state_wkv': nrm((N_EVEN, DEC_BATCH, RWKV_HEADS, RWKV_HEAD, RWKV_HEAD), 0.5),
        'state_conv': nrm((N_ODD, DEC_BATCH, CONV_WIDTH - 1, CONV_DIM)),
        'norm_mix': 1.0 + nrm((DEPTH, D), 0.02),
        'norm_xattn': 1.0 + nrm((DEPTH, D), 0.02),
        'norm_mem': 1.0 + nrm((DEPTH, D), 0.02),
        'norm_ffn': 1.0 + nrm((DEPTH, D), 0.02),
        'norm_final': 1.0 + nrm((D,), 0.02),
        'w_xq': nrm((DEPTH, D, D), D ** -0.5),
        'w_xk': nrm((DEPTH, D, D), D ** -0.5),
        'w_xv': nrm((DEPTH, D, D), D ** -0.5),
        'w_xo': nrm((DEPTH, D, D), D ** -0.5),
        'w_in_ab': nrm((N_EVEN, D, IN_AB_WIDTH), D ** -0.5),
        'pool_w': nrm((N_EVEN, POOL_GROUPS, POOL_GC, POOL_GC), POOL_GC ** -0.5),
        'pool_scale': 1.0 + nrm((N_EVEN, POOL_WIDTH), 0.02),
        'mu_shift': uni((N_EVEN, SHIFT_WIDTH), 0.0, 1.0),
        'rw_w0': uni((N_EVEN, RWKV_WIDTH), -6.0, 0.0),
        'rw_w2': nrm((N_EVEN, W_RANK, RWKV_WIDTH), 0.5 * W_RANK ** -0.5),
        'rw_a0': nrm((N_EVEN, RWKV_WIDTH), 0.1),
        'rw_a2': nrm((N_EVEN, A_RANK, RWKV_WIDTH), 0.5 * A_RANK ** -0.5),
        'rw_g2': nrm((N_EVEN, G_RANK, RWKV_WIDTH), G_RANK ** -0.5),
        'rw_kk': 0.85 + nrm((N_EVEN, RWKV_WIDTH), 0.05),
        'rw_ka': 1.0 + nrm((N_EVEN, RWKV_WIDTH), 0.05),
        'rw_rk': nrm((N_EVEN, RWKV_HEADS, RWKV_HEAD), 0.1),
        'rw_lnx_w': 1.0 + nrm((N_EVEN, RWKV_WIDTH), 0.02),
        'rw_lnx_b': nrm((N_EVEN, RWKV_WIDTH), 0.02),
        'w_out_ab': nrm((N_EVEN, MIX_WIDTH, D), MIX_WIDTH ** -0.5),
        'ffn_gate': nrm((N_EVEN, D, D_FF), D ** -0.5),
        'ffn_up': nrm((N_EVEN, D, D_FF), D ** -0.5),
        'ffn_down': nrm((N_EVEN, D_FF, D), D_FF ** -0.5),
        'w_in_c': nrm((N_ODD, D, 3 * CONV_DIM), D ** -0.5),
        'conv_w': nrm((N_ODD, CONV_WIDTH, CONV_DIM), CONV_WIDTH ** -0.5),
        'w_out_c': nrm((N_ODD, CONV_DIM, D), CONV_DIM ** -0.5),
        'router_w': nrm((N_ODD, D, N_EXPERTS), D ** -0.5),
        'router_b': nrm((N_ODD, N_EXPERTS), 0.01),
        'moe_gate': nrm((N_ODD, N_EXPERTS, D, E_FF), D ** -0.5),
        'moe_up': nrm((N_ODD, N_EXPERTS, D, E_FF), D ** -0.5),
        'moe_down': nrm((N_ODD, N_EXPERTS, E_FF, D), E_FF ** -0.5),
    }


def reference(x_prompt, x_sample, mem_prompt, cache_mem_k, cache_mem_v, state_pool, state_shift, state_wkv, state_conv,
              norm_mix, norm_xattn, norm_mem, norm_ffn, norm_final, w_xq, w_xk, w_xv, w_xo,
              w_in_ab, pool_w, pool_scale, mu_shift, rw_w0, rw_w2, rw_a0, rw_a2, rw_g2, rw_kk, rw_ka, rw_rk,
              rw_lnx_w, rw_lnx_b, w_out_ab, ffn_gate, ffn_up, ffn_down, w_in_c, conv_w, w_out_c,
              router_w, router_b, moe_gate, moe_up, moe_down):
    P = dict(norm_mix=norm_mix, norm_xattn=norm_xattn, norm_ffn=norm_ffn, norm_final=norm_final,
             w_xq=w_xq, w_xo=w_xo, w_in_ab=w_in_ab, pool_w=pool_w, pool_scale=pool_scale, mu_shift=mu_shift,
             rw_w0=rw_w0, rw_w2=rw_w2, rw_a0=rw_a0, rw_a2=rw_a2, rw_g2=rw_g2, rw_kk=rw_kk, rw_ka=rw_ka,
             rw_rk=rw_rk, rw_lnx_w=rw_lnx_w, rw_lnx_b=rw_lnx_b, w_out_ab=w_out_ab, ffn_gate=ffn_gate,
             ffn_up=ffn_up, ffn_down=ffn_down, w_in_c=w_in_c, conv_w=conv_w, w_out_c=w_out_c,
             router_w=router_w, router_b=router_b, moe_gate=moe_gate, moe_up=moe_up, moe_down=moe_down)
    bp = x_prompt.shape[0]
    dt = x_prompt.dtype
    mk, mv = [], []
    for i in range(DEPTH):
        mn = rmsnorm(mem_prompt, norm_mem[i])
        mk.append((mn @ w_xk[i]).reshape(bp, N_MEM, X_HEADS, X_HEAD_DIM))
        mv.append((mn @ w_xv[i]).reshape(bp, N_MEM, X_HEADS, X_HEAD_DIM))
    mem_k_p = jnp.stack(mk)
    mem_v_p = jnp.stack(mv)
    y_prompt, pool_p, shift_p, wkv_p, conv_p = trunk(
        x_prompt, 0, mem_k_p, mem_v_p,
        jnp.zeros((N_EVEN, bp, POOL_BUF, POOL_WIDTH), dt),
        jnp.zeros((N_EVEN, bp, SHIFT_WIDTH), dt),
        jnp.zeros((N_EVEN, bp, RWKV_HEADS, RWKV_HEAD, RWKV_HEAD), dt),
        jnp.zeros((N_ODD, bp, CONV_WIDTH - 1, CONV_DIM), dt), P)
    y_sample, pool_s, shift_s, wkv_s, conv_s = trunk(
        x_sample, PAST_LEN, cache_mem_k, cache_mem_v, state_pool, state_shift, state_wkv, state_conv, P)
    return (y_prompt, y_sample, pool_p, pool_s, shift_p, shift_s, wkv_p, wkv_s, conv_p, conv_s, mem_k_p, mem_v_p)
```

```python
import functools

import jax
import jax.numpy as jnp
from jax import lax
from jax.experimental import pallas as pl
from jax.experimental.pallas import tpu as pltpu

F32 = jnp.float32
BF16 = jnp.bfloat16

D_MODEL = 1024
POOL_WIDTH = 512
POOL_GROUPS = 4
POOL_GC = 128
POOL_WINDOWS = (2, 4, 8, 16)
POOL_BUF = 15
RWKV_WIDTH = 512
RWKV_HEAD = 64
RWKV_HEADS = 8
W_RANK = 64
A_RANK = 64
G_RANK = 128
SHIFT_WIDTH = 3 * RWKV_WIDTH + W_RANK + A_RANK + G_RANK
IN_AB_WIDTH = POOL_WIDTH + SHIFT_WIDTH
LNX_EPS = 64e-5
CONV_WIDTH = 3
N_EXPERTS = 8
N_MEM = 256
X_HEADS = 4
X_HEAD_DIM = 256
RMS_EPS = 1e-6
PAST_LEN = 16384

CHUNK = 64
POOL_HALO = 16
SHIFT_HALO = 8
ROUTER_PAD = 128


def _rmsnorm(x, g):
    ms = jnp.mean(x * x, axis=-1, keepdims=True)
    return x * lax.rsqrt(ms + RMS_EPS) * g


def _dot(a, b):
    return jnp.dot(a.astype(BF16), b.astype(BF16), preferred_element_type=F32)


def _dot_nt(a, b):
    return lax.dot_general(a.astype(BF16), b.astype(BF16), (((1,), (1,)), ((), ())),
                           preferred_element_type=F32)


def _dot_tn(a, b):
    return lax.dot_general(a.astype(BF16), b.astype(BF16), (((0,), (0,)), ((), ())),
                           preferred_element_type=F32)


def _split3(x):
    hi = x.astype(BF16)
    r1 = x - hi.astype(F32)
    mid = r1.astype(BF16)
    lo = (r1 - mid.astype(F32)).astype(BF16)
    return hi, mid, lo


def _dot_exact_rhs(x, m01):
    hi, mid, lo = _split3(x)
    f = lambda p: jnp.dot(p, m01, preferred_element_type=F32)
    return f(hi) + f(mid) + f(lo)


def _dot_exact_lhs(m01, x):
    hi, mid, lo = _split3(x)
    f = lambda p: jnp.dot(m01, p, preferred_element_type=F32)
    return f(hi) + f(mid) + f(lo)


def _softplus(x):
    return jnp.maximum(x, 0.0) + jnp.log1p(jnp.exp(-jnp.abs(x)))


def _sigmoid(x):
    return 1.0 / (1.0 + jnp.exp(-x))


def _row_tile(m, want):
    t = min(m, want)
    assert m % t == 0, (m, t)
    return t


def _norm_matmul_body(x_ref, g_ref, w_ref, o_ref):
    xn = _rmsnorm(x_ref[...], g_ref[...]).astype(BF16)
    o_ref[...] = jnp.dot(xn, w_ref[...], preferred_element_type=F32)


def norm_matmul(x, g, w):
    m, k = x.shape
    n = w.shape[1]
    tm = _row_tile(m, 512)
    return pl.pallas_call(
        _norm_matmul_body,
        grid=(m // tm,),
        in_specs=[pl.BlockSpec((tm, k), lambda i: (i, 0)),
                  pl.BlockSpec((1, k), lambda i: (0, 0)),
                  pl.BlockSpec((k, n), lambda i: (0, 0))],
        out_specs=pl.BlockSpec((tm, n), lambda i: (i, 0)),
        out_shape=jax.ShapeDtypeStruct((m, n), F32),
        compiler_params=pltpu.CompilerParams(dimension_semantics=("parallel",)),
        name="norm_matmul",
    )(x, g.reshape(1, k), w)


def _matmul_res_body(a_ref, w_ref, r_ref, o_ref):
    o_ref[...] = r_ref[...] + jnp.dot(a_ref[...].astype(BF16), w_ref[...], preferred_element_type=F32)


def matmul_res(a, w, res):
    m, k = a.shape
    n = w.shape[1]
    tm = _row_tile(m, 512)
    return pl.pallas_call(
        _matmul_res_body,
        grid=(m // tm,),
        in_specs=[pl.BlockSpec((tm, k), lambda i: (i, 0)),
                  pl.BlockSpec((k, n), lambda i: (0, 0)),
                  pl.BlockSpec((tm, n), lambda i: (i, 0))],
        out_specs=pl.BlockSpec((tm, n), lambda i: (i, 0)),
        out_shape=jax.ShapeDtypeStruct((m, n), F32),
        compiler_params=pltpu.CompilerParams(dimension_semantics=("parallel",)),
        name="matmul_res",
    )(a, w, res)


def _ffn_body(x_ref, g_ref, wg_ref, wu_ref, wd_ref, o_ref, xn_ref, acc_ref):
    f = pl.program_id(1)

    @pl.when(f == 0)
    def _():
        xn_ref[...] = _rmsnorm(x_ref[...], g_ref[...]).astype(BF16)
        acc_ref[...] = jnp.zeros_like(acc_ref)

    xn = xn_ref[...]
    gate = jnp.dot(xn, wg_ref[...], preferred_element_type=F32)
    up = jnp.dot(xn, wu_ref[...], preferred_element_type=F32)
    hid = (gate * _sigmoid(gate) * up).astype(BF16)
    acc_ref[...] += jnp.dot(hid, wd_ref[...], preferred_element_type=F32)

    @pl.when(f == pl.num_programs(1) - 1)
    def _():
        o_ref[...] = x_ref[...] + acc_ref[...]


def ffn_dense(x, g, wg, wu, wd, tf):
    m, d = x.shape
    ff = wg.shape[1]
    tm = _row_tile(m, 512)
    return pl.pallas_call(
        _ffn_body,
        grid=(m // tm, ff // tf),
        in_specs=[pl.BlockSpec((tm, d), lambda i, f: (i, 0)),
                  pl.BlockSpec((1, d), lambda i, f: (0, 0)),
                  pl.BlockSpec((d, tf), lambda i, f: (0, f)),
                  pl.BlockSpec((d, tf), lambda i, f: (0, f)),
                  pl.BlockSpec((tf, d), lambda i, f: (f, 0))],
        out_specs=pl.BlockSpec((tm, d), lambda i, f: (i, 0)),
        out_shape=jax.ShapeDtypeStruct((m, d), F32),
        scratch_shapes=[pltpu.VMEM((tm, d), BF16), pltpu.VMEM((tm, d), F32)],
        compiler_params=pltpu.CompilerParams(dimension_semantics=("parallel", "arbitrary")),
        name="ffn_dense",
    )(x, g.reshape(1, d), wg, wu, wd)


def _router_body(x_ref, g_ref, w_ref, b_ref, o_ref):
    xn = _rmsnorm(x_ref[...], g_ref[...])
    logits = _dot_exact_both(xn, w_ref[...]) + b_ref[...]
    lane = lax.broadcasted_iota(jnp.int32, logits.shape, 1)
    neg = jnp.float32(-jnp.inf)
    logits = jnp.where(lane < N_EXPERTS, logits, neg)
    m1 = jnp.max(logits, axis=-1, keepdims=True)
    i1 = jnp.min(jnp.where(logits == m1, lane, ROUTER_PAD), axis=-1, keepdims=True)
    rest = jnp.where(lane == i1, neg, logits)
    m2 = jnp.max(rest, axis=-1, keepdims=True)
    i2 = jnp.min(jnp.where(rest == m2, lane, ROUTER_PAD), axis=-1, keepdims=True)
    e2 = jnp.exp(m2 - m1)
    den = 1.0 + e2
    gates = jnp.where(lane == i1, 1.0 / den, 0.0) + jnp.where(lane == i2, e2 / den, 0.0)
    o_ref[...] = gates[:, :N_EXPERTS]


def _dot_exact_both(x, w):
    xh, xm, xl = _split3(x)
    wh, wm, wl = _split3(w)
    f = lambda p, q: jnp.dot(p, q, preferred_element_type=F32)
    return (f(xh, wh) + (f(xh, wm) + f(xm, wh))) + ((f(xm, wm) + f(xh, wl)) + f(xl, wh))


def moe_router(x, g, router_w, router_b):
    m, d = x.shape
    tm = _row_tile(m, 512)
    w = jnp.zeros((d, ROUTER_PAD), F32).at[:, :N_EXPERTS].set(router_w)
    b = jnp.zeros((1, ROUTER_PAD), F32).at[0, :N_EXPERTS].set(router_b)
    return pl.pallas_call(
        _router_body,
        grid=(m // tm,),
        in_specs=[pl.BlockSpec((tm, d), lambda i: (i, 0)),
                  pl.BlockSpec((1, d), lambda i: (0, 0)),
                  pl.BlockSpec((d, ROUTER_PAD), lambda i: (0, 0)),
                  pl.BlockSpec((1, ROUTER_PAD), lambda i: (0, 0))],
        out_specs=pl.BlockSpec((tm, N_EXPERTS), lambda i: (i, 0)),
        out_shape=jax.ShapeDtypeStruct((m, N_EXPERTS), F32),
        compiler_params=pltpu.CompilerParams(dimension_semantics=("parallel",)),
        name="moe_router",
    )(x, g.reshape(1, d), w, b)


def _moe_body(x_ref, g_ref, gates_ref, wg_ref, wu_ref, wd_ref, o_ref, xn_ref, acc_ref, part_ref):
    e = pl.program_id(1)
    f = pl.program_id(2)
    nf = pl.num_programs(2)

    @pl.when(jnp.logical_and(e == 0, f == 0))
    def _():
        xn_ref[...] = _rmsnorm(x_ref[...], g_ref[...]).astype(BF16)
        acc_ref[...] = jnp.zeros_like(acc_ref)

    @pl.when(f == 0)
    def _():
        part_ref[...] = jnp.zeros_like(part_ref)

    xn = xn_ref[...]
    gate = jnp.dot(xn, wg_ref[0], preferred_element_type=F32)
    up = jnp.dot(xn, wu_ref[0], preferred_element_type=F32)
    hid = (gate * _sigmoid(gate) * up).astype(BF16)
    part_ref[...] += jnp.dot(hid, wd_ref[0], preferred_element_type=F32)

    @pl.when(f == nf - 1)
    def _():
        gts = gates_ref[...]
        lane = lax.broadcasted_iota(jnp.int32, gts.shape, 1)
        ge = jnp.sum(jnp.where(lane == e, gts, 0.0), axis=-1, keepdims=True)
        acc_ref[...] += ge * part_ref[...]

    @pl.when(jnp.logical_and(e == pl.num_programs(1) - 1, f == nf - 1))
    def _():
        o_ref[...] = x_ref[...] + acc_ref[...]


def moe_all_experts(x, g, gates, wg, wu, wd, tf):
    m, d = x.shape
    ne, _, ff = wg.shape
    tm = _row_tile(m, 512)
    return pl.pallas_call(
        _moe_body,
        grid=(m // tm, ne, ff // tf),
        in_specs=[pl.BlockSpec((tm, d), lambda i, e, f: (i, 0)),
                  pl.BlockSpec((1, d), lambda i, e, f: (0, 0)),
                  pl.BlockSpec((tm, ne), lambda i, e, f: (i, 0)),
                  pl.BlockSpec((1, d, tf), lambda i, e, f: (e, 0, f)),
                  pl.BlockSpec((1, d, tf), lambda i, e, f: (e, 0, f)),
                  pl.BlockSpec((1, tf, d), lambda i, e, f: (e, f, 0))],
        out_specs=pl.BlockSpec((tm, d), lambda i, e, f: (i, 0)),
        out_shape=jax.ShapeDtypeStruct((m, d), F32),
        scratch_shapes=[pltpu.VMEM((tm, d), BF16), pltpu.VMEM((tm, d), F32), pltpu.VMEM((tm, d), F32)],
        compiler_params=pltpu.CompilerParams(dimension_semantics=("parallel", "arbitrary", "arbitrary")),
        name="moe_all_experts",
    )(x, g.reshape(1, d), gates, wg, wu, wd)


def _xattn_seq_body(x_ref, g_ref, wq_ref, wo_ref, k_ref, v_ref, o_ref):
    x = x_ref[0]
    xn = _rmsnorm(x, g_ref[...]).astype(BF16)
    q = jnp.dot(xn, wq_ref[...], preferred_element_type=F32).astype(BF16)
    kb = k_ref[0].astype(BF16)
    vb = v_ref[0].astype(BF16)
    heads = []
    for h in range(X_HEADS):
        sl = slice(h * X_HEAD_DIM, (h + 1) * X_HEAD_DIM)
        s = _dot_nt(q[:, sl], kb[:, sl]) * (X_HEAD_DIM ** -0.5)
        mx = jnp.max(s, axis=-1, keepdims=True)
        ex = jnp.exp(s - mx)
        pr = ex / jnp.sum(ex, axis=-1, keepdims=True)
        heads.append(jnp.dot(pr.astype(BF16), vb[:, sl], preferred_element_type=F32))
    att = jnp.concatenate(heads, axis=-1).astype(BF16)
    o_ref[0] = x + jnp.dot(att, wo_ref[...], preferred_element_type=F32)


def xattn_seq(x, g, wq, wo, mem_k, mem_v):
    b, t, d = x.shape
    tq = _row_tile(t, 512)
    return pl.pallas_call(
        _xattn_seq_body,
        grid=(b, t // tq),
        in_specs=[pl.BlockSpec((1, tq, d), lambda i, j: (i, j, 0)),
                  pl.BlockSpec((1, d), lambda i, j: (0, 0)),
                  pl.BlockSpec((d, d), lambda i, j: (0, 0)),
                  pl.BlockSpec((d, d), lambda i, j: (0, 0)),
                  pl.BlockSpec((1, N_MEM, d), lambda i, j: (i, 0, 0)),
                  pl.BlockSpec((1, N_MEM, d), lambda i, j: (i, 0, 0))],
        out_specs=pl.BlockSpec((1, tq, d), lambda i, j: (i, j, 0)),
        out_shape=jax.ShapeDtypeStruct((b, t, d), F32),
        compiler_params=pltpu.CompilerParams(dimension_semantics=("parallel", "parallel")),
        name="xattn_seq",
    )(x, g.reshape(1, d), wq, wo, mem_k, mem_v)


XATTN_STEP_BATCH = 8


def _xattn_step_body(x_ref, g_ref, wq_ref, wo_ref, k_ref, v_ref, o_ref, att_ref):
    x = x_ref[...]
    xn = _rmsnorm(x, g_ref[...]).astype(BF16)
    q = jnp.dot(xn, wq_ref[...], preferred_element_type=F32) * (X_HEAD_DIM ** -0.5)
    for b in range(XATTN_STEP_BATCH):
        prod = k_ref[b] * q[b:b + 1, :]
        vb = v_ref[b]
        for h in range(X_HEADS):
            sl = slice(h * X_HEAD_DIM, (h + 1) * X_HEAD_DIM)
            s = jnp.sum(prod[:, sl], axis=1, keepdims=True)
            mx = jnp.max(s, axis=0, keepdims=True)
            ex = jnp.exp(s - mx)
            pr = ex / jnp.sum(ex, axis=0, keepdims=True)
            att_ref[b:b + 1, sl] = jnp.sum(pr * vb[:, sl], axis=0, keepdims=True)
    o_ref[...] = x + jnp.dot(att_ref[...].astype(BF16), wo_ref[...], preferred_element_type=F32)


def xattn_step(x, g, wq, wo, mem_k, mem_v):
    b, d = x.shape
    bb = XATTN_STEP_BATCH
    assert b % bb == 0
    return pl.pallas_call(
        _xattn_step_body,
        grid=(b // bb,),
        in_specs=[pl.BlockSpec((bb, d), lambda i: (i, 0)),
                  pl.BlockSpec((1, d), lambda i: (0, 0)),
                  pl.BlockSpec((d, d), lambda i: (0, 0)),
                  pl.BlockSpec((d, d), lambda i: (0, 0)),
                  pl.BlockSpec((bb, N_MEM, d), lambda i: (i, 0, 0)),
                  pl.BlockSpec((bb, N_MEM, d), lambda i: (i, 0, 0))],
        out_specs=pl.BlockSpec((bb, d), lambda i: (i, 0)),
        out_shape=jax.ShapeDtypeStruct((b, d), F32),
        scratch_shapes=[pltpu.VMEM((bb, d), F32)],
        compiler_params=pltpu.CompilerParams(dimension_semantics=("parallel",)),
        name="xattn_step",
    )(x, g.reshape(1, d), wq, wo, mem_k, mem_v)


def _head_sum(x, bdiag_ref):
    return _dot_exact_rhs(x, bdiag_ref[...])


def _pool_group_out(acc, cnt, u, pool_w_ref, pool_scale_ref, g):
    cs = slice(g * POOL_GC, (g + 1) * POOL_GC)
    d = acc / cnt - u
    return _dot(d, pool_w_ref[g]) * pool_scale_ref[:, cs]


def _rwkv_prep(p, shifted, prm):
    (mu_ref, w0_ref, w2_ref, a0_ref, a2_ref, g2_ref, kk_ref, ka_ref, bdiag_ref) = prm
    m = p + (shifted - p) * mu_ref[...]
    c = RWKV_WIDTH
    r = m[:, 0:c]
    k = m[:, c:2 * c]
    v = m[:, 2 * c:3 * c]
    dwa = m[:, 3 * c:3 * c + W_RANK + A_RANK]
    dg = m[:, 3 * c + W_RANK + A_RANK:]
    w_log = -_softplus(-(w0_ref[...] + _dot(jnp.tanh(dwa), w2_ref[...]))) - 0.5
    lw = -jnp.exp(w_log)
    a = _sigmoid(a0_ref[...] + _dot(dwa, a2_ref[...]))
    gate = _dot(_sigmoid(dg), g2_ref[...])
    kk = k * kk_ref[...]
    kk = kk / jnp.maximum(jnp.sqrt(_head_sum(kk * kk, bdiag_ref)), 1e-12)
    kmod = k * (1.0 + (a - 1.0) * ka_ref[...])
    return r, lw, kmod, v, kk, a, gate


def _rwkv_finish(y, r, kmod, v, gate, rk_ref, lnw_ref, lnb_ref, bdiag_ref):
    mu = _head_sum(y, bdiag_ref) * (1.0 / RWKV_HEAD)
    dlt = y - mu
    var = _head_sum(dlt * dlt, bdiag_ref) * (1.0 / RWKV_HEAD)
    yn = dlt * lax.rsqrt(var + LNX_EPS) * lnw_ref[...] + lnb_ref[...]
    bonus = _head_sum(r * kmod * rk_ref[...], bdiag_ref) * v
    return (yn + bonus) * gate


def _mix0_seq_body(h_ref, pool_w_ref, pool_scale_ref, mu_ref, w0_ref, w2_ref, a0_ref, a2_ref, g2_ref,
                   kk_ref, ka_ref, rk_ref, lnw_ref, lnb_ref, bdiag_ref, tril_ref,
                   o_ref, s_out_ref,
                   ext_u, ext_p, s_ref, r_s, lw_s, k_s, v_s, kk_s, a_s, g_s, y_s):
    t = pl.program_id(1)
    tt = o_ref.shape[1]

    @pl.when(t == 0)
    def _():
        ext_u[0:POOL_HALO, :] = jnp.zeros((POOL_HALO, POOL_WIDTH), F32)
        ext_p[0:SHIFT_HALO, :] = jnp.zeros((SHIFT_HALO, SHIFT_WIDTH), F32)
        s_ref[...] = jnp.zeros_like(s_ref)

    ext_u[POOL_HALO:POOL_HALO + tt, :] = h_ref[0, :, 0:POOL_WIDTH]
    ext_p[SHIFT_HALO:SHIFT_HALO + tt, :] = h_ref[0, :, POOL_WIDTH:IN_AB_WIDTH]

    pos = t * tt + lax.broadcasted_iota(jnp.int32, (tt, 1), 0)
    for g, win in enumerate(POOL_WINDOWS):
        cs = slice(g * POOL_GC, (g + 1) * POOL_GC)
        u = ext_u[POOL_HALO:POOL_HALO + tt, cs]
        acc = u
        for j in range(1, win):
            acc = acc + ext_u[POOL_HALO - j:POOL_HALO - j + tt, cs]
        cnt = jnp.minimum(win, pos + 1).astype(F32)
        o_ref[0, :, cs] = _pool_group_out(acc, cnt, u, pool_w_ref, pool_scale_ref, g)

    p = ext_p[SHIFT_HALO:SHIFT_HALO + tt, :]
    shifted = ext_p[SHIFT_HALO - 1:SHIFT_HALO - 1 + tt, :]
    prm = (mu_ref, w0_ref, w2_ref, a0_ref, a2_ref, g2_ref, kk_ref, ka_ref, bdiag_ref)
    r, lw, kmod, v, kk, a, gate = _rwkv_prep(p, shifted, prm)
    r_s[...] = r
    lw_s[...] = lw
    k_s[...] = kmod
    v_s[...] = v
    kk_s[...] = kk
    a_s[...] = a
    g_s[...] = gate

    ext_u[0:POOL_HALO, :] = ext_u[tt:tt + POOL_HALO, :]
    ext_p[0:SHIFT_HALO, :] = ext_p[tt:tt + SHIFT_HALO, :]

    li = lax.broadcasted_iota(jnp.int32, (CHUNK, CHUNK), 0)
    si = lax.broadcasted_iota(jnp.int32, (CHUNK, CHUNK), 1)
    strict = li > si
    incl = li >= si
    eye = li == si

    def chunk_step(c, carry):
        rows = pl.ds(pl.multiple_of(c * CHUNK, CHUNK), CHUNK)
        rc = r_s[rows, :]
        lwc = lw_s[rows, :]
        kc = k_s[rows, :]
        vc = v_s[rows, :]
        kkc = kk_s[rows, :]
        ac = a_s[rows, :]
        cl = _dot_exact_lhs(tril_ref[...], lwc)
        gam = jnp.exp(cl)
        gam_prev = jnp.exp(cl - lwc)
        gam_inv = jnp.exp(-cl)
        gam_end = gam[CHUNK - 1:CHUNK, :]
        beta = kkc * ac
        kq = kkc * gam_prev
        rq = rc * gam
        kd = kc * gam_inv
        bd = beta * gam_inv
        kdg = kd * gam_end
        bdg = bd * gam_end
        ys = []
        for h in range(RWKV_HEADS):
            hs = slice(h * RWKV_HEAD, (h + 1) * RWKV_HEAD)
            kq_h, rq_h, kd_h, bd_h, v_h = kq[:, hs], rq[:, hs], kd[:, hs], bd[:, hs], vc[:, hs]
            a_k = jnp.where(strict, _dot_nt(kq_h, kd_h), 0.0)
            a_b = jnp.where(strict, _dot_nt(kq_h, bd_h), 0.0)
            p_k = jnp.where(incl, _dot_nt(rq_h, kd_h), 0.0)
            p_b = jnp.where(incl, _dot_nt(rq_h, bd_h), 0.0)
            tm = jnp.where(eye, 1.0, 0.0) - a_b
            apow = a_b
            n = 1
            while 2 * n < CHUNK:
                apow = _dot(apow, apow)
                tm = tm + _dot(tm, apow)
                n *= 2
            t_kq = _dot(tm, kq_h)
            t_akv = _dot(tm, _dot(a_k, v_h))
            r_y = rq_h - _dot(p_b, t_kq)
            y0 = _dot(p_k, v_h) - _dot(p_b, t_akv)
            g_m = jnp.where(eye, gam_end[:, hs], 0.0) - _dot_tn(bdg[:, hs], t_kq)
            h_t = _dot_tn(v_h, kdg[:, hs]) - _dot_tn(t_akv, bdg[:, hs])
            s0 = s_ref[h]
            ys.append(_dot_nt(r_y, s0) + y0)
            s_ref[h] = _dot_nt(s0, g_m) + h_t
        y_s[rows, :] = jnp.concatenate(ys, axis=-1)
        return carry

    lax.fori_loop(0, tt // CHUNK, chunk_step, 0)

    o_ref[0, :, POOL_WIDTH:] = _rwkv_finish(y_s[...], r_s[...], k_s[...], v_s[...], g_s[...],
                                            rk_ref, lnw_ref, lnb_ref, bdiag_ref)

    @pl.when(t == pl.num_programs(1) - 1)
    def _():
        s_out_ref[0] = s_ref[...]


def _mix0_params(P):
    c = RWKV_WIDTH
    row = lambda x: x.reshape(1, -1).astype(F32)
    w2 = jnp.zeros((W_RANK + A_RANK, c), F32).at[:W_RANK].set(P['rw_w2']).astype(BF16)
    a2 = jnp.zeros((W_RANK + A_RANK, c), F32).at[W_RANK:].set(P['rw_a2']).astype(BF16)
    hid = jnp.arange(c) // RWKV_HEAD
    bdiag = (hid[:, None] == hid[None, :]).astype(BF16)
    return (P['pool_w'].astype(BF16), row(P['pool_scale']), row(P['mu_shift']), row(P['rw_w0']), w2,
            row(P['rw_a0']), a2, P['rw_g2'].astype(BF16), row(P['rw_kk']), row(P['rw_ka']),
            row(P['rw_rk']), row(P['rw_lnx_w']), row(P['rw_lnx_b']), bdiag)


def _full_spec(x):
    nd = x.ndim
    return pl.BlockSpec(x.shape, lambda *_: (0,) * nd)


def mix0_seq(h, P):
    b, t, _ = h.shape
    tt = _row_tile(t, 256)
    assert tt % CHUNK == 0 and tt >= POOL_HALO
    prm = _mix0_params(P)
    tril = (jnp.arange(CHUNK)[:, None] >= jnp.arange(CHUNK)[None, :]).astype(BF16)
    vec = lambda: pltpu.VMEM((tt, RWKV_WIDTH), F32)
    return pl.pallas_call(
        _mix0_seq_body,
        grid=(b, t // tt),
        in_specs=[pl.BlockSpec((1, tt, IN_AB_WIDTH), lambda i, j: (i, j, 0))]
                 + [_full_spec(x) for x in prm] + [_full_spec(tril)],
        out_specs=[pl.BlockSpec((1, tt, D_MODEL), lambda i, j: (i, j, 0)),
                   pl.BlockSpec((1, RWKV_HEADS, RWKV_HEAD, RWKV_HEAD), lambda i, j: (i, 0, 0, 0))],
        out_shape=[jax.ShapeDtypeStruct((b, t, D_MODEL), F32),
                   jax.ShapeDtypeStruct((b, RWKV_HEADS, RWKV_HEAD, RWKV_HEAD), F32)],
        scratch_shapes=[pltpu.VMEM((tt + POOL_HALO, POOL_WIDTH), F32),
                        pltpu.VMEM((tt + SHIFT_HALO, SHIFT_WIDTH), F32),
                        pltpu.VMEM((RWKV_HEADS, RWKV_HEAD, RWKV_HEAD), F32),
                        vec(), vec(), vec(), vec(), vec(), vec(), vec(), vec()],
        compiler_params=pltpu.CompilerParams(dimension_semantics=("parallel", "arbitrary")),
        name="mix0_seq",
    )(h, *prm, tril)


def _to_leading(x):
    n = x.shape[0]
    ii = lax.broadcasted_iota(jnp.int32, (n, n, 1), 0)
    jj = lax.broadcasted_iota(jnp.int32, (n, n, 1), 1)
    return jnp.sum(jnp.where(ii == jj, x[None, :, :], 0.0), axis=1, keepdims=True)


def _from_leading(x3):
    n = x3.shape[0]
    ii = lax.broadcasted_iota(jnp.int32, (n, n, 1), 0)
    jj = lax.broadcasted_iota(jnp.int32, (n, n, 1), 1)
    return jnp.sum(jnp.where(ii == jj, x3, 0.0), axis=0)


def _mix0_step_body(h_ref, pool_prev_ref, shift_prev_ref, s_in_ref,
                    pool_w_ref, pool_scale_ref, mu_ref, w0_ref, w2_ref, a0_ref, a2_ref, g2_ref,
                    kk_ref, ka_ref, rk_ref, lnw_ref, lnb_ref, bdiag_ref,
                    o_ref, s_out_ref,
                    r_t, w_t, k_t, v_t, kk_t, bt_t, y_t, r_s, k_s, v_s, g_s):
    hh = pl.program_id(0)
    nb = h_ref.shape[0]

    @pl.when(hh == 0)
    def _():
        for g, win in enumerate(POOL_WINDOWS):
            cs = slice(g * POOL_GC, (g + 1) * POOL_GC)
            u = h_ref[:, cs]
            acc = u
            for j in range(1, win):
                acc = acc + pool_prev_ref[POOL_BUF - j][:, cs]
            cnt = jnp.float32(min(win, PAST_LEN + 1))
            o_ref[:, cs] = _pool_group_out(acc, cnt, u, pool_w_ref, pool_scale_ref, g)
        prm = (mu_ref, w0_ref, w2_ref, a0_ref, a2_ref, g2_ref, kk_ref, ka_ref, bdiag_ref)
        r, lw, kmod, v, kk, a, gate = _rwkv_prep(h_ref[:, POOL_WIDTH:IN_AB_WIDTH], shift_prev_ref[...], prm)
        r_s[...] = r
        k_s[...] = kmod
        v_s[...] = v
        g_s[...] = gate
        r_t[...] = r.T
        w_t[...] = jnp.exp(lw).T
        k_t[...] = kmod.T
        v_t[...] = v.T
        kk_t[...] = kk.T
        bt_t[...] = (kk * a).T

    rows = pl.ds(pl.multiple_of(hh * RWKV_HEAD, RWKV_HEAD), RWKV_HEAD)
    s = s_in_ref[...].T.reshape(RWKV_HEAD, RWKV_HEAD, nb)
    kk = kk_t[rows, :][None]
    s_kk = jnp.sum(s * kk, axis=1, keepdims=True)
    v3 = _to_leading(v_t[rows, :])
    s = s * w_t[rows, :][None] - s_kk * bt_t[rows, :][None] + v3 * k_t[rows, :][None]
    y3 = jnp.sum(s * r_t[rows, :][None], axis=1, keepdims=True)
    y_t[rows, :] = _from_leading(y3)
    s_out_ref[...] = s.reshape(RWKV_HEAD * RWKV_HEAD, nb).T

    @pl.when(hh == pl.num_programs(0) - 1)
    def _():
        o_ref[:, POOL_WIDTH:] = _rwkv_finish(y_t[...].T, r_s[...], k_s[...], v_s[...], g_s[...],
                                             rk_ref, lnw_ref, lnb_ref, bdiag_ref)


def mix0_step(h, pool_prev, shift_prev, s_prev, P):
    b = h.shape[0]
    prm = _mix0_params(P)
    hw = RWKV_HEAD * RWKV_HEAD
    s2 = s_prev.reshape(b, RWKV_HEADS * hw)
    pool_t = jnp.swapaxes(pool_prev, 0, 1)
    tvec = lambda: pltpu.VMEM((RWKV_WIDTH, b), F32)
    svec = lambda: pltpu.VMEM((b, RWKV_WIDTH), F32)
    out, s_new = pl.pallas_call(
        _mix0_step_body,
        grid=(RWKV_HEADS,),
        in_specs=[_full_spec(h), _full_spec(pool_t), _full_spec(shift_prev),
                  pl.BlockSpec((b, hw), lambda i: (0, i))] + [_full_spec(x) for x in prm],
        out_specs=[pl.BlockSpec((b, D_MODEL), lambda i: (0, 0)),
                   pl.BlockSpec((b, hw), lambda i: (0, i))],
        out_shape=[jax.ShapeDtypeStruct((b, D_MODEL), F32),
                   jax.ShapeDtypeStruct((b, RWKV_HEADS * hw), F32)],
        scratch_shapes=[tvec(), tvec(), tvec(), tvec(), tvec(), tvec(), tvec(),
                        svec(), svec(), svec(), svec()],
        compiler_params=pltpu.CompilerParams(dimension_semantics=("arbitrary",)),
        name="mix0_step",
    )(h, pool_t, shift_prev, s2, *prm)
    return out, s_new.reshape(b, RWKV_HEADS, RWKV_HEAD, RWKV_HEAD)


def _conv_seq_body(h_ref, x_ref, cw_ref, wo_ref, o_ref, tail_ref, ext):
    t = pl.program_id(1)
    tt = o_ref.shape[1]
    c = D_MODEL

    @pl.when(t == 0)
    def _():
        ext[0:SHIFT_HALO, :] = jnp.zeros((SHIFT_HALO, c), F32)

    ext[SHIFT_HALO:SHIFT_HALO + tt, :] = h_ref[0, :, c:2 * c] * h_ref[0, :, 2 * c:3 * c]
    z = cw_ref[0:1, :] * ext[SHIFT_HALO - 2:SHIFT_HALO - 2 + tt, :]
    z = z + cw_ref[1:2, :] * ext[SHIFT_HALO - 1:SHIFT_HALO - 1 + tt, :]
    z = z + cw_ref[2:3, :] * ext[SHIFT_HALO:SHIFT_HALO + tt, :]
    gated = (h_ref[0, :, 0:c] * z).astype(BF16)
    o_ref[0] = x_ref[0] + jnp.dot(gated, wo_ref[...], preferred_element_type=F32)
    ext[0:SHIFT_HALO, :] = ext[tt:tt + SHIFT_HALO, :]

    @pl.when(t == pl.num_programs(1) - 1)
    def _():
        tail_ref[0] = ext[SHIFT_HALO - (CONV_WIDTH - 1):SHIFT_HALO, :]


def conv_seq(h, x, conv_w, w_out):
    b, t, _ = h.shape
    c = D_MODEL
    tt = _row_tile(t, 512)
    return pl.pallas_call(
        _conv_seq_body,
        grid=(b, t // tt),
        in_specs=[pl.BlockSpec((1, tt, 3 * c), lambda i, j: (i, j, 0)),
                  pl.BlockSpec((1, tt, c), lambda i, j: (i, j, 0)),
                  _full_spec(conv_w), _full_spec(w_out)],
        out_specs=[pl.BlockSpec((1, tt, c), lambda i, j: (i, j, 0)),
                   pl.BlockSpec((1, CONV_WIDTH - 1, c), lambda i, j: (i, 0, 0))],
        out_shape=[jax.ShapeDtypeStruct((b, t, c), F32),
                   jax.ShapeDtypeStruct((b, CONV_WIDTH - 1, c), F32)],
        scratch_shapes=[pltpu.VMEM((tt + SHIFT_HALO, c), F32)],
        compiler_params=pltpu.CompilerParams(dimension_semantics=("parallel", "arbitrary")),
        name="conv_seq",
    )(h, x, conv_w, w_out)


def _conv_step_body(h_ref, x_ref, p0_ref, p1_ref, cw_ref, wo_ref, o_ref, e_ref):
    c = D_MODEL
    e = h_ref[:, c:2 * c] * h_ref[:, 2 * c:3 * c]
    z = cw_ref[0:1, :] * p0_ref[...] + cw_ref[1:2, :] * p1_ref[...] + cw_ref[2:3, :] * e
    gated = (h_ref[:, 0:c] * z).astype(BF16)
    o_ref[...] = x_ref[...] + jnp.dot(gated, wo_ref[...], preferred_element_type=F32)
    e_ref[...] = e


def conv_step(h, x, prev, conv_w, w_out):
    b = h.shape[0]
    c = D_MODEL
    args = (h, x, prev[:, 0, :], prev[:, 1, :], conv_w, w_out)
    return pl.pallas_call(
        _conv_step_body,
        grid=(1,),
        in_specs=[_full_spec(a) for a in args],
        out_specs=[pl.BlockSpec((b, c), lambda i: (0, 0)), pl.BlockSpec((b, c), lambda i: (0, 0))],
        out_shape=[jax.ShapeDtypeStruct((b, c), F32), jax.ShapeDtypeStruct((b, c), F32)],
        compiler_params=pltpu.CompilerParams(dimension_semantics=("arbitrary",)),
        name="conv_step",
    )(*args)


def _final_norm_body(x_ref, g_ref, o_ref):
    o_ref[...] = _rmsnorm(x_ref[...], g_ref[...])


def final_norm(x, g):
    m, d = x.shape
    tm = _row_tile(m, 1024)
    return pl.pallas_call(
        _final_norm_body,
        grid=(m // tm,),
        in_specs=[pl.BlockSpec((tm, d), lambda i: (i, 0)), pl.BlockSpec((1, d), lambda i: (0, 0))],
        out_specs=pl.BlockSpec((tm, d), lambda i: (i, 0)),
        out_shape=jax.ShapeDtypeStruct((m, d), F32),
        compiler_params=pltpu.CompilerParams(dimension_semantics=("parallel",)),
        name="final_norm",
    )(x, g.reshape(1, d))


FFN_TILE = 256
MOE_TILE = 512


def _layer_tail(x2, i, mem_k, mem_v, W, seq_shape):
    if seq_shape is not None:
        b, t = seq_shape
        x2 = xattn_seq(x2.reshape(b, t, D_MODEL), W['norm_xattn'][i], W['w_xq'][i], W['w_xo'][i],
                       mem_k, mem_v).reshape(b * t, D_MODEL)
    else:
        x2 = xattn_step(x2, W['norm_xattn'][i], W['w_xq'][i], W['w_xo'][i], mem_k, mem_v)
    if i % 2 == 0:
        e = i // 2
        return ffn_dense(x2, W['norm_ffn'][i], W['ffn_gate'][e], W['ffn_up'][e], W['ffn_down'][e], FFN_TILE)
    o = i // 2
    gates = moe_router(x2, W['norm_ffn'][i], W['router_w'][o], W['router_b'][o])
    return moe_all_experts(x2, W['norm_ffn'][i], gates, W['moe_gate'][o], W['moe_up'][o], W['moe_down'][o],
                           MOE_TILE)


def _trunk_seq(x, mem_k, mem_v, W):
    b, t, d = x.shape
    x2 = x.reshape(b * t, d)
    P0 = {k: v[0] for k, v in W['mix0'].items()}
    h = norm_matmul(x2, W['norm_mix'][0], W['w_in_ab'][0]).reshape(b, t, IN_AB_WIDTH)
    mix, wkv = mix0_seq(h, P0)
    pool = h[:, t - POOL_BUF:, :POOL_WIDTH]
    shift = h[:, t - 1, POOL_WIDTH:]
    x2 = matmul_res(mix.reshape(b * t, d), W['w_out_ab'][0], x2)
    x2 = _layer_tail(x2, 0, mem_k[0], mem_v[0], W, (b, t))
    h = norm_matmul(x2, W['norm_mix'][1], W['w_in_c'][0]).reshape(b, t, 3 * d)
    x3, conv = conv_seq(h, x2.reshape(b, t, d), W['conv_w'][0], W['w_out_c'][0])
    x2 = _layer_tail(x3.reshape(b * t, d), 1, mem_k[1], mem_v[1], W, (b, t))
    y = final_norm(x2, W['norm_final']).reshape(b, t, d)
    return y, pool[None], shift[None], wkv[None], conv[None]


def _trunk_step(x, mem_k, mem_v, pool_prev, shift_prev, wkv_prev, conv_prev, W):
    b, _, d = x.shape
    x2 = x.reshape(b, d)
    P0 = {k: v[0] for k, v in W['mix0'].items()}
    h = norm_matmul(x2, W['norm_mix'][0], W['w_in_ab'][0])
    mix, wkv = mix0_step(h, pool_prev[0], shift_prev[0], wkv_prev[0], P0)
    pool = jnp.concatenate([pool_prev[0][:, 1:], h[:, None, :POOL_WIDTH]], axis=1)
    shift = h[:, POOL_WIDTH:]
    x2 = matmul_res(mix, W['w_out_ab'][0], x2)
    x2 = _layer_tail(x2, 0, mem_k[0], mem_v[0], W, None)
    h = norm_matmul(x2, W['norm_mix'][1], W['w_in_c'][0])
    x2, e = conv_step(h, x2, conv_prev[0], W['conv_w'][0], W['w_out_c'][0])
    conv = jnp.concatenate([conv_prev[0][:, 1:], e[:, None]], axis=1)
    x2 = _layer_tail(x2, 1, mem_k[1], mem_v[1], W, None)
    y = final_norm(x2, W['norm_final']).reshape(b, 1, d)
    return y, pool[None], shift[None], wkv[None], conv[None]


def kernel(x_prompt, x_sample, mem_prompt, cache_mem_k, cache_mem_v, state_pool, state_shift, state_wkv, state_conv, norm_mix, norm_xattn, norm_mem, norm_ffn, norm_final, w_xq, w_xk, w_xv, w_xo, w_in_ab, pool_w, pool_scale, mu_shift, rw_w0, rw_w2, rw_a0, rw_a2, rw_g2, rw_kk, rw_ka, rw_rk, rw_lnx_w, rw_lnx_b, w_out_ab, ffn_gate, ffn_up, ffn_down, w_in_c, conv_w, w_out_c, router_w, router_b, moe_gate, moe_up, moe_down):
    depth = norm_mix.shape[0]
    assert depth == 2 and w_in_ab.shape[0] == 1 and w_in_c.shape[0] == 1
    bp = x_prompt.shape[0]
    bs = x_sample.shape[0]
    d = D_MODEL
    bf = lambda w: w.astype(BF16)
    W = dict(norm_mix=norm_mix, norm_xattn=norm_xattn, norm_ffn=norm_ffn, norm_final=norm_final,
             w_xq=bf(w_xq), w_xo=bf(w_xo), w_in_ab=bf(w_in_ab), w_out_ab=bf(w_out_ab),
             ffn_gate=bf(ffn_gate), ffn_up=bf(ffn_up), ffn_down=bf(ffn_down),
             w_in_c=bf(w_in_c), conv_w=conv_w, w_out_c=bf(w_out_c),
             router_w=router_w, router_b=router_b,
             moe_gate=bf(moe_gate), moe_up=bf(moe_up), moe_down=bf(moe_down),
             mix0=dict(pool_w=pool_w, pool_scale=pool_scale, mu_shift=mu_shift, rw_w0=rw_w0, rw_w2=rw_w2,
                       rw_a0=rw_a0, rw_a2=rw_a2, rw_g2=rw_g2, rw_kk=rw_kk, rw_ka=rw_ka, rw_rk=rw_rk,
                       rw_lnx_w=rw_lnx_w, rw_lnx_b=rw_lnx_b))

    mem2 = mem_prompt.reshape(bp * N_MEM, d)
    mk, mv = [], []
    for i in range(depth):
        wkv_i = jnp.concatenate([bf(w_xk[i]), bf(w_xv[i])], axis=1)
        kv = norm_matmul(mem2, norm_mem[i], wkv_i)
        mk.append(kv[:, :d].reshape(bp, N_MEM, d))
        mv.append(kv[:, d:].reshape(bp, N_MEM, d))

    y_p, pool_p, shift_p, wkv_p, conv_p = _trunk_seq(x_prompt, mk, mv, W)
    ck = cache_mem_k.reshape(depth, bs, N_MEM, d)
    cv = cache_mem_v.reshape(depth, bs, N_MEM, d)
    y_s, pool_s, shift_s, wkv_s, conv_s = _trunk_step(x_sample, ck, cv, state_pool, state_shift, state_wkv,
                                                       state_conv, W)
    mem_k_p = jnp.stack(mk).reshape(depth, bp, N_MEM, X_HEADS, X_HEAD_DIM)
    mem_v_p = jnp.stack(mv).reshape(depth, bp, N_MEM, X_HEADS, X_HEAD_DIM)
    return (y_p, y_s, pool_p, pool_s, shift_p, shift_s, wkv_p, wkv_s, conv_p, conv_s, mem_k_p, mem_v_p)
```

```python
import functools

import jax
import jax.numpy as jnp
from jax import lax
from jax.experimental import pallas as pl
from jax.experimental.pallas import tpu as pltpu

F32 = jnp.float32
BF16 = jnp.bfloat16

D_MODEL = 1024
POOL_WIDTH = 512
POOL_GROUPS = 4
POOL_GC = 128
POOL_WINDOWS = (2, 4, 8, 16)
POOL_BUF = 15
RWKV_WIDTH = 512
RWKV_HEAD = 64
RWKV_HEADS = 8
W_RANK = 64
A_RANK = 64
G_RANK = 128
SHIFT_WIDTH = 3 * RWKV_WIDTH + W_RANK + A_RANK + G_RANK
IN_AB_WIDTH = POOL_WIDTH + SHIFT_WIDTH
LNX_EPS = 64e-5
CONV_WIDTH = 3
N_EXPERTS = 8
N_MEM = 256
X_HEADS = 4
X_HEAD_DIM = 256
RMS_EPS = 1e-6
PAST_LEN = 16384

CHUNK = 64
GROUP_LANES = 256
POOL_HALO = 16
SHIFT_HALO = 8
ROUTER_PAD = 128
XATTN_STEP_VMEM = 48 * 1024 * 1024


def _rmsnorm(x, g):
    ms = jnp.mean(x * x, axis=-1, keepdims=True)
    return x * lax.rsqrt(ms + RMS_EPS) * g


def _dot(a, b):
    return jnp.dot(a.astype(BF16), b.astype(BF16), preferred_element_type=F32)


def _dot_nt(a, b):
    return lax.dot_general(a.astype(BF16), b.astype(BF16), (((1,), (1,)), ((), ())),
                           preferred_element_type=F32)


def _dot_tn(a, b):
    return lax.dot_general(a.astype(BF16), b.astype(BF16), (((0,), (0,)), ((), ())),
                           preferred_element_type=F32)


def _split3(x):
    hi = x.astype(BF16)
    r1 = x - hi.astype(F32)
    mid = r1.astype(BF16)
    lo = (r1 - mid.astype(F32)).astype(BF16)
    return hi, mid, lo


def _dot_exact_rhs(x, m01):
    hi, mid, lo = _split3(x)
    f = lambda p: jnp.dot(p, m01, preferred_element_type=F32)
    return f(hi) + f(mid) + f(lo)


def _dot_exact_lhs(m01, x):
    hi, mid, lo = _split3(x)
    f = lambda p: jnp.dot(m01, p, preferred_element_type=F32)
    return f(hi) + f(mid) + f(lo)


def _softplus(x):
    return jnp.maximum(x, 0.0) + jnp.log1p(jnp.exp(-jnp.abs(x)))


def _sigmoid(x):
    return 1.0 / (1.0 + jnp.exp(-x))


def _row_tile(m, want):
    t = min(m, want)
    assert m % t == 0, (m, t)
    return t


def _norm_matmul_body(x_ref, g_ref, w_ref, o_ref):
    xn = _rmsnorm(x_ref[...], g_ref[...]).astype(BF16)
    o_ref[...] = jnp.dot(xn, w_ref[...], preferred_element_type=F32)


def norm_matmul(x, g, w):
    m, k = x.shape
    n = w.shape[1]
    tm = _row_tile(m, 512)
    return pl.pallas_call(
        _norm_matmul_body,
        grid=(m // tm,),
        in_specs=[pl.BlockSpec((tm, k), lambda i: (i, 0)),
                  pl.BlockSpec((1, k), lambda i: (0, 0)),
                  pl.BlockSpec((k, n), lambda i: (0, 0))],
        out_specs=pl.BlockSpec((tm, n), lambda i: (i, 0)),
        out_shape=jax.ShapeDtypeStruct((m, n), F32),
        compiler_params=pltpu.CompilerParams(dimension_semantics=("parallel",)),
        name="norm_matmul",
    )(x, g.reshape(1, k), w)


def _matmul_res_body(a_ref, w_ref, r_ref, o_ref):
    o_ref[...] = r_ref[...] + jnp.dot(a_ref[...].astype(BF16), w_ref[...], preferred_element_type=F32)


def matmul_res(a, w, res):
    m, k = a.shape
    n = w.shape[1]
    tm = _row_tile(m, 512)
    return pl.pallas_call(
        _matmul_res_body,
        grid=(m // tm,),
        in_specs=[pl.BlockSpec((tm, k), lambda i: (i, 0)),
                  pl.BlockSpec((k, n), lambda i: (0, 0)),
                  pl.BlockSpec((tm, n), lambda i: (i, 0))],
        out_specs=pl.BlockSpec((tm, n), lambda i: (i, 0)),
        out_shape=jax.ShapeDtypeStruct((m, n), F32),
        compiler_params=pltpu.CompilerParams(dimension_semantics=("parallel",)),
        name="matmul_res",
    )(a, w, res)


def _ffn_body(x_ref, g_ref, wg_ref, wu_ref, wd_ref, o_ref, xn_ref, acc_ref):
    f = pl.program_id(1)

    @pl.when(f == 0)
    def _():
        xn_ref[...] = _rmsnorm(x_ref[...], g_ref[...]).astype(BF16)
        acc_ref[...] = jnp.zeros_like(acc_ref)

    xn = xn_ref[...]
    gate = jnp.dot(xn, wg_ref[...], preferred_element_type=F32)
    up = jnp.dot(xn, wu_ref[...], preferred_element_type=F32)
    hid = (gate * _sigmoid(gate) * up).astype(BF16)
    acc_ref[...] += jnp.dot(hid, wd_ref[...], preferred_element_type=F32)

    @pl.when(f == pl.num_programs(1) - 1)
    def _():
        o_ref[...] = x_ref[...] + acc_ref[...]


def ffn_dense(x, g, wg, wu, wd, tf):
    m, d = x.shape
    ff = wg.shape[1]
    tm = _row_tile(m, 512)
    return pl.pallas_call(
        _ffn_body,
        grid=(m // tm, ff // tf),
        in_specs=[pl.BlockSpec((tm, d), lambda i, f: (i, 0)),
                  pl.BlockSpec((1, d), lambda i, f: (0, 0)),
                  pl.BlockSpec((d, tf), lambda i, f: (0, f)),
                  pl.BlockSpec((d, tf), lambda i, f: (0, f)),
                  pl.BlockSpec((tf, d), lambda i, f: (f, 0))],
        out_specs=pl.BlockSpec((tm, d), lambda i, f: (i, 0)),
        out_shape=jax.ShapeDtypeStruct((m, d), F32),
        scratch_shapes=[pltpu.VMEM((tm, d), BF16), pltpu.VMEM((tm, d), F32)],
        compiler_params=pltpu.CompilerParams(dimension_semantics=("parallel", "arbitrary")),
        name="ffn_dense",
    )(x, g.reshape(1, d), wg, wu, wd)


def _router_body(x_ref, g_ref, w_ref, b_ref, o_ref):
    xn = _rmsnorm(x_ref[...], g_ref[...])
    logits = _dot_exact_both(xn, w_ref[...]) + b_ref[...]
    lane = lax.broadcasted_iota(jnp.int32, logits.shape, 1)
    neg = jnp.float32(-jnp.inf)
    logits = jnp.where(lane < N_EXPERTS, logits, neg)
    m1 = jnp.max(logits, axis=-1, keepdims=True)
    i1 = jnp.min(jnp.where(logits == m1, lane, ROUTER_PAD), axis=-1, keepdims=True)
    rest = jnp.where(lane == i1, neg, logits)
    m2 = jnp.max(rest, axis=-1, keepdims=True)
    i2 = jnp.min(jnp.where(rest == m2, lane, ROUTER_PAD), axis=-1, keepdims=True)
    e2 = jnp.exp(m2 - m1)
    den = 1.0 + e2
    gates = jnp.where(lane == i1, 1.0 / den, 0.0) + jnp.where(lane == i2, e2 / den, 0.0)
    o_ref[...] = gates[:, :N_EXPERTS]


def _dot_exact_both(x, w):
    xh, xm, xl = _split3(x)
    wh, wm, wl = _split3(w)
    f = lambda p, q: jnp.dot(p, q, preferred_element_type=F32)
    return (f(xh, wh) + (f(xh, wm) + f(xm, wh))) + ((f(xm, wm) + f(xh, wl)) + f(xl, wh))


def moe_router(x, g, router_w, router_b):
    m, d = x.shape
    tm = _row_tile(m, 512)
    w = jnp.zeros((d, ROUTER_PAD), F32).at[:, :N_EXPERTS].set(router_w)
    b = jnp.zeros((1, ROUTER_PAD), F32).at[0, :N_EXPERTS].set(router_b)
    return pl.pallas_call(
        _router_body,
        grid=(m // tm,),
        in_specs=[pl.BlockSpec((tm, d), lambda i: (i, 0)),
                  pl.BlockSpec((1, d), lambda i: (0, 0)),
                  pl.BlockSpec((d, ROUTER_PAD), lambda i: (0, 0)),
                  pl.BlockSpec((1, ROUTER_PAD), lambda i: (0, 0))],
        out_specs=pl.BlockSpec((tm, N_EXPERTS), lambda i: (i, 0)),
        out_shape=jax.ShapeDtypeStruct((m, N_EXPERTS), F32),
        compiler_params=pltpu.CompilerParams(dimension_semantics=("parallel",)),
        name="moe_router",
    )(x, g.reshape(1, d), w, b)


def _moe_body(x_ref, g_ref, gates_ref, wg_ref, wu_ref, wd_ref, o_ref, xn_ref, acc_ref, part_ref):
    e = pl.program_id(1)
    f = pl.program_id(2)
    nf = pl.num_programs(2)

    @pl.when(jnp.logical_and(e == 0, f == 0))
    def _():
        xn_ref[...] = _rmsnorm(x_ref[...], g_ref[...]).astype(BF16)
        acc_ref[...] = jnp.zeros_like(acc_ref)

    @pl.when(f == 0)
    def _():
        part_ref[...] = jnp.zeros_like(part_ref)

    xn = xn_ref[...]
    gate = jnp.dot(xn, wg_ref[0], preferred_element_type=F32)
    up = jnp.dot(xn, wu_ref[0], preferred_element_type=F32)
    hid = (gate * _sigmoid(gate) * up).astype(BF16)
    part_ref[...] += jnp.dot(hid, wd_ref[0], preferred_element_type=F32)

    @pl.when(f == nf - 1)
    def _():
        gts = gates_ref[...]
        lane = lax.broadcasted_iota(jnp.int32, gts.shape, 1)
        ge = jnp.sum(jnp.where(lane == e, gts, 0.0), axis=-1, keepdims=True)
        acc_ref[...] += ge * part_ref[...]

    @pl.when(jnp.logical_and(e == pl.num_programs(1) - 1, f == nf - 1))
    def _():
        o_ref[...] = x_ref[...] + acc_ref[...]


def moe_all_experts(x, g, gates, wg, wu, wd, tf):
    m, d = x.shape
    ne, _, ff = wg.shape
    tm = _row_tile(m, 512)
    return pl.pallas_call(
        _moe_body,
        grid=(m // tm, ne, ff // tf),
        in_specs=[pl.BlockSpec((tm, d), lambda i, e, f: (i, 0)),
                  pl.BlockSpec((1, d), lambda i, e, f: (0, 0)),
                  pl.BlockSpec((tm, ne), lambda i, e, f: (i, 0)),
                  pl.BlockSpec((1, d, tf), lambda i, e, f: (e, 0, f)),
                  pl.BlockSpec((1, d, tf), lambda i, e, f: (e, 0, f)),
                  pl.BlockSpec((1, tf, d), lambda i, e, f: (e, f, 0))],
        out_specs=pl.BlockSpec((tm, d), lambda i, e, f: (i, 0)),
        out_shape=jax.ShapeDtypeStruct((m, d), F32),
        scratch_shapes=[pltpu.VMEM((tm, d), BF16), pltpu.VMEM((tm, d), F32), pltpu.VMEM((tm, d), F32)],
        compiler_params=pltpu.CompilerParams(dimension_semantics=("parallel", "arbitrary", "arbitrary")),
        name="moe_all_experts",
    )(x, g.reshape(1, d), gates, wg, wu, wd)


def _xattn_seq_body(x_ref, g_ref, wq_ref, wo_ref, k_ref, v_ref, o_ref):
    x = x_ref[0]
    xn = _rmsnorm(x, g_ref[...]).astype(BF16)
    q = jnp.dot(xn, wq_ref[...], preferred_element_type=F32).astype(BF16)
    kb = k_ref[0].astype(BF16)
    vb = v_ref[0].astype(BF16)
    heads = []
    for h in range(X_HEADS):
        sl = slice(h * X_HEAD_DIM, (h + 1) * X_HEAD_DIM)
        s = _dot_nt(q[:, sl], kb[:, sl]) * (X_HEAD_DIM ** -0.5)
        mx = jnp.max(s, axis=-1, keepdims=True)
        ex = jnp.exp(s - mx)
        pr = ex / jnp.sum(ex, axis=-1, keepdims=True)
        heads.append(jnp.dot(pr.astype(BF16), vb[:, sl], preferred_element_type=F32))
    att = jnp.concatenate(heads, axis=-1).astype(BF16)
    o_ref[0] = x + jnp.dot(att, wo_ref[...], preferred_element_type=F32)


def xattn_seq(x, g, wq, wo, mem_k, mem_v):
    b, t, d = x.shape
    tq = _row_tile(t, 512)
    return pl.pallas_call(
        _xattn_seq_body,
        grid=(b, t // tq),
        in_specs=[pl.BlockSpec((1, tq, d), lambda i, j: (i, j, 0)),
                  pl.BlockSpec((1, d), lambda i, j: (0, 0)),
                  pl.BlockSpec((d, d), lambda i, j: (0, 0)),
                  pl.BlockSpec((d, d), lambda i, j: (0, 0)),
                  pl.BlockSpec((1, N_MEM, d), lambda i, j: (i, 0, 0)),
                  pl.BlockSpec((1, N_MEM, d), lambda i, j: (i, 0, 0))],
        out_specs=pl.BlockSpec((1, tq, d), lambda i, j: (i, j, 0)),
        out_shape=jax.ShapeDtypeStruct((b, t, d), F32),
        compiler_params=pltpu.CompilerParams(dimension_semantics=("parallel", "parallel")),
        name="xattn_seq",
    )(x, g.reshape(1, d), wq, wo, mem_k, mem_v)


XATTN_STEP_ROWS = 8
XATTN_STEP_KV = 4


def _xattn_step_body(x_ref, g_ref, wq_ref, wo_ref, k_ref, v_ref, o_ref, q_s, att_s):
    j = pl.program_id(1)

    @pl.when(j == 0)
    def _():
        xn = _rmsnorm(x_ref[...], g_ref[...]).astype(BF16)
        q_s[...] = jnp.dot(xn, wq_ref[...], preferred_element_type=F32) * (X_HEAD_DIM ** -0.5)

    for b in range(XATTN_STEP_KV):
        rsel = pl.ds(j * XATTN_STEP_KV + b, 1)
        q4 = jnp.concatenate([q_s[rsel, h * X_HEAD_DIM:(h + 1) * X_HEAD_DIM] for h in range(X_HEADS)], axis=0)
        s = jnp.sum(k_ref[0, b] * q4[None], axis=-1, keepdims=True)
        mx = jnp.max(s, axis=0, keepdims=True)
        ex = jnp.exp(s - mx)
        pr = ex / jnp.sum(ex, axis=0, keepdims=True)
        o4 = jnp.sum(pr * v_ref[0, b], axis=0)
        for h in range(X_HEADS):
            att_s[rsel, h * X_HEAD_DIM:(h + 1) * X_HEAD_DIM] = o4[h:h + 1, :]

    @pl.when(j == pl.num_programs(1) - 1)
    def _():
        o_ref[...] = x_ref[...] + jnp.dot(att_s[...].astype(BF16), wo_ref[...], preferred_element_type=F32)


def xattn_step(x, g, wq, wo, mem_k, mem_v, layer):
    b, d = x.shape
    rows, kvb = XATTN_STEP_ROWS, XATTN_STEP_KV
    assert b % rows == 0 and rows % kvb == 0
    nj = rows // kvb
    kv_spec = pl.BlockSpec((1, kvb, N_MEM, X_HEADS, X_HEAD_DIM), lambda i, j: (layer, i * nj + j, 0, 0, 0))
    return pl.pallas_call(
        _xattn_step_body,
        grid=(b // rows, nj),
        in_specs=[pl.BlockSpec((rows, d), lambda i, j: (i, 0)),
                  pl.BlockSpec((1, d), lambda i, j: (0, 0)),
                  pl.BlockSpec((d, d), lambda i, j: (0, 0)),
                  pl.BlockSpec((d, d), lambda i, j: (0, 0)),
                  kv_spec, kv_spec],
        out_specs=pl.BlockSpec((rows, d), lambda i, j: (i, 0)),
        out_shape=jax.ShapeDtypeStruct((b, d), F32),
        scratch_shapes=[pltpu.VMEM((rows, d), F32), pltpu.VMEM((rows, d), F32)],
        compiler_params=pltpu.CompilerParams(dimension_semantics=("parallel", "arbitrary"),
                                             vmem_limit_bytes=XATTN_STEP_VMEM),
        name="xattn_step",
    )(x, g.reshape(1, d), wq, wo, mem_k, mem_v)


def _head_sum(x, bdiag_ref):
    return _dot_exact_rhs(x, bdiag_ref[...])


def _pool_group_out(acc, cnt, u, pool_w_ref, pool_scale_ref, g):
    cs = slice(g * POOL_GC, (g + 1) * POOL_GC)
    d = acc / cnt - u
    return _dot(d, pool_w_ref[g]) * pool_scale_ref[:, cs]


def _rwkv_prep(p, shifted, prm):
    (mu_ref, w0_ref, w2_ref, a0_ref, a2_ref, g2_ref, kk_ref, ka_ref, bdiag_ref) = prm
    m = p + (shifted - p) * mu_ref[...]
    c = RWKV_WIDTH
    r = m[:, 0:c]
    k = m[:, c:2 * c]
    v = m[:, 2 * c:3 * c]
    dwa = m[:, 3 * c:3 * c + W_RANK + A_RANK]
    dg = m[:, 3 * c + W_RANK + A_RANK:]
    w_log = -_softplus(-(w0_ref[...] + _dot(jnp.tanh(dwa), w2_ref[...]))) - 0.5
    lw = -jnp.exp(w_log)
    a = _sigmoid(a0_ref[...] + _dot(dwa, a2_ref[...]))
    gate = _dot(_sigmoid(dg), g2_ref[...])
    kk = k * kk_ref[...]
    kk = kk / jnp.maximum(jnp.sqrt(_head_sum(kk * kk, bdiag_ref)), 1e-12)
    kmod = k * (1.0 + (a - 1.0) * ka_ref[...])
    return r, lw, kmod, v, kk, a, gate


def _rwkv_finish(y, r, kmod, v, gate, rk_ref, lnw_ref, lnb_ref, bdiag_ref):
    mu = _head_sum(y, bdiag_ref) * (1.0 / RWKV_HEAD)
    dlt = y - mu
    var = _head_sum(dlt * dlt, bdiag_ref) * (1.0 / RWKV_HEAD)
    yn = dlt * lax.rsqrt(var + LNX_EPS) * lnw_ref[...] + lnb_ref[...]
    bonus = _head_sum(r * kmod * rk_ref[...], bdiag_ref) * v
    return (yn + bonus) * gate


def _chains(x, nc):
    return [x[c * CHUNK:(c + 1) * CHUNK, g * GROUP_LANES:(g + 1) * GROUP_LANES]
            for g in range(RWKV_WIDTH // GROUP_LANES) for c in range(nc)]


def _block_diag(x, m4):
    xb = x.astype(BF16)
    return jnp.concatenate([xb] * (GROUP_LANES // RWKV_HEAD), axis=0) * m4


def _mix0_seq_body(h_ref, pool_w_ref, pool_scale_ref, mu_ref, w0_ref, w2_ref, a0_ref, a2_ref, g2_ref,
                   kk_ref, ka_ref, rk_ref, lnw_ref, lnb_ref, bdiag_ref, tril_ref, blk_ref,
                   o_ref, s_out_ref,
                   ext_u, ext_p, s_ref):
    t = pl.program_id(1)
    tt = o_ref.shape[1]
    nc = tt // CHUNK
    ng = RWKV_WIDTH // GROUP_LANES

    @pl.when(t == 0)
    def _():
        ext_u[0:POOL_HALO, :] = jnp.zeros((POOL_HALO, POOL_WIDTH), F32)
        ext_p[0:SHIFT_HALO, :] = jnp.zeros((SHIFT_HALO, SHIFT_WIDTH), F32)
        s_ref[...] = jnp.zeros_like(s_ref)

    ext_u[POOL_HALO:POOL_HALO + tt, :] = h_ref[0, :, 0:POOL_WIDTH]
    ext_p[SHIFT_HALO:SHIFT_HALO + tt, :] = h_ref[0, :, POOL_WIDTH:IN_AB_WIDTH]

    pos = t * tt + lax.broadcasted_iota(jnp.int32, (tt, 1), 0)
    for g, win in enumerate(POOL_WINDOWS):
        cs = slice(g * POOL_GC, (g + 1) * POOL_GC)
        u = ext_u[POOL_HALO:POOL_HALO + tt, cs]
        acc = u
        for j in range(1, win):
            acc = acc + ext_u[POOL_HALO - j:POOL_HALO - j + tt, cs]
        cnt = jnp.minimum(win, pos + 1).astype(F32)
        o_ref[0, :, cs] = _pool_group_out(acc, cnt, u, pool_w_ref, pool_scale_ref, g)

    p = ext_p[SHIFT_HALO:SHIFT_HALO + tt, :]
    shifted = ext_p[SHIFT_HALO - 1:SHIFT_HALO - 1 + tt, :]
    prm = (mu_ref, w0_ref, w2_ref, a0_ref, a2_ref, g2_ref, kk_ref, ka_ref, bdiag_ref)
    r, lw, kmod, v, kk, a, gate = _rwkv_prep(p, shifted, prm)

    ext_u[0:POOL_HALO, :] = ext_u[tt:tt + POOL_HALO, :]
    ext_p[0:SHIFT_HALO, :] = ext_p[tt:tt + SHIFT_HALO, :]

    cl = _dot_exact_lhs(tril_ref[...], lw)
    tot = _dot_exact_lhs(blk_ref[...], lw)
    gam = jnp.exp(cl)
    gam_inv = jnp.exp(-cl)
    gam_end = jnp.exp(tot)
    beta = kk * a
    kd_full = kmod * gam_inv
    bd_full = beta * gam_inv
    kq_c = _chains(kk * jnp.exp(cl - lw), nc)
    rq_c = _chains(r * gam, nc)
    kd_c = _chains(kd_full, nc)
    bd_c = _chains(bd_full, nc)
    v_c = _chains(v, nc)
    kdg_c = _chains(kd_full * gam_end, nc)
    bdg_c = _chains(bd_full * gam_end, nc)
    ge_c = _chains(gam_end, nc)
    n_ch = ng * nc
    every = range(n_ch)

    m4 = bdiag_ref[0:GROUP_LANES, 0:GROUP_LANES]
    m4f = m4.astype(F32)
    row = lax.broadcasted_iota(jnp.int32, (CHUNK, GROUP_LANES), 0)
    col = lax.broadcasted_iota(jnp.int32, (CHUNK, GROUP_LANES), 1) % RWKV_HEAD
    strict = row > col
    incl = row >= col
    eye_c = jnp.where(row == col, 1.0, 0.0)
    eye_g = (lax.broadcasted_iota(jnp.int32, (GROUP_LANES, GROUP_LANES), 0)
             == lax.broadcasted_iota(jnp.int32, (GROUP_LANES, GROUP_LANES), 1))

    bd_kd = [_block_diag(kd_c[i], m4) for i in every]
    bd_bd = [_block_diag(bd_c[i], m4) for i in every]
    lhs = [jnp.concatenate([kq_c[i], rq_c[i]], axis=0) for i in every]
    pk = [_dot_nt(lhs[i], bd_kd[i]) for i in every]
    pb = [_dot_nt(lhs[i], bd_bd[i]) for i in every]
    a_k = [jnp.where(strict, pk[i][:CHUNK], 0.0) for i in every]
    p_k = [jnp.where(incl, pk[i][CHUNK:], 0.0) for i in every]
    a_b = [jnp.where(strict, pb[i][:CHUNK], 0.0) for i in every]
    p_b = [jnp.where(incl, pb[i][CHUNK:], 0.0) for i in every]
    tm = [eye_c - a_b[i] for i in every]
    apow = a_b
    bd_ap = [_block_diag(apow[i], m4) for i in every]
    n = 1
    while 2 * n < CHUNK:
        apow = [_dot(apow[i], bd_ap[i]) for i in every]
        bd_ap = [_block_diag(apow[i], m4) for i in every]
        tm = [tm[i] + _dot(tm[i], bd_ap[i]) for i in every]
        n *= 2
    bd_v = [_block_diag(v_c[i], m4) for i in every]
    t_kq = [_dot(tm[i], _block_diag(kq_c[i], m4)) for i in every]
    akv = [_dot(a_k[i], bd_v[i]) for i in every]
    t_akv = [_dot(tm[i], _block_diag(akv[i], m4)) for i in every]
    r_y = [rq_c[i] - _dot(p_b[i], _block_diag(t_kq[i], m4)) for i in every]
    y0 = [_dot(p_k[i], bd_v[i]) - _dot(p_b[i], _block_diag(t_akv[i], m4)) for i in every]
    g_m = [m4f * (jnp.where(eye_g, ge_c[i][0:1, :], 0.0) - _dot_tn(t_kq[i], bdg_c[i])) for i in every]
    h_full = [m4f * _dot_tn(jnp.concatenate([v_c[i], t_akv[i]], axis=0),
                            jnp.concatenate([kdg_c[i], -bdg_c[i]], axis=0)) for i in every]
    h_t = [sum(h_full[i][j * RWKV_HEAD:(j + 1) * RWKV_HEAD] for j in range(GROUP_LANES // RWKV_HEAD))
           for i in every]

    ys = [None] * n_ch
    for g in range(ng):
        s = s_ref[g]
        for c in range(nc):
            i = g * nc + c
            ys[i] = _dot_nt(r_y[i], _block_diag(s, m4)) + y0[i]
            s = _dot(s, g_m[i]) + h_t[i]
        s_ref[g] = s
    y = jnp.concatenate([jnp.concatenate([ys[g * nc + c] for g in range(ng)], axis=1) for c in range(nc)],
                        axis=0)

    o_ref[0, :, POOL_WIDTH:] = _rwkv_finish(y, r, kmod, v, gate, rk_ref, lnw_ref, lnb_ref, bdiag_ref)

    @pl.when(t == pl.num_programs(1) - 1)
    def _():
        for hh in range(RWKV_HEADS):
            g, j = divmod(hh, GROUP_LANES // RWKV_HEAD)
            s_out_ref[0, hh] = s_ref[g][:, j * RWKV_HEAD:(j + 1) * RWKV_HEAD]


def _mix0_params(P):
    c = RWKV_WIDTH
    row = lambda x: x.reshape(1, -1).astype(F32)
    w2 = jnp.zeros((W_RANK + A_RANK, c), F32).at[:W_RANK].set(P['rw_w2']).astype(BF16)
    a2 = jnp.zeros((W_RANK + A_RANK, c), F32).at[W_RANK:].set(P['rw_a2']).astype(BF16)
    hid = jnp.arange(c) // RWKV_HEAD
    bdiag = (hid[:, None] == hid[None, :]).astype(BF16)
    return (P['pool_w'].astype(BF16), row(P['pool_scale']), row(P['mu_shift']), row(P['rw_w0']), w2,
            row(P['rw_a0']), a2, P['rw_g2'].astype(BF16), row(P['rw_kk']), row(P['rw_ka']),
            row(P['rw_rk']), row(P['rw_lnx_w']), row(P['rw_lnx_b']), bdiag)


def _full_spec(x):
    nd = x.ndim
    return pl.BlockSpec(x.shape, lambda *_: (0,) * nd)


def mix0_seq(h, P):
    b, t, _ = h.shape
    tt = _row_tile(t, 256)
    assert tt % CHUNK == 0 and tt >= POOL_HALO
    prm = _mix0_params(P)
    ti = jnp.arange(tt)
    same_chunk = (ti[:, None] // CHUNK) == (ti[None, :] // CHUNK)
    tril = (same_chunk & (ti[:, None] >= ti[None, :])).astype(BF16)
    blk = same_chunk.astype(BF16)
    return pl.pallas_call(
        _mix0_seq_body,
        grid=(b, t // tt),
        in_specs=[pl.BlockSpec((1, tt, IN_AB_WIDTH), lambda i, j: (i, j, 0))]
                 + [_full_spec(x) for x in prm] + [_full_spec(tril), _full_spec(blk)],
        out_specs=[pl.BlockSpec((1, tt, D_MODEL), lambda i, j: (i, j, 0)),
                   pl.BlockSpec((1, RWKV_HEADS, RWKV_HEAD, RWKV_HEAD), lambda i, j: (i, 0, 0, 0))],
        out_shape=[jax.ShapeDtypeStruct((b, t, D_MODEL), F32),
                   jax.ShapeDtypeStruct((b, RWKV_HEADS, RWKV_HEAD, RWKV_HEAD), F32)],
        scratch_shapes=[pltpu.VMEM((tt + POOL_HALO, POOL_WIDTH), F32),
                        pltpu.VMEM((tt + SHIFT_HALO, SHIFT_WIDTH), F32),
                        pltpu.VMEM((RWKV_WIDTH // GROUP_LANES, RWKV_HEAD, GROUP_LANES), F32)],
        compiler_params=pltpu.CompilerParams(dimension_semantics=("parallel", "arbitrary")),
        name="mix0_seq",
    )(h, *prm, tril, blk)


def _to_leading(x):
    n = x.shape[0]
    ii = lax.broadcasted_iota(jnp.int32, (n, n, 1), 0)
    jj = lax.broadcasted_iota(jnp.int32, (n, n, 1), 1)
    return jnp.sum(jnp.where(ii == jj, x[None, :, :], 0.0), axis=1, keepdims=True)


def _from_leading(x3):
    n = x3.shape[0]
    ii = lax.broadcasted_iota(jnp.int32, (n, n, 1), 0)
    jj = lax.broadcasted_iota(jnp.int32, (n, n, 1), 1)
    return jnp.sum(jnp.where(ii == jj, x3, 0.0), axis=0)


def _mix0_step_body(h_ref, pool_prev_ref, shift_prev_ref, s_in_ref,
                    pool_w_ref, pool_scale_ref, mu_ref, w0_ref, w2_ref, a0_ref, a2_ref, g2_ref,
                    kk_ref, ka_ref, rk_ref, lnw_ref, lnb_ref, bdiag_ref,
                    o_ref, s_out_ref,
                    r_t, w_t, k_t, v_t, kk_t, bt_t, y_t, r_s, k_s, v_s, g_s):
    hh = pl.program_id(0)
    nb = h_ref.shape[0]

    @pl.when(hh == 0)
    def _():
        for g, win in enumerate(POOL_WINDOWS):
            cs = slice(g * POOL_GC, (g + 1) * POOL_GC)
            u = h_ref[:, cs]
            acc = u
            for j in range(1, win):
                acc = acc + pool_prev_ref[POOL_BUF - j][:, cs]
            cnt = jnp.float32(min(win, PAST_LEN + 1))
            o_ref[:, cs] = _pool_group_out(acc, cnt, u, pool_w_ref, pool_scale_ref, g)
        prm = (mu_ref, w0_ref, w2_ref, a0_ref, a2_ref, g2_ref, kk_ref, ka_ref, bdiag_ref)
        r, lw, kmod, v, kk, a, gate = _rwkv_prep(h_ref[:, POOL_WIDTH:IN_AB_WIDTH], shift_prev_ref[...], prm)
        r_s[...] = r
        k_s[...] = kmod
        v_s[...] = v
        g_s[...] = gate
        r_t[...] = r.T
        w_t[...] = jnp.exp(lw).T
        k_t[...] = kmod.T
        v_t[...] = v.T
        kk_t[...] = kk.T
        bt_t[...] = (kk * a).T

    rows = pl.ds(pl.multiple_of(hh * RWKV_HEAD, RWKV_HEAD), RWKV_HEAD)
    s = s_in_ref[...].T.reshape(RWKV_HEAD, RWKV_HEAD, nb)
    kk = kk_t[rows, :][None]
    s_kk = jnp.sum(s * kk, axis=1, keepdims=True)
    v3 = _to_leading(v_t[rows, :])
    s = s * w_t[rows, :][None] - s_kk * bt_t[rows, :][None] + v3 * k_t[rows, :][None]
    y3 = jnp.sum(s * r_t[rows, :][None], axis=1, keepdims=True)
    y_t[rows, :] = _from_leading(y3)
    s_out_ref[...] = s.reshape(RWKV_HEAD * RWKV_HEAD, nb).T

    @pl.when(hh == pl.num_programs(0) - 1)
    def _():
        o_ref[:, POOL_WIDTH:] = _rwkv_finish(y_t[...].T, r_s[...], k_s[...], v_s[...], g_s[...],
                                             rk_ref, lnw_ref, lnb_ref, bdiag_ref)


def mix0_step(h, pool_prev, shift_prev, s_prev, P):
    b = h.shape[0]
    prm = _mix0_params(P)
    hw = RWKV_HEAD * RWKV_HEAD
    s2 = s_prev.reshape(b, RWKV_HEADS * hw)
    pool_t = jnp.swapaxes(pool_prev, 0, 1)
    tvec = lambda: pltpu.VMEM((RWKV_WIDTH, b), F32)
    svec = lambda: pltpu.VMEM((b, RWKV_WIDTH), F32)
    out, s_new = pl.pallas_call(
        _mix0_step_body,
        grid=(RWKV_HEADS,),
        in_specs=[_full_spec(h), _full_spec(pool_t), _full_spec(shift_prev),
                  pl.BlockSpec((b, hw), lambda i: (0, i))] + [_full_spec(x) for x in prm],
        out_specs=[pl.BlockSpec((b, D_MODEL), lambda i: (0, 0)),
                   pl.BlockSpec((b, hw), lambda i: (0, i))],
        out_shape=[jax.ShapeDtypeStruct((b, D_MODEL), F32),
                   jax.ShapeDtypeStruct((b, RWKV_HEADS * hw), F32)],
        scratch_shapes=[tvec(), tvec(), tvec(), tvec(), tvec(), tvec(), tvec(),
                        svec(), svec(), svec(), svec()],
        compiler_params=pltpu.CompilerParams(dimension_semantics=("arbitrary",)),
        name="mix0_step",
    )(h, pool_t, shift_prev, s2, *prm)
    return out, s_new.reshape(b, RWKV_HEADS, RWKV_HEAD, RWKV_HEAD)


def _conv_seq_body(h_ref, x_ref, cw_ref, wo_ref, o_ref, tail_ref, ext):
    t = pl.program_id(1)
    tt = o_ref.shape[1]
    c = D_MODEL

    @pl.when(t == 0)
    def _():
        ext[0:SHIFT_HALO, :] = jnp.zeros((SHIFT_HALO, c), F32)

    ext[SHIFT_HALO:SHIFT_HALO + tt, :] = h_ref[0, :, c:2 * c] * h_ref[0, :, 2 * c:3 * c]
    z = cw_ref[0:1, :] * ext[SHIFT_HALO - 2:SHIFT_HALO - 2 + tt, :]
    z = z + cw_ref[1:2, :] * ext[SHIFT_HALO - 1:SHIFT_HALO - 1 + tt, :]
    z = z + cw_ref[2:3, :] * ext[SHIFT_HALO:SHIFT_HALO + tt, :]
    gated = (h_ref[0, :, 0:c] * z).astype(BF16)
    o_ref[0] = x_ref[0] + jnp.dot(gated, wo_ref[...], preferred_element_type=F32)
    ext[0:SHIFT_HALO, :] = ext[tt:tt + SHIFT_HALO, :]

    @pl.when(t == pl.num_programs(1) - 1)
    def _():
        tail_ref[0] = ext[SHIFT_HALO - (CONV_WIDTH - 1):SHIFT_HALO, :]


def conv_seq(h, x, conv_w, w_out):
    b, t, _ = h.shape
    c = D_MODEL
    tt = _row_tile(t, 512)
    return pl.pallas_call(
        _conv_seq_body,
        grid=(b, t // tt),
        in_specs=[pl.BlockSpec((1, tt, 3 * c), lambda i, j: (i, j, 0)),
                  pl.BlockSpec((1, tt, c), lambda i, j: (i, j, 0)),
                  _full_spec(conv_w), _full_spec(w_out)],
        out_specs=[pl.BlockSpec((1, tt, c), lambda i, j: (i, j, 0)),
                   pl.BlockSpec((1, CONV_WIDTH - 1, c), lambda i, j: (i, 0, 0))],
        out_shape=[jax.ShapeDtypeStruct((b, t, c), F32),
                   jax.ShapeDtypeStruct((b, CONV_WIDTH - 1, c), F32)],
        scratch_shapes=[pltpu.VMEM((tt + SHIFT_HALO, c), F32)],
        compiler_params=pltpu.CompilerParams(dimension_semantics=("parallel", "arbitrary")),
        name="conv_seq",
    )(h, x, conv_w, w_out)


def _conv_step_body(h_ref, x_ref, p0_ref, p1_ref, cw_ref, wo_ref, o_ref, e_ref):
    c = D_MODEL
    e = h_ref[:, c:2 * c] * h_ref[:, 2 * c:3 * c]
    z = cw_ref[0:1, :] * p0_ref[...] + cw_ref[1:2, :] * p1_ref[...] + cw_ref[2:3, :] * e
    gated = (h_ref[:, 0:c] * z).astype(BF16)
    o_ref[...] = x_ref[...] + jnp.dot(gated, wo_ref[...], preferred_element_type=F32)
    e_ref[...] = e


def conv_step(h, x, prev, conv_w, w_out):
    b = h.shape[0]
    c = D_MODEL
    args = (h, x, prev[:, 0, :], prev[:, 1, :], conv_w, w_out)
    return pl.pallas_call(
        _conv_step_body,
        grid=(1,),
        in_specs=[_full_spec(a) for a in args],
        out_specs=[pl.BlockSpec((b, c), lambda i: (0, 0)), pl.BlockSpec((b, c), lambda i: (0, 0))],
        out_shape=[jax.ShapeDtypeStruct((b, c), F32), jax.ShapeDtypeStruct((b, c), F32)],
        compiler_params=pltpu.CompilerParams(dimension_semantics=("arbitrary",)),
        name="conv_step",
    )(*args)


def _final_norm_body(x_ref, g_ref, o_ref):
    o_ref[...] = _rmsnorm(x_ref[...], g_ref[...])


def final_norm(x, g):
    m, d = x.shape
    tm = _row_tile(m, 1024)
    return pl.pallas_call(
        _final_norm_body,
        grid=(m // tm,),
        in_specs=[pl.BlockSpec((tm, d), lambda i: (i, 0)), pl.BlockSpec((1, d), lambda i: (0, 0))],
        out_specs=pl.BlockSpec((tm, d), lambda i: (i, 0)),
        out_shape=jax.ShapeDtypeStruct((m, d), F32),
        compiler_params=pltpu.CompilerParams(dimension_semantics=("parallel",)),
        name="final_norm",
    )(x, g.reshape(1, d))


FFN_TILE = 256
MOE_TILE = 512


def _layer_tail(x2, i, mem_k, mem_v, W, seq_shape):
    if seq_shape is not None:
        b, t = seq_shape
        x2 = xattn_seq(x2.reshape(b, t, D_MODEL), W['norm_xattn'][i], W['w_xq'][i], W['w_xo'][i],
                       mem_k, mem_v).reshape(b * t, D_MODEL)
    else:
        x2 = xattn_step(x2, W['norm_xattn'][i], W['w_xq'][i], W['w_xo'][i], mem_k, mem_v, i)
    if i % 2 == 0:
        e = i // 2
        return ffn_dense(x2, W['norm_ffn'][i], W['ffn_gate'][e], W['ffn_up'][e], W['ffn_down'][e], FFN_TILE)
    o = i // 2
    gates = moe_router(x2, W['norm_ffn'][i], W['router_w'][o], W['router_b'][o])
    return moe_all_experts(x2, W['norm_ffn'][i], gates, W['moe_gate'][o], W['moe_up'][o], W['moe_down'][o],
                           MOE_TILE)


def _trunk_seq(x, mem_k, mem_v, W):
    b, t, d = x.shape
    x2 = x.reshape(b * t, d)
    P0 = {k: v[0] for k, v in W['mix0'].items()}
    h = norm_matmul(x2, W['norm_mix'][0], W['w_in_ab'][0]).reshape(b, t, IN_AB_WIDTH)
    mix, wkv = mix0_seq(h, P0)
    pool = h[:, t - POOL_BUF:, :POOL_WIDTH]
    shift = h[:, t - 1, POOL_WIDTH:]
    x2 = matmul_res(mix.reshape(b * t, d), W['w_out_ab'][0], x2)
    x2 = _layer_tail(x2, 0, mem_k[0], mem_v[0], W, (b, t))
    h = norm_matmul(x2, W['norm_mix'][1], W['w_in_c'][0]).reshape(b, t, 3 * d)
    x3, conv = conv_seq(h, x2.reshape(b, t, d), W['conv_w'][0], W['w_out_c'][0])
    x2 = _layer_tail(x3.reshape(b * t, d), 1, mem_k[1], mem_v[1], W, (b, t))
    y = final_norm(x2, W['norm_final']).reshape(b, t, d)
    return y, pool[None], shift[None], wkv[None], conv[None]


def _trunk_step(x, mem_k, mem_v, pool_prev, shift_prev, wkv_prev, conv_prev, W):
    b, _, d = x.shape
    x2 = x.reshape(b, d)
    P0 = {k: v[0] for k, v in W['mix0'].items()}
    h = norm_matmul(x2, W['norm_mix'][0], W['w_in_ab'][0])
    mix, wkv = mix0_step(h, pool_prev[0], shift_prev[0], wkv_prev[0], P0)
    pool = jnp.concatenate([pool_prev[0][:, 1:], h[:, None, :POOL_WIDTH]], axis=1)
    shift = h[:, POOL_WIDTH:]
    x2 = matmul_res(mix, W['w_out_ab'][0], x2)
    x2 = _layer_tail(x2, 0, mem_k, mem_v, W, None)
    h = norm_matmul(x2, W['norm_mix'][1], W['w_in_c'][0])
    x2, e = conv_step(h, x2, conv_prev[0], W['conv_w'][0], W['w_out_c'][0])
    conv = jnp.concatenate([conv_prev[0][:, 1:], e[:, None]], axis=1)
    x2 = _layer_tail(x2, 1, mem_k, mem_v, W, None)
    y = final_norm(x2, W['norm_final']).reshape(b, 1, d)
    return y, pool[None], shift[None], wkv[None], conv[None]


def kernel(x_prompt, x_sample, mem_prompt, cache_mem_k, cache_mem_v, state_pool, state_shift, state_wkv, state_conv, norm_mix, norm_xattn, norm_mem, norm_ffn, norm_final, w_xq, w_xk, w_xv, w_xo, w_in_ab, pool_w, pool_scale, mu_shift, rw_w0, rw_w2, rw_a0, rw_a2, rw_g2, rw_kk, rw_ka, rw_rk, rw_lnx_w, rw_lnx_b, w_out_ab, ffn_gate, ffn_up, ffn_down, w_in_c, conv_w, w_out_c, router_w, router_b, moe_gate, moe_up, moe_down):
    depth = norm_mix.shape[0]
    assert depth == 2 and w_in_ab.shape[0] == 1 and w_in_c.shape[0] == 1
    bp = x_prompt.shape[0]
    bs = x_sample.shape[0]
    d = D_MODEL
    bf = lambda w: w.astype(BF16)
    W = dict(norm_mix=norm_mix, norm_xattn=norm_xattn, norm_ffn=norm_ffn, norm_final=norm_final,
             w_xq=bf(w_xq), w_xo=bf(w_xo), w_in_ab=bf(w_in_ab), w_out_ab=bf(w_out_ab),
             ffn_gate=bf(ffn_gate), ffn_up=bf(ffn_up), ffn_down=bf(ffn_down),
             w_in_c=bf(w_in_c), conv_w=conv_w, w_out_c=bf(w_out_c),
             router_w=router_w, router_b=router_b,
             moe_gate=bf(moe_gate), moe_up=bf(moe_up), moe_down=bf(moe_down),
             mix0=dict(pool_w=pool_w, pool_scale=pool_scale, mu_shift=mu_shift, rw_w0=rw_w0, rw_w2=rw_w2,
                       rw_a0=rw_a0, rw_a2=rw_a2, rw_g2=rw_g2, rw_kk=rw_kk, rw_ka=rw_ka, rw_rk=rw_rk,
                       rw_lnx_w=rw_lnx_w, rw_lnx_b=rw_lnx_b))

    mem2 = mem_prompt.reshape(bp * N_MEM, d)
    mk, mv = [], []
    for i in range(depth):
        wkv_i = jnp.concatenate([bf(w_xk[i]), bf(w_xv[i])], axis=1)
        kv = norm_matmul(mem2, norm_mem[i], wkv_i)
        mk.append(kv[:, :d].reshape(bp, N_MEM, d))
        mv.append(kv[:, d:].reshape(bp, N_MEM, d))

    y_p, pool_p, shift_p, wkv_p, conv_p = _trunk_seq(x_prompt, mk, mv, W)
    y_s, pool_s, shift_s, wkv_s, conv_s = _trunk_step(x_sample, cache_mem_k, cache_mem_v, state_pool,
                                                       state_shift, state_wkv, state_conv, W)
    mem_k_p = jnp.stack(mk).reshape(depth, bp, N_MEM, X_HEADS, X_HEAD_DIM)
    mem_v_p = jnp.stack(mv).reshape(depth, bp, N_MEM, X_HEADS, X_HEAD_DIM)
    return (y_p, y_s, pool_p, pool_s, shift_p, shift_s, wkv_p, wkv_s, conv_p, conv_s, mem_k_p, mem_v_p)
```

```python
import functools

import jax
import jax.numpy as jnp
from jax import lax
from jax.experimental import pallas as pl
from jax.experimental.pallas import tpu as pltpu

F32 = jnp.float32
BF16 = jnp.bfloat16

D_MODEL = 1024
POOL_WIDTH = 512
POOL_GROUPS = 4
POOL_GC = 128
POOL_WINDOWS = (2, 4, 8, 16)
POOL_BUF = 15
RWKV_WIDTH = 512
RWKV_HEAD = 64
RWKV_HEADS = 8
W_RANK = 64
A_RANK = 64
G_RANK = 128
SHIFT_WIDTH = 3 * RWKV_WIDTH + W_RANK + A_RANK + G_RANK
IN_AB_WIDTH = POOL_WIDTH + SHIFT_WIDTH
LNX_EPS = 64e-5
CONV_WIDTH = 3
N_EXPERTS = 8
N_MEM = 256
X_HEADS = 4
X_HEAD_DIM = 256
RMS_EPS = 1e-6
PAST_LEN = 16384

CHUNK = 64
GROUP_LANES = 256
POOL_HALO = 16
SHIFT_HALO = 8
ROUTER_PAD = 128
TOP_K = 2
ROW_SUBLANES = 8
MOE_ROWS = 512
FFN_TILE = 256
MOE_TILE = 512
XATTN_STEP_VMEM = 48 * 1024 * 1024


def _rmsnorm(x, g):
    ms = jnp.mean(x * x, axis=-1, keepdims=True)
    return x * lax.rsqrt(ms + RMS_EPS) * g


def _dot(a, b):
    return jnp.dot(a.astype(BF16), b.astype(BF16), preferred_element_type=F32)


def _dot_nt(a, b):
    return lax.dot_general(a.astype(BF16), b.astype(BF16), (((1,), (1,)), ((), ())),
                           preferred_element_type=F32)


def _dot_tn(a, b):
    return lax.dot_general(a.astype(BF16), b.astype(BF16), (((0,), (0,)), ((), ())),
                           preferred_element_type=F32)


def _split3(x):
    hi = x.astype(BF16)
    r1 = x - hi.astype(F32)
    mid = r1.astype(BF16)
    lo = (r1 - mid.astype(F32)).astype(BF16)
    return hi, mid, lo


def _dot_exact_rhs(x, m01):
    hi, mid, lo = _split3(x)
    f = lambda p: jnp.dot(p, m01, preferred_element_type=F32)
    return f(hi) + f(mid) + f(lo)


def _dot_exact_lhs(m01, x):
    hi, mid, lo = _split3(x)
    f = lambda p: jnp.dot(m01, p, preferred_element_type=F32)
    return f(hi) + f(mid) + f(lo)


def _softplus(x):
    return jnp.maximum(x, 0.0) + jnp.log1p(jnp.exp(-jnp.abs(x)))


def _sigmoid(x):
    return 1.0 / (1.0 + jnp.exp(-x))


def _row_tile(m, want):
    t = min(m, want)
    assert m % t == 0, (m, t)
    return t


def _norm_matmul_body(x_ref, g_ref, w_ref, o_ref):
    xn = _rmsnorm(x_ref[...], g_ref[...]).astype(BF16)
    o_ref[...] = jnp.dot(xn, w_ref[...], preferred_element_type=F32)


def norm_matmul(x, g, w):
    m, k = x.shape
    n = w.shape[1]
    tm = _row_tile(m, 512)
    return pl.pallas_call(
        _norm_matmul_body,
        grid=(m // tm,),
        in_specs=[pl.BlockSpec((tm, k), lambda i: (i, 0)),
                  pl.BlockSpec((1, k), lambda i: (0, 0)),
                  pl.BlockSpec((k, n), lambda i: (0, 0))],
        out_specs=pl.BlockSpec((tm, n), lambda i: (i, 0)),
        out_shape=jax.ShapeDtypeStruct((m, n), F32),
        compiler_params=pltpu.CompilerParams(dimension_semantics=("parallel",)),
        name="norm_matmul",
    )(x, g.reshape(1, k), w)


def _matmul_res_body(a_ref, w_ref, r_ref, o_ref):
    o_ref[...] = r_ref[...] + jnp.dot(a_ref[...].astype(BF16), w_ref[...], preferred_element_type=F32)


def matmul_res(a, w, res):
    m, k = a.shape
    n = w.shape[1]
    tm = _row_tile(m, 512)
    return pl.pallas_call(
        _matmul_res_body,
        grid=(m // tm,),
        in_specs=[pl.BlockSpec((tm, k), lambda i: (i, 0)),
                  pl.BlockSpec((k, n), lambda i: (0, 0)),
                  pl.BlockSpec((tm, n), lambda i: (i, 0))],
        out_specs=pl.BlockSpec((tm, n), lambda i: (i, 0)),
        out_shape=jax.ShapeDtypeStruct((m, n), F32),
        compiler_params=pltpu.CompilerParams(dimension_semantics=("parallel",)),
        name="matmul_res",
    )(a, w, res)


def _ffn_body(x_ref, g_ref, wg_ref, wu_ref, wd_ref, o_ref, xn_ref, acc_ref):
    f = pl.program_id(1)

    @pl.when(f == 0)
    def _():
        xn_ref[...] = _rmsnorm(x_ref[...], g_ref[...]).astype(BF16)
        acc_ref[...] = jnp.zeros_like(acc_ref)

    xn = xn_ref[...]
    gate = jnp.dot(xn, wg_ref[...], preferred_element_type=F32)
    up = jnp.dot(xn, wu_ref[...], preferred_element_type=F32)
    hid = (gate * _sigmoid(gate) * up).astype(BF16)
    acc_ref[...] += jnp.dot(hid, wd_ref[...], preferred_element_type=F32)

    @pl.when(f == pl.num_programs(1) - 1)
    def _():
        o_ref[...] = x_ref[...] + acc_ref[...]


def ffn_dense(x, g, wg, wu, wd, tf):
    m, d = x.shape
    ff = wg.shape[1]
    tm = _row_tile(m, 512)
    return pl.pallas_call(
        _ffn_body,
        grid=(m // tm, ff // tf),
        in_specs=[pl.BlockSpec((tm, d), lambda i, f: (i, 0)),
                  pl.BlockSpec((1, d), lambda i, f: (0, 0)),
                  pl.BlockSpec((d, tf), lambda i, f: (0, f)),
                  pl.BlockSpec((d, tf), lambda i, f: (0, f)),
                  pl.BlockSpec((tf, d), lambda i, f: (f, 0))],
        out_specs=pl.BlockSpec((tm, d), lambda i, f: (i, 0)),
        out_shape=jax.ShapeDtypeStruct((m, d), F32),
        scratch_shapes=[pltpu.VMEM((tm, d), BF16), pltpu.VMEM((tm, d), F32)],
        compiler_params=pltpu.CompilerParams(dimension_semantics=("parallel", "arbitrary")),
        name="ffn_dense",
    )(x, g.reshape(1, d), wg, wu, wd)


def _router_body(x_ref, g_ref, w_ref, b_ref, oi_ref, ow_ref):
    xn = _rmsnorm(x_ref[...], g_ref[...])
    logits = _dot_exact_both(xn, w_ref[...]) + b_ref[...]
    lane = lax.broadcasted_iota(jnp.int32, logits.shape, 1)
    neg = jnp.float32(-jnp.inf)
    logits = jnp.where(lane < N_EXPERTS, logits, neg)
    m1 = jnp.max(logits, axis=-1, keepdims=True)
    i1 = jnp.min(jnp.where(logits == m1, lane, ROUTER_PAD), axis=-1, keepdims=True)
    rest = jnp.where(lane == i1, neg, logits)
    m2 = jnp.max(rest, axis=-1, keepdims=True)
    i2 = jnp.min(jnp.where(rest == m2, lane, ROUTER_PAD), axis=-1, keepdims=True)
    e2 = jnp.exp(m2 - m1)
    den = 1.0 + e2
    slot = lax.broadcasted_iota(jnp.int32, oi_ref.shape, 1)
    oi_ref[...] = jnp.where(slot == 0, i1, i2)
    ow_ref[...] = jnp.where(slot == 0, 1.0 / den, e2 / den)


def _dot_exact_both(x, w):
    xh, xm, xl = _split3(x)
    wh, wm, wl = _split3(w)
    f = lambda p, q: jnp.dot(p, q, preferred_element_type=F32)
    return (f(xh, wh) + (f(xh, wm) + f(xm, wh))) + ((f(xm, wm) + f(xh, wl)) + f(xl, wh))


def moe_router(x, g, router_w, router_b):
    m, d = x.shape
    tm = _row_tile(m, 512)
    w = jnp.zeros((d, ROUTER_PAD), F32).at[:, :N_EXPERTS].set(router_w)
    b = jnp.zeros((1, ROUTER_PAD), F32).at[0, :N_EXPERTS].set(router_b)
    return pl.pallas_call(
        _router_body,
        grid=(m // tm,),
        in_specs=[pl.BlockSpec((tm, d), lambda i: (i, 0)),
                  pl.BlockSpec((1, d), lambda i: (0, 0)),
                  pl.BlockSpec((d, ROUTER_PAD), lambda i: (0, 0)),
                  pl.BlockSpec((1, ROUTER_PAD), lambda i: (0, 0))],
        out_specs=[pl.BlockSpec((tm, TOP_K), lambda i: (i, 0)), pl.BlockSpec((tm, TOP_K), lambda i: (i, 0))],
        out_shape=[jax.ShapeDtypeStruct((m, TOP_K), jnp.int32), jax.ShapeDtypeStruct((m, TOP_K), F32)],
        compiler_params=pltpu.CompilerParams(dimension_semantics=("parallel",)),
        name="moe_router",
    )(x, g.reshape(1, d), w, b)


def _row_tiles_to_2d(ref, slot, n):
    return jnp.concatenate([ref[slot, pl.ds(j, n, stride=ROW_SUBLANES), :] for j in range(ROW_SUBLANES)],
                           axis=1)


def _row_copy(src_hbm, row, dst, slot, i, sem):
    return pltpu.make_async_copy(src_hbm.at[pl.ds(row * ROW_SUBLANES, ROW_SUBLANES)],
                                 dst.at[slot, pl.ds(i * ROW_SUBLANES, ROW_SUBLANES)], sem.at[slot])


def _gather_start(src_hbm, rows_smem, dst, slot, n, sem):
    def body(i, carry):
        _row_copy(src_hbm, rows_smem[0, 0, i], dst, slot, i, sem).start()
        return carry
    lax.fori_loop(0, n, body, 0, unroll=8)


def _gather_wait(src_hbm, rows_smem, dst, slot, n, sem):
    def body(i, carry):
        _row_copy(src_hbm, rows_smem[0, 0, i], dst, slot, i, sem).wait()
        return carry
    lax.fori_loop(0, n, body, 0, unroll=8)


def _moe_group_body(te_ref, nu_ref, tok0_ref, tokc_ref, tokn_ref, x8_ref, g_ref, wg_ref, wu_ref, wd_ref,
                    o_ref, xbuf, sem, xn_ref, acc_ref):
    t = pl.program_id(0)
    f = pl.program_id(1)
    nf = pl.num_programs(1)
    tm = xn_ref.shape[0]
    used = t < nu_ref[0]

    @pl.when(f == 0)
    def _():
        @pl.when(t == 0)
        def _():
            _gather_start(x8_ref, tok0_ref, xbuf, 0, tm, sem)

        @pl.when(t + 1 < nu_ref[0])
        def _():
            _gather_start(x8_ref, tokn_ref, xbuf, (t + 1) % 2, tm, sem)

        @pl.when(used)
        def _():
            slot = t % 2
            _gather_wait(x8_ref, tokc_ref, xbuf, slot, tm, sem)
            x = _row_tiles_to_2d(xbuf, slot, tm)
            xn_ref[...] = _rmsnorm(x, g_ref[...]).astype(BF16)
            acc_ref[...] = jnp.zeros_like(acc_ref)

    @pl.when(used)
    def _():
        xn = xn_ref[...]
        gate = jnp.dot(xn, wg_ref[0], preferred_element_type=F32)
        up = jnp.dot(xn, wu_ref[0], preferred_element_type=F32)
        hid = (gate * _sigmoid(gate) * up).astype(BF16)
        acc_ref[...] += jnp.dot(hid, wd_ref[0], preferred_element_type=F32)

    @pl.when(f == nf - 1)
    def _():
        y = jnp.where(used, acc_ref[...], 0.0)
        for j in range(ROW_SUBLANES):
            o_ref[pl.ds(j, tm, stride=ROW_SUBLANES), :] = y[:, j * 128:(j + 1) * 128]


def moe_grouped(x8, g, tile_expert, n_used, row_token, wg, wu, wd, tf):
    n_tiles, _, tm = row_token.shape
    d = D_MODEL
    ne, _, ff = wg.shape
    nf = ff // tf
    live_f = lambda t, f, nu: jnp.where(t < nu[0], f, nf - 1)
    smem_rows = lambda imap: pl.BlockSpec((1, 1, tm), imap, memory_space=pltpu.SMEM)
    grid_spec = pltpu.PrefetchScalarGridSpec(
        num_scalar_prefetch=2,
        grid=(n_tiles, nf),
        in_specs=[smem_rows(lambda t, f, te, nu: (0, 0, 0)),
                  smem_rows(lambda t, f, te, nu: (t, 0, 0)),
                  smem_rows(lambda t, f, te, nu: (jnp.minimum(t + 1, n_tiles - 1), 0, 0)),
                  pl.BlockSpec(memory_space=pl.ANY),
                  pl.BlockSpec((1, d), lambda t, f, te, nu: (0, 0)),
                  pl.BlockSpec((1, d, tf), lambda t, f, te, nu: (te[t], 0, live_f(t, f, nu))),
                  pl.BlockSpec((1, d, tf), lambda t, f, te, nu: (te[t], 0, live_f(t, f, nu))),
                  pl.BlockSpec((1, tf, d), lambda t, f, te, nu: (te[t], live_f(t, f, nu), 0))],
        out_specs=pl.BlockSpec((tm * ROW_SUBLANES, 128), lambda t, f, te, nu: (t, 0)),
        scratch_shapes=[pltpu.VMEM((2, tm * ROW_SUBLANES, 128), F32), pltpu.SemaphoreType.DMA((2,)),
                        pltpu.VMEM((tm, d), BF16), pltpu.VMEM((tm, d), F32)])
    return pl.pallas_call(
        _moe_group_body,
        grid_spec=grid_spec,
        out_shape=jax.ShapeDtypeStruct((n_tiles * tm * ROW_SUBLANES, 128), F32),
        compiler_params=pltpu.CompilerParams(dimension_semantics=("arbitrary", "arbitrary")),
        name="moe_grouped",
    )(tile_expert, n_used, row_token, row_token, row_token, x8, g.reshape(1, d), wg, wu, wd)


def _moe_combine_body(pos0_ref, posc_ref, posn_ref, y8_ref, x_ref, w_ref, g_ref, o_ref, ybuf, sem):
    t = pl.program_id(0)
    tc = x_ref.shape[0]
    n = TOP_K * tc

    @pl.when(t == 0)
    def _():
        _gather_start(y8_ref, pos0_ref, ybuf, 0, n, sem)

    @pl.when(t + 1 < pl.num_programs(0))
    def _():
        _gather_start(y8_ref, posn_ref, ybuf, (t + 1) % 2, n, sem)

    slot = t % 2
    _gather_wait(y8_ref, posc_ref, ybuf, slot, n, sem)
    y = _row_tiles_to_2d(ybuf, slot, n)
    w = w_ref[...]
    moe = w[:, 0:1] * y[0:tc] + w[:, 1:2] * y[tc:n]
    o_ref[...] = _rmsnorm(x_ref[...] + moe, g_ref[...])


def moe_combine_norm(y8, x, pos, topw, g):
    m, d = x.shape
    nt, _, n = pos.shape
    tc = n // TOP_K
    assert nt * tc == m
    smem_rows = lambda imap: pl.BlockSpec((1, 1, n), imap, memory_space=pltpu.SMEM)
    return pl.pallas_call(
        _moe_combine_body,
        grid=(nt,),
        in_specs=[smem_rows(lambda t: (0, 0, 0)),
                  smem_rows(lambda t: (t, 0, 0)),
                  smem_rows(lambda t: (jnp.minimum(t + 1, nt - 1), 0, 0)),
                  pl.BlockSpec(memory_space=pl.ANY),
                  pl.BlockSpec((tc, d), lambda t: (t, 0)),
                  pl.BlockSpec((tc, TOP_K), lambda t: (t, 0)),
                  pl.BlockSpec((1, d), lambda t: (0, 0))],
        out_specs=pl.BlockSpec((tc, d), lambda t: (t, 0)),
        out_shape=jax.ShapeDtypeStruct((m, d), F32),
        scratch_shapes=[pltpu.VMEM((2, n * ROW_SUBLANES, 128), F32), pltpu.SemaphoreType.DMA((2,))],
        compiler_params=pltpu.CompilerParams(dimension_semantics=("arbitrary",)),
        name="moe_combine_norm",
    )(pos, pos, pos, y8, x, topw, g.reshape(1, d))


def _moe_plan(topi, tm):
    n = topi.shape[0]
    a = TOP_K * n
    n_tiles = (a + N_EXPERTS * (tm - 1) + tm - 1) // tm
    e = topi.reshape(a)
    onehot = (e[:, None] == jnp.arange(N_EXPERTS, dtype=jnp.int32)[None, :]).astype(jnp.int32)
    csum = jnp.cumsum(onehot, axis=0)
    cnt = csum[-1]
    rank = jnp.take_along_axis(csum - onehot, e[:, None], axis=1)[:, 0]
    tiles = (cnt + tm - 1) // tm
    tile_end = jnp.cumsum(tiles)
    tile_start = tile_end - tiles
    n_used = tile_end[-1]
    pos = (tile_start[e] * tm + rank).reshape(n, TOP_K)
    tid = jnp.arange(n_tiles, dtype=jnp.int32)
    tile_expert = jnp.sum(jnp.minimum(tid, n_used - 1)[:, None] >= tile_end[None, :], axis=1).astype(jnp.int32)
    order = jnp.argsort(e, stable=True).astype(jnp.int32)
    row = jnp.arange(n_tiles * tm, dtype=jnp.int32)
    re = tile_expert[row // tm]
    k = row - tile_start[re] * tm
    valid = (k < cnt[re]) & (row // tm < n_used)
    src = jnp.clip(cnt.cumsum()[re] - cnt[re] + k, 0, a - 1)
    row_token = jnp.where(valid, order[src] // TOP_K, 0)
    return tile_expert, n_used.reshape(1).astype(jnp.int32), row_token.reshape(n_tiles, 1, tm), pos


def moe_top2_norm(xs, g, router_w, router_b, wg, wu, wd, g_final):
    tops = [moe_router(x, g, router_w, router_b) for x in xs]
    topi = jnp.concatenate([t[0] for t in tops], axis=0)
    tile_expert, n_used, row_token, pos = _moe_plan(topi, MOE_ROWS)
    x8 = jnp.concatenate(xs, axis=0).reshape(-1, 128)
    y8 = moe_grouped(x8, g, tile_expert, n_used, row_token, wg, wu, wd, MOE_TILE)
    outs, start = [], 0
    for x, (_, topw) in zip(xs, tops):
        m = x.shape[0]
        tc = _row_tile(m, 256)
        p = pos[start:start + m].reshape(m // tc, tc, TOP_K)
        p = jnp.swapaxes(p, 1, 2).reshape(m // tc, 1, TOP_K * tc)
        outs.append(moe_combine_norm(y8, x, p, topw, g_final))
        start += m
    return outs


def _xattn_seq_body(x_ref, g_ref, wq_ref, wo_ref, k_ref, v_ref, o_ref):
    x = x_ref[0]
    xn = _rmsnorm(x, g_ref[...]).astype(BF16)
    q = jnp.dot(xn, wq_ref[...], preferred_element_type=F32).astype(BF16)
    kb = k_ref[0].astype(BF16)
    vb = v_ref[0].astype(BF16)
    heads = []
    for h in range(X_HEADS):
        sl = slice(h * X_HEAD_DIM, (h + 1) * X_HEAD_DIM)
        s = _dot_nt(q[:, sl], kb[:, sl]) * (X_HEAD_DIM ** -0.5)
        mx = jnp.max(s, axis=-1, keepdims=True)
        ex = jnp.exp(s - mx)
        pr = ex / jnp.sum(ex, axis=-1, keepdims=True)
        heads.append(jnp.dot(pr.astype(BF16), vb[:, sl], preferred_element_type=F32))
    att = jnp.concatenate(heads, axis=-1).astype(BF16)
    o_ref[0] = x + jnp.dot(att, wo_ref[...], preferred_element_type=F32)


def xattn_seq(x, g, wq, wo, mem_k, mem_v):
    b, t, d = x.shape
    tq = _row_tile(t, 512)
    return pl.pallas_call(
        _xattn_seq_body,
        grid=(b, t // tq),
        in_specs=[pl.BlockSpec((1, tq, d), lambda i, j: (i, j, 0)),
                  pl.BlockSpec((1, d), lambda i, j: (0, 0)),
                  pl.BlockSpec((d, d), lambda i, j: (0, 0)),
                  pl.BlockSpec((d, d), lambda i, j: (0, 0)),
                  pl.BlockSpec((1, N_MEM, d), lambda i, j: (i, 0, 0)),
                  pl.BlockSpec((1, N_MEM, d), lambda i, j: (i, 0, 0))],
        out_specs=pl.BlockSpec((1, tq, d), lambda i, j: (i, j, 0)),
        out_shape=jax.ShapeDtypeStruct((b, t, d), F32),
        compiler_params=pltpu.CompilerParams(dimension_semantics=("parallel", "parallel")),
        name="xattn_seq",
    )(x, g.reshape(1, d), wq, wo, mem_k, mem_v)


XATTN_STEP_ROWS = 8
XATTN_STEP_KV = 4


def _xattn_step_body(x_ref, g_ref, wq_ref, wo_ref, k_ref, v_ref, o_ref, q_s, att_s):
    j = pl.program_id(1)

    @pl.when(j == 0)
    def _():
        xn = _rmsnorm(x_ref[...], g_ref[...]).astype(BF16)
        q_s[...] = jnp.dot(xn, wq_ref[...], preferred_element_type=F32) * (X_HEAD_DIM ** -0.5)

    for b in range(XATTN_STEP_KV):
        rsel = pl.ds(j * XATTN_STEP_KV + b, 1)
        q4 = jnp.concatenate([q_s[rsel, h * X_HEAD_DIM:(h + 1) * X_HEAD_DIM] for h in range(X_HEADS)], axis=0)
        s = jnp.sum(k_ref[0, b] * q4[None], axis=-1, keepdims=True)
        mx = jnp.max(s, axis=0, keepdims=True)
        ex = jnp.exp(s - mx)
        pr = ex / jnp.sum(ex, axis=0, keepdims=True)
        o4 = jnp.sum(pr * v_ref[0, b], axis=0)
        for h in range(X_HEADS):
            att_s[rsel, h * X_HEAD_DIM:(h + 1) * X_HEAD_DIM] = o4[h:h + 1, :]

    @pl.when(j == pl.num_programs(1) - 1)
    def _():
        o_ref[...] = x_ref[...] + jnp.dot(att_s[...].astype(BF16), wo_ref[...], preferred_element_type=F32)


def xattn_step(x, g, wq, wo, mem_k, mem_v, layer):
    b, d = x.shape
    rows, kvb = XATTN_STEP_ROWS, XATTN_STEP_KV
    assert b % rows == 0 and rows % kvb == 0
    nj = rows // kvb
    kv_spec = pl.BlockSpec((1, kvb, N_MEM, X_HEADS, X_HEAD_DIM), lambda i, j: (layer, i * nj + j, 0, 0, 0))
    return pl.pallas_call(
        _xattn_step_body,
        grid=(b // rows, nj),
        in_specs=[pl.BlockSpec((rows, d), lambda i, j: (i, 0)),
                  pl.BlockSpec((1, d), lambda i, j: (0, 0)),
                  pl.BlockSpec((d, d), lambda i, j: (0, 0)),
                  pl.BlockSpec((d, d), lambda i, j: (0, 0)),
                  kv_spec, kv_spec],
        out_specs=pl.BlockSpec((rows, d), lambda i, j: (i, 0)),
        out_shape=jax.ShapeDtypeStruct((b, d), F32),
        scratch_shapes=[pltpu.VMEM((rows, d), F32), pltpu.VMEM((rows, d), F32)],
        compiler_params=pltpu.CompilerParams(dimension_semantics=("parallel", "arbitrary"),
                                             vmem_limit_bytes=XATTN_STEP_VMEM),
        name="xattn_step",
    )(x, g.reshape(1, d), wq, wo, mem_k, mem_v)


def _head_sum(x, bdiag_ref):
    return _dot_exact_rhs(x, bdiag_ref[...])


def _pool_group_out(acc, cnt, u, pool_w_ref, pool_scale_ref, g):
    cs = slice(g * POOL_GC, (g + 1) * POOL_GC)
    d = acc / cnt - u
    return _dot(d, pool_w_ref[g]) * pool_scale_ref[:, cs]


def _rwkv_prep(p, shifted, prm):
    (mu_ref, w0_ref, w2_ref, a0_ref, a2_ref, g2_ref, kk_ref, ka_ref, bdiag_ref) = prm
    m = p + (shifted - p) * mu_ref[...]
    c = RWKV_WIDTH
    r = m[:, 0:c]
    k = m[:, c:2 * c]
    v = m[:, 2 * c:3 * c]
    dwa = m[:, 3 * c:3 * c + W_RANK + A_RANK]
    dg = m[:, 3 * c + W_RANK + A_RANK:]
    w_log = -_softplus(-(w0_ref[...] + _dot(jnp.tanh(dwa), w2_ref[...]))) - 0.5
    lw = -jnp.exp(w_log)
    a = _sigmoid(a0_ref[...] + _dot(dwa, a2_ref[...]))
    gate = _dot(_sigmoid(dg), g2_ref[...])
    kk = k * kk_ref[...]
    kk = kk / jnp.maximum(jnp.sqrt(_head_sum(kk * kk, bdiag_ref)), 1e-12)
    kmod = k * (1.0 + (a - 1.0) * ka_ref[...])
    return r, lw, kmod, v, kk, a, gate


def _rwkv_finish(y, r, kmod, v, gate, rk_ref, lnw_ref, lnb_ref, bdiag_ref):
    mu = _head_sum(y, bdiag_ref) * (1.0 / RWKV_HEAD)
    dlt = y - mu
    var = _head_sum(dlt * dlt, bdiag_ref) * (1.0 / RWKV_HEAD)
    yn = dlt * lax.rsqrt(var + LNX_EPS) * lnw_ref[...] + lnb_ref[...]
    bonus = _head_sum(r * kmod * rk_ref[...], bdiag_ref) * v
    return (yn + bonus) * gate


def _chains(x, nc):
    return [x[c * CHUNK:(c + 1) * CHUNK, g * GROUP_LANES:(g + 1) * GROUP_LANES]
            for g in range(RWKV_WIDTH // GROUP_LANES) for c in range(nc)]


def _block_diag(x, m4):
    xb = x.astype(BF16)
    return jnp.concatenate([xb] * (GROUP_LANES // RWKV_HEAD), axis=0) * m4


def _mix0_seq_body(h_ref, pool_w_ref, pool_scale_ref, mu_ref, w0_ref, w2_ref, a0_ref, a2_ref, g2_ref,
                   kk_ref, ka_ref, rk_ref, lnw_ref, lnb_ref, bdiag_ref, tril_ref, blk_ref,
                   o_ref, s_out_ref,
                   ext_u, ext_p, s_ref):
    t = pl.program_id(1)
    tt = o_ref.shape[1]
    nc = tt // CHUNK
    ng = RWKV_WIDTH // GROUP_LANES

    @pl.when(t == 0)
    def _():
        ext_u[0:POOL_HALO, :] = jnp.zeros((POOL_HALO, POOL_WIDTH), F32)
        ext_p[0:SHIFT_HALO, :] = jnp.zeros((SHIFT_HALO, SHIFT_WIDTH), F32)
        s_ref[...] = jnp.zeros_like(s_ref)

    ext_u[POOL_HALO:POOL_HALO + tt, :] = h_ref[0, :, 0:POOL_WIDTH]
    ext_p[SHIFT_HALO:SHIFT_HALO + tt, :] = h_ref[0, :, POOL_WIDTH:IN_AB_WIDTH]

    pos = t * tt + lax.broadcasted_iota(jnp.int32, (tt, 1), 0)
    for g, win in enumerate(POOL_WINDOWS):
        cs = slice(g * POOL_GC, (g + 1) * POOL_GC)
        u = ext_u[POOL_HALO:POOL_HALO + tt, cs]
        acc = u
        for j in range(1, win):
            acc = acc + ext_u[POOL_HALO - j:POOL_HALO - j + tt, cs]
        cnt = jnp.minimum(win, pos + 1).astype(F32)
        o_ref[0, :, cs] = _pool_group_out(acc, cnt, u, pool_w_ref, pool_scale_ref, g)

    p = ext_p[SHIFT_HALO:SHIFT_HALO + tt, :]
    shifted = ext_p[SHIFT_HALO - 1:SHIFT_HALO - 1 + tt, :]
    prm = (mu_ref, w0_ref, w2_ref, a0_ref, a2_ref, g2_ref, kk_ref, ka_ref, bdiag_ref)
    r, lw, kmod, v, kk, a, gate = _rwkv_prep(p, shifted, prm)

    ext_u[0:POOL_HALO, :] = ext_u[tt:tt + POOL_HALO, :]
    ext_p[0:SHIFT_HALO, :] = ext_p[tt:tt + SHIFT_HALO, :]

    cl = _dot_exact_lhs(tril_ref[...], lw)
    tot = _dot_exact_lhs(blk_ref[...], lw)
    gam = jnp.exp(cl)
    gam_inv = jnp.exp(-cl)
    gam_end = jnp.exp(tot)
    beta = kk * a
    kd_full = kmod * gam_inv
    bd_full = beta * gam_inv
    kq_c = _chains(kk * jnp.exp(cl - lw), nc)
    rq_c = _chains(r * gam, nc)
    kd_c = _chains(kd_full, nc)
    bd_c = _chains(bd_full, nc)
    v_c = _chains(v, nc)
    kdg_c = _chains(kd_full * gam_end, nc)
    bdg_c = _chains(bd_full * gam_end, nc)
    ge_c = _chains(gam_end, nc)
    n_ch = ng * nc
    every = range(n_ch)

    m4 = bdiag_ref[0:GROUP_LANES, 0:GROUP_LANES]
    m4f = m4.astype(F32)
    row = lax.broadcasted_iota(jnp.int32, (CHUNK, GROUP_LANES), 0)
    col = lax.broadcasted_iota(jnp.int32, (CHUNK, GROUP_LANES), 1) % RWKV_HEAD
    strict = row > col
    incl = row >= col
    eye_c = jnp.where(row == col, 1.0, 0.0)
    eye_g = (lax.broadcasted_iota(jnp.int32, (GROUP_LANES, GROUP_LANES), 0)
             == lax.broadcasted_iota(jnp.int32, (GROUP_LANES, GROUP_LANES), 1))

    bd_kd = [_block_diag(kd_c[i], m4) for i in every]
    bd_bd = [_block_diag(bd_c[i], m4) for i in every]
    lhs = [jnp.concatenate([kq_c[i], rq_c[i]], axis=0) for i in every]
    pk = [_dot_nt(lhs[i], bd_kd[i]) for i in every]
    pb = [_dot_nt(lhs[i], bd_bd[i]) for i in every]
    a_k = [jnp.where(strict, pk[i][:CHUNK], 0.0) for i in every]
    p_k = [jnp.where(incl, pk[i][CHUNK:], 0.0) for i in every]
    a_b = [jnp.where(strict, pb[i][:CHUNK], 0.0) for i in every]
    p_b = [jnp.where(incl, pb[i][CHUNK:], 0.0) for i in every]
    tm = [eye_c - a_b[i] for i in every]
    apow = a_b
    bd_ap = [_block_diag(apow[i], m4) for i in every]
    n = 1
    while 2 * n < CHUNK:
        apow = [_dot(apow[i], bd_ap[i]) for i in every]
        bd_ap = [_block_diag(apow[i], m4) for i in every]
        tm = [tm[i] + _dot(tm[i], bd_ap[i]) for i in every]
        n *= 2
    bd_v = [_block_diag(v_c[i], m4) for i in every]
    t_kq = [_dot(tm[i], _block_diag(kq_c[i], m4)) for i in every]
    akv = [_dot(a_k[i], bd_v[i]) for i in every]
    t_akv = [_dot(tm[i], _block_diag(akv[i], m4)) for i in every]
    r_y = [rq_c[i] - _dot(p_b[i], _block_diag(t_kq[i], m4)) for i in every]
    y0 = [_dot(p_k[i], bd_v[i]) - _dot(p_b[i], _block_diag(t_akv[i], m4)) for i in every]
    g_m = [m4f * (jnp.where(eye_g, ge_c[i][0:1, :], 0.0) - _dot_tn(t_kq[i], bdg_c[i])) for i in every]
    h_full = [m4f * _dot_tn(jnp.concatenate([v_c[i], t_akv[i]], axis=0),
                            jnp.concatenate([kdg_c[i], -bdg_c[i]], axis=0)) for i in every]
    h_t = [sum(h_full[i][j * RWKV_HEAD:(j + 1) * RWKV_HEAD] for j in range(GROUP_LANES // RWKV_HEAD))
           for i in every]

    ys = [None] * n_ch
    for g in range(ng):
        s = s_ref[g]
        for c in range(nc):
            i = g * nc + c
            ys[i] = _dot_nt(r_y[i], _block_diag(s, m4)) + y0[i]
            s = _dot(s, g_m[i]) + h_t[i]
        s_ref[g] = s
    y = jnp.concatenate([jnp.concatenate([ys[g * nc + c] for g in range(ng)], axis=1) for c in range(nc)],
                        axis=0)

    o_ref[0, :, POOL_WIDTH:] = _rwkv_finish(y, r, kmod, v, gate, rk_ref, lnw_ref, lnb_ref, bdiag_ref)

    @pl.when(t == pl.num_programs(1) - 1)
    def _():
        for hh in range(RWKV_HEADS):
            g, j = divmod(hh, GROUP_LANES // RWKV_HEAD)
            s_out_ref[0, hh] = s_ref[g][:, j * RWKV_HEAD:(j + 1) * RWKV_HEAD]


def _mix0_params(P):
    c = RWKV_WIDTH
    row = lambda x: x.reshape(1, -1).astype(F32)
    w2 = jnp.zeros((W_RANK + A_RANK, c), F32).at[:W_RANK].set(P['rw_w2']).astype(BF16)
    a2 = jnp.zeros((W_RANK + A_RANK, c), F32).at[W_RANK:].set(P['rw_a2']).astype(BF16)
    hid = jnp.arange(c) // RWKV_HEAD
    bdiag = (hid[:, None] == hid[None, :]).astype(BF16)
    return (P['pool_w'].astype(BF16), row(P['pool_scale']), row(P['mu_shift']), row(P['rw_w0']), w2,
            row(P['rw_a0']), a2, P['rw_g2'].astype(BF16), row(P['rw_kk']), row(P['rw_ka']),
            row(P['rw_rk']), row(P['rw_lnx_w']), row(P['rw_lnx_b']), bdiag)


def _full_spec(x):
    nd = x.ndim
    return pl.BlockSpec(x.shape, lambda *_: (0,) * nd)


def mix0_seq(h, P):
    b, t, _ = h.shape
    tt = _row_tile(t, 256)
    assert tt % CHUNK == 0 and tt >= POOL_HALO
    prm = _mix0_params(P)
    ti = jnp.arange(tt)
    same_chunk = (ti[:, None] // CHUNK) == (ti[None, :] // CHUNK)
    tril = (same_chunk & (ti[:, None] >= ti[None, :])).astype(BF16)
    blk = same_chunk.astype(BF16)
    return pl.pallas_call(
        _mix0_seq_body,
        grid=(b, t // tt),
        in_specs=[pl.BlockSpec((1, tt, IN_AB_WIDTH), lambda i, j: (i, j, 0))]
                 + [_full_spec(x) for x in prm] + [_full_spec(tril), _full_spec(blk)],
        out_specs=[pl.BlockSpec((1, tt, D_MODEL), lambda i, j: (i, j, 0)),
                   pl.BlockSpec((1, RWKV_HEADS, RWKV_HEAD, RWKV_HEAD), lambda i, j: (i, 0, 0, 0))],
        out_shape=[jax.ShapeDtypeStruct((b, t, D_MODEL), F32),
                   jax.ShapeDtypeStruct((b, RWKV_HEADS, RWKV_HEAD, RWKV_HEAD), F32)],
        scratch_shapes=[pltpu.VMEM((tt + POOL_HALO, POOL_WIDTH), F32),
                        pltpu.VMEM((tt + SHIFT_HALO, SHIFT_WIDTH), F32),
                        pltpu.VMEM((RWKV_WIDTH // GROUP_LANES, RWKV_HEAD, GROUP_LANES), F32)],
        compiler_params=pltpu.CompilerParams(dimension_semantics=("parallel", "arbitrary")),
        name="mix0_seq",
    )(h, *prm, tril, blk)


def _to_leading(x):
    n = x.shape[0]
    ii = lax.broadcasted_iota(jnp.int32, (n, n, 1), 0)
    jj = lax.broadcasted_iota(jnp.int32, (n, n, 1), 1)
    return jnp.sum(jnp.where(ii == jj, x[None, :, :], 0.0), axis=1, keepdims=True)


def _from_leading(x3):
    n = x3.shape[0]
    ii = lax.broadcasted_iota(jnp.int32, (n, n, 1), 0)
    jj = lax.broadcasted_iota(jnp.int32, (n, n, 1), 1)
    return jnp.sum(jnp.where(ii == jj, x3, 0.0), axis=0)


def _mix0_step_body(h_ref, pool_prev_ref, shift_prev_ref, s_in_ref,
                    pool_w_ref, pool_scale_ref, mu_ref, w0_ref, w2_ref, a0_ref, a2_ref, g2_ref,
                    kk_ref, ka_ref, rk_ref, lnw_ref, lnb_ref, bdiag_ref,
                    o_ref, s_out_ref,
                    r_t, w_t, k_t, v_t, kk_t, bt_t, y_t, r_s, k_s, v_s, g_s):
    hh = pl.program_id(0)
    nb = h_ref.shape[0]

    @pl.when(hh == 0)
    def _():
        for g, win in enumerate(POOL_WINDOWS):
            cs = slice(g * POOL_GC, (g + 1) * POOL_GC)
            u = h_ref[:, cs]
            acc = u
            for j in range(1, win):
                acc = acc + pool_prev_ref[POOL_BUF - j][:, cs]
            cnt = jnp.float32(min(win, PAST_LEN + 1))
            o_ref[:, cs] = _pool_group_out(acc, cnt, u, pool_w_ref, pool_scale_ref, g)
        prm = (mu_ref, w0_ref, w2_ref, a0_ref, a2_ref, g2_ref, kk_ref, ka_ref, bdiag_ref)
        r, lw, kmod, v, kk, a, gate = _rwkv_prep(h_ref[:, POOL_WIDTH:IN_AB_WIDTH], shift_prev_ref[...], prm)
        r_s[...] = r
        k_s[...] = kmod
        v_s[...] = v
        g_s[...] = gate
        r_t[...] = r.T
        w_t[...] = jnp.exp(lw).T
        k_t[...] = kmod.T
        v_t[...] = v.T
        kk_t[...] = kk.T
        bt_t[...] = (kk * a).T

    rows = pl.ds(pl.multiple_of(hh * RWKV_HEAD, RWKV_HEAD), RWKV_HEAD)
    s = s_in_ref[...].T.reshape(RWKV_HEAD, RWKV_HEAD, nb)
    kk = kk_t[rows, :][None]
    s_kk = jnp.sum(s * kk, axis=1, keepdims=True)
    v3 = _to_leading(v_t[rows, :])
    s = s * w_t[rows, :][None] - s_kk * bt_t[rows, :][None] + v3 * k_t[rows, :][None]
    y3 = jnp.sum(s * r_t[rows, :][None], axis=1, keepdims=True)
    y_t[rows, :] = _from_leading(y3)
    s_out_ref[...] = s.reshape(RWKV_HEAD * RWKV_HEAD, nb).T

    @pl.when(hh == pl.num_programs(0) - 1)
    def _():
        o_ref[:, POOL_WIDTH:] = _rwkv_finish(y_t[...].T, r_s[...], k_s[...], v_s[...], g_s[...],
                                             rk_ref, lnw_ref, lnb_ref, bdiag_ref)


def mix0_step(h, pool_prev, shift_prev, s_prev, P):
    b = h.shape[0]
    prm = _mix0_params(P)
    hw = RWKV_HEAD * RWKV_HEAD
    s2 = s_prev.reshape(b, RWKV_HEADS * hw)
    pool_t = jnp.swapaxes(pool_prev, 0, 1)
    tvec = lambda: pltpu.VMEM((RWKV_WIDTH, b), F32)
    svec = lambda: pltpu.VMEM((b, RWKV_WIDTH), F32)
    out, s_new = pl.pallas_call(
        _mix0_step_body,
        grid=(RWKV_HEADS,),
        in_specs=[_full_spec(h), _full_spec(pool_t), _full_spec(shift_prev),
                  pl.BlockSpec((b, hw), lambda i: (0, i))] + [_full_spec(x) for x in prm],
        out_specs=[pl.BlockSpec((b, D_MODEL), lambda i: (0, 0)),
                   pl.BlockSpec((b, hw), lambda i: (0, i))],
        out_shape=[jax.ShapeDtypeStruct((b, D_MODEL), F32),
                   jax.ShapeDtypeStruct((b, RWKV_HEADS * hw), F32)],
        scratch_shapes=[tvec(), tvec(), tvec(), tvec(), tvec(), tvec(), tvec(),
                        svec(), svec(), svec(), svec()],
        compiler_params=pltpu.CompilerParams(dimension_semantics=("arbitrary",)),
        name="mix0_step",
    )(h, pool_t, shift_prev, s2, *prm)
    return out, s_new.reshape(b, RWKV_HEADS, RWKV_HEAD, RWKV_HEAD)


def _conv_seq_body(h_ref, x_ref, cw_ref, wo_ref, o_ref, tail_ref, ext):
    t = pl.program_id(1)
    tt = o_ref.shape[1]
    c = D_MODEL

    @pl.when(t == 0)
    def _():
        ext[0:SHIFT_HALO, :] = jnp.zeros((SHIFT_HALO, c), F32)

    ext[SHIFT_HALO:SHIFT_HALO + tt, :] = h_ref[0, :, c:2 * c] * h_ref[0, :, 2 * c:3 * c]
    z = cw_ref[0:1, :] * ext[SHIFT_HALO - 2:SHIFT_HALO - 2 + tt, :]
    z = z + cw_ref[1:2, :] * ext[SHIFT_HALO - 1:SHIFT_HALO - 1 + tt, :]
    z = z + cw_ref[2:3, :] * ext[SHIFT_HALO:SHIFT_HALO + tt, :]
    gated = (h_ref[0, :, 0:c] * z).astype(BF16)
    o_ref[0] = x_ref[0] + jnp.dot(gated, wo_ref[...], preferred_element_type=F32)
    ext[0:SHIFT_HALO, :] = ext[tt:tt + SHIFT_HALO, :]

    @pl.when(t == pl.num_programs(1) - 1)
    def _():
        tail_ref[0] = ext[SHIFT_HALO - (CONV_WIDTH - 1):SHIFT_HALO, :]


def conv_seq(h, x, conv_w, w_out):
    b, t, _ = h.shape
    c = D_MODEL
    tt = _row_tile(t, 512)
    return pl.pallas_call(
        _conv_seq_body,
        grid=(b, t // tt),
        in_specs=[pl.BlockSpec((1, tt, 3 * c), lambda i, j: (i, j, 0)),
                  pl.BlockSpec((1, tt, c), lambda i, j: (i, j, 0)),
                  _full_spec(conv_w), _full_spec(w_out)],
        out_specs=[pl.BlockSpec((1, tt, c), lambda i, j: (i, j, 0)),
                   pl.BlockSpec((1, CONV_WIDTH - 1, c), lambda i, j: (i, 0, 0))],
        out_shape=[jax.ShapeDtypeStruct((b, t, c), F32),
                   jax.ShapeDtypeStruct((b, CONV_WIDTH - 1, c), F32)],
        scratch_shapes=[pltpu.VMEM((tt + SHIFT_HALO, c), F32)],
        compiler_params=pltpu.CompilerParams(dimension_semantics=("parallel", "arbitrary")),
        name="conv_seq",
    )(h, x, conv_w, w_out)


def _conv_step_body(h_ref, x_ref, p0_ref, p1_ref, cw_ref, wo_ref, o_ref, e_ref):
    c = D_MODEL
    e = h_ref[:, c:2 * c] * h_ref[:, 2 * c:3 * c]
    z = cw_ref[0:1, :] * p0_ref[...] + cw_ref[1:2, :] * p1_ref[...] + cw_ref[2:3, :] * e
    gated = (h_ref[:, 0:c] * z).astype(BF16)
    o_ref[...] = x_ref[...] + jnp.dot(gated, wo_ref[...], preferred_element_type=F32)
    e_ref[...] = e


def conv_step(h, x, prev, conv_w, w_out):
    b = h.shape[0]
    c = D_MODEL
    args = (h, x, prev[:, 0, :], prev[:, 1, :], conv_w, w_out)
    return pl.pallas_call(
        _conv_step_body,
        grid=(1,),
        in_specs=[_full_spec(a) for a in args],
        out_specs=[pl.BlockSpec((b, c), lambda i: (0, 0)), pl.BlockSpec((b, c), lambda i: (0, 0))],
        out_shape=[jax.ShapeDtypeStruct((b, c), F32), jax.ShapeDtypeStruct((b, c), F32)],
        compiler_params=pltpu.CompilerParams(dimension_semantics=("arbitrary",)),
        name="conv_step",
    )(*args)


def _xattn(x2, i, mem_k, mem_v, W, seq_shape):
    if seq_shape is not None:
        b, t = seq_shape
        return xattn_seq(x2.reshape(b, t, D_MODEL), W['norm_xattn'][i], W['w_xq'][i], W['w_xo'][i],
                         mem_k, mem_v).reshape(b * t, D_MODEL)
    return xattn_step(x2, W['norm_xattn'][i], W['w_xq'][i], W['w_xo'][i], mem_k, mem_v, i)


def _ffn0(x2, W):
    return ffn_dense(x2, W['norm_ffn'][0], W['ffn_gate'][0], W['ffn_up'][0], W['ffn_down'][0], FFN_TILE)


def _trunk_seq(x, mem_k, mem_v, W):
    b, t, d = x.shape
    x2 = x.reshape(b * t, d)
    P0 = {k: v[0] for k, v in W['mix0'].items()}
    h = norm_matmul(x2, W['norm_mix'][0], W['w_in_ab'][0]).reshape(b, t, IN_AB_WIDTH)
    mix, wkv = mix0_seq(h, P0)
    pool = h[:, t - POOL_BUF:, :POOL_WIDTH]
    shift = h[:, t - 1, POOL_WIDTH:]
    x2 = matmul_res(mix.reshape(b * t, d), W['w_out_ab'][0], x2)
    x2 = _ffn0(_xattn(x2, 0, mem_k[0], mem_v[0], W, (b, t)), W)
    h = norm_matmul(x2, W['norm_mix'][1], W['w_in_c'][0]).reshape(b, t, 3 * d)
    x3, conv = conv_seq(h, x2.reshape(b, t, d), W['conv_w'][0], W['w_out_c'][0])
    x2 = _xattn(x3.reshape(b * t, d), 1, mem_k[1], mem_v[1], W, (b, t))
    return x2, pool[None], shift[None], wkv[None], conv[None]


def _trunk_step(x, mem_k, mem_v, pool_prev, shift_prev, wkv_prev, conv_prev, W):
    b, _, d = x.shape
    x2 = x.reshape(b, d)
    P0 = {k: v[0] for k, v in W['mix0'].items()}
    h = norm_matmul(x2, W['norm_mix'][0], W['w_in_ab'][0])
    mix, wkv = mix0_step(h, pool_prev[0], shift_prev[0], wkv_prev[0], P0)
    pool = jnp.concatenate([pool_prev[0][:, 1:], h[:, None, :POOL_WIDTH]], axis=1)
    shift = h[:, POOL_WIDTH:]
    x2 = matmul_res(mix, W['w_out_ab'][0], x2)
    x2 = _ffn0(_xattn(x2, 0, mem_k, mem_v, W, None), W)
    h = norm_matmul(x2, W['norm_mix'][1], W['w_in_c'][0])
    x2, e = conv_step(h, x2, conv_prev[0], W['conv_w'][0], W['w_out_c'][0])
    conv = jnp.concatenate([conv_prev[0][:, 1:], e[:, None]], axis=1)
    x2 = _xattn(x2, 1, mem_k, mem_v, W, None)
    return x2, pool[None], shift[None], wkv[None], conv[None]


def kernel(x_prompt, x_sample, mem_prompt, cache_mem_k, cache_mem_v, state_pool, state_shift, state_wkv, state_conv, norm_mix, norm_xattn, norm_mem, norm_ffn, norm_final, w_xq, w_xk, w_xv, w_xo, w_in_ab, pool_w, pool_scale, mu_shift, rw_w0, rw_w2, rw_a0, rw_a2, rw_g2, rw_kk, rw_ka, rw_rk, rw_lnx_w, rw_lnx_b, w_out_ab, ffn_gate, ffn_up, ffn_down, w_in_c, conv_w, w_out_c, router_w, router_b, moe_gate, moe_up, moe_down):
    depth = norm_mix.shape[0]
    assert depth == 2 and w_in_ab.shape[0] == 1 and w_in_c.shape[0] == 1
    bp = x_prompt.shape[0]
    bs = x_sample.shape[0]
    d = D_MODEL
    bf = lambda w: w.astype(BF16)
    W = dict(norm_mix=norm_mix, norm_xattn=norm_xattn, norm_ffn=norm_ffn, norm_final=norm_final,
             w_xq=bf(w_xq), w_xo=bf(w_xo), w_in_ab=bf(w_in_ab), w_out_ab=bf(w_out_ab),
             ffn_gate=bf(ffn_gate), ffn_up=bf(ffn_up), ffn_down=bf(ffn_down),
             w_in_c=bf(w_in_c), conv_w=conv_w, w_out_c=bf(w_out_c),
             router_w=router_w, router_b=router_b,
             moe_gate=bf(moe_gate), moe_up=bf(moe_up), moe_down=bf(moe_down),
             mix0=dict(pool_w=pool_w, pool_scale=pool_scale, mu_shift=mu_shift, rw_w0=rw_w0, rw_w2=rw_w2,
                       rw_a0=rw_a0, rw_a2=rw_a2, rw_g2=rw_g2, rw_kk=rw_kk, rw_ka=rw_ka, rw_rk=rw_rk,
                       rw_lnx_w=rw_lnx_w, rw_lnx_b=rw_lnx_b))

    mem2 = mem_prompt.reshape(bp * N_MEM, d)
    mk, mv = [], []
    for i in range(depth):
        wkv_i = jnp.concatenate([bf(w_xk[i]), bf(w_xv[i])], axis=1)
        kv = norm_matmul(mem2, norm_mem[i], wkv_i)
        mk.append(kv[:, :d].reshape(bp, N_MEM, d))
        mv.append(kv[:, d:].reshape(bp, N_MEM, d))

    x_p, pool_p, shift_p, wkv_p, conv_p = _trunk_seq(x_prompt, mk, mv, W)
    x_s, pool_s, shift_s, wkv_s, conv_s = _trunk_step(x_sample, cache_mem_k, cache_mem_v, state_pool,
                                                       state_shift, state_wkv, state_conv, W)
    y_p, y_s = moe_top2_norm([x_p, x_s], norm_ffn[1], router_w[0], router_b[0], W['moe_gate'][0],
                             W['moe_up'][0], W['moe_down'][0], norm_final)
    y_p = y_p.reshape(x_prompt.shape)
    y_s = y_s.reshape(x_sample.shape)
    mem_k_p = jnp.stack(mk).reshape(depth, bp, N_MEM, X_HEADS, X_HEAD_DIM)
    mem_v_p = jnp.stack(mv).reshape(depth, bp, N_MEM, X_HEADS, X_HEAD_DIM)
    return (y_p, y_s, pool_p, pool_s, shift_p, shift_s, wkv_p, wkv_s, conv_p, conv_s, mem_k_p, mem_v_p)
```

```python
import functools

import jax
import jax.numpy as jnp
from jax import lax
from jax.experimental import pallas as pl
from jax.experimental.pallas import tpu as pltpu

F32 = jnp.float32
BF16 = jnp.bfloat16

D_MODEL = 1024
POOL_WIDTH = 512
POOL_GROUPS = 4
POOL_GC = 128
POOL_WINDOWS = (2, 4, 8, 16)
POOL_BUF = 15
RWKV_WIDTH = 512
RWKV_HEAD = 64
RWKV_HEADS = 8
W_RANK = 64
A_RANK = 64
G_RANK = 128
SHIFT_WIDTH = 3 * RWKV_WIDTH + W_RANK + A_RANK + G_RANK
IN_AB_WIDTH = POOL_WIDTH + SHIFT_WIDTH
LNX_EPS = 64e-5
CONV_WIDTH = 3
N_EXPERTS = 8
N_MEM = 256
X_HEADS = 4
X_HEAD_DIM = 256
RMS_EPS = 1e-6
PAST_LEN = 16384

CHUNK = 64
GROUP_LANES = 256
POOL_HALO = 16
SHIFT_HALO = 8
ROUTER_PAD = 128
TOP_K = 2
ROW_SUBLANES = 8
MOE_ROWS = 512
FFN_TILE = 1408
MOE_TILE = 1792
XATTN_STEP_VMEM = 48 * 1024 * 1024


def _rmsnorm(x, g):
    ms = jnp.mean(x * x, axis=-1, keepdims=True)
    return x * lax.rsqrt(ms + RMS_EPS) * g


def _dot(a, b):
    return jnp.dot(a.astype(BF16), b.astype(BF16), preferred_element_type=F32)


def _dot_nt(a, b):
    return lax.dot_general(a.astype(BF16), b.astype(BF16), (((1,), (1,)), ((), ())),
                           preferred_element_type=F32)


def _dot_tn(a, b):
    return lax.dot_general(a.astype(BF16), b.astype(BF16), (((0,), (0,)), ((), ())),
                           preferred_element_type=F32)


def _split3(x):
    hi = x.astype(BF16)
    r1 = x - hi.astype(F32)
    mid = r1.astype(BF16)
    lo = (r1 - mid.astype(F32)).astype(BF16)
    return hi, mid, lo


def _dot_exact_rhs(x, m01):
    hi, mid, lo = _split3(x)
    f = lambda p: jnp.dot(p, m01, preferred_element_type=F32)
    return f(hi) + f(mid) + f(lo)


def _dot_exact_lhs(m01, x):
    hi, mid, lo = _split3(x)
    f = lambda p: jnp.dot(m01, p, preferred_element_type=F32)
    return f(hi) + f(mid) + f(lo)


def _softplus(x):
    return jnp.maximum(x, 0.0) + jnp.log1p(jnp.exp(-jnp.abs(x)))


def _sigmoid(x):
    return 1.0 / (1.0 + jnp.exp(-x))


def _row_tile(m, want):
    t = min(m, want)
    assert m % t == 0, (m, t)
    return t


def _norm_matmul_body(x_ref, g_ref, w_ref, o_ref):
    xn = _rmsnorm(x_ref[...], g_ref[...]).astype(BF16)
    o_ref[...] = jnp.dot(xn, w_ref[...], preferred_element_type=F32)


def norm_matmul(x, g, w):
    m, k = x.shape
    n = w.shape[1]
    tm = _row_tile(m, 512)
    return pl.pallas_call(
        _norm_matmul_body,
        grid=(m // tm,),
        in_specs=[pl.BlockSpec((tm, k), lambda i: (i, 0)),
                  pl.BlockSpec((1, k), lambda i: (0, 0)),
                  pl.BlockSpec((k, n), lambda i: (0, 0))],
        out_specs=pl.BlockSpec((tm, n), lambda i: (i, 0)),
        out_shape=jax.ShapeDtypeStruct((m, n), F32),
        compiler_params=pltpu.CompilerParams(dimension_semantics=("parallel",)),
        name="norm_matmul",
    )(x, g.reshape(1, k), w)


def _matmul_res_body(a_ref, w_ref, r_ref, o_ref):
    o_ref[...] = r_ref[...] + jnp.dot(a_ref[...].astype(BF16), w_ref[...], preferred_element_type=F32)


def matmul_res(a, w, res):
    m, k = a.shape
    n = w.shape[1]
    tm = _row_tile(m, 512)
    return pl.pallas_call(
        _matmul_res_body,
        grid=(m // tm,),
        in_specs=[pl.BlockSpec((tm, k), lambda i: (i, 0)),
                  pl.BlockSpec((k, n), lambda i: (0, 0)),
                  pl.BlockSpec((tm, n), lambda i: (i, 0))],
        out_specs=pl.BlockSpec((tm, n), lambda i: (i, 0)),
        out_shape=jax.ShapeDtypeStruct((m, n), F32),
        compiler_params=pltpu.CompilerParams(dimension_semantics=("parallel",)),
        name="matmul_res",
    )(a, w, res)


def _ffn_body(x_ref, g_ref, wg_ref, wu_ref, wd_ref, o_ref, xn_ref, acc_ref):
    f = pl.program_id(1)

    @pl.when(f == 0)
    def _():
        xn_ref[...] = _rmsnorm(x_ref[...], g_ref[...]).astype(BF16)
        acc_ref[...] = jnp.zeros_like(acc_ref)

    xn = xn_ref[...]
    gate = jnp.dot(xn, wg_ref[...], preferred_element_type=F32)
    up = jnp.dot(xn, wu_ref[...], preferred_element_type=F32)
    hid = (gate * _sigmoid(gate) * up).astype(BF16)
    acc_ref[...] += jnp.dot(hid, wd_ref[...], preferred_element_type=F32)

    @pl.when(f == pl.num_programs(1) - 1)
    def _():
        o_ref[...] = x_ref[...] + acc_ref[...]


def ffn_dense(x, g, wg, wu, wd, tf):
    m, d = x.shape
    ff = wg.shape[1]
    tm = _row_tile(m, 512)
    return pl.pallas_call(
        _ffn_body,
        grid=(m // tm, ff // tf),
        in_specs=[pl.BlockSpec((tm, d), lambda i, f: (i, 0)),
                  pl.BlockSpec((1, d), lambda i, f: (0, 0)),
                  pl.BlockSpec((d, tf), lambda i, f: (0, f)),
                  pl.BlockSpec((d, tf), lambda i, f: (0, f)),
                  pl.BlockSpec((tf, d), lambda i, f: (f, 0))],
        out_specs=pl.BlockSpec((tm, d), lambda i, f: (i, 0)),
        out_shape=jax.ShapeDtypeStruct((m, d), F32),
        scratch_shapes=[pltpu.VMEM((tm, d), BF16), pltpu.VMEM((tm, d), F32)],
        compiler_params=pltpu.CompilerParams(dimension_semantics=("parallel", "arbitrary")),
        name="ffn_dense",
    )(x, g.reshape(1, d), wg, wu, wd)


def _router_body(x_ref, g_ref, w_ref, b_ref, oi_ref, ow_ref):
    xn = _rmsnorm(x_ref[...], g_ref[...])
    logits = _dot_exact_both(xn, w_ref[...]) + b_ref[...]
    lane = lax.broadcasted_iota(jnp.int32, logits.shape, 1)
    neg = jnp.float32(-jnp.inf)
    logits = jnp.where(lane < N_EXPERTS, logits, neg)
    m1 = jnp.max(logits, axis=-1, keepdims=True)
    i1 = jnp.min(jnp.where(logits == m1, lane, ROUTER_PAD), axis=-1, keepdims=True)
    rest = jnp.where(lane == i1, neg, logits)
    m2 = jnp.max(rest, axis=-1, keepdims=True)
    i2 = jnp.min(jnp.where(rest == m2, lane, ROUTER_PAD), axis=-1, keepdims=True)
    e2 = jnp.exp(m2 - m1)
    den = 1.0 + e2
    slot = lax.broadcasted_iota(jnp.int32, oi_ref.shape, 1)
    oi_ref[...] = jnp.where(slot == 0, i1, i2)
    ow_ref[...] = jnp.where(slot == 0, 1.0 / den, e2 / den)


def _dot_exact_both(x, w):
    xh, xm, xl = _split3(x)
    wh, wm, wl = _split3(w)
    f = lambda p, q: jnp.dot(p, q, preferred_element_type=F32)
    return (f(xh, wh) + (f(xh, wm) + f(xm, wh))) + ((f(xm, wm) + f(xh, wl)) + f(xl, wh))


def moe_router(x, g, router_w, router_b):
    m, d = x.shape
    tm = _row_tile(m, 512)
    w = jnp.zeros((d, ROUTER_PAD), F32).at[:, :N_EXPERTS].set(router_w)
    b = jnp.zeros((1, ROUTER_PAD), F32).at[0, :N_EXPERTS].set(router_b)
    return pl.pallas_call(
        _router_body,
        grid=(m // tm,),
        in_specs=[pl.BlockSpec((tm, d), lambda i: (i, 0)),
                  pl.BlockSpec((1, d), lambda i: (0, 0)),
                  pl.BlockSpec((d, ROUTER_PAD), lambda i: (0, 0)),
                  pl.BlockSpec((1, ROUTER_PAD), lambda i: (0, 0))],
        out_specs=[pl.BlockSpec((tm, TOP_K), lambda i: (i, 0)), pl.BlockSpec((tm, TOP_K), lambda i: (i, 0))],
        out_shape=[jax.ShapeDtypeStruct((m, TOP_K), jnp.int32), jax.ShapeDtypeStruct((m, TOP_K), F32)],
        compiler_params=pltpu.CompilerParams(dimension_semantics=("parallel",)),
        name="moe_router",
    )(x, g.reshape(1, d), w, b)


def _row_tiles_to_2d(ref, slot, n):
    return jnp.concatenate([ref[slot, pl.ds(j, n, stride=ROW_SUBLANES), :] for j in range(ROW_SUBLANES)],
                           axis=1)


def _row_copy(src_hbm, row, dst, slot, i, sem):
    return pltpu.make_async_copy(src_hbm.at[pl.ds(row * ROW_SUBLANES, ROW_SUBLANES)],
                                 dst.at[slot, pl.ds(i * ROW_SUBLANES, ROW_SUBLANES)], sem.at[slot])


def _gather_start(src_hbm, rows_smem, dst, slot, n, sem):
    def body(i, carry):
        _row_copy(src_hbm, rows_smem[0, 0, i], dst, slot, i, sem).start()
        return carry
    lax.fori_loop(0, n, body, 0, unroll=8)


def _gather_wait(src_hbm, dst, slot, n, sem):
    pltpu.make_async_copy(src_hbm.at[pl.ds(0, n * ROW_SUBLANES)], dst.at[slot], sem.at[slot]).wait()


def _moe_group_body(te_ref, nu_ref, tok0_ref, tokn_ref, x8_ref, g_ref, wg_ref, wu_ref, wd_ref,
                    o_ref, xbuf, sem, xn_ref, acc_ref, *, nf):
    t = pl.program_id(0)
    f = pl.program_id(1)
    tm = xn_ref.shape[0]
    per_step = tm // nf
    used = t < nu_ref[0]
    slot = t % 2

    @pl.when(jnp.logical_and(t == 0, f == 0))
    def _():
        _gather_start(x8_ref, tok0_ref, xbuf, 0, tm, sem)

    @pl.when(jnp.logical_and(used, f == 0))
    def _():
        _gather_wait(x8_ref, xbuf, slot, tm, sem)
        x = _row_tiles_to_2d(xbuf, slot, tm)
        xn_ref[...] = _rmsnorm(x, g_ref[...]).astype(BF16)
        acc_ref[...] = jnp.zeros_like(acc_ref)

    @pl.when(used)
    def _():
        base = f * per_step
        for i in range(per_step):
            _row_copy(x8_ref, tokn_ref[0, 0, base + i], xbuf, 1 - slot, base + i, sem).start()
        xn = xn_ref[...]
        gate = jnp.dot(xn, wg_ref[0], preferred_element_type=F32)
        up = jnp.dot(xn, wu_ref[0], preferred_element_type=F32)
        hid = (gate * _sigmoid(gate) * up).astype(BF16)
        acc_ref[...] += jnp.dot(hid, wd_ref[0], preferred_element_type=F32)

    @pl.when(f == nf - 1)
    def _():
        y = jnp.where(used, acc_ref[...], 0.0)
        for j in range(ROW_SUBLANES):
            o_ref[pl.ds(j, tm, stride=ROW_SUBLANES), :] = y[:, j * 128:(j + 1) * 128]

    @pl.when(jnp.logical_and(t == nu_ref[0] - 1, f == nf - 1))
    def _():
        _gather_wait(x8_ref, xbuf, 1 - slot, tm, sem)


def moe_grouped(x8, g, tile_expert, n_used, row_token, wg, wu, wd, tf):
    n_tiles, _, tm = row_token.shape
    d = D_MODEL
    ne, _, ff = wg.shape
    nf = ff // tf
    assert tm % nf == 0
    live_f = lambda t, f, nu: jnp.where(t < nu[0], f, nf - 1)
    smem_rows = lambda imap: pl.BlockSpec((1, 1, tm), imap, memory_space=pltpu.SMEM)
    grid_spec = pltpu.PrefetchScalarGridSpec(
        num_scalar_prefetch=2,
        grid=(n_tiles, nf),
        in_specs=[smem_rows(lambda t, f, te, nu: (0, 0, 0)),
                  smem_rows(lambda t, f, te, nu: (jnp.minimum(t + 1, nu[0] - 1), 0, 0)),
                  pl.BlockSpec(memory_space=pl.ANY),
                  pl.BlockSpec((1, d), lambda t, f, te, nu: (0, 0)),
                  pl.BlockSpec((1, d, tf), lambda t, f, te, nu: (te[t], 0, live_f(t, f, nu))),
                  pl.BlockSpec((1, d, tf), lambda t, f, te, nu: (te[t], 0, live_f(t, f, nu))),
                  pl.BlockSpec((1, tf, d), lambda t, f, te, nu: (te[t], live_f(t, f, nu), 0))],
        out_specs=pl.BlockSpec((tm * ROW_SUBLANES, 128), lambda t, f, te, nu: (t, 0)),
        scratch_shapes=[pltpu.VMEM((2, tm * ROW_SUBLANES, 128), F32), pltpu.SemaphoreType.DMA((2,)),
                        pltpu.VMEM((tm, d), BF16), pltpu.VMEM((tm, d), F32)])
    return pl.pallas_call(
        functools.partial(_moe_group_body, nf=nf),
        grid_spec=grid_spec,
        out_shape=jax.ShapeDtypeStruct((n_tiles * tm * ROW_SUBLANES, 128), F32),
        compiler_params=pltpu.CompilerParams(dimension_semantics=("arbitrary", "arbitrary")),
        name="moe_grouped",
    )(tile_expert, n_used, row_token, row_token, x8, g.reshape(1, d), wg, wu, wd)


def _moe_combine_body(pos0_ref, posn_ref, y8_ref, x_ref, w_ref, g_ref, o_ref, ybuf, sem):
    t = pl.program_id(0)
    tc = x_ref.shape[0]
    n = TOP_K * tc

    @pl.when(t == 0)
    def _():
        _gather_start(y8_ref, pos0_ref, ybuf, 0, n, sem)

    @pl.when(t + 1 < pl.num_programs(0))
    def _():
        _gather_start(y8_ref, posn_ref, ybuf, (t + 1) % 2, n, sem)

    slot = t % 2
    _gather_wait(y8_ref, ybuf, slot, n, sem)
    y = _row_tiles_to_2d(ybuf, slot, n)
    w = w_ref[...]
    moe = w[:, 0:1] * y[0:tc] + w[:, 1:2] * y[tc:n]
    o_ref[...] = _rmsnorm(x_ref[...] + moe, g_ref[...])


def moe_combine_norm(y8, x, pos, topw, g):
    m, d = x.shape
    nt, _, n = pos.shape
    tc = n // TOP_K
    assert nt * tc == m
    smem_rows = lambda imap: pl.BlockSpec((1, 1, n), imap, memory_space=pltpu.SMEM)
    return pl.pallas_call(
        _moe_combine_body,
        grid=(nt,),
        in_specs=[smem_rows(lambda t: (0, 0, 0)),
                  smem_rows(lambda t: (jnp.minimum(t + 1, nt - 1), 0, 0)),
                  pl.BlockSpec(memory_space=pl.ANY),
                  pl.BlockSpec((tc, d), lambda t: (t, 0)),
                  pl.BlockSpec((tc, TOP_K), lambda t: (t, 0)),
                  pl.BlockSpec((1, d), lambda t: (0, 0))],
        out_specs=pl.BlockSpec((tc, d), lambda t: (t, 0)),
        out_shape=jax.ShapeDtypeStruct((m, d), F32),
        scratch_shapes=[pltpu.VMEM((2, n * ROW_SUBLANES, 128), F32), pltpu.SemaphoreType.DMA((2,))],
        compiler_params=pltpu.CompilerParams(dimension_semantics=("arbitrary",)),
        name="moe_combine_norm",
    )(pos, pos, y8, x, topw, g.reshape(1, d))


def _moe_plan(topi, tm):
    n = topi.shape[0]
    a = TOP_K * n
    n_tiles = (a + N_EXPERTS * (tm - 1) + tm - 1) // tm
    e = topi.reshape(a)
    eid = jnp.arange(N_EXPERTS, dtype=jnp.int32)
    onehot = (e[:, None] == eid[None, :]).astype(jnp.int32)
    csum = jnp.cumsum(onehot, axis=0)
    cnt = csum[-1]
    rank = jnp.sum((csum - onehot) * onehot, axis=1)
    tiles = (cnt + tm - 1) // tm
    tile_end = jnp.cumsum(tiles)
    tile_start = tile_end - tiles
    seg_start = jnp.cumsum(cnt) - cnt
    n_used = tile_end[-1]
    pos = (jnp.sum(onehot * tile_start[None, :], axis=1) * tm + rank).reshape(n, TOP_K)
    tid = jnp.minimum(jnp.arange(n_tiles, dtype=jnp.int32), n_used - 1)
    tile_expert = jnp.sum(tid[:, None] >= tile_end[None, :], axis=1).astype(jnp.int32)
    te_onehot = (tile_expert[:, None] == eid[None, :]).astype(jnp.int32)
    first = jnp.sum(te_onehot * (seg_start - tile_start * tm)[None, :], axis=1) + tid * tm
    order_tok = (jnp.argsort(e, stable=True) // TOP_K).astype(jnp.int32)
    order_ext = jnp.concatenate([order_tok, jnp.zeros((tm,), jnp.int32)])
    row_token = jax.vmap(lambda s: lax.dynamic_slice(order_ext, (s,), (tm,)))(first)
    return tile_expert, n_used.reshape(1).astype(jnp.int32), row_token.reshape(n_tiles, 1, tm), pos


def moe_top2_norm(xs, g, router_w, router_b, wg, wu, wd, g_final):
    tops = [moe_router(x, g, router_w, router_b) for x in xs]
    topi = jnp.concatenate([t[0] for t in tops], axis=0)
    tile_expert, n_used, row_token, pos = _moe_plan(topi, MOE_ROWS)
    x8 = jnp.concatenate(xs, axis=0).reshape(-1, 128)
    y8 = moe_grouped(x8, g, tile_expert, n_used, row_token, wg, wu, wd, MOE_TILE)
    outs, start = [], 0
    for x, (_, topw) in zip(xs, tops):
        m = x.shape[0]
        tc = _row_tile(m, 256)
        p = pos[start:start + m].reshape(m // tc, tc, TOP_K)
        p = jnp.swapaxes(p, 1, 2).reshape(m // tc, 1, TOP_K * tc)
        outs.append(moe_combine_norm(y8, x, p, topw, g_final))
        start += m
    return outs


def _xattn_seq_body(x_ref, g_ref, wq_ref, wo_ref, k_ref, v_ref, o_ref):
    x = x_ref[0]
    xn = _rmsnorm(x, g_ref[...]).astype(BF16)
    q = jnp.dot(xn, wq_ref[...], preferred_element_type=F32).astype(BF16)
    kb = k_ref[0].astype(BF16)
    vb = v_ref[0].astype(BF16)
    heads = []
    for h in range(X_HEADS):
        sl = slice(h * X_HEAD_DIM, (h + 1) * X_HEAD_DIM)
        s = _dot_nt(q[:, sl], kb[:, sl]) * (X_HEAD_DIM ** -0.5)
        mx = jnp.max(s, axis=-1, keepdims=True)
        ex = jnp.exp(s - mx)
        pr = ex / jnp.sum(ex, axis=-1, keepdims=True)
        heads.append(jnp.dot(pr.astype(BF16), vb[:, sl], preferred_element_type=F32))
    att = jnp.concatenate(heads, axis=-1).astype(BF16)
    o_ref[0] = x + jnp.dot(att, wo_ref[...], preferred_element_type=F32)


def xattn_seq(x, g, wq, wo, mem_k, mem_v):
    b, t, d = x.shape
    tq = _row_tile(t, 512)
    return pl.pallas_call(
        _xattn_seq_body,
        grid=(b, t // tq),
        in_specs=[pl.BlockSpec((1, tq, d), lambda i, j: (i, j, 0)),
                  pl.BlockSpec((1, d), lambda i, j: (0, 0)),
                  pl.BlockSpec((d, d), lambda i, j: (0, 0)),
                  pl.BlockSpec((d, d), lambda i, j: (0, 0)),
                  pl.BlockSpec((1, N_MEM, d), lambda i, j: (i, 0, 0)),
                  pl.BlockSpec((1, N_MEM, d), lambda i, j: (i, 0, 0))],
        out_specs=pl.BlockSpec((1, tq, d), lambda i, j: (i, j, 0)),
        out_shape=jax.ShapeDtypeStruct((b, t, d), F32),
        compiler_params=pltpu.CompilerParams(dimension_semantics=("parallel", "parallel")),
        name="xattn_seq",
    )(x, g.reshape(1, d), wq, wo, mem_k, mem_v)


XATTN_STEP_ROWS = 8
XATTN_STEP_KV = 4


def _xattn_step_body(x_ref, g_ref, wq_ref, wo_ref, k_ref, v_ref, o_ref, q_s, att_s):
    j = pl.program_id(1)

    @pl.when(j == 0)
    def _():
        xn = _rmsnorm(x_ref[...], g_ref[...]).astype(BF16)
        q_s[...] = jnp.dot(xn, wq_ref[...], preferred_element_type=F32) * (X_HEAD_DIM ** -0.5)

    for b in range(XATTN_STEP_KV):
        rsel = pl.ds(j * XATTN_STEP_KV + b, 1)
        q4 = jnp.concatenate([q_s[rsel, h * X_HEAD_DIM:(h + 1) * X_HEAD_DIM] for h in range(X_HEADS)], axis=0)
        s = jnp.sum(k_ref[0, b] * q4[None], axis=-1, keepdims=True)
        mx = jnp.max(s, axis=0, keepdims=True)
        ex = jnp.exp(s - mx)
        pr = ex / jnp.sum(ex, axis=0, keepdims=True)
        o4 = jnp.sum(pr * v_ref[0, b], axis=0)
        for h in range(X_HEADS):
            att_s[rsel, h * X_HEAD_DIM:(h + 1) * X_HEAD_DIM] = o4[h:h + 1, :]

    @pl.when(j == pl.num_programs(1) - 1)
    def _():
        o_ref[...] = x_ref[...] + jnp.dot(att_s[...].astype(BF16), wo_ref[...], preferred_element_type=F32)


def xattn_step(x, g, wq, wo, mem_k, mem_v, layer):
    b, d = x.shape
    rows, kvb = XATTN_STEP_ROWS, XATTN_STEP_KV
    assert b % rows == 0 and rows % kvb == 0
    nj = rows // kvb
    kv_spec = pl.BlockSpec((1, kvb, N_MEM, X_HEADS, X_HEAD_DIM), lambda i, j: (layer, i * nj + j, 0, 0, 0))
    return pl.pallas_call(
        _xattn_step_body,
        grid=(b // rows, nj),
        in_specs=[pl.BlockSpec((rows, d), lambda i, j: (i, 0)),
                  pl.BlockSpec((1, d), lambda i, j: (0, 0)),
                  pl.BlockSpec((d, d), lambda i, j: (0, 0)),
                  pl.BlockSpec((d, d), lambda i, j: (0, 0)),
                  kv_spec, kv_spec],
        out_specs=pl.BlockSpec((rows, d), lambda i, j: (i, 0)),
        out_shape=jax.ShapeDtypeStruct((b, d), F32),
        scratch_shapes=[pltpu.VMEM((rows, d), F32), pltpu.VMEM((rows, d), F32)],
        compiler_params=pltpu.CompilerParams(dimension_semantics=("parallel", "arbitrary"),
                                             vmem_limit_bytes=XATTN_STEP_VMEM),
        name="xattn_step",
    )(x, g.reshape(1, d), wq, wo, mem_k, mem_v)


def _head_sum(x, bdiag_ref):
    return _dot_exact_rhs(x, bdiag_ref[...])


def _pool_group_out(acc, cnt, u, pool_w_ref, pool_scale_ref, g):
    cs = slice(g * POOL_GC, (g + 1) * POOL_GC)
    d = acc / cnt - u
    return _dot(d, pool_w_ref[g]) * pool_scale_ref[:, cs]


def _rwkv_prep(p, shifted, prm):
    (mu_ref, w0_ref, w2_ref, a0_ref, a2_ref, g2_ref, kk_ref, ka_ref, bdiag_ref) = prm
    m = p + (shifted - p) * mu_ref[...]
    c = RWKV_WIDTH
    r = m[:, 0:c]
    k = m[:, c:2 * c]
    v = m[:, 2 * c:3 * c]
    dwa = m[:, 3 * c:3 * c + W_RANK + A_RANK]
    dg = m[:, 3 * c + W_RANK + A_RANK:]
    w_log = -_softplus(-(w0_ref[...] + _dot(jnp.tanh(dwa), w2_ref[...]))) - 0.5
    lw = -jnp.exp(w_log)
    a = _sigmoid(a0_ref[...] + _dot(dwa, a2_ref[...]))
    gate = _dot(_sigmoid(dg), g2_ref[...])
    kk = k * kk_ref[...]
    kk = kk / jnp.maximum(jnp.sqrt(_head_sum(kk * kk, bdiag_ref)), 1e-12)
    kmod = k * (1.0 + (a - 1.0) * ka_ref[...])
    return r, lw, kmod, v, kk, a, gate


def _rwkv_finish(y, r, kmod, v, gate, rk_ref, lnw_ref, lnb_ref, bdiag_ref):
    mu = _head_sum(y, bdiag_ref) * (1.0 / RWKV_HEAD)
    dlt = y - mu
    var = _head_sum(dlt * dlt, bdiag_ref) * (1.0 / RWKV_HEAD)
    yn = dlt * lax.rsqrt(var + LNX_EPS) * lnw_ref[...] + lnb_ref[...]
    bonus = _head_sum(r * kmod * rk_ref[...], bdiag_ref) * v
    return (yn + bonus) * gate


def _chains(x, nc):
    return [x[c * CHUNK:(c + 1) * CHUNK, g * GROUP_LANES:(g + 1) * GROUP_LANES]
            for g in range(RWKV_WIDTH // GROUP_LANES) for c in range(nc)]


def _block_diag(x, m4):
    xb = x.astype(BF16)
    return jnp.concatenate([xb] * (GROUP_LANES // RWKV_HEAD), axis=0) * m4


def _mix0_seq_body(x_ref, gmix_ref, win_ref, wout_ref,
                   pool_w_ref, pool_scale_ref, mu_ref, w0_ref, w2_ref, a0_ref, a2_ref, g2_ref,
                   kk_ref, ka_ref, rk_ref, lnw_ref, lnb_ref, bdiag_ref, tril_ref, blk_ref,
                   o_ref, s_out_ref, pool_tail_ref, shift_tail_ref,
                   ext_u, ext_p, s_ref, ycat):
    t = pl.program_id(1)
    tt = o_ref.shape[1]
    nc = tt // CHUNK
    ng = RWKV_WIDTH // GROUP_LANES

    @pl.when(t == 0)
    def _():
        ext_u[0:POOL_HALO, :] = jnp.zeros((POOL_HALO, POOL_WIDTH), F32)
        ext_p[0:SHIFT_HALO, :] = jnp.zeros((SHIFT_HALO, SHIFT_WIDTH), F32)
        s_ref[...] = jnp.zeros_like(s_ref)

    x = x_ref[0]
    h = jnp.dot(_rmsnorm(x, gmix_ref[...]).astype(BF16), win_ref[...], preferred_element_type=F32)
    ext_u[POOL_HALO:POOL_HALO + tt, :] = h[:, 0:POOL_WIDTH]
    ext_p[SHIFT_HALO:SHIFT_HALO + tt, :] = h[:, POOL_WIDTH:IN_AB_WIDTH]

    pos = t * tt + lax.broadcasted_iota(jnp.int32, (tt, 1), 0)
    for g, win in enumerate(POOL_WINDOWS):
        cs = slice(g * POOL_GC, (g + 1) * POOL_GC)
        u = ext_u[POOL_HALO:POOL_HALO + tt, cs]
        acc = u
        for j in range(1, win):
            acc = acc + ext_u[POOL_HALO - j:POOL_HALO - j + tt, cs]
        cnt = jnp.minimum(win, pos + 1).astype(F32)
        ycat[:, cs] = _pool_group_out(acc, cnt, u, pool_w_ref, pool_scale_ref, g)

    p = ext_p[SHIFT_HALO:SHIFT_HALO + tt, :]
    shifted = ext_p[SHIFT_HALO - 1:SHIFT_HALO - 1 + tt, :]
    prm = (mu_ref, w0_ref, w2_ref, a0_ref, a2_ref, g2_ref, kk_ref, ka_ref, bdiag_ref)
    r, lw, kmod, v, kk, a, gate = _rwkv_prep(p, shifted, prm)

    ext_u[0:POOL_HALO, :] = ext_u[tt:tt + POOL_HALO, :]
    ext_p[0:SHIFT_HALO, :] = ext_p[tt:tt + SHIFT_HALO, :]

    cl = _dot_exact_lhs(tril_ref[...], lw)
    tot = _dot_exact_lhs(blk_ref[...], lw)
    gam = jnp.exp(cl)
    gam_inv = jnp.exp(-cl)
    gam_end = jnp.exp(tot)
    beta = kk * a
    kd_full = kmod * gam_inv
    bd_full = beta * gam_inv
    kq_c = _chains(kk * jnp.exp(cl - lw), nc)
    rq_c = _chains(r * gam, nc)
    kd_c = _chains(kd_full, nc)
    bd_c = _chains(bd_full, nc)
    v_c = _chains(v, nc)
    kdg_c = _chains(kd_full * gam_end, nc)
    bdg_c = _chains(bd_full * gam_end, nc)
    ge_c = _chains(gam_end, nc)
    n_ch = ng * nc
    every = range(n_ch)

    m4 = bdiag_ref[0:GROUP_LANES, 0:GROUP_LANES]
    m4f = m4.astype(F32)
    row = lax.broadcasted_iota(jnp.int32, (CHUNK, GROUP_LANES), 0)
    col = lax.broadcasted_iota(jnp.int32, (CHUNK, GROUP_LANES), 1) % RWKV_HEAD
    strict = row > col
    incl = row >= col
    eye_c = jnp.where(row == col, 1.0, 0.0)
    eye_g = (lax.broadcasted_iota(jnp.int32, (GROUP_LANES, GROUP_LANES), 0)
             == lax.broadcasted_iota(jnp.int32, (GROUP_LANES, GROUP_LANES), 1))

    bd_kd = [_block_diag(kd_c[i], m4) for i in every]
    bd_bd = [_block_diag(bd_c[i], m4) for i in every]
    lhs = [jnp.concatenate([kq_c[i], rq_c[i]], axis=0) for i in every]
    pk = [_dot_nt(lhs[i], bd_kd[i]) for i in every]
    pb = [_dot_nt(lhs[i], bd_bd[i]) for i in every]
    a_k = [jnp.where(strict, pk[i][:CHUNK], 0.0) for i in every]
    p_k = [jnp.where(incl, pk[i][CHUNK:], 0.0) for i in every]
    a_b = [jnp.where(strict, pb[i][:CHUNK], 0.0) for i in every]
    p_b = [jnp.where(incl, pb[i][CHUNK:], 0.0) for i in every]
    tm = [eye_c - a_b[i] for i in every]
    apow = a_b
    bd_ap = [_block_diag(apow[i], m4) for i in every]
    n = 1
    while 2 * n < CHUNK:
        apow = [_dot(apow[i], bd_ap[i]) for i in every]
        bd_ap = [_block_diag(apow[i], m4) for i in every]
        tm = [tm[i] + _dot(tm[i], bd_ap[i]) for i in every]
        n *= 2
    bd_v = [_block_diag(v_c[i], m4) for i in every]
    t_kq = [_dot(tm[i], _block_diag(kq_c[i], m4)) for i in every]
    akv = [_dot(a_k[i], bd_v[i]) for i in every]
    t_akv = [_dot(tm[i], _block_diag(akv[i], m4)) for i in every]
    r_y = [rq_c[i] - _dot(p_b[i], _block_diag(t_kq[i], m4)) for i in every]
    y0 = [_dot(p_k[i], bd_v[i]) - _dot(p_b[i], _block_diag(t_akv[i], m4)) for i in every]
    g_m = [m4f * (jnp.where(eye_g, ge_c[i][0:1, :], 0.0) - _dot_tn(t_kq[i], bdg_c[i])) for i in every]
    h_full = [m4f * _dot_tn(jnp.concatenate([v_c[i], t_akv[i]], axis=0),
                            jnp.concatenate([kdg_c[i], -bdg_c[i]], axis=0)) for i in every]
    h_t = [sum(h_full[i][j * RWKV_HEAD:(j + 1) * RWKV_HEAD] for j in range(GROUP_LANES // RWKV_HEAD))
           for i in every]

    ys = [None] * n_ch
    for g in range(ng):
        s = s_ref[g]
        for c in range(nc):
            i = g * nc + c
            ys[i] = _dot_nt(r_y[i], _block_diag(s, m4)) + y0[i]
            s = _dot(s, g_m[i]) + h_t[i]
        s_ref[g] = s
    y = jnp.concatenate([jnp.concatenate([ys[g * nc + c] for g in range(ng)], axis=1) for c in range(nc)],
                        axis=0)

    ycat[:, POOL_WIDTH:] = _rwkv_finish(y, r, kmod, v, gate, rk_ref, lnw_ref, lnb_ref, bdiag_ref)
    o_ref[0] = x + jnp.dot(ycat[...].astype(BF16), wout_ref[...], preferred_element_type=F32)

    @pl.when(t == pl.num_programs(1) - 1)
    def _():
        pool_tail_ref[0] = ext_u[0:POOL_HALO, :]
        shift_tail_ref[0] = ext_p[0:SHIFT_HALO, :]
        for hh in range(RWKV_HEADS):
            g, j = divmod(hh, GROUP_LANES // RWKV_HEAD)
            s_out_ref[0, hh] = s_ref[g][:, j * RWKV_HEAD:(j + 1) * RWKV_HEAD]


def _mix0_params(P):
    c = RWKV_WIDTH
    row = lambda x: x.reshape(1, -1).astype(F32)
    w2 = jnp.zeros((W_RANK + A_RANK, c), F32).at[:W_RANK].set(P['rw_w2']).astype(BF16)
    a2 = jnp.zeros((W_RANK + A_RANK, c), F32).at[W_RANK:].set(P['rw_a2']).astype(BF16)
    hid = jnp.arange(c) // RWKV_HEAD
    bdiag = (hid[:, None] == hid[None, :]).astype(BF16)
    return (P['pool_w'].astype(BF16), row(P['pool_scale']), row(P['mu_shift']), row(P['rw_w0']), w2,
            row(P['rw_a0']), a2, P['rw_g2'].astype(BF16), row(P['rw_kk']), row(P['rw_ka']),
            row(P['rw_rk']), row(P['rw_lnx_w']), row(P['rw_lnx_b']), bdiag)


def _full_spec(x):
    nd = x.ndim
    return pl.BlockSpec(x.shape, lambda *_: (0,) * nd)


def mix0_seq(x, g, w_in, w_out, P):
    b, t, _ = x.shape
    tt = _row_tile(t, 256)
    assert tt % CHUNK == 0 and tt >= POOL_HALO
    prm = _mix0_params(P)
    ti = jnp.arange(tt)
    same_chunk = (ti[:, None] // CHUNK) == (ti[None, :] // CHUNK)
    tril = (same_chunk & (ti[:, None] >= ti[None, :])).astype(BF16)
    blk = same_chunk.astype(BF16)
    return pl.pallas_call(
        _mix0_seq_body,
        grid=(b, t // tt),
        in_specs=[pl.BlockSpec((1, tt, D_MODEL), lambda i, j: (i, j, 0)),
                  pl.BlockSpec((1, D_MODEL), lambda i, j: (0, 0)), _full_spec(w_in), _full_spec(w_out)]
                 + [_full_spec(p) for p in prm] + [_full_spec(tril), _full_spec(blk)],
        out_specs=[pl.BlockSpec((1, tt, D_MODEL), lambda i, j: (i, j, 0)),
                   pl.BlockSpec((1, RWKV_HEADS, RWKV_HEAD, RWKV_HEAD), lambda i, j: (i, 0, 0, 0)),
                   pl.BlockSpec((1, POOL_HALO, POOL_WIDTH), lambda i, j: (i, 0, 0)),
                   pl.BlockSpec((1, SHIFT_HALO, SHIFT_WIDTH), lambda i, j: (i, 0, 0))],
        out_shape=[jax.ShapeDtypeStruct((b, t, D_MODEL), F32),
                   jax.ShapeDtypeStruct((b, RWKV_HEADS, RWKV_HEAD, RWKV_HEAD), F32),
                   jax.ShapeDtypeStruct((b, POOL_HALO, POOL_WIDTH), F32),
                   jax.ShapeDtypeStruct((b, SHIFT_HALO, SHIFT_WIDTH), F32)],
        scratch_shapes=[pltpu.VMEM((tt + POOL_HALO, POOL_WIDTH), F32),
                        pltpu.VMEM((tt + SHIFT_HALO, SHIFT_WIDTH), F32),
                        pltpu.VMEM((RWKV_WIDTH // GROUP_LANES, RWKV_HEAD, GROUP_LANES), F32),
                        pltpu.VMEM((tt, D_MODEL), F32)],
        compiler_params=pltpu.CompilerParams(dimension_semantics=("parallel", "arbitrary")),
        name="mix0_seq",
    )(x, g.reshape(1, D_MODEL), w_in, w_out, *prm, tril, blk)


def _to_leading(x):
    n = x.shape[0]
    ii = lax.broadcasted_iota(jnp.int32, (n, n, 1), 0)
    jj = lax.broadcasted_iota(jnp.int32, (n, n, 1), 1)
    return jnp.sum(jnp.where(ii == jj, x[None, :, :], 0.0), axis=1, keepdims=True)


def _from_leading(x3):
    n = x3.shape[0]
    ii = lax.broadcasted_iota(jnp.int32, (n, n, 1), 0)
    jj = lax.broadcasted_iota(jnp.int32, (n, n, 1), 1)
    return jnp.sum(jnp.where(ii == jj, x3, 0.0), axis=0)


def _mix0_step_body(h_ref, pool_prev_ref, shift_prev_ref, s_in_ref,
                    pool_w_ref, pool_scale_ref, mu_ref, w0_ref, w2_ref, a0_ref, a2_ref, g2_ref,
                    kk_ref, ka_ref, rk_ref, lnw_ref, lnb_ref, bdiag_ref,
                    o_ref, s_out_ref,
                    r_t, w_t, k_t, v_t, kk_t, bt_t, y_t, r_s, k_s, v_s, g_s):
    hh = pl.program_id(0)
    nb = h_ref.shape[0]

    @pl.when(hh == 0)
    def _():
        for g, win in enumerate(POOL_WINDOWS):
            cs = slice(g * POOL_GC, (g + 1) * POOL_GC)
            u = h_ref[:, cs]
            acc = u
            for j in range(1, win):
                acc = acc + pool_prev_ref[POOL_BUF - j][:, cs]
            cnt = jnp.float32(min(win, PAST_LEN + 1))
            o_ref[:, cs] = _pool_group_out(acc, cnt, u, pool_w_ref, pool_scale_ref, g)
        prm = (mu_ref, w0_ref, w2_ref, a0_ref, a2_ref, g2_ref, kk_ref, ka_ref, bdiag_ref)
        r, lw, kmod, v, kk, a, gate = _rwkv_prep(h_ref[:, POOL_WIDTH:IN_AB_WIDTH], shift_prev_ref[...], prm)
        r_s[...] = r
        k_s[...] = kmod
        v_s[...] = v
        g_s[...] = gate
        r_t[...] = r.T
        w_t[...] = jnp.exp(lw).T
        k_t[...] = kmod.T
        v_t[...] = v.T
        kk_t[...] = kk.T
        bt_t[...] = (kk * a).T

    rows = pl.ds(pl.multiple_of(hh * RWKV_HEAD, RWKV_HEAD), RWKV_HEAD)
    s = s_in_ref[...].T.reshape(RWKV_HEAD, RWKV_HEAD, nb)
    kk = kk_t[rows, :][None]
    s_kk = jnp.sum(s * kk, axis=1, keepdims=True)
    v3 = _to_leading(v_t[rows, :])
    s = s * w_t[rows, :][None] - s_kk * bt_t[rows, :][None] + v3 * k_t[rows, :][None]
    y3 = jnp.sum(s * r_t[rows, :][None], axis=1, keepdims=True)
    y_t[rows, :] = _from_leading(y3)
    s_out_ref[...] = s.reshape(RWKV_HEAD * RWKV_HEAD, nb).T

    @pl.when(hh == pl.num_programs(0) - 1)
    def _():
        o_ref[:, POOL_WIDTH:] = _rwkv_finish(y_t[...].T, r_s[...], k_s[...], v_s[...], g_s[...],
                                             rk_ref, lnw_ref, lnb_ref, bdiag_ref)


def mix0_step(h, pool_prev, shift_prev, s_prev, P):
    b = h.shape[0]
    prm = _mix0_params(P)
    hw = RWKV_HEAD * RWKV_HEAD
    s2 = s_prev.reshape(b, RWKV_HEADS * hw)
    pool_t = jnp.swapaxes(pool_prev, 0, 1)
    tvec = lambda: pltpu.VMEM((RWKV_WIDTH, b), F32)
    svec = lambda: pltpu.VMEM((b, RWKV_WIDTH), F32)
    out, s_new = pl.pallas_call(
        _mix0_step_body,
        grid=(RWKV_HEADS,),
        in_specs=[_full_spec(h), _full_spec(pool_t), _full_spec(shift_prev),
                  pl.BlockSpec((b, hw), lambda i: (0, i))] + [_full_spec(x) for x in prm],
        out_specs=[pl.BlockSpec((b, D_MODEL), lambda i: (0, 0)),
                   pl.BlockSpec((b, hw), lambda i: (0, i))],
        out_shape=[jax.ShapeDtypeStruct((b, D_MODEL), F32),
                   jax.ShapeDtypeStruct((b, RWKV_HEADS * hw), F32)],
        scratch_shapes=[tvec(), tvec(), tvec(), tvec(), tvec(), tvec(), tvec(),
                        svec(), svec(), svec(), svec()],
        compiler_params=pltpu.CompilerParams(dimension_semantics=("arbitrary",)),
        name="mix0_step",
    )(h, pool_t, shift_prev, s2, *prm)
    return out, s_new.reshape(b, RWKV_HEADS, RWKV_HEAD, RWKV_HEAD)


def _conv_seq_body(x_ref, g_ref, wi_ref, cw_ref, wo_ref, o_ref, tail_ref, ext):
    t = pl.program_id(1)
    tt = o_ref.shape[1]
    c = D_MODEL

    @pl.when(t == 0)
    def _():
        ext[0:SHIFT_HALO, :] = jnp.zeros((SHIFT_HALO, c), F32)

    x = x_ref[0]
    h = jnp.dot(_rmsnorm(x, g_ref[...]).astype(BF16), wi_ref[...], preferred_element_type=F32)
    ext[SHIFT_HALO:SHIFT_HALO + tt, :] = h[:, c:2 * c] * h[:, 2 * c:3 * c]
    z = cw_ref[0:1, :] * ext[SHIFT_HALO - 2:SHIFT_HALO - 2 + tt, :]
    z = z + cw_ref[1:2, :] * ext[SHIFT_HALO - 1:SHIFT_HALO - 1 + tt, :]
    z = z + cw_ref[2:3, :] * ext[SHIFT_HALO:SHIFT_HALO + tt, :]
    gated = (h[:, 0:c] * z).astype(BF16)
    o_ref[0] = x + jnp.dot(gated, wo_ref[...], preferred_element_type=F32)
    ext[0:SHIFT_HALO, :] = ext[tt:tt + SHIFT_HALO, :]

    @pl.when(t == pl.num_programs(1) - 1)
    def _():
        tail_ref[0] = ext[SHIFT_HALO - (CONV_WIDTH - 1):SHIFT_HALO, :]


def conv_seq(x, g, w_in, conv_w, w_out):
    b, t, c = x.shape
    tt = _row_tile(t, 512)
    return pl.pallas_call(
        _conv_seq_body,
        grid=(b, t // tt),
        in_specs=[pl.BlockSpec((1, tt, c), lambda i, j: (i, j, 0)),
                  pl.BlockSpec((1, c), lambda i, j: (0, 0)),
                  _full_spec(w_in), _full_spec(conv_w), _full_spec(w_out)],
        out_specs=[pl.BlockSpec((1, tt, c), lambda i, j: (i, j, 0)),
                   pl.BlockSpec((1, CONV_WIDTH - 1, c), lambda i, j: (i, 0, 0))],
        out_shape=[jax.ShapeDtypeStruct((b, t, c), F32),
                   jax.ShapeDtypeStruct((b, CONV_WIDTH - 1, c), F32)],
        scratch_shapes=[pltpu.VMEM((tt + SHIFT_HALO, c), F32)],
        compiler_params=pltpu.CompilerParams(dimension_semantics=("parallel", "arbitrary")),
        name="conv_seq",
    )(x, g.reshape(1, c), w_in, conv_w, w_out)


def _conv_step_body(h_ref, x_ref, p0_ref, p1_ref, cw_ref, wo_ref, o_ref, e_ref):
    c = D_MODEL
    e = h_ref[:, c:2 * c] * h_ref[:, 2 * c:3 * c]
    z = cw_ref[0:1, :] * p0_ref[...] + cw_ref[1:2, :] * p1_ref[...] + cw_ref[2:3, :] * e
    gated = (h_ref[:, 0:c] * z).astype(BF16)
    o_ref[...] = x_ref[...] + jnp.dot(gated, wo_ref[...], preferred_element_type=F32)
    e_ref[...] = e


def conv_step(h, x, prev, conv_w, w_out):
    b = h.shape[0]
    c = D_MODEL
    args = (h, x, prev[:, 0, :], prev[:, 1, :], conv_w, w_out)
    return pl.pallas_call(
        _conv_step_body,
        grid=(1,),
        in_specs=[_full_spec(a) for a in args],
        out_specs=[pl.BlockSpec((b, c), lambda i: (0, 0)), pl.BlockSpec((b, c), lambda i: (0, 0))],
        out_shape=[jax.ShapeDtypeStruct((b, c), F32), jax.ShapeDtypeStruct((b, c), F32)],
        compiler_params=pltpu.CompilerParams(dimension_semantics=("arbitrary",)),
        name="conv_step",
    )(*args)


def _xattn(x2, i, mem_k, mem_v, W, seq_shape):
    if seq_shape is not None:
        b, t = seq_shape
        return xattn_seq(x2.reshape(b, t, D_MODEL), W['norm_xattn'][i], W['w_xq'][i], W['w_xo'][i],
                         mem_k, mem_v).reshape(b * t, D_MODEL)
    return xattn_step(x2, W['norm_xattn'][i], W['w_xq'][i], W['w_xo'][i], mem_k, mem_v, i)


def _ffn0(x2, W):
    return ffn_dense(x2, W['norm_ffn'][0], W['ffn_gate'][0], W['ffn_up'][0], W['ffn_down'][0], FFN_TILE)


def _trunk_seq(x, mem_k, mem_v, W):
    b, t, d = x.shape
    P0 = {k: v[0] for k, v in W['mix0'].items()}
    x1, wkv, pool_tail, shift_tail = mix0_seq(x, W['norm_mix'][0], W['w_in_ab'][0], W['w_out_ab'][0], P0)
    pool = pool_tail[:, POOL_HALO - POOL_BUF:]
    shift = shift_tail[:, SHIFT_HALO - 1]
    x2 = _ffn0(_xattn(x1.reshape(b * t, d), 0, mem_k[0], mem_v[0], W, (b, t)), W)
    x3, conv = conv_seq(x2.reshape(b, t, d), W['norm_mix'][1], W['w_in_c'][0], W['conv_w'][0], W['w_out_c'][0])
    x2 = _xattn(x3.reshape(b * t, d), 1, mem_k[1], mem_v[1], W, (b, t))
    return x2, pool[None], shift[None], wkv[None], conv[None]


def _trunk_step(x, mem_k, mem_v, pool_prev, shift_prev, wkv_prev, conv_prev, W):
    b, _, d = x.shape
    x2 = x.reshape(b, d)
    P0 = {k: v[0] for k, v in W['mix0'].items()}
    h = norm_matmul(x2, W['norm_mix'][0], W['w_in_ab'][0])
    mix, wkv = mix0_step(h, pool_prev[0], shift_prev[0], wkv_prev[0], P0)
    pool = jnp.concatenate([pool_prev[0][:, 1:], h[:, None, :POOL_WIDTH]], axis=1)
    shift = h[:, POOL_WIDTH:]
    x2 = matmul_res(mix, W['w_out_ab'][0], x2)
    x2 = _ffn0(_xattn(x2, 0, mem_k, mem_v, W, None), W)
    h = norm_matmul(x2, W['norm_mix'][1], W['w_in_c'][0])
    x2, e = conv_step(h, x2, conv_prev[0], W['conv_w'][0], W['w_out_c'][0])
    conv = jnp.concatenate([conv_prev[0][:, 1:], e[:, None]], axis=1)
    x2 = _xattn(x2, 1, mem_k, mem_v, W, None)
    return x2, pool[None], shift[None], wkv[None], conv[None]


def kernel(x_prompt, x_sample, mem_prompt, cache_mem_k, cache_mem_v, state_pool, state_shift, state_wkv, state_conv, norm_mix, norm_xattn, norm_mem, norm_ffn, norm_final, w_xq, w_xk, w_xv, w_xo, w_in_ab, pool_w, pool_scale, mu_shift, rw_w0, rw_w2, rw_a0, rw_a2, rw_g2, rw_kk, rw_ka, rw_rk, rw_lnx_w, rw_lnx_b, w_out_ab, ffn_gate, ffn_up, ffn_down, w_in_c, conv_w, w_out_c, router_w, router_b, moe_gate, moe_up, moe_down):
    depth = norm_mix.shape[0]
    assert depth == 2 and w_in_ab.shape[0] == 1 and w_in_c.shape[0] == 1
    bp = x_prompt.shape[0]
    bs = x_sample.shape[0]
    d = D_MODEL
    bf = lambda w: w.astype(BF16)
    W = dict(norm_mix=norm_mix, norm_xattn=norm_xattn, norm_ffn=norm_ffn, norm_final=norm_final,
             w_xq=bf(w_xq), w_xo=bf(w_xo), w_in_ab=bf(w_in_ab), w_out_ab=bf(w_out_ab),
             ffn_gate=bf(ffn_gate), ffn_up=bf(ffn_up), ffn_down=bf(ffn_down),
             w_in_c=bf(w_in_c), conv_w=conv_w, w_out_c=bf(w_out_c),
             router_w=router_w, router_b=router_b,
             moe_gate=bf(moe_gate), moe_up=bf(moe_up), moe_down=bf(moe_down),
             mix0=dict(pool_w=pool_w, pool_scale=pool_scale, mu_shift=mu_shift, rw_w0=rw_w0, rw_w2=rw_w2,
                       rw_a0=rw_a0, rw_a2=rw_a2, rw_g2=rw_g2, rw_kk=rw_kk, rw_ka=rw_ka, rw_rk=rw_rk,
                       rw_lnx_w=rw_lnx_w, rw_lnx_b=rw_lnx_b))

    mem2 = mem_prompt.reshape(bp * N_MEM, d)
    mk, mv = [], []
    for i in range(depth):
        wkv_i = jnp.concatenate([bf(w_xk[i]), bf(w_xv[i])], axis=1)
        kv = norm_matmul(mem2, norm_mem[i], wkv_i)
        mk.append(kv[:, :d].reshape(bp, N_MEM, d))
        mv.append(kv[:, d:].reshape(bp, N_MEM, d))

    x_p, pool_p, shift_p, wkv_p, conv_p = _trunk_seq(x_prompt, mk, mv, W)
    x_s, pool_s, shift_s, wkv_s, conv_s = _trunk_step(x_sample, cache_mem_k, cache_mem_v, state_pool,
                                                       state_shift, state_wkv, state_conv, W)
    y_p, y_s = moe_top2_norm([x_p, x_s], norm_ffn[1], router_w[0], router_b[0], W['moe_gate'][0],
                             W['moe_up'][0], W['moe_down'][0], norm_final)
    y_p = y_p.reshape(x_prompt.shape)
    y_s = y_s.reshape(x_sample.shape)
    mem_k_p = jnp.stack(mk).reshape(depth, bp, N_MEM, X_HEADS, X_HEAD_DIM)
    mem_v_p = jnp.stack(mv).reshape(depth, bp, N_MEM, X_HEADS, X_HEAD_DIM)
    return (y_p, y_s, pool_p, pool_s, shift_p, shift_s, wkv_p, wkv_s, conv_p, conv_s, mem_k_p, mem_v_p)
```

```python
import functools

import jax
import jax.numpy as jnp
from jax import lax
from jax.experimental import pallas as pl
from jax.experimental.pallas import tpu as pltpu

F32 = jnp.float32
BF16 = jnp.bfloat16

D_MODEL = 1024
POOL_WIDTH = 512
POOL_GROUPS = 4
POOL_GC = 128
POOL_WINDOWS = (2, 4, 8, 16)
POOL_BUF = 15
RWKV_WIDTH = 512
RWKV_HEAD = 64
RWKV_HEADS = 8
W_RANK = 64
A_RANK = 64
G_RANK = 128
SHIFT_WIDTH = 3 * RWKV_WIDTH + W_RANK + A_RANK + G_RANK
IN_AB_WIDTH = POOL_WIDTH + SHIFT_WIDTH
LNX_EPS = 64e-5
CONV_WIDTH = 3
N_EXPERTS = 8
N_MEM = 256
X_HEADS = 4
X_HEAD_DIM = 256
RMS_EPS = 1e-6
PAST_LEN = 16384

CHUNK = 64
GROUP_LANES = 256
POOL_HALO = 16
SHIFT_HALO = 8
ROUTER_PAD = 128
TOP_K = 2
ROW_SUBLANES = 8
MOE_ROWS = 512
FFN_TILE = 1408
MOE_TILE = 1792
XATTN_STEP_VMEM = 48 * 1024 * 1024


def _rmsnorm(x, g):
    ms = jnp.mean(x * x, axis=-1, keepdims=True)
    return x * lax.rsqrt(ms + RMS_EPS) * g


def _dot(a, b):
    return jnp.dot(a.astype(BF16), b.astype(BF16), preferred_element_type=F32)


def _dot_nt(a, b):
    return lax.dot_general(a.astype(BF16), b.astype(BF16), (((1,), (1,)), ((), ())),
                           preferred_element_type=F32)


def _dot_tn(a, b):
    return lax.dot_general(a.astype(BF16), b.astype(BF16), (((0,), (0,)), ((), ())),
                           preferred_element_type=F32)


def _split3(x):
    hi = x.astype(BF16)
    r1 = x - hi.astype(F32)
    mid = r1.astype(BF16)
    lo = (r1 - mid.astype(F32)).astype(BF16)
    return hi, mid, lo


def _dot_exact_lhs(m01, x):
    hi, mid, lo = _split3(x)
    f = lambda p: jnp.dot(m01, p, preferred_element_type=F32)
    return f(hi) + f(mid) + f(lo)


def _softplus(x):
    return jnp.maximum(x, 0.0) + jnp.log1p(jnp.exp(-jnp.abs(x)))


def _sigmoid(x):
    return 1.0 / (1.0 + jnp.exp(-x))


def _row_tile(m, want):
    t = min(m, want)
    assert m % t == 0, (m, t)
    return t


def _norm_matmul_body(x_ref, g_ref, w_ref, o_ref):
    xn = _rmsnorm(x_ref[...], g_ref[...]).astype(BF16)
    o_ref[...] = jnp.dot(xn, w_ref[...], preferred_element_type=F32)


def norm_matmul(x, g, w):
    m, k = x.shape
    n = w.shape[1]
    tm = _row_tile(m, 512)
    return pl.pallas_call(
        _norm_matmul_body,
        grid=(m // tm,),
        in_specs=[pl.BlockSpec((tm, k), lambda i: (i, 0)),
                  pl.BlockSpec((1, k), lambda i: (0, 0)),
                  pl.BlockSpec((k, n), lambda i: (0, 0))],
        out_specs=pl.BlockSpec((tm, n), lambda i: (i, 0)),
        out_shape=jax.ShapeDtypeStruct((m, n), F32),
        compiler_params=pltpu.CompilerParams(dimension_semantics=("parallel",)),
        name="norm_matmul",
    )(x, g.reshape(1, k), w)


def _matmul_res_body(a_ref, w_ref, r_ref, o_ref):
    o_ref[...] = r_ref[...] + jnp.dot(a_ref[...].astype(BF16), w_ref[...], preferred_element_type=F32)


def matmul_res(a, w, res):
    m, k = a.shape
    n = w.shape[1]
    tm = _row_tile(m, 512)
    return pl.pallas_call(
        _matmul_res_body,
        grid=(m // tm,),
        in_specs=[pl.BlockSpec((tm, k), lambda i: (i, 0)),
                  pl.BlockSpec((k, n), lambda i: (0, 0)),
                  pl.BlockSpec((tm, n), lambda i: (i, 0))],
        out_specs=pl.BlockSpec((tm, n), lambda i: (i, 0)),
        out_shape=jax.ShapeDtypeStruct((m, n), F32),
        compiler_params=pltpu.CompilerParams(dimension_semantics=("parallel",)),
        name="matmul_res",
    )(a, w, res)


def _ffn_body(x_ref, g_ref, wg_ref, wu_ref, wd_ref, o_ref, xn_ref, acc_ref):
    f = pl.program_id(1)

    @pl.when(f == 0)
    def _():
        xn_ref[...] = _rmsnorm(x_ref[...], g_ref[...]).astype(BF16)
        acc_ref[...] = jnp.zeros_like(acc_ref)

    xn = xn_ref[...]
    gate = jnp.dot(xn, wg_ref[...], preferred_element_type=F32)
    up = jnp.dot(xn, wu_ref[...], preferred_element_type=F32)
    hid = (gate * _sigmoid(gate) * up).astype(BF16)
    acc_ref[...] += jnp.dot(hid, wd_ref[...], preferred_element_type=F32)

    @pl.when(f == pl.num_programs(1) - 1)
    def _():
        o_ref[...] = x_ref[...] + acc_ref[...]


def ffn_dense(x, g, wg, wu, wd, tf):
    m, d = x.shape
    ff = wg.shape[1]
    tm = _row_tile(m, 512)
    return pl.pallas_call(
        _ffn_body,
        grid=(m // tm, ff // tf),
        in_specs=[pl.BlockSpec((tm, d), lambda i, f: (i, 0)),
                  pl.BlockSpec((1, d), lambda i, f: (0, 0)),
                  pl.BlockSpec((d, tf), lambda i, f: (0, f)),
                  pl.BlockSpec((d, tf), lambda i, f: (0, f)),
                  pl.BlockSpec((tf, d), lambda i, f: (f, 0))],
        out_specs=pl.BlockSpec((tm, d), lambda i, f: (i, 0)),
        out_shape=jax.ShapeDtypeStruct((m, d), F32),
        scratch_shapes=[pltpu.VMEM((tm, d), BF16), pltpu.VMEM((tm, d), F32)],
        compiler_params=pltpu.CompilerParams(dimension_semantics=("parallel", "arbitrary")),
        name="ffn_dense",
    )(x, g.reshape(1, d), wg, wu, wd)


def _router_body(x_ref, g_ref, w_ref, b_ref, oi_ref, ow_ref):
    xn = _rmsnorm(x_ref[...], g_ref[...])
    logits = _dot_3pass(xn, w_ref[...]) + b_ref[...]
    lane = lax.broadcasted_iota(jnp.int32, logits.shape, 1)
    neg = jnp.float32(-jnp.inf)
    logits = jnp.where(lane < N_EXPERTS, logits, neg)
    m1 = jnp.max(logits, axis=-1, keepdims=True)
    i1 = jnp.min(jnp.where(logits == m1, lane, ROUTER_PAD), axis=-1, keepdims=True)
    rest = jnp.where(lane == i1, neg, logits)
    m2 = jnp.max(rest, axis=-1, keepdims=True)
    i2 = jnp.min(jnp.where(rest == m2, lane, ROUTER_PAD), axis=-1, keepdims=True)
    e2 = jnp.exp(m2 - m1)
    den = 1.0 + e2
    slot = lax.broadcasted_iota(jnp.int32, oi_ref.shape, 1)
    oi_ref[...] = jnp.where(slot == 0, i1, i2)
    ow_ref[...] = jnp.where(slot == 0, 1.0 / den, e2 / den)


def _dot_3pass(x, w):
    xh = x.astype(BF16)
    xm = (x - xh.astype(F32)).astype(BF16)
    wh = w.astype(BF16)
    wm = (w - wh.astype(F32)).astype(BF16)
    f = lambda p, q: jnp.dot(p, q, preferred_element_type=F32)
    return f(xh, wh) + (f(xh, wm) + f(xm, wh))


def moe_router(x, g, router_w, router_b):
    m, d = x.shape
    tm = _row_tile(m, 512)
    w = jnp.zeros((d, ROUTER_PAD), F32).at[:, :N_EXPERTS].set(router_w)
    b = jnp.zeros((1, ROUTER_PAD), F32).at[0, :N_EXPERTS].set(router_b)
    return pl.pallas_call(
        _router_body,
        grid=(m // tm,),
        in_specs=[pl.BlockSpec((tm, d), lambda i: (i, 0)),
                  pl.BlockSpec((1, d), lambda i: (0, 0)),
                  pl.BlockSpec((d, ROUTER_PAD), lambda i: (0, 0)),
                  pl.BlockSpec((1, ROUTER_PAD), lambda i: (0, 0))],
        out_specs=[pl.BlockSpec((tm, TOP_K), lambda i: (i, 0)), pl.BlockSpec((tm, TOP_K), lambda i: (i, 0))],
        out_shape=[jax.ShapeDtypeStruct((m, TOP_K), jnp.int32), jax.ShapeDtypeStruct((m, TOP_K), F32)],
        compiler_params=pltpu.CompilerParams(dimension_semantics=("parallel",)),
        name="moe_router",
    )(x, g.reshape(1, d), w, b)


def _row_tiles_to_2d(ref, slot, n):
    return jnp.concatenate([ref[slot, pl.ds(j, n, stride=ROW_SUBLANES), :] for j in range(ROW_SUBLANES)],
                           axis=1)


def _row_copy(src_hbm, row, dst, slot, i, sem):
    return pltpu.make_async_copy(src_hbm.at[pl.ds(row * ROW_SUBLANES, ROW_SUBLANES)],
                                 dst.at[slot, pl.ds(i * ROW_SUBLANES, ROW_SUBLANES)], sem.at[slot])


def _gather_start(src_hbm, rows_smem, dst, slot, n, sem):
    def body(i, carry):
        for k in range(2):
            r = 2 * i + k
            _row_copy(src_hbm, rows_smem[0, 0, r], dst, slot, r, sem).start(priority=k)
        return carry
    lax.fori_loop(0, n // 2, body, 0, unroll=4)


def _gather_wait(src_hbm, dst, slot, n, sem):
    pltpu.make_async_copy(src_hbm.at[pl.ds(0, n * ROW_SUBLANES)], dst.at[slot], sem.at[slot]).wait()


def _moe_group_body(te_ref, nu_ref, tok0_ref, tokn_ref, x8_ref, g_ref, wg_ref, wu_ref, wd_ref,
                    o_ref, xbuf, sem, xn_ref, acc_ref, *, nf):
    t = pl.program_id(0)
    f = pl.program_id(1)
    tm = xn_ref.shape[0]
    per_step = tm // nf
    used = t < nu_ref[0]
    slot = t % 2

    @pl.when(jnp.logical_and(t == 0, f == 0))
    def _():
        _gather_start(x8_ref, tok0_ref, xbuf, 0, tm, sem)

    @pl.when(jnp.logical_and(used, f == 0))
    def _():
        _gather_wait(x8_ref, xbuf, slot, tm, sem)
        x = _row_tiles_to_2d(xbuf, slot, tm)
        xn_ref[...] = _rmsnorm(x, g_ref[...]).astype(BF16)
        acc_ref[...] = jnp.zeros_like(acc_ref)

    @pl.when(used)
    def _():
        base = f * per_step
        for i in range(per_step):
            _row_copy(x8_ref, tokn_ref[0, 0, base + i], xbuf, 1 - slot, base + i, sem).start()
        xn = xn_ref[...]
        gate = jnp.dot(xn, wg_ref[0], preferred_element_type=F32)
        up = jnp.dot(xn, wu_ref[0], preferred_element_type=F32)
        hid = (gate * _sigmoid(gate) * up).astype(BF16)
        acc_ref[...] += jnp.dot(hid, wd_ref[0], preferred_element_type=F32)

    @pl.when(f == nf - 1)
    def _():
        y = jnp.where(used, acc_ref[...], 0.0)
        for j in range(ROW_SUBLANES):
            o_ref[pl.ds(j, tm, stride=ROW_SUBLANES), :] = y[:, j * 128:(j + 1) * 128]

    @pl.when(jnp.logical_and(t == nu_ref[0] - 1, f == nf - 1))
    def _():
        _gather_wait(x8_ref, xbuf, 1 - slot, tm, sem)


def moe_grouped(x8, g, tile_expert, n_used, row_token, wg, wu, wd, tf):
    n_tiles, _, tm = row_token.shape
    d = D_MODEL
    ne, _, ff = wg.shape
    nf = ff // tf
    assert tm % nf == 0
    live_f = lambda t, f, nu: jnp.where(t < nu[0], f, nf - 1)
    smem_rows = lambda imap: pl.BlockSpec((1, 1, tm), imap, memory_space=pltpu.SMEM)
    grid_spec = pltpu.PrefetchScalarGridSpec(
        num_scalar_prefetch=2,
        grid=(n_tiles, nf),
        in_specs=[smem_rows(lambda t, f, te, nu: (0, 0, 0)),
                  smem_rows(lambda t, f, te, nu: (jnp.minimum(t + 1, nu[0] - 1), 0, 0)),
                  pl.BlockSpec(memory_space=pl.ANY),
                  pl.BlockSpec((1, d), lambda t, f, te, nu: (0, 0)),
                  pl.BlockSpec((1, d, tf), lambda t, f, te, nu: (te[t], 0, live_f(t, f, nu))),
                  pl.BlockSpec((1, d, tf), lambda t, f, te, nu: (te[t], 0, live_f(t, f, nu))),
                  pl.BlockSpec((1, tf, d), lambda t, f, te, nu: (te[t], live_f(t, f, nu), 0))],
        out_specs=pl.BlockSpec((tm * ROW_SUBLANES, 128), lambda t, f, te, nu: (t, 0)),
        scratch_shapes=[pltpu.VMEM((2, tm * ROW_SUBLANES, 128), F32), pltpu.SemaphoreType.DMA((2,)),
                        pltpu.VMEM((tm, d), BF16), pltpu.VMEM((tm, d), F32)])
    return pl.pallas_call(
        functools.partial(_moe_group_body, nf=nf),
        grid_spec=grid_spec,
        out_shape=jax.ShapeDtypeStruct((n_tiles * tm * ROW_SUBLANES, 128), F32),
        compiler_params=pltpu.CompilerParams(dimension_semantics=("arbitrary", "arbitrary")),
        name="moe_grouped",
    )(tile_expert, n_used, row_token, row_token, x8, g.reshape(1, d), wg, wu, wd)


def _moe_combine_body(pos0_ref, posn_ref, y8_ref, x_ref, w_ref, g_ref, o_ref, ybuf, sem):
    t = pl.program_id(0)
    tc = x_ref.shape[0]
    n = TOP_K * tc

    @pl.when(t == 0)
    def _():
        _gather_start(y8_ref, pos0_ref, ybuf, 0, n, sem)

    @pl.when(t + 1 < pl.num_programs(0))
    def _():
        _gather_start(y8_ref, posn_ref, ybuf, (t + 1) % 2, n, sem)

    slot = t % 2
    _gather_wait(y8_ref, ybuf, slot, n, sem)
    y = _row_tiles_to_2d(ybuf, slot, n)
    w = w_ref[...]
    moe = w[:, 0:1] * y[0:tc] + w[:, 1:2] * y[tc:n]
    o_ref[...] = _rmsnorm(x_ref[...] + moe, g_ref[...])


def moe_combine_norm(y8, x, pos, topw, g):
    m, d = x.shape
    nt, _, n = pos.shape
    tc = n // TOP_K
    assert nt * tc == m
    smem_rows = lambda imap: pl.BlockSpec((1, 1, n), imap, memory_space=pltpu.SMEM)
    return pl.pallas_call(
        _moe_combine_body,
        grid=(nt,),
        in_specs=[smem_rows(lambda t: (0, 0, 0)),
                  smem_rows(lambda t: (jnp.minimum(t + 1, nt - 1), 0, 0)),
                  pl.BlockSpec(memory_space=pl.ANY),
                  pl.BlockSpec((tc, d), lambda t: (t, 0)),
                  pl.BlockSpec((tc, TOP_K), lambda t: (t, 0)),
                  pl.BlockSpec((1, d), lambda t: (0, 0))],
        out_specs=pl.BlockSpec((tc, d), lambda t: (t, 0)),
        out_shape=jax.ShapeDtypeStruct((m, d), F32),
        scratch_shapes=[pltpu.VMEM((2, n * ROW_SUBLANES, 128), F32), pltpu.SemaphoreType.DMA((2,))],
        compiler_params=pltpu.CompilerParams(dimension_semantics=("arbitrary",)),
        name="moe_combine_norm",
    )(pos, pos, y8, x, topw, g.reshape(1, d))


def _moe_plan(topi, tm):
    n = topi.shape[0]
    a = TOP_K * n
    n_tiles = (a + N_EXPERTS * (tm - 1) + tm - 1) // tm
    e = topi.reshape(a)
    eid = jnp.arange(N_EXPERTS, dtype=jnp.int32)
    onehot = (e[:, None] == eid[None, :]).astype(jnp.int32)
    csum = jnp.cumsum(onehot, axis=0)
    cnt = csum[-1]
    rank = jnp.sum((csum - onehot) * onehot, axis=1)
    tiles = (cnt + tm - 1) // tm
    tile_end = jnp.cumsum(tiles)
    tile_start = tile_end - tiles
    n_used = tile_end[-1]
    pos = (jnp.sum(onehot * tile_start[None, :], axis=1) * tm + rank).reshape(n, TOP_K)
    tid = jnp.minimum(jnp.arange(n_tiles, dtype=jnp.int32), n_used - 1)
    tile_expert = jnp.sum(tid[:, None] >= tile_end[None, :], axis=1).astype(jnp.int32)
    fill = tiles * tm - cnt
    j = jnp.arange(tm - 1, dtype=jnp.int32)
    dummy_key = jnp.where(j[None, :] < fill[:, None], eid[:, None], N_EXPERTS).reshape(-1)
    spare = n_tiles * tm - a - dummy_key.shape[0]
    keys = jnp.concatenate([e, dummy_key, jnp.full((spare,), N_EXPERTS, jnp.int32)])
    toks = jnp.concatenate([jnp.arange(a, dtype=jnp.int32) // TOP_K,
                            jnp.zeros((keys.shape[0] - a,), jnp.int32)])
    _, row_token = lax.sort((keys, toks), num_keys=1, is_stable=True)
    return tile_expert, n_used.reshape(1).astype(jnp.int32), row_token.reshape(n_tiles, 1, tm), pos


def moe_top2_norm(xs, g, router_w, router_b, wg, wu, wd, g_final):
    tops = [moe_router(x, g, router_w, router_b) for x in xs]
    topi = jnp.concatenate([t[0] for t in tops], axis=0)
    tile_expert, n_used, row_token, pos = _moe_plan(topi, MOE_ROWS)
    x8 = jnp.concatenate(xs, axis=0).reshape(-1, 128)
    y8 = moe_grouped(x8, g, tile_expert, n_used, row_token, wg, wu, wd, MOE_TILE)
    outs, start = [], 0
    for x, (_, topw) in zip(xs, tops):
        m = x.shape[0]
        tc = _row_tile(m, 256)
        p = pos[start:start + m].reshape(m // tc, tc, TOP_K)
        p = jnp.swapaxes(p, 1, 2).reshape(m // tc, 1, TOP_K * tc)
        outs.append(moe_combine_norm(y8, x, p, topw, g_final))
        start += m
    return outs


def _xattn_seq_body(x_ref, g_ref, wq_ref, wo_ref, k_ref, v_ref, o_ref):
    x = x_ref[0]
    xn = _rmsnorm(x, g_ref[...]).astype(BF16)
    q = jnp.dot(xn, wq_ref[...], preferred_element_type=F32).astype(BF16)
    kb = k_ref[0].astype(BF16)
    vb = v_ref[0].astype(BF16)
    heads = []
    for h in range(X_HEADS):
        sl = slice(h * X_HEAD_DIM, (h + 1) * X_HEAD_DIM)
        s = _dot_nt(q[:, sl], kb[:, sl]) * (X_HEAD_DIM ** -0.5)
        mx = jnp.max(s, axis=-1, keepdims=True)
        ex = jnp.exp(s - mx)
        pr = ex / jnp.sum(ex, axis=-1, keepdims=True)
        heads.append(jnp.dot(pr.astype(BF16), vb[:, sl], preferred_element_type=F32))
    att = jnp.concatenate(heads, axis=-1).astype(BF16)
    o_ref[0] = x + jnp.dot(att, wo_ref[...], preferred_element_type=F32)


def xattn_seq(x, g, wq, wo, mem_k, mem_v):
    b, t, d = x.shape
    tq = _row_tile(t, 512)
    return pl.pallas_call(
        _xattn_seq_body,
        grid=(b, t // tq),
        in_specs=[pl.BlockSpec((1, tq, d), lambda i, j: (i, j, 0)),
                  pl.BlockSpec((1, d), lambda i, j: (0, 0)),
                  pl.BlockSpec((d, d), lambda i, j: (0, 0)),
                  pl.BlockSpec((d, d), lambda i, j: (0, 0)),
                  pl.BlockSpec((1, N_MEM, d), lambda i, j: (i, 0, 0)),
                  pl.BlockSpec((1, N_MEM, d), lambda i, j: (i, 0, 0))],
        out_specs=pl.BlockSpec((1, tq, d), lambda i, j: (i, j, 0)),
        out_shape=jax.ShapeDtypeStruct((b, t, d), F32),
        compiler_params=pltpu.CompilerParams(dimension_semantics=("parallel", "parallel")),
        name="xattn_seq",
    )(x, g.reshape(1, d), wq, wo, mem_k, mem_v)


XATTN_STEP_ROWS = 8
XATTN_STEP_KV = 4


def _xattn_step_body(x_ref, g_ref, wq_ref, wo_ref, k_ref, v_ref, o_ref, q_s, att_s):
    j = pl.program_id(1)

    @pl.when(j == 0)
    def _():
        xn = _rmsnorm(x_ref[...], g_ref[...]).astype(BF16)
        q_s[...] = jnp.dot(xn, wq_ref[...], preferred_element_type=F32) * (X_HEAD_DIM ** -0.5)

    for b in range(XATTN_STEP_KV):
        rsel = pl.ds(j * XATTN_STEP_KV + b, 1)
        q4 = jnp.concatenate([q_s[rsel, h * X_HEAD_DIM:(h + 1) * X_HEAD_DIM] for h in range(X_HEADS)], axis=0)
        s = jnp.sum(k_ref[0, b] * q4[None], axis=-1, keepdims=True)
        mx = jnp.max(s, axis=0, keepdims=True)
        ex = jnp.exp(s - mx)
        den = jnp.sum(ex, axis=0)
        o4 = jnp.sum(ex * v_ref[0, b], axis=0) / den
        for h in range(X_HEADS):
            att_s[rsel, h * X_HEAD_DIM:(h + 1) * X_HEAD_DIM] = o4[h:h + 1, :]

    @pl.when(j == pl.num_programs(1) - 1)
    def _():
        o_ref[...] = x_ref[...] + jnp.dot(att_s[...].astype(BF16), wo_ref[...], preferred_element_type=F32)


def xattn_step(x, g, wq, wo, mem_k, mem_v, layer):
    b, d = x.shape
    rows, kvb = XATTN_STEP_ROWS, XATTN_STEP_KV
    assert b % rows == 0 and rows % kvb == 0
    nj = rows // kvb
    kv_spec = pl.BlockSpec((1, kvb, N_MEM, X_HEADS, X_HEAD_DIM), lambda i, j: (layer, i * nj + j, 0, 0, 0))
    return pl.pallas_call(
        _xattn_step_body,
        grid=(b // rows, nj),
        in_specs=[pl.BlockSpec((rows, d), lambda i, j: (i, 0)),
                  pl.BlockSpec((1, d), lambda i, j: (0, 0)),
                  pl.BlockSpec((d, d), lambda i, j: (0, 0)),
                  pl.BlockSpec((d, d), lambda i, j: (0, 0)),
                  kv_spec, kv_spec],
        out_specs=pl.BlockSpec((rows, d), lambda i, j: (i, 0)),
        out_shape=jax.ShapeDtypeStruct((b, d), F32),
        scratch_shapes=[pltpu.VMEM((rows, d), F32), pltpu.VMEM((rows, d), F32)],
        compiler_params=pltpu.CompilerParams(dimension_semantics=("parallel", "arbitrary"),
                                             vmem_limit_bytes=XATTN_STEP_VMEM),
        name="xattn_step",
    )(x, g.reshape(1, d), wq, wo, mem_k, mem_v)


def _head_sum(x, bdiag_ref):
    m = bdiag_ref[0:GROUP_LANES, 0:GROUP_LANES]
    parts = []
    for g in range(RWKV_WIDTH // GROUP_LANES):
        xs = x[:, g * GROUP_LANES:(g + 1) * GROUP_LANES]
        hi = xs.astype(BF16)
        lo = (xs - hi.astype(F32)).astype(BF16)
        parts.append(jnp.dot(hi, m, preferred_element_type=F32) + jnp.dot(lo, m, preferred_element_type=F32))
    return jnp.concatenate(parts, axis=1)


def _pool_group_out(acc, cnt, u, pool_w_ref, pool_scale_ref, g):
    cs = slice(g * POOL_GC, (g + 1) * POOL_GC)
    d = acc / cnt - u
    return _dot(d, pool_w_ref[g]) * pool_scale_ref[:, cs]


def _rwkv_prep(p, shifted, prm):
    (mu_ref, w0_ref, w2_ref, a0_ref, a2_ref, g2_ref, kk_ref, ka_ref, bdiag_ref) = prm
    m = p + (shifted - p) * mu_ref[...]
    c = RWKV_WIDTH
    r = m[:, 0:c]
    k = m[:, c:2 * c]
    v = m[:, 2 * c:3 * c]
    dwa = m[:, 3 * c:3 * c + W_RANK + A_RANK]
    dg = m[:, 3 * c + W_RANK + A_RANK:]
    w_log = -_softplus(-(w0_ref[...] + _dot(jnp.tanh(dwa), w2_ref[...]))) - 0.5
    lw = -jnp.exp(w_log)
    a = _sigmoid(a0_ref[...] + _dot(dwa, a2_ref[...]))
    gate = _dot(_sigmoid(dg), g2_ref[...])
    kk = k * kk_ref[...]
    kk = kk / jnp.maximum(jnp.sqrt(_head_sum(kk * kk, bdiag_ref)), 1e-12)
    kmod = k * (1.0 + (a - 1.0) * ka_ref[...])
    return r, lw, kmod, v, kk, a, gate


def _rwkv_finish(y, r, kmod, v, gate, rk_ref, lnw_ref, lnb_ref, bdiag_ref):
    mu = _head_sum(y, bdiag_ref) * (1.0 / RWKV_HEAD)
    dlt = y - mu
    var = _head_sum(dlt * dlt, bdiag_ref) * (1.0 / RWKV_HEAD)
    yn = dlt * lax.rsqrt(var + LNX_EPS) * lnw_ref[...] + lnb_ref[...]
    bonus = _head_sum(r * kmod * rk_ref[...], bdiag_ref) * v
    return (yn + bonus) * gate


def _chains(x, nc):
    return [x[c * CHUNK:(c + 1) * CHUNK, g * GROUP_LANES:(g + 1) * GROUP_LANES]
            for g in range(RWKV_WIDTH // GROUP_LANES) for c in range(nc)]


def _block_diag(x, m4):
    xb = x.astype(BF16)
    return jnp.concatenate([xb] * (GROUP_LANES // RWKV_HEAD), axis=0) * m4


def _mix0_seq_body(x_ref, gmix_ref, win_ref, wout_ref,
                   pool_w_ref, pool_scale_ref, mu_ref, w0_ref, w2_ref, a0_ref, a2_ref, g2_ref,
                   kk_ref, ka_ref, rk_ref, lnw_ref, lnb_ref, bdiag_ref, tril_ref, blk_ref,
                   o_ref, s_out_ref, pool_tail_ref, shift_tail_ref,
                   ext_u, ext_p, s_ref, ycat):
    t = pl.program_id(1)
    tt = o_ref.shape[1]
    nc = tt // CHUNK
    ng = RWKV_WIDTH // GROUP_LANES

    @pl.when(t == 0)
    def _():
        ext_u[0:POOL_HALO, :] = jnp.zeros((POOL_HALO, POOL_WIDTH), F32)
        ext_p[0:SHIFT_HALO, :] = jnp.zeros((SHIFT_HALO, SHIFT_WIDTH), F32)
        s_ref[...] = jnp.zeros_like(s_ref)

    x = x_ref[0]
    h = jnp.dot(_rmsnorm(x, gmix_ref[...]).astype(BF16), win_ref[...], preferred_element_type=F32)
    ext_u[POOL_HALO:POOL_HALO + tt, :] = h[:, 0:POOL_WIDTH]
    ext_p[SHIFT_HALO:SHIFT_HALO + tt, :] = h[:, POOL_WIDTH:IN_AB_WIDTH]

    pos = t * tt + lax.broadcasted_iota(jnp.int32, (tt, 1), 0)
    for g, win in enumerate(POOL_WINDOWS):
        cs = slice(g * POOL_GC, (g + 1) * POOL_GC)
        u = ext_u[POOL_HALO:POOL_HALO + tt, cs]
        acc = u
        for j in range(1, win):
            acc = acc + ext_u[POOL_HALO - j:POOL_HALO - j + tt, cs]
        cnt = jnp.minimum(win, pos + 1).astype(F32)
        ycat[:, cs] = _pool_group_out(acc, cnt, u, pool_w_ref, pool_scale_ref, g)

    p = ext_p[SHIFT_HALO:SHIFT_HALO + tt, :]
    shifted = ext_p[SHIFT_HALO - 1:SHIFT_HALO - 1 + tt, :]
    prm = (mu_ref, w0_ref, w2_ref, a0_ref, a2_ref, g2_ref, kk_ref, ka_ref, bdiag_ref)
    r, lw, kmod, v, kk, a, gate = _rwkv_prep(p, shifted, prm)

    ext_u[0:POOL_HALO, :] = ext_u[tt:tt + POOL_HALO, :]
    ext_p[0:SHIFT_HALO, :] = ext_p[tt:tt + SHIFT_HALO, :]

    cl = _dot_exact_lhs(tril_ref[...], lw)
    tot = _dot_exact_lhs(blk_ref[...], lw)
    gam = jnp.exp(cl)
    gam_inv = jnp.exp(-cl)
    gam_end = jnp.exp(tot)
    beta = kk * a
    kd_full = kmod * gam_inv
    bd_full = beta * gam_inv
    kq_c = _chains(kk * jnp.exp(cl - lw), nc)
    rq_c = _chains(r * gam, nc)
    kd_c = _chains(kd_full, nc)
    bd_c = _chains(bd_full, nc)
    v_c = _chains(v, nc)
    kdg_c = _chains(kd_full * gam_end, nc)
    bdg_c = _chains(bd_full * gam_end, nc)
    ge_c = _chains(gam_end, nc)
    n_ch = ng * nc
    every = range(n_ch)

    m4 = bdiag_ref[0:GROUP_LANES, 0:GROUP_LANES]
    m4f = m4.astype(F32)
    row = lax.broadcasted_iota(jnp.int32, (CHUNK, GROUP_LANES), 0)
    col = lax.broadcasted_iota(jnp.int32, (CHUNK, GROUP_LANES), 1) % RWKV_HEAD
    strict = row > col
    incl = row >= col
    eye_c = jnp.where(row == col, 1.0, 0.0)
    eye_g = (lax.broadcasted_iota(jnp.int32, (GROUP_LANES, GROUP_LANES), 0)
             == lax.broadcasted_iota(jnp.int32, (GROUP_LANES, GROUP_LANES), 1))

    bd_kd = [_block_diag(kd_c[i], m4) for i in every]
    bd_bd = [_block_diag(bd_c[i], m4) for i in every]
    lhs = [jnp.concatenate([kq_c[i], rq_c[i]], axis=0) for i in every]
    pk = [_dot_nt(lhs[i], bd_kd[i]) for i in every]
    pb = [_dot_nt(lhs[i], bd_bd[i]) for i in every]
    a_k = [jnp.where(strict, pk[i][:CHUNK], 0.0) for i in every]
    p_k = [jnp.where(incl, pk[i][CHUNK:], 0.0) for i in every]
    a_b = [jnp.where(strict, pb[i][:CHUNK], 0.0) for i in every]
    p_b = [jnp.where(incl, pb[i][CHUNK:], 0.0) for i in every]
    tm = [eye_c - a_b[i] for i in every]
    apow = a_b
    bd_ap = [_block_diag(apow[i], m4) for i in every]
    n = 1
    while 2 * n < CHUNK:
        apow = [_dot(apow[i], bd_ap[i]) for i in every]
        bd_ap = [_block_diag(apow[i], m4) for i in every]
        tm = [tm[i] + _dot(tm[i], bd_ap[i]) for i in every]
        n *= 2
    bd_v = [_block_diag(v_c[i], m4) for i in every]
    t_kq = [_dot(tm[i], _block_diag(kq_c[i], m4)) for i in every]
    akv = [_dot(a_k[i], bd_v[i]) for i in every]
    t_akv = [_dot(tm[i], _block_diag(akv[i], m4)) for i in every]
    r_y = [rq_c[i] - _dot(p_b[i], _block_diag(t_kq[i], m4)) for i in every]
    y0 = [_dot(p_k[i], bd_v[i]) - _dot(p_b[i], _block_diag(t_akv[i], m4)) for i in every]
    g_m = [m4f * (jnp.where(eye_g, ge_c[i][0:1, :], 0.0) - _dot_tn(t_kq[i], bdg_c[i])) for i in every]
    h_full = [m4f * _dot_tn(jnp.concatenate([v_c[i], t_akv[i]], axis=0),
                            jnp.concatenate([kdg_c[i], -bdg_c[i]], axis=0)) for i in every]
    h_t = [sum(h_full[i][j * RWKV_HEAD:(j + 1) * RWKV_HEAD] for j in range(GROUP_LANES // RWKV_HEAD))
           for i in every]

    ys = [None] * n_ch
    for g in range(ng):
        s = s_ref[g]
        for c in range(nc):
            i = g * nc + c
            ys[i] = _dot_nt(r_y[i], _block_diag(s, m4)) + y0[i]
            s = _dot(s, g_m[i]) + h_t[i]
        s_ref[g] = s
    y = jnp.concatenate([jnp.concatenate([ys[g * nc + c] for g in range(ng)], axis=1) for c in range(nc)],
                        axis=0)

    ycat[:, POOL_WIDTH:] = _rwkv_finish(y, r, kmod, v, gate, rk_ref, lnw_ref, lnb_ref, bdiag_ref)
    o_ref[0] = x + jnp.dot(ycat[...].astype(BF16), wout_ref[...], preferred_element_type=F32)

    @pl.when(t == pl.num_programs(1) - 1)
    def _():
        pool_tail_ref[0] = ext_u[0:POOL_HALO, :]
        shift_tail_ref[0] = ext_p[0:SHIFT_HALO, :]
        for hh in range(RWKV_HEADS):
            g, j = divmod(hh, GROUP_LANES // RWKV_HEAD)
            s_out_ref[0, hh] = s_ref[g][:, j * RWKV_HEAD:(j + 1) * RWKV_HEAD]


def _mix0_params(P):
    c = RWKV_WIDTH
    row = lambda x: x.reshape(1, -1).astype(F32)
    w2 = jnp.zeros((W_RANK + A_RANK, c), F32).at[:W_RANK].set(P['rw_w2']).astype(BF16)
    a2 = jnp.zeros((W_RANK + A_RANK, c), F32).at[W_RANK:].set(P['rw_a2']).astype(BF16)
    hid = jnp.arange(c) // RWKV_HEAD
    bdiag = (hid[:, None] == hid[None, :]).astype(BF16)
    return (P['pool_w'].astype(BF16), row(P['pool_scale']), row(P['mu_shift']), row(P['rw_w0']), w2,
            row(P['rw_a0']), a2, P['rw_g2'].astype(BF16), row(P['rw_kk']), row(P['rw_ka']),
            row(P['rw_rk']), row(P['rw_lnx_w']), row(P['rw_lnx_b']), bdiag)


def _full_spec(x):
    nd = x.ndim
    return pl.BlockSpec(x.shape, lambda *_: (0,) * nd)


def mix0_seq(x, g, w_in, w_out, P):
    b, t, _ = x.shape
    tt = _row_tile(t, 256)
    assert tt % CHUNK == 0 and tt >= POOL_HALO
    prm = _mix0_params(P)
    ti = jnp.arange(tt)
    same_chunk = (ti[:, None] // CHUNK) == (ti[None, :] // CHUNK)
    tril = (same_chunk & (ti[:, None] >= ti[None, :])).astype(BF16)
    blk = same_chunk.astype(BF16)
    return pl.pallas_call(
        _mix0_seq_body,
        grid=(b, t // tt),
        in_specs=[pl.BlockSpec((1, tt, D_MODEL), lambda i, j: (i, j, 0)),
                  pl.BlockSpec((1, D_MODEL), lambda i, j: (0, 0)), _full_spec(w_in), _full_spec(w_out)]
                 + [_full_spec(p) for p in prm] + [_full_spec(tril), _full_spec(blk)],
        out_specs=[pl.BlockSpec((1, tt, D_MODEL), lambda i, j: (i, j, 0)),
                   pl.BlockSpec((1, RWKV_HEADS, RWKV_HEAD, RWKV_HEAD), lambda i, j: (i, 0, 0, 0)),
                   pl.BlockSpec((1, POOL_HALO, POOL_WIDTH), lambda i, j: (i, 0, 0)),
                   pl.BlockSpec((1, SHIFT_HALO, SHIFT_WIDTH), lambda i, j: (i, 0, 0))],
        out_shape=[jax.ShapeDtypeStruct((b, t, D_MODEL), F32),
                   jax.ShapeDtypeStruct((b, RWKV_HEADS, RWKV_HEAD, RWKV_HEAD), F32),
                   jax.ShapeDtypeStruct((b, POOL_HALO, POOL_WIDTH), F32),
                   jax.ShapeDtypeStruct((b, SHIFT_HALO, SHIFT_WIDTH), F32)],
        scratch_shapes=[pltpu.VMEM((tt + POOL_HALO, POOL_WIDTH), F32),
                        pltpu.VMEM((tt + SHIFT_HALO, SHIFT_WIDTH), F32),
                        pltpu.VMEM((RWKV_WIDTH // GROUP_LANES, RWKV_HEAD, GROUP_LANES), F32),
                        pltpu.VMEM((tt, D_MODEL), F32)],
        compiler_params=pltpu.CompilerParams(dimension_semantics=("parallel", "arbitrary")),
        name="mix0_seq",
    )(x, g.reshape(1, D_MODEL), w_in, w_out, *prm, tril, blk)


def _to_leading(x):
    n = x.shape[0]
    ii = lax.broadcasted_iota(jnp.int32, (n, n, 1), 0)
    jj = lax.broadcasted_iota(jnp.int32, (n, n, 1), 1)
    return jnp.sum(jnp.where(ii == jj, x[None, :, :], 0.0), axis=1, keepdims=True)


def _from_leading(x3):
    n = x3.shape[0]
    ii = lax.broadcasted_iota(jnp.int32, (n, n, 1), 0)
    jj = lax.broadcasted_iota(jnp.int32, (n, n, 1), 1)
    return jnp.sum(jnp.where(ii == jj, x3, 0.0), axis=0)


def _mix0_step_body(h_ref, pool_prev_ref, shift_prev_ref, s_in_ref,
                    pool_w_ref, pool_scale_ref, mu_ref, w0_ref, w2_ref, a0_ref, a2_ref, g2_ref,
                    kk_ref, ka_ref, rk_ref, lnw_ref, lnb_ref, bdiag_ref,
                    o_ref, s_out_ref,
                    r_t, w_t, k_t, v_t, kk_t, bt_t, y_t, r_s, k_s, v_s, g_s):
    hh = pl.program_id(0)
    nb = h_ref.shape[0]

    @pl.when(hh == 0)
    def _():
        for g, win in enumerate(POOL_WINDOWS):
            cs = slice(g * POOL_GC, (g + 1) * POOL_GC)
            u = h_ref[:, cs]
            acc = u
            for j in range(1, win):
                acc = acc + pool_prev_ref[POOL_BUF - j][:, cs]
            cnt = jnp.float32(min(win, PAST_LEN + 1))
            o_ref[:, cs] = _pool_group_out(acc, cnt, u, pool_w_ref, pool_scale_ref, g)
        prm = (mu_ref, w0_ref, w2_ref, a0_ref, a2_ref, g2_ref, kk_ref, ka_ref, bdiag_ref)
        r, lw, kmod, v, kk, a, gate = _rwkv_prep(h_ref[:, POOL_WIDTH:IN_AB_WIDTH], shift_prev_ref[...], prm)
        r_s[...] = r
        k_s[...] = kmod
        v_s[...] = v
        g_s[...] = gate
        r_t[...] = r.T
        w_t[...] = jnp.exp(lw).T
        k_t[...] = kmod.T
        v_t[...] = v.T
        kk_t[...] = kk.T
        bt_t[...] = (kk * a).T

    rows = pl.ds(pl.multiple_of(hh * RWKV_HEAD, RWKV_HEAD), RWKV_HEAD)
    s = s_in_ref[...].T.reshape(RWKV_HEAD, RWKV_HEAD, nb)
    kk = kk_t[rows, :][None]
    s_kk = jnp.sum(s * kk, axis=1, keepdims=True)
    v3 = _to_leading(v_t[rows, :])
    s = s * w_t[rows, :][None] - s_kk * bt_t[rows, :][None] + v3 * k_t[rows, :][None]
    y3 = jnp.sum(s * r_t[rows, :][None], axis=1, keepdims=True)
    y_t[rows, :] = _from_leading(y3)
    s_out_ref[...] = s.reshape(RWKV_HEAD * RWKV_HEAD, nb).T

    @pl.when(hh == pl.num_programs(0) - 1)
    def _():
        o_ref[:, POOL_WIDTH:] = _rwkv_finish(y_t[...].T, r_s[...], k_s[...], v_s[...], g_s[...],
                                             rk_ref, lnw_ref, lnb_ref, bdiag_ref)


def mix0_step(h, pool_prev, shift_prev, s_prev, P):
    b = h.shape[0]
    prm = _mix0_params(P)
    hw = RWKV_HEAD * RWKV_HEAD
    s2 = s_prev.reshape(b, RWKV_HEADS * hw)
    pool_t = jnp.swapaxes(pool_prev, 0, 1)
    tvec = lambda: pltpu.VMEM((RWKV_WIDTH, b), F32)
    svec = lambda: pltpu.VMEM((b, RWKV_WIDTH), F32)
    out, s_new = pl.pallas_call(
        _mix0_step_body,
        grid=(RWKV_HEADS,),
        in_specs=[_full_spec(h), _full_spec(pool_t), _full_spec(shift_prev),
                  pl.BlockSpec((b, hw), lambda i: (0, i))] + [_full_spec(x) for x in prm],
        out_specs=[pl.BlockSpec((b, D_MODEL), lambda i: (0, 0)),
                   pl.BlockSpec((b, hw), lambda i: (0, i))],
        out_shape=[jax.ShapeDtypeStruct((b, D_MODEL), F32),
                   jax.ShapeDtypeStruct((b, RWKV_HEADS * hw), F32)],
        scratch_shapes=[tvec(), tvec(), tvec(), tvec(), tvec(), tvec(), tvec(),
                        svec(), svec(), svec(), svec()],
        compiler_params=pltpu.CompilerParams(dimension_semantics=("arbitrary",)),
        name="mix0_step",
    )(h, pool_t, shift_prev, s2, *prm)
    return out, s_new.reshape(b, RWKV_HEADS, RWKV_HEAD, RWKV_HEAD)


def _conv_seq_body(x_ref, g_ref, wi_ref, cw_ref, wo_ref, o_ref, tail_ref, ext):
    t = pl.program_id(1)
    tt = o_ref.shape[1]
    c = D_MODEL

    @pl.when(t == 0)
    def _():
        ext[0:SHIFT_HALO, :] = jnp.zeros((SHIFT_HALO, c), F32)

    x = x_ref[0]
    h = jnp.dot(_rmsnorm(x, g_ref[...]).astype(BF16), wi_ref[...], preferred_element_type=F32)
    ext[SHIFT_HALO:SHIFT_HALO + tt, :] = h[:, c:2 * c] * h[:, 2 * c:3 * c]
    z = cw_ref[0:1, :] * ext[SHIFT_HALO - 2:SHIFT_HALO - 2 + tt, :]
    z = z + cw_ref[1:2, :] * ext[SHIFT_HALO - 1:SHIFT_HALO - 1 + tt, :]
    z = z + cw_ref[2:3, :] * ext[SHIFT_HALO:SHIFT_HALO + tt, :]
    gated = (h[:, 0:c] * z).astype(BF16)
    o_ref[0] = x + jnp.dot(gated, wo_ref[...], preferred_element_type=F32)
    ext[0:SHIFT_HALO, :] = ext[tt:tt + SHIFT_HALO, :]

    @pl.when(t == pl.num_programs(1) - 1)
    def _():
        tail_ref[0] = ext[SHIFT_HALO - (CONV_WIDTH - 1):SHIFT_HALO, :]


def conv_seq(x, g, w_in, conv_w, w_out):
    b, t, c = x.shape
    tt = _row_tile(t, 512)
    return pl.pallas_call(
        _conv_seq_body,
        grid=(b, t // tt),
        in_specs=[pl.BlockSpec((1, tt, c), lambda i, j: (i, j, 0)),
                  pl.BlockSpec((1, c), lambda i, j: (0, 0)),
                  _full_spec(w_in), _full_spec(conv_w), _full_spec(w_out)],
        out_specs=[pl.BlockSpec((1, tt, c), lambda i, j: (i, j, 0)),
                   pl.BlockSpec((1, CONV_WIDTH - 1, c), lambda i, j: (i, 0, 0))],
        out_shape=[jax.ShapeDtypeStruct((b, t, c), F32),
                   jax.ShapeDtypeStruct((b, CONV_WIDTH - 1, c), F32)],
        scratch_shapes=[pltpu.VMEM((tt + SHIFT_HALO, c), F32)],
        compiler_params=pltpu.CompilerParams(dimension_semantics=("parallel", "arbitrary")),
        name="conv_seq",
    )(x, g.reshape(1, c), w_in, conv_w, w_out)


def _conv_step_body(h_ref, x_ref, p0_ref, p1_ref, cw_ref, wo_ref, o_ref, e_ref):
    c = D_MODEL
    e = h_ref[:, c:2 * c] * h_ref[:, 2 * c:3 * c]
    z = cw_ref[0:1, :] * p0_ref[...] + cw_ref[1:2, :] * p1_ref[...] + cw_ref[2:3, :] * e
    gated = (h_ref[:, 0:c] * z).astype(BF16)
    o_ref[...] = x_ref[...] + jnp.dot(gated, wo_ref[...], preferred_element_type=F32)
    e_ref[...] = e


def conv_step(h, x, prev, conv_w, w_out):
    b = h.shape[0]
    c = D_MODEL
    args = (h, x, prev[:, 0, :], prev[:, 1, :], conv_w, w_out)
    return pl.pallas_call(
        _conv_step_body,
        grid=(1,),
        in_specs=[_full_spec(a) for a in args],
        out_specs=[pl.BlockSpec((b, c), lambda i: (0, 0)), pl.BlockSpec((b, c), lambda i: (0, 0))],
        out_shape=[jax.ShapeDtypeStruct((b, c), F32), jax.ShapeDtypeStruct((b, c), F32)],
        compiler_params=pltpu.CompilerParams(dimension_semantics=("arbitrary",)),
        name="conv_step",
    )(*args)


def _xattn(x2, i, mem_k, mem_v, W, seq_shape):
    if seq_shape is not None:
        b, t = seq_shape
        return xattn_seq(x2.reshape(b, t, D_MODEL), W['norm_xattn'][i], W['w_xq'][i], W['w_xo'][i],
                         mem_k, mem_v).reshape(b * t, D_MODEL)
    return xattn_step(x2, W['norm_xattn'][i], W['w_xq'][i], W['w_xo'][i], mem_k, mem_v, i)


def _ffn0(x2, W):
    return ffn_dense(x2, W['norm_ffn'][0], W['ffn_gate'][0], W['ffn_up'][0], W['ffn_down'][0], FFN_TILE)


def _trunk_seq(x, mem_k, mem_v, W):
    b, t, d = x.shape
    P0 = {k: v[0] for k, v in W['mix0'].items()}
    x1, wkv, pool_tail, shift_tail = mix0_seq(x, W['norm_mix'][0], W['w_in_ab'][0], W['w_out_ab'][0], P0)
    pool = pool_tail[:, POOL_HALO - POOL_BUF:]
    shift = shift_tail[:, SHIFT_HALO - 1]
    x2 = _ffn0(_xattn(x1.reshape(b * t, d), 0, mem_k[0], mem_v[0], W, (b, t)), W)
    x3, conv = conv_seq(x2.reshape(b, t, d), W['norm_mix'][1], W['w_in_c'][0], W['conv_w'][0], W['w_out_c'][0])
    x2 = _xattn(x3.reshape(b * t, d), 1, mem_k[1], mem_v[1], W, (b, t))
    return x2, pool[None], shift[None], wkv[None], conv[None]


def _trunk_step(x, mem_k, mem_v, pool_prev, shift_prev, wkv_prev, conv_prev, W):
    b, _, d = x.shape
    x2 = x.reshape(b, d)
    P0 = {k: v[0] for k, v in W['mix0'].items()}
    h = norm_matmul(x2, W['norm_mix'][0], W['w_in_ab'][0])
    mix, wkv = mix0_step(h, pool_prev[0], shift_prev[0], wkv_prev[0], P0)
    pool = jnp.concatenate([pool_prev[0][:, 1:], h[:, None, :POOL_WIDTH]], axis=1)
    shift = h[:, POOL_WIDTH:]
    x2 = matmul_res(mix, W['w_out_ab'][0], x2)
    x2 = _ffn0(_xattn(x2, 0, mem_k, mem_v, W, None), W)
    h = norm_matmul(x2, W['norm_mix'][1], W['w_in_c'][0])
    x2, e = conv_step(h, x2, conv_prev[0], W['conv_w'][0], W['w_out_c'][0])
    conv = jnp.concatenate([conv_prev[0][:, 1:], e[:, None]], axis=1)
    x2 = _xattn(x2, 1, mem_k, mem_v, W, None)
    return x2, pool[None], shift[None], wkv[None], conv[None]


def kernel(x_prompt, x_sample, mem_prompt, cache_mem_k, cache_mem_v, state_pool, state_shift, state_wkv, state_conv, norm_mix, norm_xattn, norm_mem, norm_ffn, norm_final, w_xq, w_xk, w_xv, w_xo, w_in_ab, pool_w, pool_scale, mu_shift, rw_w0, rw_w2, rw_a0, rw_a2, rw_g2, rw_kk, rw_ka, rw_rk, rw_lnx_w, rw_lnx_b, w_out_ab, ffn_gate, ffn_up, ffn_down, w_in_c, conv_w, w_out_c, router_w, router_b, moe_gate, moe_up, moe_down):
    depth = norm_mix.shape[0]
    assert depth == 2 and w_in_ab.shape[0] == 1 and w_in_c.shape[0] == 1
    bp = x_prompt.shape[0]
    bs = x_sample.shape[0]
    d = D_MODEL
    bf = lambda w: w.astype(BF16)
    W = dict(norm_mix=norm_mix, norm_xattn=norm_xattn, norm_ffn=norm_ffn, norm_final=norm_final,
             w_xq=bf(w_xq), w_xo=bf(w_xo), w_in_ab=bf(w_in_ab), w_out_ab=bf(w_out_ab),
             ffn_gate=bf(ffn_gate), ffn_up=bf(ffn_up), ffn_down=bf(ffn_down),
             w_in_c=bf(w_in_c), conv_w=conv_w, w_out_c=bf(w_out_c),
             router_w=router_w, router_b=router_b,
             moe_gate=bf(moe_gate), moe_up=bf(moe_up), moe_down=bf(moe_down),
             mix0=dict(pool_w=pool_w, pool_scale=pool_scale, mu_shift=mu_shift, rw_w0=rw_w0, rw_w2=rw_w2,
                       rw_a0=rw_a0, rw_a2=rw_a2, rw_g2=rw_g2, rw_kk=rw_kk, rw_ka=rw_ka, rw_rk=rw_rk,
                       rw_lnx_w=rw_lnx_w, rw_lnx_b=rw_lnx_b))

    mem2 = mem_prompt.reshape(bp * N_MEM, d)
    mk, mv = [], []
    for i in range(depth):
        wkv_i = jnp.concatenate([bf(w_xk[i]), bf(w_xv[i])], axis=1)
        kv = norm_matmul(mem2, norm_mem[i], wkv_i)
        mk.append(kv[:, :d].reshape(bp, N_MEM, d))
        mv.append(kv[:, d:].reshape(bp, N_MEM, d))

    x_p, pool_p, shift_p, wkv_p, conv_p = _trunk_seq(x_prompt, mk, mv, W)
    x_s, pool_s, shift_s, wkv_s, conv_s = _trunk_step(x_sample, cache_mem_k, cache_mem_v, state_pool,
                                                       state_shift, state_wkv, state_conv, W)
    y_p, y_s = moe_top2_norm([x_p, x_s], norm_ffn[1], router_w[0], router_b[0], W['moe_gate'][0],
                             W['moe_up'][0], W['moe_down'][0], norm_final)
    y_p = y_p.reshape(x_prompt.shape)
    y_s = y_s.reshape(x_sample.shape)
    mem_k_p = jnp.stack(mk).reshape(depth, bp, N_MEM, X_HEADS, X_HEAD_DIM)
    mem_v_p = jnp.stack(mv).reshape(depth, bp, N_MEM, X_HEADS, X_HEAD_DIM)
    return (y_p, y_s, pool_p, pool_s, shift_p, shift_s, wkv_p, wkv_s, conv_p, conv_s, mem_k_p, mem_v_p)
```

```python
import functools

import jax
import jax.numpy as jnp
from jax import lax
from jax.experimental import pallas as pl
from jax.experimental.pallas import tpu as pltpu

F32 = jnp.float32
BF16 = jnp.bfloat16

D_MODEL = 1024
POOL_WIDTH = 512
POOL_GROUPS = 4
POOL_GC = 128
POOL_WINDOWS = (2, 4, 8, 16)
POOL_BUF = 15
RWKV_WIDTH = 512
RWKV_HEAD = 64
RWKV_HEADS = 8
W_RANK = 64
A_RANK = 64
G_RANK = 128
SHIFT_WIDTH = 3 * RWKV_WIDTH + W_RANK + A_RANK + G_RANK
IN_AB_WIDTH = POOL_WIDTH + SHIFT_WIDTH
LNX_EPS = 64e-5
CONV_WIDTH = 3
N_EXPERTS = 8
N_MEM = 256
X_HEADS = 4
X_HEAD_DIM = 256
RMS_EPS = 1e-6
PAST_LEN = 16384

CHUNK = 64
GROUP_LANES = 256
POOL_HALO = 16
SHIFT_HALO = 8
ROUTER_PAD = 128
TOP_K = 2
ROW_SUBLANES = 8
MOE_ROWS = 512
FFN_TILE = 1408
MOE_TILE = 1792
XATTN_STEP_VMEM = 48 * 1024 * 1024


def _rmsnorm(x, g):
    ms = jnp.mean(x * x, axis=-1, keepdims=True)
    return x * lax.rsqrt(ms + RMS_EPS) * g


def _dot(a, b):
    return jnp.dot(a.astype(BF16), b.astype(BF16), preferred_element_type=F32)


def _dot_nt(a, b):
    return lax.dot_general(a.astype(BF16), b.astype(BF16), (((1,), (1,)), ((), ())),
                           preferred_element_type=F32)


def _dot_tn(a, b):
    return lax.dot_general(a.astype(BF16), b.astype(BF16), (((0,), (0,)), ((), ())),
                           preferred_element_type=F32)


def _split3(x):
    hi = x.astype(BF16)
    r1 = x - hi.astype(F32)
    mid = r1.astype(BF16)
    lo = (r1 - mid.astype(F32)).astype(BF16)
    return hi, mid, lo


def _dot_exact_lhs(m01, x):
    hi, mid, lo = _split3(x)
    f = lambda p: jnp.dot(m01, p, preferred_element_type=F32)
    return f(hi) + f(mid) + f(lo)


def _softplus(x):
    return jnp.maximum(x, 0.0) + jnp.log1p(jnp.exp(-jnp.abs(x)))


def _sigmoid(x):
    return 1.0 / (1.0 + jnp.exp(-x))


def _row_tile(m, want):
    t = min(m, want)
    assert m % t == 0, (m, t)
    return t


def _norm_matmul_body(x_ref, g_ref, w_ref, o_ref):
    xn = _rmsnorm(x_ref[...], g_ref[...]).astype(BF16)
    o_ref[...] = jnp.dot(xn, w_ref[...], preferred_element_type=F32)


def norm_matmul(x, g, w):
    m, k = x.shape
    n = w.shape[1]
    tm = _row_tile(m, 512)
    return pl.pallas_call(
        _norm_matmul_body,
        grid=(m // tm,),
        in_specs=[pl.BlockSpec((tm, k), lambda i: (i, 0)),
                  pl.BlockSpec((1, k), lambda i: (0, 0)),
                  pl.BlockSpec((k, n), lambda i: (0, 0))],
        out_specs=pl.BlockSpec((tm, n), lambda i: (i, 0)),
        out_shape=jax.ShapeDtypeStruct((m, n), F32),
        compiler_params=pltpu.CompilerParams(dimension_semantics=("parallel",)),
        name="norm_matmul",
    )(x, g.reshape(1, k), w)


def _matmul_res_body(a_ref, w_ref, r_ref, o_ref):
    o_ref[...] = r_ref[...] + jnp.dot(a_ref[...].astype(BF16), w_ref[...], preferred_element_type=F32)


def matmul_res(a, w, res):
    m, k = a.shape
    n = w.shape[1]
    tm = _row_tile(m, 512)
    return pl.pallas_call(
        _matmul_res_body,
        grid=(m // tm,),
        in_specs=[pl.BlockSpec((tm, k), lambda i: (i, 0)),
                  pl.BlockSpec((k, n), lambda i: (0, 0)),
                  pl.BlockSpec((tm, n), lambda i: (i, 0))],
        out_specs=pl.BlockSpec((tm, n), lambda i: (i, 0)),
        out_shape=jax.ShapeDtypeStruct((m, n), F32),
        compiler_params=pltpu.CompilerParams(dimension_semantics=("parallel",)),
        name="matmul_res",
    )(a, w, res)


def _ffn_body(x_ref, g_ref, wg_ref, wu_ref, wd_ref, o_ref, xn_ref, acc_ref):
    f = pl.program_id(1)

    @pl.when(f == 0)
    def _():
        xn_ref[...] = _rmsnorm(x_ref[...], g_ref[...]).astype(BF16)
        acc_ref[...] = jnp.zeros_like(acc_ref)

    xn = xn_ref[...]
    gate = jnp.dot(xn, wg_ref[...], preferred_element_type=F32)
    up = jnp.dot(xn, wu_ref[...], preferred_element_type=F32)
    hid = (gate * _sigmoid(gate) * up).astype(BF16)
    acc_ref[...] += jnp.dot(hid, wd_ref[...], preferred_element_type=F32)

    @pl.when(f == pl.num_programs(1) - 1)
    def _():
        o_ref[...] = x_ref[...] + acc_ref[...]


def ffn_dense(x, g, wg, wu, wd, tf):
    m, d = x.shape
    ff = wg.shape[1]
    tm = _row_tile(m, 512)
    return pl.pallas_call(
        _ffn_body,
        grid=(m // tm, ff // tf),
        in_specs=[pl.BlockSpec((tm, d), lambda i, f: (i, 0)),
                  pl.BlockSpec((1, d), lambda i, f: (0, 0)),
                  pl.BlockSpec((d, tf), lambda i, f: (0, f)),
                  pl.BlockSpec((d, tf), lambda i, f: (0, f)),
                  pl.BlockSpec((tf, d), lambda i, f: (f, 0))],
        out_specs=pl.BlockSpec((tm, d), lambda i, f: (i, 0)),
        out_shape=jax.ShapeDtypeStruct((m, d), F32),
        scratch_shapes=[pltpu.VMEM((tm, d), BF16), pltpu.VMEM((tm, d), F32)],
        compiler_params=pltpu.CompilerParams(dimension_semantics=("parallel", "arbitrary")),
        name="ffn_dense",
    )(x, g.reshape(1, d), wg, wu, wd)


def _router_body(x_ref, g_ref, w_ref, b_ref, oi_ref, ow_ref):
    xn = _rmsnorm(x_ref[...], g_ref[...])
    logits = _dot_3pass(xn, w_ref[...]) + b_ref[...]
    lane = lax.broadcasted_iota(jnp.int32, logits.shape, 1)
    neg = jnp.float32(-jnp.inf)
    logits = jnp.where(lane < N_EXPERTS, logits, neg)
    m1 = jnp.max(logits, axis=-1, keepdims=True)
    i1 = jnp.min(jnp.where(logits == m1, lane, ROUTER_PAD), axis=-1, keepdims=True)
    rest = jnp.where(lane == i1, neg, logits)
    m2 = jnp.max(rest, axis=-1, keepdims=True)
    i2 = jnp.min(jnp.where(rest == m2, lane, ROUTER_PAD), axis=-1, keepdims=True)
    e2 = jnp.exp(m2 - m1)
    den = 1.0 + e2
    slot = lax.broadcasted_iota(jnp.int32, oi_ref.shape, 1)
    oi_ref[...] = jnp.where(slot == 0, i1, i2)
    ow_ref[...] = jnp.where(slot == 0, 1.0 / den, e2 / den)


def _dot_3pass(x, w):
    xh = x.astype(BF16)
    xm = (x - xh.astype(F32)).astype(BF16)
    wh = w.astype(BF16)
    wm = (w - wh.astype(F32)).astype(BF16)
    f = lambda p, q: jnp.dot(p, q, preferred_element_type=F32)
    return f(xh, wh) + (f(xh, wm) + f(xm, wh))


def moe_router(x, g, router_w, router_b):
    m, d = x.shape
    tm = _row_tile(m, 512)
    w = jnp.zeros((d, ROUTER_PAD), F32).at[:, :N_EXPERTS].set(router_w)
    b = jnp.zeros((1, ROUTER_PAD), F32).at[0, :N_EXPERTS].set(router_b)
    return pl.pallas_call(
        _router_body,
        grid=(m // tm,),
        in_specs=[pl.BlockSpec((tm, d), lambda i: (i, 0)),
                  pl.BlockSpec((1, d), lambda i: (0, 0)),
                  pl.BlockSpec((d, ROUTER_PAD), lambda i: (0, 0)),
                  pl.BlockSpec((1, ROUTER_PAD), lambda i: (0, 0))],
        out_specs=[pl.BlockSpec((tm, TOP_K), lambda i: (i, 0)), pl.BlockSpec((tm, TOP_K), lambda i: (i, 0))],
        out_shape=[jax.ShapeDtypeStruct((m, TOP_K), jnp.int32), jax.ShapeDtypeStruct((m, TOP_K), F32)],
        compiler_params=pltpu.CompilerParams(dimension_semantics=("parallel",)),
        name="moe_router",
    )(x, g.reshape(1, d), w, b)


def _row_tiles_to_2d(ref, slot, n):
    return jnp.concatenate([ref[slot, pl.ds(j, n, stride=ROW_SUBLANES), :] for j in range(ROW_SUBLANES)],
                           axis=1)


def _row_copy(src_hbm, row, dst, slot, i, sem):
    return pltpu.make_async_copy(src_hbm.at[pl.ds(row * ROW_SUBLANES, ROW_SUBLANES)],
                                 dst.at[slot, pl.ds(i * ROW_SUBLANES, ROW_SUBLANES)], sem.at[slot])


def _gather_start(src_hbm, rows_smem, dst, slot, n, sem):
    def body(i, carry):
        for k in range(2):
            r = 2 * i + k
            _row_copy(src_hbm, rows_smem[0, 0, r], dst, slot, r, sem).start(priority=k)
        return carry
    lax.fori_loop(0, n // 2, body, 0, unroll=4)


def _gather_wait(src_hbm, dst, slot, n, sem):
    pltpu.make_async_copy(src_hbm.at[pl.ds(0, n * ROW_SUBLANES)], dst.at[slot], sem.at[slot]).wait()


def _moe_group_body(te_ref, nu_ref, tok0_ref, tokn_ref, x8_ref, g_ref, wg_ref, wu_ref, wd_ref,
                    o_ref, xbuf, sem, xn_ref, acc_ref, *, nf):
    t = pl.program_id(0)
    f = pl.program_id(1)
    tm = xn_ref.shape[0]
    per_step = tm // nf
    used = t < nu_ref[0]
    slot = t % 2

    @pl.when(jnp.logical_and(t == 0, f == 0))
    def _():
        _gather_start(x8_ref, tok0_ref, xbuf, 0, tm, sem)

    @pl.when(jnp.logical_and(used, f == 0))
    def _():
        _gather_wait(x8_ref, xbuf, slot, tm, sem)
        x = _row_tiles_to_2d(xbuf, slot, tm)
        xn_ref[...] = _rmsnorm(x, g_ref[...]).astype(BF16)
        acc_ref[...] = jnp.zeros_like(acc_ref)

    @pl.when(used)
    def _():
        base = f * per_step
        for i in range(per_step):
            _row_copy(x8_ref, tokn_ref[0, 0, base + i], xbuf, 1 - slot, base + i, sem).start()
        xn = xn_ref[...]
        gate = jnp.dot(xn, wg_ref[0], preferred_element_type=F32)
        up = jnp.dot(xn, wu_ref[0], preferred_element_type=F32)
        hid = (gate * _sigmoid(gate) * up).astype(BF16)
        acc_ref[...] += jnp.dot(hid, wd_ref[0], preferred_element_type=F32)

    @pl.when(f == nf - 1)
    def _():
        y = jnp.where(used, acc_ref[...], 0.0)
        for j in range(ROW_SUBLANES):
            o_ref[pl.ds(j, tm, stride=ROW_SUBLANES), :] = y[:, j * 128:(j + 1) * 128]

    @pl.when(jnp.logical_and(t == nu_ref[0] - 1, f == nf - 1))
    def _():
        _gather_wait(x8_ref, xbuf, 1 - slot, tm, sem)


def moe_grouped(x8, g, tile_expert, n_used, row_token, wg, wu, wd, tf):
    n_tiles, _, tm = row_token.shape
    d = D_MODEL
    ne, _, ff = wg.shape
    nf = ff // tf
    assert tm % nf == 0
    live_f = lambda t, f, nu: jnp.where(t < nu[0], f, nf - 1)
    smem_rows = lambda imap: pl.BlockSpec((1, 1, tm), imap, memory_space=pltpu.SMEM)
    grid_spec = pltpu.PrefetchScalarGridSpec(
        num_scalar_prefetch=2,
        grid=(n_tiles, nf),
        in_specs=[smem_rows(lambda t, f, te, nu: (0, 0, 0)),
                  smem_rows(lambda t, f, te, nu: (jnp.minimum(t + 1, nu[0] - 1), 0, 0)),
                  pl.BlockSpec(memory_space=pl.ANY),
                  pl.BlockSpec((1, d), lambda t, f, te, nu: (0, 0)),
                  pl.BlockSpec((1, d, tf), lambda t, f, te, nu: (te[t], 0, live_f(t, f, nu))),
                  pl.BlockSpec((1, d, tf), lambda t, f, te, nu: (te[t], 0, live_f(t, f, nu))),
                  pl.BlockSpec((1, tf, d), lambda t, f, te, nu: (te[t], live_f(t, f, nu), 0))],
        out_specs=pl.BlockSpec((tm * ROW_SUBLANES, 128), lambda t, f, te, nu: (t, 0)),
        scratch_shapes=[pltpu.VMEM((2, tm * ROW_SUBLANES, 128), F32), pltpu.SemaphoreType.DMA((2,)),
                        pltpu.VMEM((tm, d), BF16), pltpu.VMEM((tm, d), F32)])
    return pl.pallas_call(
        functools.partial(_moe_group_body, nf=nf),
        grid_spec=grid_spec,
        out_shape=jax.ShapeDtypeStruct((n_tiles * tm * ROW_SUBLANES, 128), F32),
        compiler_params=pltpu.CompilerParams(dimension_semantics=("arbitrary", "arbitrary")),
        name="moe_grouped",
    )(tile_expert, n_used, row_token, row_token, x8, g.reshape(1, d), wg, wu, wd)


def _moe_combine_body(pos0_ref, posn_ref, y8_ref, x_ref, w_ref, g_ref, o_ref, ybuf, sem):
    t = pl.program_id(0)
    tc = x_ref.shape[0]
    n = TOP_K * tc

    @pl.when(t == 0)
    def _():
        _gather_start(y8_ref, pos0_ref, ybuf, 0, n, sem)

    @pl.when(t + 1 < pl.num_programs(0))
    def _():
        _gather_start(y8_ref, posn_ref, ybuf, (t + 1) % 2, n, sem)

    slot = t % 2
    _gather_wait(y8_ref, ybuf, slot, n, sem)
    y = _row_tiles_to_2d(ybuf, slot, n)
    w = w_ref[...]
    moe = w[:, 0:1] * y[0:tc] + w[:, 1:2] * y[tc:n]
    o_ref[...] = _rmsnorm(x_ref[...] + moe, g_ref[...])


def moe_combine_norm(y8, x, pos, topw, g):
    m, d = x.shape
    nt, _, n = pos.shape
    tc = n // TOP_K
    assert nt * tc == m
    smem_rows = lambda imap: pl.BlockSpec((1, 1, n), imap, memory_space=pltpu.SMEM)
    return pl.pallas_call(
        _moe_combine_body,
        grid=(nt,),
        in_specs=[smem_rows(lambda t: (0, 0, 0)),
                  smem_rows(lambda t: (jnp.minimum(t + 1, nt - 1), 0, 0)),
                  pl.BlockSpec(memory_space=pl.ANY),
                  pl.BlockSpec((tc, d), lambda t: (t, 0)),
                  pl.BlockSpec((tc, TOP_K), lambda t: (t, 0)),
                  pl.BlockSpec((1, d), lambda t: (0, 0))],
        out_specs=pl.BlockSpec((tc, d), lambda t: (t, 0)),
        out_shape=jax.ShapeDtypeStruct((m, d), F32),
        scratch_shapes=[pltpu.VMEM((2, n * ROW_SUBLANES, 128), F32), pltpu.SemaphoreType.DMA((2,))],
        compiler_params=pltpu.CompilerParams(dimension_semantics=("arbitrary",)),
        name="moe_combine_norm",
    )(pos, pos, y8, x, topw, g.reshape(1, d))


def _moe_plan(topi, tm):
    n = topi.shape[0]
    a = TOP_K * n
    n_tiles = (a + N_EXPERTS * (tm - 1) + tm - 1) // tm
    e = topi.reshape(a)
    eid = jnp.arange(N_EXPERTS, dtype=jnp.int32)
    onehot = (e[:, None] == eid[None, :]).astype(jnp.int32)
    csum = jnp.cumsum(onehot, axis=0)
    cnt = csum[-1]
    rank = jnp.sum((csum - onehot) * onehot, axis=1)
    tiles = (cnt + tm - 1) // tm
    tile_end = jnp.cumsum(tiles)
    tile_start = tile_end - tiles
    n_used = tile_end[-1]
    pos = (jnp.sum(onehot * tile_start[None, :], axis=1) * tm + rank).reshape(n, TOP_K)
    tid = jnp.minimum(jnp.arange(n_tiles, dtype=jnp.int32), n_used - 1)
    tile_expert = jnp.sum(tid[:, None] >= tile_end[None, :], axis=1).astype(jnp.int32)
    fill = tiles * tm - cnt
    j = jnp.arange(tm - 1, dtype=jnp.int32)
    dummy_key = jnp.where(j[None, :] < fill[:, None], eid[:, None], N_EXPERTS).reshape(-1)
    dummy_tok = jnp.tile(j % n, N_EXPERTS)
    spare = n_tiles * tm - a - dummy_key.shape[0]
    keys = jnp.concatenate([e, dummy_key, jnp.full((spare,), N_EXPERTS, jnp.int32)])
    toks = jnp.concatenate([jnp.arange(a, dtype=jnp.int32) // TOP_K, dummy_tok,
                            jnp.zeros((spare,), jnp.int32)])
    _, row_token = lax.sort((keys, toks), num_keys=1, is_stable=True)
    return tile_expert, n_used.reshape(1).astype(jnp.int32), row_token.reshape(n_tiles, 1, tm), pos


def moe_top2_norm(xs, g, router_w, router_b, wg, wu, wd, g_final):
    tops = [moe_router(x, g, router_w, router_b) for x in xs]
    topi = jnp.concatenate([t[0] for t in tops], axis=0)
    tile_expert, n_used, row_token, pos = _moe_plan(topi, MOE_ROWS)
    x8 = jnp.concatenate([x.reshape(-1, 128) for x in xs], axis=0)
    y8 = moe_grouped(x8, g, tile_expert, n_used, row_token, wg, wu, wd, MOE_TILE)
    outs, start = [], 0
    for x, (_, topw) in zip(xs, tops):
        m = x.shape[0]
        tc = _row_tile(m, 256)
        p = pos[start:start + m].reshape(m // tc, tc, TOP_K)
        p = jnp.swapaxes(p, 1, 2).reshape(m // tc, 1, TOP_K * tc)
        outs.append(moe_combine_norm(y8, x, p, topw, g_final))
        start += m
    return outs


def _xattn_seq_body(x_ref, g_ref, wq_ref, wo_ref, k_ref, v_ref, o_ref):
    x = x_ref[0]
    xn = _rmsnorm(x, g_ref[...]).astype(BF16)
    q = jnp.dot(xn, wq_ref[...], preferred_element_type=F32).astype(BF16)
    kb = k_ref[0].astype(BF16)
    vb = v_ref[0].astype(BF16)
    heads = []
    for h in range(X_HEADS):
        sl = slice(h * X_HEAD_DIM, (h + 1) * X_HEAD_DIM)
        s = _dot_nt(q[:, sl], kb[:, sl]) * (X_HEAD_DIM ** -0.5)
        mx = jnp.max(s, axis=-1, keepdims=True)
        ex = jnp.exp(s - mx)
        pr = ex / jnp.sum(ex, axis=-1, keepdims=True)
        heads.append(jnp.dot(pr.astype(BF16), vb[:, sl], preferred_element_type=F32))
    att = jnp.concatenate(heads, axis=-1).astype(BF16)
    o_ref[0] = x + jnp.dot(att, wo_ref[...], preferred_element_type=F32)


def xattn_seq(x, g, wq, wo, mem_k, mem_v):
    b, t, d = x.shape
    tq = _row_tile(t, 512)
    return pl.pallas_call(
        _xattn_seq_body,
        grid=(b, t // tq),
        in_specs=[pl.BlockSpec((1, tq, d), lambda i, j: (i, j, 0)),
                  pl.BlockSpec((1, d), lambda i, j: (0, 0)),
                  pl.BlockSpec((d, d), lambda i, j: (0, 0)),
                  pl.BlockSpec((d, d), lambda i, j: (0, 0)),
                  pl.BlockSpec((1, N_MEM, d), lambda i, j: (i, 0, 0)),
                  pl.BlockSpec((1, N_MEM, d), lambda i, j: (i, 0, 0))],
        out_specs=pl.BlockSpec((1, tq, d), lambda i, j: (i, j, 0)),
        out_shape=jax.ShapeDtypeStruct((b, t, d), F32),
        compiler_params=pltpu.CompilerParams(dimension_semantics=("parallel", "parallel")),
        name="xattn_seq",
    )(x, g.reshape(1, d), wq, wo, mem_k, mem_v)


XATTN_STEP_ROWS = 8
XATTN_STEP_KV = 4


def _xattn_step_body(x_ref, g_ref, wq_ref, wo_ref, k_ref, v_ref, o_ref, q_s, att_s):
    j = pl.program_id(1)

    @pl.when(j == 0)
    def _():
        xn = _rmsnorm(x_ref[...], g_ref[...]).astype(BF16)
        q_s[...] = jnp.dot(xn, wq_ref[...], preferred_element_type=F32) * (X_HEAD_DIM ** -0.5)

    for b in range(XATTN_STEP_KV):
        rsel = pl.ds(j * XATTN_STEP_KV + b, 1)
        q4 = jnp.concatenate([q_s[rsel, h * X_HEAD_DIM:(h + 1) * X_HEAD_DIM] for h in range(X_HEADS)], axis=0)
        s = jnp.sum(k_ref[0, b] * q4[None], axis=-1, keepdims=True)
        mx = jnp.max(s, axis=0, keepdims=True)
        ex = jnp.exp(s - mx)
        den = jnp.sum(ex, axis=0)
        o4 = jnp.sum(ex * v_ref[0, b], axis=0) / den
        for h in range(X_HEADS):
            att_s[rsel, h * X_HEAD_DIM:(h + 1) * X_HEAD_DIM] = o4[h:h + 1, :]

    @pl.when(j == pl.num_programs(1) - 1)
    def _():
        o_ref[...] = x_ref[...] + jnp.dot(att_s[...].astype(BF16), wo_ref[...], preferred_element_type=F32)


def xattn_step(x, g, wq, wo, mem_k, mem_v, layer):
    b, d = x.shape
    rows, kvb = XATTN_STEP_ROWS, XATTN_STEP_KV
    assert b % rows == 0 and rows % kvb == 0
    nj = rows // kvb
    kv_spec = pl.BlockSpec((1, kvb, N_MEM, X_HEADS, X_HEAD_DIM), lambda i, j: (layer, i * nj + j, 0, 0, 0))
    return pl.pallas_call(
        _xattn_step_body,
        grid=(b // rows, nj),
        in_specs=[pl.BlockSpec((rows, d), lambda i, j: (i, 0)),
                  pl.BlockSpec((1, d), lambda i, j: (0, 0)),
                  pl.BlockSpec((d, d), lambda i, j: (0, 0)),
                  pl.BlockSpec((d, d), lambda i, j: (0, 0)),
                  kv_spec, kv_spec],
        out_specs=pl.BlockSpec((rows, d), lambda i, j: (i, 0)),
        out_shape=jax.ShapeDtypeStruct((b, d), F32),
        scratch_shapes=[pltpu.VMEM((rows, d), F32), pltpu.VMEM((rows, d), F32)],
        compiler_params=pltpu.CompilerParams(dimension_semantics=("parallel", "arbitrary"),
                                             vmem_limit_bytes=XATTN_STEP_VMEM),
        name="xattn_step",
    )(x, g.reshape(1, d), wq, wo, mem_k, mem_v)


def _head_sum(x, bdiag_ref):
    m = bdiag_ref[0:GROUP_LANES, 0:GROUP_LANES]
    parts = []
    for g in range(RWKV_WIDTH // GROUP_LANES):
        xs = x[:, g * GROUP_LANES:(g + 1) * GROUP_LANES]
        hi = xs.astype(BF16)
        lo = (xs - hi.astype(F32)).astype(BF16)
        parts.append(jnp.dot(hi, m, preferred_element_type=F32) + jnp.dot(lo, m, preferred_element_type=F32))
    return jnp.concatenate(parts, axis=1)


def _pool_group_out(acc, cnt, u, pool_w_ref, pool_scale_ref, g):
    cs = slice(g * POOL_GC, (g + 1) * POOL_GC)
    d = acc / cnt - u
    return _dot(d, pool_w_ref[g]) * pool_scale_ref[:, cs]


def _rwkv_prep(p, shifted, prm):
    (mu_ref, w0_ref, w2_ref, a0_ref, a2_ref, g2_ref, kk_ref, ka_ref, bdiag_ref) = prm
    m = p + (shifted - p) * mu_ref[...]
    c = RWKV_WIDTH
    r = m[:, 0:c]
    k = m[:, c:2 * c]
    v = m[:, 2 * c:3 * c]
    dwa = m[:, 3 * c:3 * c + W_RANK + A_RANK]
    dg = m[:, 3 * c + W_RANK + A_RANK:]
    w_log = -_softplus(-(w0_ref[...] + _dot(jnp.tanh(dwa), w2_ref[...]))) - 0.5
    lw = -jnp.exp(w_log)
    a = _sigmoid(a0_ref[...] + _dot(dwa, a2_ref[...]))
    gate = _dot(_sigmoid(dg), g2_ref[...])
    kk = k * kk_ref[...]
    kk = kk / jnp.maximum(jnp.sqrt(_head_sum(kk * kk, bdiag_ref)), 1e-12)
    kmod = k * (1.0 + (a - 1.0) * ka_ref[...])
    return r, lw, kmod, v, kk, a, gate


def _rwkv_finish(y, r, kmod, v, gate, rk_ref, lnw_ref, lnb_ref, bdiag_ref):
    mu = _head_sum(y, bdiag_ref) * (1.0 / RWKV_HEAD)
    dlt = y - mu
    var = _head_sum(dlt * dlt, bdiag_ref) * (1.0 / RWKV_HEAD)
    yn = dlt * lax.rsqrt(var + LNX_EPS) * lnw_ref[...] + lnb_ref[...]
    bonus = _head_sum(r * kmod * rk_ref[...], bdiag_ref) * v
    return (yn + bonus) * gate


def _chains(x, nc):
    return [x[c * CHUNK:(c + 1) * CHUNK, g * GROUP_LANES:(g + 1) * GROUP_LANES]
            for g in range(RWKV_WIDTH // GROUP_LANES) for c in range(nc)]


def _block_diag(x, m4):
    xb = x.astype(BF16)
    return jnp.concatenate([xb] * (GROUP_LANES // RWKV_HEAD), axis=0) * m4


def _mix0_seq_body(x_ref, gmix_ref, win_ref, wout_ref,
                   pool_w_ref, pool_scale_ref, mu_ref, w0_ref, w2_ref, a0_ref, a2_ref, g2_ref,
                   kk_ref, ka_ref, rk_ref, lnw_ref, lnb_ref, bdiag_ref, tril_ref, blk_ref,
                   o_ref, s_out_ref, pool_tail_ref, shift_tail_ref,
                   ext_u, ext_p, s_ref, ycat):
    t = pl.program_id(1)
    tt = o_ref.shape[1]
    nc = tt // CHUNK
    ng = RWKV_WIDTH // GROUP_LANES

    @pl.when(t == 0)
    def _():
        ext_u[0:POOL_HALO, :] = jnp.zeros((POOL_HALO, POOL_WIDTH), F32)
        ext_p[0:SHIFT_HALO, :] = jnp.zeros((SHIFT_HALO, SHIFT_WIDTH), F32)
        s_ref[...] = jnp.zeros_like(s_ref)

    x = x_ref[0]
    h = jnp.dot(_rmsnorm(x, gmix_ref[...]).astype(BF16), win_ref[...], preferred_element_type=F32)
    ext_u[POOL_HALO:POOL_HALO + tt, :] = h[:, 0:POOL_WIDTH]
    ext_p[SHIFT_HALO:SHIFT_HALO + tt, :] = h[:, POOL_WIDTH:IN_AB_WIDTH]

    pos = t * tt + lax.broadcasted_iota(jnp.int32, (tt, 1), 0)
    for g, win in enumerate(POOL_WINDOWS):
        cs = slice(g * POOL_GC, (g + 1) * POOL_GC)
        u = ext_u[POOL_HALO:POOL_HALO + tt, cs]
        acc = u
        for j in range(1, win):
            acc = acc + ext_u[POOL_HALO - j:POOL_HALO - j + tt, cs]
        cnt = jnp.minimum(win, pos + 1).astype(F32)
        ycat[:, cs] = _pool_group_out(acc, cnt, u, pool_w_ref, pool_scale_ref, g)

    p = ext_p[SHIFT_HALO:SHIFT_HALO + tt, :]
    shifted = ext_p[SHIFT_HALO - 1:SHIFT_HALO - 1 + tt, :]
    prm = (mu_ref, w0_ref, w2_ref, a0_ref, a2_ref, g2_ref, kk_ref, ka_ref, bdiag_ref)
    r, lw, kmod, v, kk, a, gate = _rwkv_prep(p, shifted, prm)

    ext_u[0:POOL_HALO, :] = ext_u[tt:tt + POOL_HALO, :]
    ext_p[0:SHIFT_HALO, :] = ext_p[tt:tt + SHIFT_HALO, :]

    cl = _dot_exact_lhs(tril_ref[...], lw)
    tot = _dot_exact_lhs(blk_ref[...], lw)
    gam = jnp.exp(cl)
    gam_inv = jnp.exp(-cl)
    gam_end = jnp.exp(tot)
    beta = kk * a
    kd_full = kmod * gam_inv
    bd_full = beta * gam_inv
    kq_c = _chains(kk * jnp.exp(cl - lw), nc)
    rq_c = _chains(r * gam, nc)
    kd_c = _chains(kd_full, nc)
    bd_c = _chains(bd_full, nc)
    v_c = _chains(v, nc)
    kdg_c = _chains(kd_full * gam_end, nc)
    bdg_c = _chains(bd_full * gam_end, nc)
    ge_c = _chains(gam_end, nc)
    n_ch = ng * nc
    every = range(n_ch)

    m4 = bdiag_ref[0:GROUP_LANES, 0:GROUP_LANES]
    m4f = m4.astype(F32)
    row = lax.broadcasted_iota(jnp.int32, (CHUNK, GROUP_LANES), 0)
    col = lax.broadcasted_iota(jnp.int32, (CHUNK, GROUP_LANES), 1) % RWKV_HEAD
    strict = row > col
    incl = row >= col
    eye_c = jnp.where(row == col, 1.0, 0.0)
    eye_g = (lax.broadcasted_iota(jnp.int32, (GROUP_LANES, GROUP_LANES), 0)
             == lax.broadcasted_iota(jnp.int32, (GROUP_LANES, GROUP_LANES), 1))

    bd_kd = [_block_diag(kd_c[i], m4) for i in every]
    bd_bd = [_block_diag(bd_c[i], m4) for i in every]
    lhs = [jnp.concatenate([kq_c[i], rq_c[i]], axis=0) for i in every]
    pk = [_dot_nt(lhs[i], bd_kd[i]) for i in every]
    pb = [_dot_nt(lhs[i], bd_bd[i]) for i in every]
    a_k = [jnp.where(strict, pk[i][:CHUNK], 0.0) for i in every]
    p_k = [jnp.where(incl, pk[i][CHUNK:], 0.0) for i in every]
    a_b = [jnp.where(strict, pb[i][:CHUNK], 0.0) for i in every]
    p_b = [jnp.where(incl, pb[i][CHUNK:], 0.0) for i in every]
    tm = [eye_c - a_b[i] for i in every]
    apow = a_b
    bd_ap = [_block_diag(apow[i], m4) for i in every]
    n = 1
    while 2 * n < CHUNK:
        apow = [_dot(apow[i], bd_ap[i]) for i in every]
        bd_ap = [_block_diag(apow[i], m4) for i in every]
        tm = [tm[i] + _dot(tm[i], bd_ap[i]) for i in every]
        n *= 2
    bd_v = [_block_diag(v_c[i], m4) for i in every]
    t_kq = [_dot(tm[i], _block_diag(kq_c[i], m4)) for i in every]
    akv = [_dot(a_k[i], bd_v[i]) for i in every]
    t_akv = [_dot(tm[i], _block_diag(akv[i], m4)) for i in every]
    r_y = [rq_c[i] - _dot(p_b[i], _block_diag(t_kq[i], m4)) for i in every]
    y0 = [_dot(p_k[i], bd_v[i]) - _dot(p_b[i], _block_diag(t_akv[i], m4)) for i in every]
    g_m = [m4f * (jnp.where(eye_g, ge_c[i][0:1, :], 0.0) - _dot_tn(t_kq[i], bdg_c[i])) for i in every]
    h_full = [m4f * _dot_tn(jnp.concatenate([v_c[i], t_akv[i]], axis=0),
                            jnp.concatenate([kdg_c[i], -bdg_c[i]], axis=0)) for i in every]
    h_t = [sum(h_full[i][j * RWKV_HEAD:(j + 1) * RWKV_HEAD] for j in range(GROUP_LANES // RWKV_HEAD))
           for i in every]

    ys = [None] * n_ch
    for g in range(ng):
        s = s_ref[g]
        for c in range(nc):
            i = g * nc + c
            ys[i] = _dot_nt(r_y[i], _block_diag(s, m4)) + y0[i]
            s = _dot(s, g_m[i]) + h_t[i]
        s_ref[g] = s
    y = jnp.concatenate([jnp.concatenate([ys[g * nc + c] for g in range(ng)], axis=1) for c in range(nc)],
                        axis=0)

    ycat[:, POOL_WIDTH:] = _rwkv_finish(y, r, kmod, v, gate, rk_ref, lnw_ref, lnb_ref, bdiag_ref)
    o_ref[0] = x + jnp.dot(ycat[...].astype(BF16), wout_ref[...], preferred_element_type=F32)

    @pl.when(t == pl.num_programs(1) - 1)
    def _():
        pool_tail_ref[0] = ext_u[0:POOL_HALO, :]
        shift_tail_ref[0] = ext_p[0:SHIFT_HALO, :]
        for hh in range(RWKV_HEADS):
            g, j = divmod(hh, GROUP_LANES // RWKV_HEAD)
            s_out_ref[0, hh] = s_ref[g][:, j * RWKV_HEAD:(j + 1) * RWKV_HEAD]


def _mix0_params(P):
    c = RWKV_WIDTH
    row = lambda x: x.reshape(1, -1).astype(F32)
    w2 = jnp.zeros((W_RANK + A_RANK, c), F32).at[:W_RANK].set(P['rw_w2']).astype(BF16)
    a2 = jnp.zeros((W_RANK + A_RANK, c), F32).at[W_RANK:].set(P['rw_a2']).astype(BF16)
    hid = jnp.arange(c) // RWKV_HEAD
    bdiag = (hid[:, None] == hid[None, :]).astype(BF16)
    return (P['pool_w'].astype(BF16), row(P['pool_scale']), row(P['mu_shift']), row(P['rw_w0']), w2,
            row(P['rw_a0']), a2, P['rw_g2'].astype(BF16), row(P['rw_kk']), row(P['rw_ka']),
            row(P['rw_rk']), row(P['rw_lnx_w']), row(P['rw_lnx_b']), bdiag)


def _full_spec(x):
    nd = x.ndim
    return pl.BlockSpec(x.shape, lambda *_: (0,) * nd)


def mix0_seq(x, g, w_in, w_out, P):
    b, t, _ = x.shape
    tt = _row_tile(t, 256)
    assert tt % CHUNK == 0 and tt >= POOL_HALO
    prm = _mix0_params(P)
    ti = jnp.arange(tt)
    same_chunk = (ti[:, None] // CHUNK) == (ti[None, :] // CHUNK)
    tril = (same_chunk & (ti[:, None] >= ti[None, :])).astype(BF16)
    blk = same_chunk.astype(BF16)
    return pl.pallas_call(
        _mix0_seq_body,
        grid=(b, t // tt),
        in_specs=[pl.BlockSpec((1, tt, D_MODEL), lambda i, j: (i, j, 0)),
                  pl.BlockSpec((1, D_MODEL), lambda i, j: (0, 0)), _full_spec(w_in), _full_spec(w_out)]
                 + [_full_spec(p) for p in prm] + [_full_spec(tril), _full_spec(blk)],
        out_specs=[pl.BlockSpec((1, tt, D_MODEL), lambda i, j: (i, j, 0)),
                   pl.BlockSpec((1, RWKV_HEADS, RWKV_HEAD, RWKV_HEAD), lambda i, j: (i, 0, 0, 0)),
                   pl.BlockSpec((1, POOL_HALO, POOL_WIDTH), lambda i, j: (i, 0, 0)),
                   pl.BlockSpec((1, SHIFT_HALO, SHIFT_WIDTH), lambda i, j: (i, 0, 0))],
        out_shape=[jax.ShapeDtypeStruct((b, t, D_MODEL), F32),
                   jax.ShapeDtypeStruct((b, RWKV_HEADS, RWKV_HEAD, RWKV_HEAD), F32),
                   jax.ShapeDtypeStruct((b, POOL_HALO, POOL_WIDTH), F32),
                   jax.ShapeDtypeStruct((b, SHIFT_HALO, SHIFT_WIDTH), F32)],
        scratch_shapes=[pltpu.VMEM((tt + POOL_HALO, POOL_WIDTH), F32),
                        pltpu.VMEM((tt + SHIFT_HALO, SHIFT_WIDTH), F32),
                        pltpu.VMEM((RWKV_WIDTH // GROUP_LANES, RWKV_HEAD, GROUP_LANES), F32),
                        pltpu.VMEM((tt, D_MODEL), F32)],
        compiler_params=pltpu.CompilerParams(dimension_semantics=("parallel", "arbitrary")),
        name="mix0_seq",
    )(x, g.reshape(1, D_MODEL), w_in, w_out, *prm, tril, blk)


def _to_leading(x):
    n = x.shape[0]
    ii = lax.broadcasted_iota(jnp.int32, (n, n, 1), 0)
    jj = lax.broadcasted_iota(jnp.int32, (n, n, 1), 1)
    return jnp.sum(jnp.where(ii == jj, x[None, :, :], 0.0), axis=1, keepdims=True)


def _from_leading(x3):
    n = x3.shape[0]
    ii = lax.broadcasted_iota(jnp.int32, (n, n, 1), 0)
    jj = lax.broadcasted_iota(jnp.int32, (n, n, 1), 1)
    return jnp.sum(jnp.where(ii == jj, x3, 0.0), axis=0)


def _mix0_step_body(h_ref, pool_prev_ref, shift_prev_ref, s_in_ref,
                    pool_w_ref, pool_scale_ref, mu_ref, w0_ref, w2_ref, a0_ref, a2_ref, g2_ref,
                    kk_ref, ka_ref, rk_ref, lnw_ref, lnb_ref, bdiag_ref,
                    o_ref, s_out_ref,
                    r_t, w_t, k_t, v_t, kk_t, bt_t, y_t, r_s, k_s, v_s, g_s):
    hh = pl.program_id(0)
    nb = h_ref.shape[0]

    @pl.when(hh == 0)
    def _():
        for g, win in enumerate(POOL_WINDOWS):
            cs = slice(g * POOL_GC, (g + 1) * POOL_GC)
            u = h_ref[:, cs]
            acc = u
            for j in range(1, win):
                acc = acc + pool_prev_ref[POOL_BUF - j][:, cs]
            cnt = jnp.float32(min(win, PAST_LEN + 1))
            o_ref[:, cs] = _pool_group_out(acc, cnt, u, pool_w_ref, pool_scale_ref, g)
        prm = (mu_ref, w0_ref, w2_ref, a0_ref, a2_ref, g2_ref, kk_ref, ka_ref, bdiag_ref)
        r, lw, kmod, v, kk, a, gate = _rwkv_prep(h_ref[:, POOL_WIDTH:IN_AB_WIDTH], shift_prev_ref[...], prm)
        r_s[...] = r
        k_s[...] = kmod
        v_s[...] = v
        g_s[...] = gate
        r_t[...] = r.T
        w_t[...] = jnp.exp(lw).T
        k_t[...] = kmod.T
        v_t[...] = v.T
        kk_t[...] = kk.T
        bt_t[...] = (kk * a).T

    rows = pl.ds(pl.multiple_of(hh * RWKV_HEAD, RWKV_HEAD), RWKV_HEAD)
    s = s_in_ref[...].T.reshape(RWKV_HEAD, RWKV_HEAD, nb)
    kk = kk_t[rows, :][None]
    s_kk = jnp.sum(s * kk, axis=1, keepdims=True)
    v3 = _to_leading(v_t[rows, :])
    s = s * w_t[rows, :][None] - s_kk * bt_t[rows, :][None] + v3 * k_t[rows, :][None]
    y3 = jnp.sum(s * r_t[rows, :][None], axis=1, keepdims=True)
    y_t[rows, :] = _from_leading(y3)
    s_out_ref[...] = s.reshape(RWKV_HEAD * RWKV_HEAD, nb).T

    @pl.when(hh == pl.num_programs(0) - 1)
    def _():
        o_ref[:, POOL_WIDTH:] = _rwkv_finish(y_t[...].T, r_s[...], k_s[...], v_s[...], g_s[...],
                                             rk_ref, lnw_ref, lnb_ref, bdiag_ref)


def mix0_step(h, pool_prev, shift_prev, s_prev, P):
    b = h.shape[0]
    prm = _mix0_params(P)
    hw = RWKV_HEAD * RWKV_HEAD
    s2 = s_prev.reshape(b, RWKV_HEADS * hw)
    pool_t = jnp.swapaxes(pool_prev, 0, 1)
    tvec = lambda: pltpu.VMEM((RWKV_WIDTH, b), F32)
    svec = lambda: pltpu.VMEM((b, RWKV_WIDTH), F32)
    out, s_new = pl.pallas_call(
        _mix0_step_body,
        grid=(RWKV_HEADS,),
        in_specs=[_full_spec(h), _full_spec(pool_t), _full_spec(shift_prev),
                  pl.BlockSpec((b, hw), lambda i: (0, i))] + [_full_spec(x) for x in prm],
        out_specs=[pl.BlockSpec((b, D_MODEL), lambda i: (0, 0)),
                   pl.BlockSpec((b, hw), lambda i: (0, i))],
        out_shape=[jax.ShapeDtypeStruct((b, D_MODEL), F32),
                   jax.ShapeDtypeStruct((b, RWKV_HEADS * hw), F32)],
        scratch_shapes=[tvec(), tvec(), tvec(), tvec(), tvec(), tvec(), tvec(),
                        svec(), svec(), svec(), svec()],
        compiler_params=pltpu.CompilerParams(dimension_semantics=("arbitrary",)),
        name="mix0_step",
    )(h, pool_t, shift_prev, s2, *prm)
    return out, s_new.reshape(b, RWKV_HEADS, RWKV_HEAD, RWKV_HEAD)


def _conv_seq_body(x_ref, g_ref, wi_ref, cw_ref, wo_ref, o_ref, tail_ref, ext):
    t = pl.program_id(1)
    tt = o_ref.shape[1]
    c = D_MODEL

    @pl.when(t == 0)
    def _():
        ext[0:SHIFT_HALO, :] = jnp.zeros((SHIFT_HALO, c), F32)

    x = x_ref[0]
    h = jnp.dot(_rmsnorm(x, g_ref[...]).astype(BF16), wi_ref[...], preferred_element_type=F32)
    ext[SHIFT_HALO:SHIFT_HALO + tt, :] = h[:, c:2 * c] * h[:, 2 * c:3 * c]
    z = cw_ref[0:1, :] * ext[SHIFT_HALO - 2:SHIFT_HALO - 2 + tt, :]
    z = z + cw_ref[1:2, :] * ext[SHIFT_HALO - 1:SHIFT_HALO - 1 + tt, :]
    z = z + cw_ref[2:3, :] * ext[SHIFT_HALO:SHIFT_HALO + tt, :]
    gated = (h[:, 0:c] * z).astype(BF16)
    o_ref[0] = x + jnp.dot(gated, wo_ref[...], preferred_element_type=F32)
    ext[0:SHIFT_HALO, :] = ext[tt:tt + SHIFT_HALO, :]

    @pl.when(t == pl.num_programs(1) - 1)
    def _():
        tail_ref[0] = ext[SHIFT_HALO - (CONV_WIDTH - 1):SHIFT_HALO, :]


def conv_seq(x, g, w_in, conv_w, w_out):
    b, t, c = x.shape
    tt = _row_tile(t, 512)
    return pl.pallas_call(
        _conv_seq_body,
        grid=(b, t // tt),
        in_specs=[pl.BlockSpec((1, tt, c), lambda i, j: (i, j, 0)),
                  pl.BlockSpec((1, c), lambda i, j: (0, 0)),
                  _full_spec(w_in), _full_spec(conv_w), _full_spec(w_out)],
        out_specs=[pl.BlockSpec((1, tt, c), lambda i, j: (i, j, 0)),
                   pl.BlockSpec((1, CONV_WIDTH - 1, c), lambda i, j: (i, 0, 0))],
        out_shape=[jax.ShapeDtypeStruct((b, t, c), F32),
                   jax.ShapeDtypeStruct((b, CONV_WIDTH - 1, c), F32)],
        scratch_shapes=[pltpu.VMEM((tt + SHIFT_HALO, c), F32)],
        compiler_params=pltpu.CompilerParams(dimension_semantics=("parallel", "arbitrary")),
        name="conv_seq",
    )(x, g.reshape(1, c), w_in, conv_w, w_out)


def _conv_step_body(h_ref, x_ref, p0_ref, p1_ref, cw_ref, wo_ref, o_ref, e_ref):
    c = D_MODEL
    e = h_ref[:, c:2 * c] * h_ref[:, 2 * c:3 * c]
    z = cw_ref[0:1, :] * p0_ref[...] + cw_ref[1:2, :] * p1_ref[...] + cw_ref[2:3, :] * e
    gated = (h_ref[:, 0:c] * z).astype(BF16)
    o_ref[...] = x_ref[...] + jnp.dot(gated, wo_ref[...], preferred_element_type=F32)
    e_ref[...] = e


def conv_step(h, x, prev, conv_w, w_out):
    b = h.shape[0]
    c = D_MODEL
    args = (h, x, prev[:, 0, :], prev[:, 1, :], conv_w, w_out)
    return pl.pallas_call(
        _conv_step_body,
        grid=(1,),
        in_specs=[_full_spec(a) for a in args],
        out_specs=[pl.BlockSpec((b, c), lambda i: (0, 0)), pl.BlockSpec((b, c), lambda i: (0, 0))],
        out_shape=[jax.ShapeDtypeStruct((b, c), F32), jax.ShapeDtypeStruct((b, c), F32)],
        compiler_params=pltpu.CompilerParams(dimension_semantics=("arbitrary",)),
        name="conv_step",
    )(*args)


def _xattn(x2, i, mem_k, mem_v, W, seq_shape):
    if seq_shape is not None:
        b, t = seq_shape
        return xattn_seq(x2.reshape(b, t, D_MODEL), W['norm_xattn'][i], W['w_xq'][i], W['w_xo'][i],
                         mem_k, mem_v).reshape(b * t, D_MODEL)
    return xattn_step(x2, W['norm_xattn'][i], W['w_xq'][i], W['w_xo'][i], mem_k, mem_v, i)


def _ffn0(x2, W):
    return ffn_dense(x2, W['norm_ffn'][0], W['ffn_gate'][0], W['ffn_up'][0], W['ffn_down'][0], FFN_TILE)


def _trunk_seq(x, mem_k, mem_v, W):
    b, t, d = x.shape
    P0 = {k: v[0] for k, v in W['mix0'].items()}
    x1, wkv, pool_tail, shift_tail = mix0_seq(x, W['norm_mix'][0], W['w_in_ab'][0], W['w_out_ab'][0], P0)
    pool = pool_tail[:, POOL_HALO - POOL_BUF:]
    shift = shift_tail[:, SHIFT_HALO - 1]
    x2 = _ffn0(_xattn(x1.reshape(b * t, d), 0, mem_k[0], mem_v[0], W, (b, t)), W)
    x3, conv = conv_seq(x2.reshape(b, t, d), W['norm_mix'][1], W['w_in_c'][0], W['conv_w'][0], W['w_out_c'][0])
    x2 = _xattn(x3.reshape(b * t, d), 1, mem_k[1], mem_v[1], W, (b, t))
    return x2, pool[None], shift[None], wkv[None], conv[None]


def _trunk_step(x, mem_k, mem_v, pool_prev, shift_prev, wkv_prev, conv_prev, W):
    b, _, d = x.shape
    x2 = x.reshape(b, d)
    P0 = {k: v[0] for k, v in W['mix0'].items()}
    h = norm_matmul(x2, W['norm_mix'][0], W['w_in_ab'][0])
    mix, wkv = mix0_step(h, pool_prev[0], shift_prev[0], wkv_prev[0], P0)
    pool = jnp.concatenate([pool_prev[0][:, 1:], h[:, None, :POOL_WIDTH]], axis=1)
    shift = h[:, POOL_WIDTH:]
    x2 = matmul_res(mix, W['w_out_ab'][0], x2)
    x2 = _ffn0(_xattn(x2, 0, mem_k, mem_v, W, None), W)
    h = norm_matmul(x2, W['norm_mix'][1], W['w_in_c'][0])
    x2, e = conv_step(h, x2, conv_prev[0], W['conv_w'][0], W['w_out_c'][0])
    conv = jnp.concatenate([conv_prev[0][:, 1:], e[:, None]], axis=1)
    x2 = _xattn(x2, 1, mem_k, mem_v, W, None)
    return x2, pool[None], shift[None], wkv[None], conv[None]


def kernel(x_prompt, x_sample, mem_prompt, cache_mem_k, cache_mem_v, state_pool, state_shift, state_wkv, state_conv, norm_mix, norm_xattn, norm_mem, norm_ffn, norm_final, w_xq, w_xk, w_xv, w_xo, w_in_ab, pool_w, pool_scale, mu_shift, rw_w0, rw_w2, rw_a0, rw_a2, rw_g2, rw_kk, rw_ka, rw_rk, rw_lnx_w, rw_lnx_b, w_out_ab, ffn_gate, ffn_up, ffn_down, w_in_c, conv_w, w_out_c, router_w, router_b, moe_gate, moe_up, moe_down):
    depth = norm_mix.shape[0]
    assert depth == 2 and w_in_ab.shape[0] == 1 and w_in_c.shape[0] == 1
    bp = x_prompt.shape[0]
    bs = x_sample.shape[0]
    d = D_MODEL
    bf = lambda w: w.astype(BF16)
    W = dict(norm_mix=norm_mix, norm_xattn=norm_xattn, norm_ffn=norm_ffn, norm_final=norm_final,
             w_xq=bf(w_xq), w_xo=bf(w_xo), w_in_ab=bf(w_in_ab), w_out_ab=bf(w_out_ab),
             ffn_gate=bf(ffn_gate), ffn_up=bf(ffn_up), ffn_down=bf(ffn_down),
             w_in_c=bf(w_in_c), conv_w=conv_w, w_out_c=bf(w_out_c),
             router_w=router_w, router_b=router_b,
             moe_gate=bf(moe_gate), moe_up=bf(moe_up), moe_down=bf(moe_down),
             mix0=dict(pool_w=pool_w, pool_scale=pool_scale, mu_shift=mu_shift, rw_w0=rw_w0, rw_w2=rw_w2,
                       rw_a0=rw_a0, rw_a2=rw_a2, rw_g2=rw_g2, rw_kk=rw_kk, rw_ka=rw_ka, rw_rk=rw_rk,
                       rw_lnx_w=rw_lnx_w, rw_lnx_b=rw_lnx_b))

    mem2 = mem_prompt.reshape(bp * N_MEM, d)
    mk, mv = [], []
    for i in range(depth):
        wkv_i = jnp.concatenate([bf(w_xk[i]), bf(w_xv[i])], axis=1)
        kv = norm_matmul(mem2, norm_mem[i], wkv_i)
        mk.append(kv[:, :d].reshape(bp, N_MEM, d))
        mv.append(kv[:, d:].reshape(bp, N_MEM, d))

    x_p, pool_p, shift_p, wkv_p, conv_p = _trunk_seq(x_prompt, mk, mv, W)
    x_s, pool_s, shift_s, wkv_s, conv_s = _trunk_step(x_sample, cache_mem_k, cache_mem_v, state_pool,
                                                       state_shift, state_wkv, state_conv, W)
    y_p, y_s = moe_top2_norm([x_p, x_s], norm_ffn[1], router_w[0], router_b[0], W['moe_gate'][0],
                             W['moe_up'][0], W['moe_down'][0], norm_final)
    y_p = y_p.reshape(x_prompt.shape)
    y_s = y_s.reshape(x_sample.shape)
    mem_k_p = jnp.stack(mk).reshape(depth, bp, N_MEM, X_HEADS, X_HEAD_DIM)
    mem_v_p = jnp.stack(mv).reshape(depth, bp, N_MEM, X_HEADS, X_HEAD_DIM)
    return (y_p, y_s, pool_p, pool_s, shift_p, shift_s, wkv_p, wkv_s, conv_p, conv_s, mem_k_p, mem_v_p)
```

```python
import functools

import jax
import jax.numpy as jnp
from jax import lax
from jax.experimental import pallas as pl
from jax.experimental.pallas import tpu as pltpu

F32 = jnp.float32
BF16 = jnp.bfloat16

D_MODEL = 1024
POOL_WIDTH = 512
POOL_GROUPS = 4
POOL_GC = 128
POOL_WINDOWS = (2, 4, 8, 16)
POOL_BUF = 15
RWKV_WIDTH = 512
RWKV_HEAD = 64
RWKV_HEADS = 8
W_RANK = 64
A_RANK = 64
G_RANK = 128
SHIFT_WIDTH = 3 * RWKV_WIDTH + W_RANK + A_RANK + G_RANK
IN_AB_WIDTH = POOL_WIDTH + SHIFT_WIDTH
LNX_EPS = 64e-5
CONV_WIDTH = 3
N_EXPERTS = 8
N_MEM = 256
X_HEADS = 4
X_HEAD_DIM = 256
RMS_EPS = 1e-6
PAST_LEN = 16384

CHUNK = 64
GROUP_LANES = 256
POOL_HALO = 16
SHIFT_HALO = 8
ROUTER_PAD = 128
TOP_K = 2
ROW_SUBLANES = 8
MOE_ROWS = 512
FFN_TILE = 1408
MOE_TILE = 1792
XATTN_STEP_VMEM = 48 * 1024 * 1024


def _rmsnorm(x, g):
    ms = jnp.mean(x * x, axis=-1, keepdims=True)
    return x * lax.rsqrt(ms + RMS_EPS) * g


def _dot(a, b):
    return jnp.dot(a.astype(BF16), b.astype(BF16), preferred_element_type=F32)


def _dot_nt(a, b):
    return lax.dot_general(a.astype(BF16), b.astype(BF16), (((1,), (1,)), ((), ())),
                           preferred_element_type=F32)


def _dot_tn(a, b):
    return lax.dot_general(a.astype(BF16), b.astype(BF16), (((0,), (0,)), ((), ())),
                           preferred_element_type=F32)


def _split3(x):
    hi = x.astype(BF16)
    r1 = x - hi.astype(F32)
    mid = r1.astype(BF16)
    lo = (r1 - mid.astype(F32)).astype(BF16)
    return hi, mid, lo


def _dot_exact_lhs(m01, x):
    hi, mid, lo = _split3(x)
    f = lambda p: jnp.dot(m01, p, preferred_element_type=F32)
    return f(hi) + f(mid) + f(lo)


def _softplus(x):
    return jnp.maximum(x, 0.0) + jnp.log1p(jnp.exp(-jnp.abs(x)))


def _sigmoid(x):
    return 1.0 / (1.0 + jnp.exp(-x))


def _row_tile(m, want):
    t = min(m, want)
    assert m % t == 0, (m, t)
    return t


def _norm_matmul_body(x_ref, g_ref, w_ref, o_ref):
    xn = _rmsnorm(x_ref[...], g_ref[...]).astype(BF16)
    o_ref[...] = jnp.dot(xn, w_ref[...], preferred_element_type=F32)


def norm_matmul(x, g, w):
    m, k = x.shape
    n = w.shape[1]
    tm = _row_tile(m, 512)
    return pl.pallas_call(
        _norm_matmul_body,
        grid=(m // tm,),
        in_specs=[pl.BlockSpec((tm, k), lambda i: (i, 0)),
                  pl.BlockSpec((1, k), lambda i: (0, 0)),
                  pl.BlockSpec((k, n), lambda i: (0, 0))],
        out_specs=pl.BlockSpec((tm, n), lambda i: (i, 0)),
        out_shape=jax.ShapeDtypeStruct((m, n), F32),
        compiler_params=pltpu.CompilerParams(dimension_semantics=("parallel",)),
        name="norm_matmul",
    )(x, g.reshape(1, k), w)


def _matmul_res_body(a_ref, w_ref, r_ref, o_ref):
    o_ref[...] = r_ref[...] + jnp.dot(a_ref[...].astype(BF16), w_ref[...], preferred_element_type=F32)


def matmul_res(a, w, res):
    m, k = a.shape
    n = w.shape[1]
    tm = _row_tile(m, 512)
    return pl.pallas_call(
        _matmul_res_body,
        grid=(m // tm,),
        in_specs=[pl.BlockSpec((tm, k), lambda i: (i, 0)),
                  pl.BlockSpec((k, n), lambda i: (0, 0)),
                  pl.BlockSpec((tm, n), lambda i: (i, 0))],
        out_specs=pl.BlockSpec((tm, n), lambda i: (i, 0)),
        out_shape=jax.ShapeDtypeStruct((m, n), F32),
        compiler_params=pltpu.CompilerParams(dimension_semantics=("parallel",)),
        name="matmul_res",
    )(a, w, res)


def _ffn_body(x_ref, g_ref, wg_ref, wu_ref, wd_ref, o_ref, xn_ref, acc_ref):
    f = pl.program_id(1)

    @pl.when(f == 0)
    def _():
        xn_ref[...] = _rmsnorm(x_ref[...], g_ref[...]).astype(BF16)
        acc_ref[...] = jnp.zeros_like(acc_ref)

    xn = xn_ref[...]
    gate = jnp.dot(xn, wg_ref[...], preferred_element_type=F32)
    up = jnp.dot(xn, wu_ref[...], preferred_element_type=F32)
    hid = (gate * _sigmoid(gate) * up).astype(BF16)
    acc_ref[...] += jnp.dot(hid, wd_ref[...], preferred_element_type=F32)

    @pl.when(f == pl.num_programs(1) - 1)
    def _():
        o_ref[...] = x_ref[...] + acc_ref[...]


def ffn_dense(x, g, wg, wu, wd, tf):
    m, d = x.shape
    ff = wg.shape[1]
    tm = _row_tile(m, 512)
    return pl.pallas_call(
        _ffn_body,
        grid=(m // tm, ff // tf),
        in_specs=[pl.BlockSpec((tm, d), lambda i, f: (i, 0)),
                  pl.BlockSpec((1, d), lambda i, f: (0, 0)),
                  pl.BlockSpec((d, tf), lambda i, f: (0, f)),
                  pl.BlockSpec((d, tf), lambda i, f: (0, f)),
                  pl.BlockSpec((tf, d), lambda i, f: (f, 0))],
        out_specs=pl.BlockSpec((tm, d), lambda i, f: (i, 0)),
        out_shape=jax.ShapeDtypeStruct((m, d), F32),
        scratch_shapes=[pltpu.VMEM((tm, d), BF16), pltpu.VMEM((tm, d), F32)],
        compiler_params=pltpu.CompilerParams(dimension_semantics=("parallel", "arbitrary")),
        name="ffn_dense",
    )(x, g.reshape(1, d), wg, wu, wd)


def _router_body(x_ref, g_ref, w_ref, b_ref, oi_ref, ow_ref):
    xn = _rmsnorm(_load_row_tiles(x_ref, oi_ref.shape[0]), g_ref[...])
    logits = _dot_3pass(xn, w_ref[...]) + b_ref[...]
    lane = lax.broadcasted_iota(jnp.int32, logits.shape, 1)
    neg = jnp.float32(-jnp.inf)
    logits = jnp.where(lane < N_EXPERTS, logits, neg)
    m1 = jnp.max(logits, axis=-1, keepdims=True)
    i1 = jnp.min(jnp.where(logits == m1, lane, ROUTER_PAD), axis=-1, keepdims=True)
    rest = jnp.where(lane == i1, neg, logits)
    m2 = jnp.max(rest, axis=-1, keepdims=True)
    i2 = jnp.min(jnp.where(rest == m2, lane, ROUTER_PAD), axis=-1, keepdims=True)
    e2 = jnp.exp(m2 - m1)
    den = 1.0 + e2
    slot = lax.broadcasted_iota(jnp.int32, oi_ref.shape, 1)
    oi_ref[...] = jnp.where(slot == 0, i1, i2)
    ow_ref[...] = jnp.where(slot == 0, 1.0 / den, e2 / den)


def _dot_3pass(x, w):
    xh = x.astype(BF16)
    xm = (x - xh.astype(F32)).astype(BF16)
    wh = w.astype(BF16)
    wm = (w - wh.astype(F32)).astype(BF16)
    f = lambda p, q: jnp.dot(p, q, preferred_element_type=F32)
    return f(xh, wh) + (f(xh, wm) + f(xm, wh))


def moe_router(x8, g, router_w, router_b):
    m, d = x8.shape[0] // ROW_SUBLANES, D_MODEL
    tm = _row_tile(m, 512)
    w = jnp.zeros((d, ROUTER_PAD), F32).at[:, :N_EXPERTS].set(router_w)
    b = jnp.zeros((1, ROUTER_PAD), F32).at[0, :N_EXPERTS].set(router_b)
    return pl.pallas_call(
        _router_body,
        grid=(m // tm,),
        in_specs=[pl.BlockSpec((tm * ROW_SUBLANES, 128), lambda i: (i, 0)),
                  pl.BlockSpec((1, d), lambda i: (0, 0)),
                  pl.BlockSpec((d, ROUTER_PAD), lambda i: (0, 0)),
                  pl.BlockSpec((1, ROUTER_PAD), lambda i: (0, 0))],
        out_specs=[pl.BlockSpec((tm, TOP_K), lambda i: (i, 0)), pl.BlockSpec((tm, TOP_K), lambda i: (i, 0))],
        out_shape=[jax.ShapeDtypeStruct((m, TOP_K), jnp.int32), jax.ShapeDtypeStruct((m, TOP_K), F32)],
        compiler_params=pltpu.CompilerParams(dimension_semantics=("parallel",)),
        name="moe_router",
    )(x8, g.reshape(1, d), w, b)


def _row_tiles_to_2d(ref, slot, n):
    return jnp.concatenate([ref[slot, pl.ds(j, n, stride=ROW_SUBLANES), :] for j in range(ROW_SUBLANES)],
                           axis=1)


def _row_copy(src_hbm, row, dst, slot, i, sem):
    return pltpu.make_async_copy(src_hbm.at[pl.ds(row * ROW_SUBLANES, ROW_SUBLANES)],
                                 dst.at[slot, pl.ds(i * ROW_SUBLANES, ROW_SUBLANES)], sem.at[slot])


def _gather_start(src_hbm, rows_smem, dst, slot, n, sem):
    def body(i, carry):
        for k in range(2):
            r = 2 * i + k
            _row_copy(src_hbm, rows_smem[0, 0, r], dst, slot, r, sem).start(priority=k)
        return carry
    lax.fori_loop(0, n // 2, body, 0, unroll=4)


def _gather_wait(src_hbm, dst, slot, n, sem):
    pltpu.make_async_copy(src_hbm.at[pl.ds(0, n * ROW_SUBLANES)], dst.at[slot], sem.at[slot]).wait()


def _moe_group_body(te_ref, nu_ref, tok0_ref, tokn_ref, x8_ref, g_ref, wg_ref, wu_ref, wd_ref,
                    o_ref, xbuf, sem, xn_ref, acc_ref, *, nf):
    t = pl.program_id(0)
    f = pl.program_id(1)
    tm = xn_ref.shape[0]
    per_step = tm // nf
    used = t < nu_ref[0]
    slot = t % 2

    @pl.when(jnp.logical_and(t == 0, f == 0))
    def _():
        _gather_start(x8_ref, tok0_ref, xbuf, 0, tm, sem)

    @pl.when(jnp.logical_and(used, f == 0))
    def _():
        _gather_wait(x8_ref, xbuf, slot, tm, sem)
        x = _row_tiles_to_2d(xbuf, slot, tm)
        xn_ref[...] = _rmsnorm(x, g_ref[...]).astype(BF16)
        acc_ref[...] = jnp.zeros_like(acc_ref)

    @pl.when(used)
    def _():
        base = f * per_step
        for i in range(per_step):
            _row_copy(x8_ref, tokn_ref[0, 0, base + i], xbuf, 1 - slot, base + i, sem).start()
        xn = xn_ref[...]
        gate = jnp.dot(xn, wg_ref[0], preferred_element_type=F32)
        up = jnp.dot(xn, wu_ref[0], preferred_element_type=F32)
        hid = (gate * _sigmoid(gate) * up).astype(BF16)
        acc_ref[...] += jnp.dot(hid, wd_ref[0], preferred_element_type=F32)

    @pl.when(f == nf - 1)
    def _():
        y = jnp.where(used, acc_ref[...], 0.0)
        for j in range(ROW_SUBLANES):
            o_ref[pl.ds(j, tm, stride=ROW_SUBLANES), :] = y[:, j * 128:(j + 1) * 128]

    @pl.when(jnp.logical_and(t == nu_ref[0] - 1, f == nf - 1))
    def _():
        _gather_wait(x8_ref, xbuf, 1 - slot, tm, sem)


def moe_grouped(x8, g, tile_expert, n_used, row_token, wg, wu, wd, tf):
    n_tiles, _, tm = row_token.shape
    d = D_MODEL
    ne, _, ff = wg.shape
    nf = ff // tf
    assert tm % nf == 0
    live_f = lambda t, f, nu: jnp.where(t < nu[0], f, nf - 1)
    smem_rows = lambda imap: pl.BlockSpec((1, 1, tm), imap, memory_space=pltpu.SMEM)
    grid_spec = pltpu.PrefetchScalarGridSpec(
        num_scalar_prefetch=2,
        grid=(n_tiles, nf),
        in_specs=[smem_rows(lambda t, f, te, nu: (0, 0, 0)),
                  smem_rows(lambda t, f, te, nu: (jnp.minimum(t + 1, nu[0] - 1), 0, 0)),
                  pl.BlockSpec(memory_space=pl.ANY),
                  pl.BlockSpec((1, d), lambda t, f, te, nu: (0, 0)),
                  pl.BlockSpec((1, d, tf), lambda t, f, te, nu: (te[t], 0, live_f(t, f, nu))),
                  pl.BlockSpec((1, d, tf), lambda t, f, te, nu: (te[t], 0, live_f(t, f, nu))),
                  pl.BlockSpec((1, tf, d), lambda t, f, te, nu: (te[t], live_f(t, f, nu), 0))],
        out_specs=pl.BlockSpec((tm * ROW_SUBLANES, 128), lambda t, f, te, nu: (t, 0)),
        scratch_shapes=[pltpu.VMEM((2, tm * ROW_SUBLANES, 128), F32), pltpu.SemaphoreType.DMA((2,)),
                        pltpu.VMEM((tm, d), BF16), pltpu.VMEM((tm, d), F32)])
    return pl.pallas_call(
        functools.partial(_moe_group_body, nf=nf),
        grid_spec=grid_spec,
        out_shape=jax.ShapeDtypeStruct((n_tiles * tm * ROW_SUBLANES, 128), F32),
        compiler_params=pltpu.CompilerParams(dimension_semantics=("arbitrary", "arbitrary")),
        name="moe_grouped",
    )(tile_expert, n_used, row_token, row_token, x8, g.reshape(1, d), wg, wu, wd)


def _moe_combine_body(pos0_ref, posn_ref, y8_ref, x_ref, w_ref, g_ref, o_ref, ybuf, sem):
    t = pl.program_id(0)
    tc = o_ref.shape[0]
    n = TOP_K * tc

    @pl.when(t == 0)
    def _():
        _gather_start(y8_ref, pos0_ref, ybuf, 0, n, sem)

    @pl.when(t + 1 < pl.num_programs(0))
    def _():
        _gather_start(y8_ref, posn_ref, ybuf, (t + 1) % 2, n, sem)

    slot = t % 2
    _gather_wait(y8_ref, ybuf, slot, n, sem)
    y = _row_tiles_to_2d(ybuf, slot, n)
    w = w_ref[...]
    moe = w[:, 0:1] * y[0:tc] + w[:, 1:2] * y[tc:n]
    o_ref[...] = _rmsnorm(_load_row_tiles(x_ref, tc) + moe, g_ref[...])


def moe_combine_norm(y8, x8, pos, topw, g):
    m, d = x8.shape[0] // ROW_SUBLANES, D_MODEL
    nt, _, n = pos.shape
    tc = n // TOP_K
    assert nt * tc == m
    smem_rows = lambda imap: pl.BlockSpec((1, 1, n), imap, memory_space=pltpu.SMEM)
    return pl.pallas_call(
        _moe_combine_body,
        grid=(nt,),
        in_specs=[smem_rows(lambda t: (0, 0, 0)),
                  smem_rows(lambda t: (jnp.minimum(t + 1, nt - 1), 0, 0)),
                  pl.BlockSpec(memory_space=pl.ANY),
                  pl.BlockSpec((tc * ROW_SUBLANES, 128), lambda t: (t, 0)),
                  pl.BlockSpec((tc, TOP_K), lambda t: (t, 0)),
                  pl.BlockSpec((1, d), lambda t: (0, 0))],
        out_specs=pl.BlockSpec((tc, d), lambda t: (t, 0)),
        out_shape=jax.ShapeDtypeStruct((m, d), F32),
        scratch_shapes=[pltpu.VMEM((2, n * ROW_SUBLANES, 128), F32), pltpu.SemaphoreType.DMA((2,))],
        compiler_params=pltpu.CompilerParams(dimension_semantics=("arbitrary",)),
        name="moe_combine_norm",
    )(pos, pos, y8, x8, topw, g.reshape(1, d))


def _moe_plan(topi, tm):
    n = topi.shape[0]
    a = TOP_K * n
    n_tiles = (a + N_EXPERTS * (tm - 1) + tm - 1) // tm
    e = topi.reshape(a)
    eid = jnp.arange(N_EXPERTS, dtype=jnp.int32)
    onehot = (e[:, None] == eid[None, :]).astype(jnp.int32)
    csum = jnp.cumsum(onehot, axis=0)
    cnt = csum[-1]
    rank = jnp.sum((csum - onehot) * onehot, axis=1)
    tiles = (cnt + tm - 1) // tm
    tile_end = jnp.cumsum(tiles)
    tile_start = tile_end - tiles
    n_used = tile_end[-1]
    pos = (jnp.sum(onehot * tile_start[None, :], axis=1) * tm + rank).reshape(n, TOP_K)
    tid = jnp.minimum(jnp.arange(n_tiles, dtype=jnp.int32), n_used - 1)
    tile_expert = jnp.sum(tid[:, None] >= tile_end[None, :], axis=1).astype(jnp.int32)
    fill = tiles * tm - cnt
    j = jnp.arange(tm - 1, dtype=jnp.int32)
    dummy_key = jnp.where(j[None, :] < fill[:, None], eid[:, None], N_EXPERTS).reshape(-1)
    dummy_tok = jnp.tile(j % n, N_EXPERTS)
    spare = n_tiles * tm - a - dummy_key.shape[0]
    keys = jnp.concatenate([e, dummy_key, jnp.full((spare,), N_EXPERTS, jnp.int32)])
    toks = jnp.concatenate([jnp.arange(a, dtype=jnp.int32) // TOP_K, dummy_tok,
                            jnp.zeros((spare,), jnp.int32)])
    _, row_token = lax.sort((keys, toks), num_keys=1, is_stable=True)
    return tile_expert, n_used.reshape(1).astype(jnp.int32), row_token.reshape(n_tiles, 1, tm), pos


def moe_top2_norm(xs, g, router_w, router_b, wg, wu, wd, g_final):
    tops = [moe_router(x, g, router_w, router_b) for x in xs]
    topi = jnp.concatenate([t[0] for t in tops], axis=0)
    tile_expert, n_used, row_token, pos = _moe_plan(topi, MOE_ROWS)
    x8 = jnp.concatenate(xs, axis=0)
    y8 = moe_grouped(x8, g, tile_expert, n_used, row_token, wg, wu, wd, MOE_TILE)
    outs, start = [], 0
    for x, (_, topw) in zip(xs, tops):
        m = x.shape[0] // ROW_SUBLANES
        tc = _row_tile(m, 256)
        p = pos[start:start + m].reshape(m // tc, tc, TOP_K)
        p = jnp.swapaxes(p, 1, 2).reshape(m // tc, 1, TOP_K * tc)
        outs.append(moe_combine_norm(y8, x, p, topw, g_final))
        start += m
    return outs


def _store_row_tiles(o_ref, y):
    n = y.shape[0]
    for j in range(ROW_SUBLANES):
        o_ref[pl.ds(j, n, stride=ROW_SUBLANES), :] = y[:, j * 128:(j + 1) * 128]


def _load_row_tiles(ref, n):
    return jnp.concatenate([ref[pl.ds(j, n, stride=ROW_SUBLANES), :] for j in range(ROW_SUBLANES)], axis=1)


def _xattn_seq_body(x_ref, g_ref, wq_ref, wo_ref, k_ref, v_ref, o_ref, *, row_tiles):
    x = x_ref[0]
    xn = _rmsnorm(x, g_ref[...]).astype(BF16)
    q = jnp.dot(xn, wq_ref[...], preferred_element_type=F32).astype(BF16)
    kb = k_ref[0].astype(BF16)
    vb = v_ref[0].astype(BF16)
    heads = []
    for h in range(X_HEADS):
        sl = slice(h * X_HEAD_DIM, (h + 1) * X_HEAD_DIM)
        s = _dot_nt(q[:, sl], kb[:, sl]) * (X_HEAD_DIM ** -0.5)
        mx = jnp.max(s, axis=-1, keepdims=True)
        ex = jnp.exp(s - mx)
        pr = ex / jnp.sum(ex, axis=-1, keepdims=True)
        heads.append(jnp.dot(pr.astype(BF16), vb[:, sl], preferred_element_type=F32))
    att = jnp.concatenate(heads, axis=-1).astype(BF16)
    y = x + jnp.dot(att, wo_ref[...], preferred_element_type=F32)
    if row_tiles:
        _store_row_tiles(o_ref, y)
    else:
        o_ref[0] = y


def xattn_seq(x, g, wq, wo, mem_k, mem_v, row_tiles=False):
    b, t, d = x.shape
    tq = _row_tile(t, 512)
    nq = t // tq
    if row_tiles:
        out_spec = pl.BlockSpec((tq * ROW_SUBLANES, 128), lambda i, j: (i * nq + j, 0))
        out_shape = jax.ShapeDtypeStruct((b * t * ROW_SUBLANES, 128), F32)
    else:
        out_spec = pl.BlockSpec((1, tq, d), lambda i, j: (i, j, 0))
        out_shape = jax.ShapeDtypeStruct((b, t, d), F32)
    return pl.pallas_call(
        functools.partial(_xattn_seq_body, row_tiles=row_tiles),
        grid=(b, nq),
        in_specs=[pl.BlockSpec((1, tq, d), lambda i, j: (i, j, 0)),
                  pl.BlockSpec((1, d), lambda i, j: (0, 0)),
                  pl.BlockSpec((d, d), lambda i, j: (0, 0)),
                  pl.BlockSpec((d, d), lambda i, j: (0, 0)),
                  pl.BlockSpec((1, N_MEM, d), lambda i, j: (i, 0, 0)),
                  pl.BlockSpec((1, N_MEM, d), lambda i, j: (i, 0, 0))],
        out_specs=out_spec,
        out_shape=out_shape,
        compiler_params=pltpu.CompilerParams(dimension_semantics=("parallel", "parallel")),
        name="xattn_seq",
    )(x, g.reshape(1, d), wq, wo, mem_k, mem_v)


XATTN_STEP_ROWS = 8
XATTN_STEP_KV = 4


def _xattn_step_body(x_ref, g_ref, wq_ref, wo_ref, k_ref, v_ref, o_ref, q_s, att_s, *, row_tiles):
    j = pl.program_id(1)

    @pl.when(j == 0)
    def _():
        xn = _rmsnorm(x_ref[...], g_ref[...]).astype(BF16)
        q_s[...] = jnp.dot(xn, wq_ref[...], preferred_element_type=F32) * (X_HEAD_DIM ** -0.5)

    for b in range(XATTN_STEP_KV):
        rsel = pl.ds(j * XATTN_STEP_KV + b, 1)
        q4 = jnp.concatenate([q_s[rsel, h * X_HEAD_DIM:(h + 1) * X_HEAD_DIM] for h in range(X_HEADS)], axis=0)
        s = jnp.sum(k_ref[0, b] * q4[None], axis=-1, keepdims=True)
        mx = jnp.max(s, axis=0, keepdims=True)
        ex = jnp.exp(s - mx)
        den = jnp.sum(ex, axis=0)
        o4 = jnp.sum(ex * v_ref[0, b], axis=0) / den
        for h in range(X_HEADS):
            att_s[rsel, h * X_HEAD_DIM:(h + 1) * X_HEAD_DIM] = o4[h:h + 1, :]

    @pl.when(j == pl.num_programs(1) - 1)
    def _():
        y = x_ref[...] + jnp.dot(att_s[...].astype(BF16), wo_ref[...], preferred_element_type=F32)
        if row_tiles:
            _store_row_tiles(o_ref, y)
        else:
            o_ref[...] = y


def xattn_step(x, g, wq, wo, mem_k, mem_v, layer, row_tiles=False):
    b, d = x.shape
    rows, kvb = XATTN_STEP_ROWS, XATTN_STEP_KV
    assert b % rows == 0 and rows % kvb == 0
    nj = rows // kvb
    kv_spec = pl.BlockSpec((1, kvb, N_MEM, X_HEADS, X_HEAD_DIM), lambda i, j: (layer, i * nj + j, 0, 0, 0))
    if row_tiles:
        out_spec = pl.BlockSpec((rows * ROW_SUBLANES, 128), lambda i, j: (i, 0))
        out_shape = jax.ShapeDtypeStruct((b * ROW_SUBLANES, 128), F32)
    else:
        out_spec = pl.BlockSpec((rows, d), lambda i, j: (i, 0))
        out_shape = jax.ShapeDtypeStruct((b, d), F32)
    return pl.pallas_call(
        functools.partial(_xattn_step_body, row_tiles=row_tiles),
        grid=(b // rows, nj),
        in_specs=[pl.BlockSpec((rows, d), lambda i, j: (i, 0)),
                  pl.BlockSpec((1, d), lambda i, j: (0, 0)),
                  pl.BlockSpec((d, d), lambda i, j: (0, 0)),
                  pl.BlockSpec((d, d), lambda i, j: (0, 0)),
                  kv_spec, kv_spec],
        out_specs=out_spec,
        out_shape=out_shape,
        scratch_shapes=[pltpu.VMEM((rows, d), F32), pltpu.VMEM((rows, d), F32)],
        compiler_params=pltpu.CompilerParams(dimension_semantics=("parallel", "arbitrary"),
                                             vmem_limit_bytes=XATTN_STEP_VMEM),
        name="xattn_step",
    )(x, g.reshape(1, d), wq, wo, mem_k, mem_v)


def _head_sum(x, bdiag_ref):
    m = bdiag_ref[0:GROUP_LANES, 0:GROUP_LANES]
    parts = []
    for g in range(RWKV_WIDTH // GROUP_LANES):
        xs = x[:, g * GROUP_LANES:(g + 1) * GROUP_LANES]
        hi = xs.astype(BF16)
        lo = (xs - hi.astype(F32)).astype(BF16)
        parts.append(jnp.dot(hi, m, preferred_element_type=F32) + jnp.dot(lo, m, preferred_element_type=F32))
    return jnp.concatenate(parts, axis=1)


def _pool_group_out(acc, cnt, u, pool_w_ref, pool_scale_ref, g):
    cs = slice(g * POOL_GC, (g + 1) * POOL_GC)
    d = acc / cnt - u
    return _dot(d, pool_w_ref[g]) * pool_scale_ref[:, cs]


def _rwkv_prep(p, shifted, prm):
    (mu_ref, w0_ref, w2_ref, a0_ref, a2_ref, g2_ref, kk_ref, ka_ref, bdiag_ref) = prm
    m = p + (shifted - p) * mu_ref[...]
    c = RWKV_WIDTH
    r = m[:, 0:c]
    k = m[:, c:2 * c]
    v = m[:, 2 * c:3 * c]
    dwa = m[:, 3 * c:3 * c + W_RANK + A_RANK]
    dg = m[:, 3 * c + W_RANK + A_RANK:]
    w_log = -_softplus(-(w0_ref[...] + _dot(jnp.tanh(dwa), w2_ref[...]))) - 0.5
    lw = -jnp.exp(w_log)
    a = _sigmoid(a0_ref[...] + _dot(dwa, a2_ref[...]))
    gate = _dot(_sigmoid(dg), g2_ref[...])
    kk = k * kk_ref[...]
    kk = kk / jnp.maximum(jnp.sqrt(_head_sum(kk * kk, bdiag_ref)), 1e-12)
    kmod = k * (1.0 + (a - 1.0) * ka_ref[...])
    return r, lw, kmod, v, kk, a, gate


def _rwkv_finish(y, r, kmod, v, gate, rk_ref, lnw_ref, lnb_ref, bdiag_ref):
    mu = _head_sum(y, bdiag_ref) * (1.0 / RWKV_HEAD)
    dlt = y - mu
    var = _head_sum(dlt * dlt, bdiag_ref) * (1.0 / RWKV_HEAD)
    yn = dlt * lax.rsqrt(var + LNX_EPS) * lnw_ref[...] + lnb_ref[...]
    bonus = _head_sum(r * kmod * rk_ref[...], bdiag_ref) * v
    return (yn + bonus) * gate


def _chains(x, nc):
    return [x[c * CHUNK:(c + 1) * CHUNK, g * GROUP_LANES:(g + 1) * GROUP_LANES]
            for g in range(RWKV_WIDTH // GROUP_LANES) for c in range(nc)]


def _block_diag(x, m4):
    xb = x.astype(BF16)
    return jnp.concatenate([xb] * (GROUP_LANES // RWKV_HEAD), axis=0) * m4


def _mix0_seq_body(x_ref, gmix_ref, win_ref, wout_ref,
                   pool_w_ref, pool_scale_ref, mu_ref, w0_ref, w2_ref, a0_ref, a2_ref, g2_ref,
                   kk_ref, ka_ref, rk_ref, lnw_ref, lnb_ref, bdiag_ref, tril_ref, blk_ref,
                   o_ref, s_out_ref, pool_tail_ref, shift_tail_ref,
                   ext_u, ext_p, s_ref, ycat):
    t = pl.program_id(1)
    tt = o_ref.shape[1]
    nc = tt // CHUNK
    ng = RWKV_WIDTH // GROUP_LANES

    @pl.when(t == 0)
    def _():
        ext_u[0:POOL_HALO, :] = jnp.zeros((POOL_HALO, POOL_WIDTH), F32)
        ext_p[0:SHIFT_HALO, :] = jnp.zeros((SHIFT_HALO, SHIFT_WIDTH), F32)
        s_ref[...] = jnp.zeros_like(s_ref)

    x = x_ref[0]
    h = jnp.dot(_rmsnorm(x, gmix_ref[...]).astype(BF16), win_ref[...], preferred_element_type=F32)
    ext_u[POOL_HALO:POOL_HALO + tt, :] = h[:, 0:POOL_WIDTH]
    ext_p[SHIFT_HALO:SHIFT_HALO + tt, :] = h[:, POOL_WIDTH:IN_AB_WIDTH]

    pos = t * tt + lax.broadcasted_iota(jnp.int32, (tt, 1), 0)
    for g, win in enumerate(POOL_WINDOWS):
        cs = slice(g * POOL_GC, (g + 1) * POOL_GC)
        u = ext_u[POOL_HALO:POOL_HALO + tt, cs]
        acc = u
        for j in range(1, win):
            acc = acc + ext_u[POOL_HALO - j:POOL_HALO - j + tt, cs]
        cnt = jnp.minimum(win, pos + 1).astype(F32)
        ycat[:, cs] = _pool_group_out(acc, cnt, u, pool_w_ref, pool_scale_ref, g)

    p = ext_p[SHIFT_HALO:SHIFT_HALO + tt, :]
    shifted = ext_p[SHIFT_HALO - 1:SHIFT_HALO - 1 + tt, :]
    prm = (mu_ref, w0_ref, w2_ref, a0_ref, a2_ref, g2_ref, kk_ref, ka_ref, bdiag_ref)
    r, lw, kmod, v, kk, a, gate = _rwkv_prep(p, shifted, prm)

    ext_u[0:POOL_HALO, :] = ext_u[tt:tt + POOL_HALO, :]
    ext_p[0:SHIFT_HALO, :] = ext_p[tt:tt + SHIFT_HALO, :]

    cl = _dot_exact_lhs(tril_ref[...], lw)
    tot = _dot_exact_lhs(blk_ref[...], lw)
    gam = jnp.exp(cl)
    gam_inv = jnp.exp(-cl)
    gam_end = jnp.exp(tot)
    beta = kk * a
    kd_full = kmod * gam_inv
    bd_full = beta * gam_inv
    kq_c = _chains(kk * jnp.exp(cl - lw), nc)
    rq_c = _chains(r * gam, nc)
    kd_c = _chains(kd_full, nc)
    bd_c = _chains(bd_full, nc)
    v_c = _chains(v, nc)
    kdg_c = _chains(kd_full * gam_end, nc)
    bdg_c = _chains(bd_full * gam_end, nc)
    ge_c = _chains(gam_end, nc)
    n_ch = ng * nc
    every = range(n_ch)

    m4 = bdiag_ref[0:GROUP_LANES, 0:GROUP_LANES]
    m4f = m4.astype(F32)
    row = lax.broadcasted_iota(jnp.int32, (CHUNK, GROUP_LANES), 0)
    col = lax.broadcasted_iota(jnp.int32, (CHUNK, GROUP_LANES), 1) % RWKV_HEAD
    strict = row > col
    incl = row >= col
    eye_c = jnp.where(row == col, 1.0, 0.0)
    eye_g = (lax.broadcasted_iota(jnp.int32, (GROUP_LANES, GROUP_LANES), 0)
             == lax.broadcasted_iota(jnp.int32, (GROUP_LANES, GROUP_LANES), 1))

    bd_kd = [_block_diag(kd_c[i], m4) for i in every]
    bd_bd = [_block_diag(bd_c[i], m4) for i in every]
    lhs = [jnp.concatenate([kq_c[i], rq_c[i]], axis=0) for i in every]
    pk = [_dot_nt(lhs[i], bd_kd[i]) for i in every]
    pb = [_dot_nt(lhs[i], bd_bd[i]) for i in every]
    a_k = [jnp.where(strict, pk[i][:CHUNK], 0.0) for i in every]
    p_k = [jnp.where(incl, pk[i][CHUNK:], 0.0) for i in every]
    a_b = [jnp.where(strict, pb[i][:CHUNK], 0.0) for i in every]
    p_b = [jnp.where(incl, pb[i][CHUNK:], 0.0) for i in every]
    tm = [eye_c - a_b[i] for i in every]
    apow = a_b
    bd_ap = [_block_diag(apow[i], m4) for i in every]
    n = 1
    while 2 * n < CHUNK:
        apow = [_dot(apow[i], bd_ap[i]) for i in every]
        bd_ap = [_block_diag(apow[i], m4) for i in every]
        tm = [tm[i] + _dot(tm[i], bd_ap[i]) for i in every]
        n *= 2
    bd_v = [_block_diag(v_c[i], m4) for i in every]
    t_kq = [_dot(tm[i], _block_diag(kq_c[i], m4)) for i in every]
    akv = [_dot(a_k[i], bd_v[i]) for i in every]
    t_akv = [_dot(tm[i], _block_diag(akv[i], m4)) for i in every]
    r_y = [rq_c[i] - _dot(p_b[i], _block_diag(t_kq[i], m4)) for i in every]
    y0 = [_dot(p_k[i], bd_v[i]) - _dot(p_b[i], _block_diag(t_akv[i], m4)) for i in every]
    g_m = [m4f * (jnp.where(eye_g, ge_c[i][0:1, :], 0.0) - _dot_tn(t_kq[i], bdg_c[i])) for i in every]
    h_full = [m4f * _dot_tn(jnp.concatenate([v_c[i], t_akv[i]], axis=0),
                            jnp.concatenate([kdg_c[i], -bdg_c[i]], axis=0)) for i in every]
    h_t = [sum(h_full[i][j * RWKV_HEAD:(j + 1) * RWKV_HEAD] for j in range(GROUP_LANES // RWKV_HEAD))
           for i in every]

    ys = [None] * n_ch
    for g in range(ng):
        s = s_ref[g]
        for c in range(nc):
            i = g * nc + c
            ys[i] = _dot_nt(r_y[i], _block_diag(s, m4)) + y0[i]
            s = _dot(s, g_m[i]) + h_t[i]
        s_ref[g] = s
    y = jnp.concatenate([jnp.concatenate([ys[g * nc + c] for g in range(ng)], axis=1) for c in range(nc)],
                        axis=0)

    ycat[:, POOL_WIDTH:] = _rwkv_finish(y, r, kmod, v, gate, rk_ref, lnw_ref, lnb_ref, bdiag_ref)
    o_ref[0] = x + jnp.dot(ycat[...].astype(BF16), wout_ref[...], preferred_element_type=F32)

    @pl.when(t == pl.num_programs(1) - 1)
    def _():
        pool_tail_ref[0] = ext_u[0:POOL_HALO, :]
        shift_tail_ref[0] = ext_p[0:SHIFT_HALO, :]
        for hh in range(RWKV_HEADS):
            g, j = divmod(hh, GROUP_LANES // RWKV_HEAD)
            s_out_ref[0, hh] = s_ref[g][:, j * RWKV_HEAD:(j + 1) * RWKV_HEAD]


def _mix0_params(P):
    c = RWKV_WIDTH
    row = lambda x: x.reshape(1, -1).astype(F32)
    w2 = jnp.zeros((W_RANK + A_RANK, c), F32).at[:W_RANK].set(P['rw_w2']).astype(BF16)
    a2 = jnp.zeros((W_RANK + A_RANK, c), F32).at[W_RANK:].set(P['rw_a2']).astype(BF16)
    hid = jnp.arange(c) // RWKV_HEAD
    bdiag = (hid[:, None] == hid[None, :]).astype(BF16)
    return (P['pool_w'].astype(BF16), row(P['pool_scale']), row(P['mu_shift']), row(P['rw_w0']), w2,
            row(P['rw_a0']), a2, P['rw_g2'].astype(BF16), row(P['rw_kk']), row(P['rw_ka']),
            row(P['rw_rk']), row(P['rw_lnx_w']), row(P['rw_lnx_b']), bdiag)


def _full_spec(x):
    nd = x.ndim
    return pl.BlockSpec(x.shape, lambda *_: (0,) * nd)


def mix0_seq(x, g, w_in, w_out, P):
    b, t, _ = x.shape
    tt = _row_tile(t, 512)
    assert tt % CHUNK == 0 and tt >= POOL_HALO
    prm = _mix0_params(P)
    ti = jnp.arange(tt)
    same_chunk = (ti[:, None] // CHUNK) == (ti[None, :] // CHUNK)
    tril = (same_chunk & (ti[:, None] >= ti[None, :])).astype(BF16)
    blk = same_chunk.astype(BF16)
    return pl.pallas_call(
        _mix0_seq_body,
        grid=(b, t // tt),
        in_specs=[pl.BlockSpec((1, tt, D_MODEL), lambda i, j: (i, j, 0)),
                  pl.BlockSpec((1, D_MODEL), lambda i, j: (0, 0)), _full_spec(w_in), _full_spec(w_out)]
                 + [_full_spec(p) for p in prm] + [_full_spec(tril), _full_spec(blk)],
        out_specs=[pl.BlockSpec((1, tt, D_MODEL), lambda i, j: (i, j, 0)),
                   pl.BlockSpec((1, RWKV_HEADS, RWKV_HEAD, RWKV_HEAD), lambda i, j: (i, 0, 0, 0)),
                   pl.BlockSpec((1, POOL_HALO, POOL_WIDTH), lambda i, j: (i, 0, 0)),
                   pl.BlockSpec((1, SHIFT_HALO, SHIFT_WIDTH), lambda i, j: (i, 0, 0))],
        out_shape=[jax.ShapeDtypeStruct((b, t, D_MODEL), F32),
                   jax.ShapeDtypeStruct((b, RWKV_HEADS, RWKV_HEAD, RWKV_HEAD), F32),
                   jax.ShapeDtypeStruct((b, POOL_HALO, POOL_WIDTH), F32),
                   jax.ShapeDtypeStruct((b, SHIFT_HALO, SHIFT_WIDTH), F32)],
        scratch_shapes=[pltpu.VMEM((tt + POOL_HALO, POOL_WIDTH), F32),
                        pltpu.VMEM((tt + SHIFT_HALO, SHIFT_WIDTH), F32),
                        pltpu.VMEM((RWKV_WIDTH // GROUP_LANES, RWKV_HEAD, GROUP_LANES), F32),
                        pltpu.VMEM((tt, D_MODEL), F32)],
        compiler_params=pltpu.CompilerParams(dimension_semantics=("parallel", "arbitrary")),
        name="mix0_seq",
    )(x, g.reshape(1, D_MODEL), w_in, w_out, *prm, tril, blk)


def _to_leading(x):
    n = x.shape[0]
    ii = lax.broadcasted_iota(jnp.int32, (n, n, 1), 0)
    jj = lax.broadcasted_iota(jnp.int32, (n, n, 1), 1)
    return jnp.sum(jnp.where(ii == jj, x[None, :, :], 0.0), axis=1, keepdims=True)


def _from_leading(x3):
    n = x3.shape[0]
    ii = lax.broadcasted_iota(jnp.int32, (n, n, 1), 0)
    jj = lax.broadcasted_iota(jnp.int32, (n, n, 1), 1)
    return jnp.sum(jnp.where(ii == jj, x3, 0.0), axis=0)


def _mix0_step_body(h_ref, pool_prev_ref, shift_prev_ref, s_in_ref,
                    pool_w_ref, pool_scale_ref, mu_ref, w0_ref, w2_ref, a0_ref, a2_ref, g2_ref,
                    kk_ref, ka_ref, rk_ref, lnw_ref, lnb_ref, bdiag_ref,
                    o_ref, s_out_ref,
                    r_t, w_t, k_t, v_t, kk_t, bt_t, y_t, r_s, k_s, v_s, g_s):
    hh = pl.program_id(0)
    nb = h_ref.shape[0]

    @pl.when(hh == 0)
    def _():
        for g, win in enumerate(POOL_WINDOWS):
            cs = slice(g * POOL_GC, (g + 1) * POOL_GC)
            u = h_ref[:, cs]
            acc = u
            for j in range(1, win):
                acc = acc + pool_prev_ref[POOL_BUF - j][:, cs]
            cnt = jnp.float32(min(win, PAST_LEN + 1))
            o_ref[:, cs] = _pool_group_out(acc, cnt, u, pool_w_ref, pool_scale_ref, g)
        prm = (mu_ref, w0_ref, w2_ref, a0_ref, a2_ref, g2_ref, kk_ref, ka_ref, bdiag_ref)
        r, lw, kmod, v, kk, a, gate = _rwkv_prep(h_ref[:, POOL_WIDTH:IN_AB_WIDTH], shift_prev_ref[...], prm)
        r_s[...] = r
        k_s[...] = kmod
        v_s[...] = v
        g_s[...] = gate
        r_t[...] = r.T
        w_t[...] = jnp.exp(lw).T
        k_t[...] = kmod.T
        v_t[...] = v.T
        kk_t[...] = kk.T
        bt_t[...] = (kk * a).T

    rows = pl.ds(pl.multiple_of(hh * RWKV_HEAD, RWKV_HEAD), RWKV_HEAD)
    s = s_in_ref[...].T.reshape(RWKV_HEAD, RWKV_HEAD, nb)
    kk = kk_t[rows, :][None]
    s_kk = jnp.sum(s * kk, axis=1, keepdims=True)
    v3 = _to_leading(v_t[rows, :])
    s = s * w_t[rows, :][None] - s_kk * bt_t[rows, :][None] + v3 * k_t[rows, :][None]
    y3 = jnp.sum(s * r_t[rows, :][None], axis=1, keepdims=True)
    y_t[rows, :] = _from_leading(y3)
    s_out_ref[...] = s.reshape(RWKV_HEAD * RWKV_HEAD, nb).T

    @pl.when(hh == pl.num_programs(0) - 1)
    def _():
        o_ref[:, POOL_WIDTH:] = _rwkv_finish(y_t[...].T, r_s[...], k_s[...], v_s[...], g_s[...],
                                             rk_ref, lnw_ref, lnb_ref, bdiag_ref)


def mix0_step(h, pool_prev, shift_prev, s_prev, P):
    b = h.shape[0]
    prm = _mix0_params(P)
    hw = RWKV_HEAD * RWKV_HEAD
    s2 = s_prev.reshape(b, RWKV_HEADS * hw)
    pool_t = jnp.swapaxes(pool_prev, 0, 1)
    tvec = lambda: pltpu.VMEM((RWKV_WIDTH, b), F32)
    svec = lambda: pltpu.VMEM((b, RWKV_WIDTH), F32)
    out, s_new = pl.pallas_call(
        _mix0_step_body,
        grid=(RWKV_HEADS,),
        in_specs=[_full_spec(h), _full_spec(pool_t), _full_spec(shift_prev),
                  pl.BlockSpec((b, hw), lambda i: (0, i))] + [_full_spec(x) for x in prm],
        out_specs=[pl.BlockSpec((b, D_MODEL), lambda i: (0, 0)),
                   pl.BlockSpec((b, hw), lambda i: (0, i))],
        out_shape=[jax.ShapeDtypeStruct((b, D_MODEL), F32),
                   jax.ShapeDtypeStruct((b, RWKV_HEADS * hw), F32)],
        scratch_shapes=[tvec(), tvec(), tvec(), tvec(), tvec(), tvec(), tvec(),
                        svec(), svec(), svec(), svec()],
        compiler_params=pltpu.CompilerParams(dimension_semantics=("arbitrary",)),
        name="mix0_step",
    )(h, pool_t, shift_prev, s2, *prm)
    return out, s_new.reshape(b, RWKV_HEADS, RWKV_HEAD, RWKV_HEAD)


def _conv_seq_body(x_ref, g_ref, wi_ref, cw_ref, wo_ref, o_ref, tail_ref, ext):
    t = pl.program_id(1)
    tt = o_ref.shape[1]
    c = D_MODEL

    @pl.when(t == 0)
    def _():
        ext[0:SHIFT_HALO, :] = jnp.zeros((SHIFT_HALO, c), F32)

    x = x_ref[0]
    h = jnp.dot(_rmsnorm(x, g_ref[...]).astype(BF16), wi_ref[...], preferred_element_type=F32)
    ext[SHIFT_HALO:SHIFT_HALO + tt, :] = h[:, c:2 * c] * h[:, 2 * c:3 * c]
    z = cw_ref[0:1, :] * ext[SHIFT_HALO - 2:SHIFT_HALO - 2 + tt, :]
    z = z + cw_ref[1:2, :] * ext[SHIFT_HALO - 1:SHIFT_HALO - 1 + tt, :]
    z = z + cw_ref[2:3, :] * ext[SHIFT_HALO:SHIFT_HALO + tt, :]
    gated = (h[:, 0:c] * z).astype(BF16)
    o_ref[0] = x + jnp.dot(gated, wo_ref[...], preferred_element_type=F32)
    ext[0:SHIFT_HALO, :] = ext[tt:tt + SHIFT_HALO, :]

    @pl.when(t == pl.num_programs(1) - 1)
    def _():
        tail_ref[0] = ext[SHIFT_HALO - (CONV_WIDTH - 1):SHIFT_HALO, :]


def conv_seq(x, g, w_in, conv_w, w_out):
    b, t, c = x.shape
    tt = _row_tile(t, 512)
    return pl.pallas_call(
        _conv_seq_body,
        grid=(b, t // tt),
        in_specs=[pl.BlockSpec((1, tt, c), lambda i, j: (i, j, 0)),
                  pl.BlockSpec((1, c), lambda i, j: (0, 0)),
                  _full_spec(w_in), _full_spec(conv_w), _full_spec(w_out)],
        out_specs=[pl.BlockSpec((1, tt, c), lambda i, j: (i, j, 0)),
                   pl.BlockSpec((1, CONV_WIDTH - 1, c), lambda i, j: (i, 0, 0))],
        out_shape=[jax.ShapeDtypeStruct((b, t, c), F32),
                   jax.ShapeDtypeStruct((b, CONV_WIDTH - 1, c), F32)],
        scratch_shapes=[pltpu.VMEM((tt + SHIFT_HALO, c), F32)],
        compiler_params=pltpu.CompilerParams(dimension_semantics=("parallel", "arbitrary")),
        name="conv_seq",
    )(x, g.reshape(1, c), w_in, conv_w, w_out)


def _conv_step_body(h_ref, x_ref, p0_ref, p1_ref, cw_ref, wo_ref, o_ref, e_ref):
    c = D_MODEL
    e = h_ref[:, c:2 * c] * h_ref[:, 2 * c:3 * c]
    z = cw_ref[0:1, :] * p0_ref[...] + cw_ref[1:2, :] * p1_ref[...] + cw_ref[2:3, :] * e
    gated = (h_ref[:, 0:c] * z).astype(BF16)
    o_ref[...] = x_ref[...] + jnp.dot(gated, wo_ref[...], preferred_element_type=F32)
    e_ref[...] = e


def conv_step(h, x, prev, conv_w, w_out):
    b = h.shape[0]
    c = D_MODEL
    args = (h, x, prev[:, 0, :], prev[:, 1, :], conv_w, w_out)
    return pl.pallas_call(
        _conv_step_body,
        grid=(1,),
        in_specs=[_full_spec(a) for a in args],
        out_specs=[pl.BlockSpec((b, c), lambda i: (0, 0)), pl.BlockSpec((b, c), lambda i: (0, 0))],
        out_shape=[jax.ShapeDtypeStruct((b, c), F32), jax.ShapeDtypeStruct((b, c), F32)],
        compiler_params=pltpu.CompilerParams(dimension_semantics=("arbitrary",)),
        name="conv_step",
    )(*args)


def _xattn(x2, i, mem_k, mem_v, W, seq_shape):
    last = i == W['norm_xattn'].shape[0] - 1
    if seq_shape is not None:
        b, t = seq_shape
        y = xattn_seq(x2.reshape(b, t, D_MODEL), W['norm_xattn'][i], W['w_xq'][i], W['w_xo'][i],
                      mem_k, mem_v, row_tiles=last)
        return y if last else y.reshape(b * t, D_MODEL)
    return xattn_step(x2, W['norm_xattn'][i], W['w_xq'][i], W['w_xo'][i], mem_k, mem_v, i, row_tiles=last)


def _ffn0(x2, W):
    return ffn_dense(x2, W['norm_ffn'][0], W['ffn_gate'][0], W['ffn_up'][0], W['ffn_down'][0], FFN_TILE)


def _trunk_seq(x, mem_k, mem_v, W):
    b, t, d = x.shape
    P0 = {k: v[0] for k, v in W['mix0'].items()}
    x1, wkv, pool_tail, shift_tail = mix0_seq(x, W['norm_mix'][0], W['w_in_ab'][0], W['w_out_ab'][0], P0)
    pool = pool_tail[:, POOL_HALO - POOL_BUF:]
    shift = shift_tail[:, SHIFT_HALO - 1]
    x2 = _ffn0(_xattn(x1.reshape(b * t, d), 0, mem_k[0], mem_v[0], W, (b, t)), W)
    x3, conv = conv_seq(x2.reshape(b, t, d), W['norm_mix'][1], W['w_in_c'][0], W['conv_w'][0], W['w_out_c'][0])
    x2 = _xattn(x3.reshape(b * t, d), 1, mem_k[1], mem_v[1], W, (b, t))
    return x2, pool[None], shift[None], wkv[None], conv[None]


def _trunk_step(x, mem_k, mem_v, pool_prev, shift_prev, wkv_prev, conv_prev, W):
    b, _, d = x.shape
    x2 = x.reshape(b, d)
    P0 = {k: v[0] for k, v in W['mix0'].items()}
    h = norm_matmul(x2, W['norm_mix'][0], W['w_in_ab'][0])
    mix, wkv = mix0_step(h, pool_prev[0], shift_prev[0], wkv_prev[0], P0)
    pool = jnp.concatenate([pool_prev[0][:, 1:], h[:, None, :POOL_WIDTH]], axis=1)
    shift = h[:, POOL_WIDTH:]
    x2 = matmul_res(mix, W['w_out_ab'][0], x2)
    x2 = _ffn0(_xattn(x2, 0, mem_k, mem_v, W, None), W)
    h = norm_matmul(x2, W['norm_mix'][1], W['w_in_c'][0])
    x2, e = conv_step(h, x2, conv_prev[0], W['conv_w'][0], W['w_out_c'][0])
    conv = jnp.concatenate([conv_prev[0][:, 1:], e[:, None]], axis=1)
    x2 = _xattn(x2, 1, mem_k, mem_v, W, None)
    return x2, pool[None], shift[None], wkv[None], conv[None]


def kernel(x_prompt, x_sample, mem_prompt, cache_mem_k, cache_mem_v, state_pool, state_shift, state_wkv, state_conv, norm_mix, norm_xattn, norm_mem, norm_ffn, norm_final, w_xq, w_xk, w_xv, w_xo, w_in_ab, pool_w, pool_scale, mu_shift, rw_w0, rw_w2, rw_a0, rw_a2, rw_g2, rw_kk, rw_ka, rw_rk, rw_lnx_w, rw_lnx_b, w_out_ab, ffn_gate, ffn_up, ffn_down, w_in_c, conv_w, w_out_c, router_w, router_b, moe_gate, moe_up, moe_down):
    depth = norm_mix.shape[0]
    assert depth == 2 and w_in_ab.shape[0] == 1 and w_in_c.shape[0] == 1
    bp = x_prompt.shape[0]
    bs = x_sample.shape[0]
    d = D_MODEL
    bf = lambda w: w.astype(BF16)
    W = dict(norm_mix=norm_mix, norm_xattn=norm_xattn, norm_ffn=norm_ffn, norm_final=norm_final,
             w_xq=bf(w_xq), w_xo=bf(w_xo), w_in_ab=bf(w_in_ab), w_out_ab=bf(w_out_ab),
             ffn_gate=bf(ffn_gate), ffn_up=bf(ffn_up), ffn_down=bf(ffn_down),
             w_in_c=bf(w_in_c), conv_w=conv_w, w_out_c=bf(w_out_c),
             router_w=router_w, router_b=router_b,
             moe_gate=bf(moe_gate), moe_up=bf(moe_up), moe_down=bf(moe_down),
             mix0=dict(pool_w=pool_w, pool_scale=pool_scale, mu_shift=mu_shift, rw_w0=rw_w0, rw_w2=rw_w2,
                       rw_a0=rw_a0, rw_a2=rw_a2, rw_g2=rw_g2, rw_kk=rw_kk, rw_ka=rw_ka, rw_rk=rw_rk,
                       rw_lnx_w=rw_lnx_w, rw_lnx_b=rw_lnx_b))

    mem2 = mem_prompt.reshape(bp * N_MEM, d)
    mk, mv = [], []
    for i in range(depth):
        wkv_i = jnp.concatenate([bf(w_xk[i]), bf(w_xv[i])], axis=1)
        kv = norm_matmul(mem2, norm_mem[i], wkv_i)
        mk.append(kv[:, :d].reshape(bp, N_MEM, d))
        mv.append(kv[:, d:].reshape(bp, N_MEM, d))

    x_p, pool_p, shift_p, wkv_p, conv_p = _trunk_seq(x_prompt, mk, mv, W)
    x_s, pool_s, shift_s, wkv_s, conv_s = _trunk_step(x_sample, cache_mem_k, cache_mem_v, state_pool,
                                                       state_shift, state_wkv, state_conv, W)
    y_p, y_s = moe_top2_norm([x_p, x_s], norm_ffn[1], router_w[0], router_b[0], W['moe_gate'][0],
                             W['moe_up'][0], W['moe_down'][0], norm_final)
    y_p = y_p.reshape(x_prompt.shape)
    y_s = y_s.reshape(x_sample.shape)
    mem_k_p = jnp.stack(mk).reshape(depth, bp, N_MEM, X_HEADS, X_HEAD_DIM)
    mem_v_p = jnp.stack(mv).reshape(depth, bp, N_MEM, X_HEADS, X_HEAD_DIM)
    return (y_p, y_s, pool_p, pool_s, shift_p, shift_s, wkv_p, wkv_s, conv_p, conv_s, mem_k_p, mem_v_p)
```

```python
import functools

import jax
import jax.numpy as jnp
from jax import lax
from jax.experimental import pallas as pl
from jax.experimental.pallas import tpu as pltpu

F32 = jnp.float32
BF16 = jnp.bfloat16

D_MODEL = 1024
POOL_WIDTH = 512
POOL_GROUPS = 4
POOL_GC = 128
POOL_WINDOWS = (2, 4, 8, 16)
POOL_BUF = 15
RWKV_WIDTH = 512
RWKV_HEAD = 64
RWKV_HEADS = 8
W_RANK = 64
A_RANK = 64
G_RANK = 128
SHIFT_WIDTH = 3 * RWKV_WIDTH + W_RANK + A_RANK + G_RANK
IN_AB_WIDTH = POOL_WIDTH + SHIFT_WIDTH
LNX_EPS = 64e-5
CONV_WIDTH = 3
N_EXPERTS = 8
N_MEM = 256
X_HEADS = 4
X_HEAD_DIM = 256
RMS_EPS = 1e-6
PAST_LEN = 16384

CHUNK = 64
GROUP_LANES = 256
POOL_HALO = 16
SHIFT_HALO = 8
ROUTER_PAD = 128
TOP_K = 2
ROW_SUBLANES = 8
MOE_ROWS = 512
FFN_TILE = 1408
MOE_TILE = 1792
XATTN_STEP_VMEM = 48 * 1024 * 1024


def _rmsnorm(x, g):
    ms = jnp.mean(x * x, axis=-1, keepdims=True)
    return x * lax.rsqrt(ms + RMS_EPS) * g


def _dot(a, b):
    return jnp.dot(a.astype(BF16), b.astype(BF16), preferred_element_type=F32)


def _dot_nt(a, b):
    return lax.dot_general(a.astype(BF16), b.astype(BF16), (((1,), (1,)), ((), ())),
                           preferred_element_type=F32)


def _dot_tn(a, b):
    return lax.dot_general(a.astype(BF16), b.astype(BF16), (((0,), (0,)), ((), ())),
                           preferred_element_type=F32)


def _split3(x):
    hi = x.astype(BF16)
    r1 = x - hi.astype(F32)
    mid = r1.astype(BF16)
    lo = (r1 - mid.astype(F32)).astype(BF16)
    return hi, mid, lo


def _dot_exact_lhs(m01, x):
    hi, mid, lo = _split3(x)
    f = lambda p: jnp.dot(m01, p, preferred_element_type=F32)
    return f(hi) + f(mid) + f(lo)


def _softplus(x):
    return jnp.maximum(x, 0.0) + jnp.log1p(jnp.exp(-jnp.abs(x)))


def _sigmoid(x):
    return 1.0 / (1.0 + jnp.exp(-x))


def _row_tile(m, want):
    t = min(m, want)
    assert m % t == 0, (m, t)
    return t


def _norm_matmul_body(x_ref, g_ref, w_ref, o_ref):
    xn = _rmsnorm(x_ref[...], g_ref[...]).astype(BF16)
    o_ref[...] = jnp.dot(xn, w_ref[...], preferred_element_type=F32)


def norm_matmul(x, g, w):
    m, k = x.shape
    n = w.shape[1]
    tm = _row_tile(m, 512)
    return pl.pallas_call(
        _norm_matmul_body,
        grid=(m // tm,),
        in_specs=[pl.BlockSpec((tm, k), lambda i: (i, 0)),
                  pl.BlockSpec((1, k), lambda i: (0, 0)),
                  pl.BlockSpec((k, n), lambda i: (0, 0))],
        out_specs=pl.BlockSpec((tm, n), lambda i: (i, 0)),
        out_shape=jax.ShapeDtypeStruct((m, n), F32),
        compiler_params=pltpu.CompilerParams(dimension_semantics=("parallel",)),
        name="norm_matmul",
    )(x, g.reshape(1, k), w)


def _mem_kv_body(m_ref, g_ref, wk_ref, wv_ref, k_ref, v_ref):
    xn = _rmsnorm(m_ref[...], g_ref[0]).astype(BF16)
    k_ref[0] = jnp.dot(xn, wk_ref[0], preferred_element_type=F32)
    v_ref[0] = jnp.dot(xn, wv_ref[0], preferred_element_type=F32)


def mem_kv(mem, g, wk, wv):
    r, d = mem.shape
    nl = g.shape[0]
    tm = _row_tile(r, 512)
    out = jax.ShapeDtypeStruct((nl, r, d), F32)
    return pl.pallas_call(
        _mem_kv_body,
        grid=(nl, r // tm),
        in_specs=[pl.BlockSpec((tm, d), lambda l, j: (j, 0)),
                  pl.BlockSpec((1, 1, d), lambda l, j: (l, 0, 0)),
                  pl.BlockSpec((1, d, d), lambda l, j: (l, 0, 0)),
                  pl.BlockSpec((1, d, d), lambda l, j: (l, 0, 0))],
        out_specs=[pl.BlockSpec((1, tm, d), lambda l, j: (l, j, 0)),
                   pl.BlockSpec((1, tm, d), lambda l, j: (l, j, 0))],
        out_shape=[out, out],
        compiler_params=pltpu.CompilerParams(dimension_semantics=("parallel", "parallel")),
        name="mem_kv",
    )(mem, g.reshape(nl, 1, d), wk, wv)


def _matmul_res_body(a_ref, w_ref, r_ref, o_ref):
    o_ref[...] = r_ref[...] + jnp.dot(a_ref[...].astype(BF16), w_ref[...], preferred_element_type=F32)


def matmul_res(a, w, res):
    m, k = a.shape
    n = w.shape[1]
    tm = _row_tile(m, 512)
    return pl.pallas_call(
        _matmul_res_body,
        grid=(m // tm,),
        in_specs=[pl.BlockSpec((tm, k), lambda i: (i, 0)),
                  pl.BlockSpec((k, n), lambda i: (0, 0)),
                  pl.BlockSpec((tm, n), lambda i: (i, 0))],
        out_specs=pl.BlockSpec((tm, n), lambda i: (i, 0)),
        out_shape=jax.ShapeDtypeStruct((m, n), F32),
        compiler_params=pltpu.CompilerParams(dimension_semantics=("parallel",)),
        name="matmul_res",
    )(a, w, res)


def _ffn_body(x_ref, g_ref, wg_ref, wu_ref, wd_ref, o_ref, xn_ref, acc_ref):
    f = pl.program_id(1)

    @pl.when(f == 0)
    def _():
        xn_ref[...] = _rmsnorm(x_ref[...], g_ref[...]).astype(BF16)
        acc_ref[...] = jnp.zeros_like(acc_ref)

    xn = xn_ref[...]
    gate = jnp.dot(xn, wg_ref[...], preferred_element_type=F32)
    up = jnp.dot(xn, wu_ref[...], preferred_element_type=F32)
    hid = (gate * _sigmoid(gate) * up).astype(BF16)
    acc_ref[...] += jnp.dot(hid, wd_ref[...], preferred_element_type=F32)

    @pl.when(f == pl.num_programs(1) - 1)
    def _():
        o_ref[...] = x_ref[...] + acc_ref[...]


def ffn_dense(x, g, wg, wu, wd, tf):
    m, d = x.shape
    ff = wg.shape[1]
    tm = _row_tile(m, 512)
    return pl.pallas_call(
        _ffn_body,
        grid=(m // tm, ff // tf),
        in_specs=[pl.BlockSpec((tm, d), lambda i, f: (i, 0)),
                  pl.BlockSpec((1, d), lambda i, f: (0, 0)),
                  pl.BlockSpec((d, tf), lambda i, f: (0, f)),
                  pl.BlockSpec((d, tf), lambda i, f: (0, f)),
                  pl.BlockSpec((tf, d), lambda i, f: (f, 0))],
        out_specs=pl.BlockSpec((tm, d), lambda i, f: (i, 0)),
        out_shape=jax.ShapeDtypeStruct((m, d), F32),
        scratch_shapes=[pltpu.VMEM((tm, d), BF16), pltpu.VMEM((tm, d), F32)],
        compiler_params=pltpu.CompilerParams(dimension_semantics=("parallel", "arbitrary")),
        name="ffn_dense",
    )(x, g.reshape(1, d), wg, wu, wd)


def _router_body(x_ref, g_ref, w_ref, b_ref, oi_ref, ow_ref):
    xn = _rmsnorm(_load_row_tiles(x_ref, oi_ref.shape[0]), g_ref[...])
    logits = _dot_3pass(xn, w_ref[...]) + b_ref[...]
    lane = lax.broadcasted_iota(jnp.int32, logits.shape, 1)
    neg = jnp.float32(-jnp.inf)
    logits = jnp.where(lane < N_EXPERTS, logits, neg)
    m1 = jnp.max(logits, axis=-1, keepdims=True)
    i1 = jnp.min(jnp.where(logits == m1, lane, ROUTER_PAD), axis=-1, keepdims=True)
    rest = jnp.where(lane == i1, neg, logits)
    m2 = jnp.max(rest, axis=-1, keepdims=True)
    i2 = jnp.min(jnp.where(rest == m2, lane, ROUTER_PAD), axis=-1, keepdims=True)
    e2 = jnp.exp(m2 - m1)
    den = 1.0 + e2
    slot = lax.broadcasted_iota(jnp.int32, oi_ref.shape, 1)
    oi_ref[...] = jnp.where(slot == 0, i1, i2)
    ow_ref[...] = jnp.where(slot == 0, 1.0 / den, e2 / den)


def _dot_3pass(x, w):
    xh = x.astype(BF16)
    xm = (x - xh.astype(F32)).astype(BF16)
    wh = w.astype(BF16)
    wm = (w - wh.astype(F32)).astype(BF16)
    f = lambda p, q: jnp.dot(p, q, preferred_element_type=F32)
    return f(xh, wh) + (f(xh, wm) + f(xm, wh))


def moe_router(x8, row0, m, g, router_w, router_b):
    d = D_MODEL
    tm = _row_tile(m, 512)
    assert row0 % tm == 0
    blk0 = row0 // tm
    w = jnp.zeros((d, ROUTER_PAD), F32).at[:, :N_EXPERTS].set(router_w)
    b = jnp.zeros((1, ROUTER_PAD), F32).at[0, :N_EXPERTS].set(router_b)
    return pl.pallas_call(
        _router_body,
        grid=(m // tm,),
        in_specs=[pl.BlockSpec((tm * ROW_SUBLANES, 128), lambda i: (blk0 + i, 0)),
                  pl.BlockSpec((1, d), lambda i: (0, 0)),
                  pl.BlockSpec((d, ROUTER_PAD), lambda i: (0, 0)),
                  pl.BlockSpec((1, ROUTER_PAD), lambda i: (0, 0))],
        out_specs=[pl.BlockSpec((tm, TOP_K), lambda i: (i, 0)), pl.BlockSpec((tm, TOP_K), lambda i: (i, 0))],
        out_shape=[jax.ShapeDtypeStruct((m, TOP_K), jnp.int32), jax.ShapeDtypeStruct((m, TOP_K), F32)],
        compiler_params=pltpu.CompilerParams(dimension_semantics=("parallel",)),
        name="moe_router",
    )(x8, g.reshape(1, d), w, b)


def _row_tiles_to_2d(ref, slot, n):
    return jnp.concatenate([ref[slot, pl.ds(j, n, stride=ROW_SUBLANES), :] for j in range(ROW_SUBLANES)],
                           axis=1)


def _row_copy(src_hbm, row, dst, slot, i, sem):
    return pltpu.make_async_copy(src_hbm.at[pl.ds(row * ROW_SUBLANES, ROW_SUBLANES)],
                                 dst.at[slot, pl.ds(i * ROW_SUBLANES, ROW_SUBLANES)], sem.at[slot])


def _gather_start(src_hbm, rows_smem, dst, slot, n, sem):
    def body(i, carry):
        for k in range(2):
            r = 2 * i + k
            _row_copy(src_hbm, rows_smem[0, 0, r], dst, slot, r, sem).start(priority=k)
        return carry
    lax.fori_loop(0, n // 2, body, 0, unroll=4)


def _gather_wait(src_hbm, dst, slot, n, sem):
    pltpu.make_async_copy(src_hbm.at[pl.ds(0, n * ROW_SUBLANES)], dst.at[slot], sem.at[slot]).wait()


def _moe_group_body(te_ref, nu_ref, tok0_ref, tokn_ref, x8_ref, g_ref, wg_ref, wu_ref, wd_ref,
                    o_ref, xbuf, sem, xn_ref, acc_ref, *, nf):
    t = pl.program_id(0)
    f = pl.program_id(1)
    tm = xn_ref.shape[0]
    per_step = tm // nf
    used = t < nu_ref[0]
    slot = t % 2

    @pl.when(jnp.logical_and(t == 0, f == 0))
    def _():
        _gather_start(x8_ref, tok0_ref, xbuf, 0, tm, sem)

    @pl.when(jnp.logical_and(used, f == 0))
    def _():
        _gather_wait(x8_ref, xbuf, slot, tm, sem)
        x = _row_tiles_to_2d(xbuf, slot, tm)
        xn_ref[...] = _rmsnorm(x, g_ref[...]).astype(BF16)
        acc_ref[...] = jnp.zeros_like(acc_ref)

    @pl.when(used)
    def _():
        base = f * per_step
        for i in range(per_step):
            _row_copy(x8_ref, tokn_ref[0, 0, base + i], xbuf, 1 - slot, base + i, sem).start()
        xn = xn_ref[...]
        gate = jnp.dot(xn, wg_ref[0], preferred_element_type=F32)
        up = jnp.dot(xn, wu_ref[0], preferred_element_type=F32)
        hid = (gate * _sigmoid(gate) * up).astype(BF16)
        acc_ref[...] += jnp.dot(hid, wd_ref[0], preferred_element_type=F32)

    @pl.when(f == nf - 1)
    def _():
        y = jnp.where(used, acc_ref[...], 0.0)
        for j in range(ROW_SUBLANES):
            o_ref[pl.ds(j, tm, stride=ROW_SUBLANES), :] = y[:, j * 128:(j + 1) * 128]

    @pl.when(jnp.logical_and(t == nu_ref[0] - 1, f == nf - 1))
    def _():
        _gather_wait(x8_ref, xbuf, 1 - slot, tm, sem)


def moe_grouped(x8, g, tile_expert, n_used, row_token, wg, wu, wd, tf):
    n_tiles, _, tm = row_token.shape
    d = D_MODEL
    ne, _, ff = wg.shape
    nf = ff // tf
    assert tm % nf == 0
    live_f = lambda t, f, nu: jnp.where(t < nu[0], f, nf - 1)
    smem_rows = lambda imap: pl.BlockSpec((1, 1, tm), imap, memory_space=pltpu.SMEM)
    grid_spec = pltpu.PrefetchScalarGridSpec(
        num_scalar_prefetch=2,
        grid=(n_tiles, nf),
        in_specs=[smem_rows(lambda t, f, te, nu: (0, 0, 0)),
                  smem_rows(lambda t, f, te, nu: (jnp.minimum(t + 1, nu[0] - 1), 0, 0)),
                  pl.BlockSpec(memory_space=pl.ANY),
                  pl.BlockSpec((1, d), lambda t, f, te, nu: (0, 0)),
                  pl.BlockSpec((1, d, tf), lambda t, f, te, nu: (te[t], 0, live_f(t, f, nu))),
                  pl.BlockSpec((1, d, tf), lambda t, f, te, nu: (te[t], 0, live_f(t, f, nu))),
                  pl.BlockSpec((1, tf, d), lambda t, f, te, nu: (te[t], live_f(t, f, nu), 0))],
        out_specs=pl.BlockSpec((tm * ROW_SUBLANES, 128), lambda t, f, te, nu: (t, 0)),
        scratch_shapes=[pltpu.VMEM((2, tm * ROW_SUBLANES, 128), F32), pltpu.SemaphoreType.DMA((2,)),
                        pltpu.VMEM((tm, d), BF16), pltpu.VMEM((tm, d), F32)])
    return pl.pallas_call(
        functools.partial(_moe_group_body, nf=nf),
        grid_spec=grid_spec,
        out_shape=jax.ShapeDtypeStruct((n_tiles * tm * ROW_SUBLANES, 128), F32),
        compiler_params=pltpu.CompilerParams(dimension_semantics=("arbitrary", "arbitrary")),
        name="moe_grouped",
    )(tile_expert, n_used, row_token, row_token, x8, g.reshape(1, d), wg, wu, wd)


def _moe_combine_body(pos0_ref, posn_ref, y8_ref, x_ref, w_ref, g_ref, o_ref, ybuf, sem):
    t = pl.program_id(0)
    tc = o_ref.shape[0]
    n = TOP_K * tc

    @pl.when(t == 0)
    def _():
        _gather_start(y8_ref, pos0_ref, ybuf, 0, n, sem)

    @pl.when(t + 1 < pl.num_programs(0))
    def _():
        _gather_start(y8_ref, posn_ref, ybuf, (t + 1) % 2, n, sem)

    slot = t % 2
    _gather_wait(y8_ref, ybuf, slot, n, sem)
    y = _row_tiles_to_2d(ybuf, slot, n)
    w = w_ref[...]
    moe = w[:, 0:1] * y[0:tc] + w[:, 1:2] * y[tc:n]
    o_ref[...] = _rmsnorm(_load_row_tiles(x_ref, tc) + moe, g_ref[...])


def moe_combine_norm(y8, x8, row0, pos, topw, g):
    d = D_MODEL
    nt, _, n = pos.shape
    tc = n // TOP_K
    m = nt * tc
    assert row0 % tc == 0
    blk0 = row0 // tc
    smem_rows = lambda imap: pl.BlockSpec((1, 1, n), imap, memory_space=pltpu.SMEM)
    return pl.pallas_call(
        _moe_combine_body,
        grid=(nt,),
        in_specs=[smem_rows(lambda t: (0, 0, 0)),
                  smem_rows(lambda t: (jnp.minimum(t + 1, nt - 1), 0, 0)),
                  pl.BlockSpec(memory_space=pl.ANY),
                  pl.BlockSpec((tc * ROW_SUBLANES, 128), lambda t: (blk0 + t, 0)),
                  pl.BlockSpec((tc, TOP_K), lambda t: (t, 0)),
                  pl.BlockSpec((1, d), lambda t: (0, 0))],
        out_specs=pl.BlockSpec((tc, d), lambda t: (t, 0)),
        out_shape=jax.ShapeDtypeStruct((m, d), F32),
        scratch_shapes=[pltpu.VMEM((2, n * ROW_SUBLANES, 128), F32), pltpu.SemaphoreType.DMA((2,))],
        compiler_params=pltpu.CompilerParams(dimension_semantics=("arbitrary",)),
        name="moe_combine_norm",
    )(pos, pos, y8, x8, topw, g.reshape(1, d))


def _moe_plan(topi, tm):
    n = topi.shape[0]
    a = TOP_K * n
    n_tiles = (a + N_EXPERTS * (tm - 1) + tm - 1) // tm
    e = topi.reshape(a)
    eid = jnp.arange(N_EXPERTS, dtype=jnp.int32)
    onehot = (e[:, None] == eid[None, :]).astype(jnp.int32)
    csum = jnp.cumsum(onehot, axis=0)
    cnt = csum[-1]
    rank = jnp.sum((csum - onehot) * onehot, axis=1)
    tiles = (cnt + tm - 1) // tm
    tile_end = jnp.cumsum(tiles)
    tile_start = tile_end - tiles
    n_used = tile_end[-1]
    pos = (jnp.sum(onehot * tile_start[None, :], axis=1) * tm + rank).reshape(n, TOP_K)
    tid = jnp.minimum(jnp.arange(n_tiles, dtype=jnp.int32), n_used - 1)
    tile_expert = jnp.sum(tid[:, None] >= tile_end[None, :], axis=1).astype(jnp.int32)
    fill = tiles * tm - cnt
    j = jnp.arange(tm - 1, dtype=jnp.int32)
    dummy_key = jnp.where(j[None, :] < fill[:, None], eid[:, None], N_EXPERTS).reshape(-1)
    dummy_tok = jnp.tile(j % n, N_EXPERTS)
    spare = n_tiles * tm - a - dummy_key.shape[0]
    keys = jnp.concatenate([e, dummy_key, jnp.full((spare,), N_EXPERTS, jnp.int32)])
    toks = jnp.concatenate([jnp.arange(a, dtype=jnp.int32) // TOP_K, dummy_tok,
                            jnp.zeros((spare,), jnp.int32)])
    _, row_token = lax.sort((keys, toks), num_keys=1, is_stable=True)
    return tile_expert, n_used.reshape(1).astype(jnp.int32), row_token.reshape(n_tiles, 1, tm), pos


def moe_top2_norm(x8, groups, g, router_w, router_b, wg, wu, wd, g_final):
    starts = [sum(groups[:i]) for i in range(len(groups))]
    tops = [moe_router(x8, s, m, g, router_w, router_b) for s, m in zip(starts, groups)]
    topi = jnp.concatenate([t[0] for t in tops], axis=0)
    tile_expert, n_used, row_token, pos = _moe_plan(topi, MOE_ROWS)
    y8 = moe_grouped(x8, g, tile_expert, n_used, row_token, wg, wu, wd, MOE_TILE)
    outs = []
    for s, m, (_, topw) in zip(starts, groups, tops):
        tc = _row_tile(m, 256)
        p = pos[s:s + m].reshape(m // tc, tc, TOP_K)
        p = jnp.swapaxes(p, 1, 2).reshape(m // tc, 1, TOP_K * tc)
        outs.append(moe_combine_norm(y8, x8, s, p, topw, g_final))
    return outs


def _store_row_tiles(o_ref, y):
    n = y.shape[0]
    for j in range(ROW_SUBLANES):
        o_ref[pl.ds(j, n, stride=ROW_SUBLANES), :] = y[:, j * 128:(j + 1) * 128]


def _load_row_tiles(ref, n):
    return jnp.concatenate([ref[pl.ds(j, n, stride=ROW_SUBLANES), :] for j in range(ROW_SUBLANES)], axis=1)


def _xattn_seq_body(x_ref, g_ref, wq_ref, wo_ref, k_ref, v_ref, o_ref, *, row_tiles, fill_step):
    if fill_step is not None:
        @pl.when(pl.program_id(0) == fill_step)
        def _():
            o_ref[...] = jnp.zeros_like(o_ref)

        @pl.when(pl.program_id(0) != fill_step)
        def _():
            _xattn_seq_tile(x_ref, g_ref, wq_ref, wo_ref, k_ref, v_ref, o_ref, row_tiles)
    else:
        _xattn_seq_tile(x_ref, g_ref, wq_ref, wo_ref, k_ref, v_ref, o_ref, row_tiles)


def _xattn_seq_tile(x_ref, g_ref, wq_ref, wo_ref, k_ref, v_ref, o_ref, row_tiles):
    x = x_ref[0]
    xn = _rmsnorm(x, g_ref[...]).astype(BF16)
    q = jnp.dot(xn, wq_ref[...], preferred_element_type=F32).astype(BF16)
    kb = k_ref[0, 0].astype(BF16)
    vb = v_ref[0, 0].astype(BF16)
    heads = []
    for h in range(X_HEADS):
        sl = slice(h * X_HEAD_DIM, (h + 1) * X_HEAD_DIM)
        s = _dot_nt(q[:, sl], kb[:, sl]) * (X_HEAD_DIM ** -0.5)
        mx = jnp.max(s, axis=-1, keepdims=True)
        ex = jnp.exp(s - mx)
        pr = ex / jnp.sum(ex, axis=-1, keepdims=True)
        heads.append(jnp.dot(pr.astype(BF16), vb[:, sl], preferred_element_type=F32))
    att = jnp.concatenate(heads, axis=-1).astype(BF16)
    y = x + jnp.dot(att, wo_ref[...], preferred_element_type=F32)
    if row_tiles:
        _store_row_tiles(o_ref, y)
    else:
        o_ref[0] = y


def xattn_seq(x, g, wq, wo, mem_k, mem_v, layer, row_tiles=False, extra_rows=0):
    b, t, d = x.shape
    tq = _row_tile(t, 512)
    nq = t // tq
    assert extra_rows <= tq and (extra_rows == 0 or row_tiles)
    steps = b * nq + (1 if extra_rows else 0)
    bi = lambda s: jnp.minimum(s // nq, b - 1)
    qi = lambda s: jnp.where(s < b * nq, s % nq, 0)
    if row_tiles:
        out_spec = pl.BlockSpec((tq * ROW_SUBLANES, 128), lambda s: (s, 0))
        out_shape = jax.ShapeDtypeStruct(((b * t + extra_rows) * ROW_SUBLANES, 128), F32)
    else:
        out_spec = pl.BlockSpec((1, tq, d), lambda s: (bi(s), qi(s), 0))
        out_shape = jax.ShapeDtypeStruct((b, t, d), F32)
    return pl.pallas_call(
        functools.partial(_xattn_seq_body, row_tiles=row_tiles, fill_step=b * nq if extra_rows else None),
        grid=(steps,),
        in_specs=[pl.BlockSpec((1, tq, d), lambda s: (bi(s), qi(s), 0)),
                  pl.BlockSpec((1, d), lambda s: (0, 0)),
                  pl.BlockSpec((d, d), lambda s: (0, 0)),
                  pl.BlockSpec((d, d), lambda s: (0, 0)),
                  pl.BlockSpec((1, 1, N_MEM, d), lambda s: (layer, bi(s), 0, 0)),
                  pl.BlockSpec((1, 1, N_MEM, d), lambda s: (layer, bi(s), 0, 0))],
        out_specs=out_spec,
        out_shape=out_shape,
        compiler_params=pltpu.CompilerParams(dimension_semantics=("parallel",)),
        name="xattn_seq",
    )(x, g.reshape(1, d), wq, wo, mem_k, mem_v)


XATTN_STEP_ROWS = 8
XATTN_STEP_KV = 4


def _xattn_step_body(x_ref, g_ref, wq_ref, wo_ref, k_ref, v_ref, *rest, row_tiles, aliased):
    o_ref, q_s, att_s = rest[1:] if aliased else rest
    j = pl.program_id(1)

    @pl.when(j == 0)
    def _():
        xn = _rmsnorm(x_ref[...], g_ref[...]).astype(BF16)
        q_s[...] = jnp.dot(xn, wq_ref[...], preferred_element_type=F32) * (X_HEAD_DIM ** -0.5)

    for b in range(XATTN_STEP_KV):
        rsel = pl.ds(j * XATTN_STEP_KV + b, 1)
        q4 = jnp.concatenate([q_s[rsel, h * X_HEAD_DIM:(h + 1) * X_HEAD_DIM] for h in range(X_HEADS)], axis=0)
        s = jnp.sum(k_ref[0, b] * q4[None], axis=-1, keepdims=True)
        mx = jnp.max(s, axis=0, keepdims=True)
        ex = jnp.exp(s - mx)
        den = jnp.sum(ex, axis=0)
        o4 = jnp.sum(ex * v_ref[0, b], axis=0) / den
        for h in range(X_HEADS):
            att_s[rsel, h * X_HEAD_DIM:(h + 1) * X_HEAD_DIM] = o4[h:h + 1, :]

    @pl.when(j == pl.num_programs(1) - 1)
    def _():
        y = x_ref[...] + jnp.dot(att_s[...].astype(BF16), wo_ref[...], preferred_element_type=F32)
        if row_tiles:
            _store_row_tiles(o_ref, y)
        else:
            o_ref[...] = y


def xattn_step(x, g, wq, wo, mem_k, mem_v, layer, row_tiles=False, into=None):
    b, d = x.shape
    rows, kvb = XATTN_STEP_ROWS, XATTN_STEP_KV
    assert b % rows == 0 and rows % kvb == 0
    nj = rows // kvb
    kv_spec = pl.BlockSpec((1, kvb, N_MEM, X_HEADS, X_HEAD_DIM), lambda i, j: (layer, i * nj + j, 0, 0, 0))
    args = [x, g.reshape(1, d), wq, wo, mem_k, mem_v]
    in_specs = [pl.BlockSpec((rows, d), lambda i, j: (i, 0)),
                pl.BlockSpec((1, d), lambda i, j: (0, 0)),
                pl.BlockSpec((d, d), lambda i, j: (0, 0)),
                pl.BlockSpec((d, d), lambda i, j: (0, 0)),
                kv_spec, kv_spec]
    aliases = {}
    if into is not None:
        buf, row0 = into
        assert row_tiles and row0 % rows == 0 and buf.shape == ((row0 + b) * ROW_SUBLANES, 128)
        blk0 = row0 // rows
        out_spec = pl.BlockSpec((rows * ROW_SUBLANES, 128), lambda i, j: (blk0 + i, 0))
        out_shape = jax.ShapeDtypeStruct(buf.shape, F32)
        aliases = {len(args): 0}
        args.append(buf)
        in_specs.append(pl.BlockSpec(memory_space=pl.ANY))
    elif row_tiles:
        out_spec = pl.BlockSpec((rows * ROW_SUBLANES, 128), lambda i, j: (i, 0))
        out_shape = jax.ShapeDtypeStruct((b * ROW_SUBLANES, 128), F32)
    else:
        out_spec = pl.BlockSpec((rows, d), lambda i, j: (i, 0))
        out_shape = jax.ShapeDtypeStruct((b, d), F32)
    return pl.pallas_call(
        functools.partial(_xattn_step_body, row_tiles=row_tiles, aliased=into is not None),
        grid=(b // rows, nj),
        in_specs=in_specs,
        out_specs=out_spec,
        out_shape=out_shape,
        input_output_aliases=aliases,
        scratch_shapes=[pltpu.VMEM((rows, d), F32), pltpu.VMEM((rows, d), F32)],
        compiler_params=pltpu.CompilerParams(dimension_semantics=("parallel", "arbitrary"),
                                             vmem_limit_bytes=XATTN_STEP_VMEM),
        name="xattn_step",
    )(*args)


def _head_sum(x, bdiag_ref):
    m = bdiag_ref[0:GROUP_LANES, 0:GROUP_LANES]
    parts = []
    for g in range(RWKV_WIDTH // GROUP_LANES):
        xs = x[:, g * GROUP_LANES:(g + 1) * GROUP_LANES]
        hi = xs.astype(BF16)
        lo = (xs - hi.astype(F32)).astype(BF16)
        parts.append(jnp.dot(hi, m, preferred_element_type=F32) + jnp.dot(lo, m, preferred_element_type=F32))
    return jnp.concatenate(parts, axis=1)


def _pool_group_out(acc, cnt, u, pool_w_ref, pool_scale_ref, g):
    cs = slice(g * POOL_GC, (g + 1) * POOL_GC)
    d = acc / cnt - u
    return _dot(d, pool_w_ref[g]) * pool_scale_ref[:, cs]


def _rwkv_prep(p, shifted, prm):
    (mu_ref, w0_ref, w2_ref, a0_ref, a2_ref, g2_ref, kk_ref, ka_ref, bdiag_ref) = prm
    m = p + (shifted - p) * mu_ref[...]
    c = RWKV_WIDTH
    r = m[:, 0:c]
    k = m[:, c:2 * c]
    v = m[:, 2 * c:3 * c]
    dwa = m[:, 3 * c:3 * c + W_RANK + A_RANK]
    dg = m[:, 3 * c + W_RANK + A_RANK:]
    w_log = -_softplus(-(w0_ref[...] + _dot(jnp.tanh(dwa), w2_ref[...]))) - 0.5
    lw = -jnp.exp(w_log)
    a = _sigmoid(a0_ref[...] + _dot(dwa, a2_ref[...]))
    gate = _dot(_sigmoid(dg), g2_ref[...])
    kk = k * kk_ref[...]
    kk = kk / jnp.maximum(jnp.sqrt(_head_sum(kk * kk, bdiag_ref)), 1e-12)
    kmod = k * (1.0 + (a - 1.0) * ka_ref[...])
    return r, lw, kmod, v, kk, a, gate


def _rwkv_finish(y, r, kmod, v, gate, rk_ref, lnw_ref, lnb_ref, bdiag_ref):
    mu = _head_sum(y, bdiag_ref) * (1.0 / RWKV_HEAD)
    dlt = y - mu
    var = _head_sum(dlt * dlt, bdiag_ref) * (1.0 / RWKV_HEAD)
    yn = dlt * lax.rsqrt(var + LNX_EPS) * lnw_ref[...] + lnb_ref[...]
    bonus = _head_sum(r * kmod * rk_ref[...], bdiag_ref) * v
    return (yn + bonus) * gate


def _chains(x, nc):
    return [x[c * CHUNK:(c + 1) * CHUNK, g * GROUP_LANES:(g + 1) * GROUP_LANES]
            for g in range(RWKV_WIDTH // GROUP_LANES) for c in range(nc)]


def _block_diag(x, m4):
    xb = x.astype(BF16)
    return jnp.concatenate([xb] * (GROUP_LANES // RWKV_HEAD), axis=0) * m4


def _mix0_seq_body(x_ref, gmix_ref, win_ref, wout_ref,
                   pool_w_ref, pool_scale_ref, mu_ref, w0_ref, w2_ref, a0_ref, a2_ref, g2_ref,
                   kk_ref, ka_ref, rk_ref, lnw_ref, lnb_ref, bdiag_ref, tril_ref, blk_ref,
                   o_ref, s_out_ref, pool_tail_ref, shift_tail_ref,
                   ext_u, ext_p, s_ref, ycat):
    t = pl.program_id(1)
    tt = o_ref.shape[1]
    nc = tt // CHUNK
    ng = RWKV_WIDTH // GROUP_LANES

    @pl.when(t == 0)
    def _():
        ext_u[0:POOL_HALO, :] = jnp.zeros((POOL_HALO, POOL_WIDTH), F32)
        ext_p[0:SHIFT_HALO, :] = jnp.zeros((SHIFT_HALO, SHIFT_WIDTH), F32)
        s_ref[...] = jnp.zeros_like(s_ref)

    x = x_ref[0]
    h = jnp.dot(_rmsnorm(x, gmix_ref[...]).astype(BF16), win_ref[...], preferred_element_type=F32)
    ext_u[POOL_HALO:POOL_HALO + tt, :] = h[:, 0:POOL_WIDTH]
    ext_p[SHIFT_HALO:SHIFT_HALO + tt, :] = h[:, POOL_WIDTH:IN_AB_WIDTH]

    pos = t * tt + lax.broadcasted_iota(jnp.int32, (tt, 1), 0)
    for g, win in enumerate(POOL_WINDOWS):
        cs = slice(g * POOL_GC, (g + 1) * POOL_GC)
        u = ext_u[POOL_HALO:POOL_HALO + tt, cs]
        acc = u
        for j in range(1, win):
            acc = acc + ext_u[POOL_HALO - j:POOL_HALO - j + tt, cs]
        cnt = jnp.minimum(win, pos + 1).astype(F32)
        ycat[:, cs] = _pool_group_out(acc, cnt, u, pool_w_ref, pool_scale_ref, g)

    p = ext_p[SHIFT_HALO:SHIFT_HALO + tt, :]
    shifted = ext_p[SHIFT_HALO - 1:SHIFT_HALO - 1 + tt, :]
    prm = (mu_ref, w0_ref, w2_ref, a0_ref, a2_ref, g2_ref, kk_ref, ka_ref, bdiag_ref)
    r, lw, kmod, v, kk, a, gate = _rwkv_prep(p, shifted, prm)

    ext_u[0:POOL_HALO, :] = ext_u[tt:tt + POOL_HALO, :]
    ext_p[0:SHIFT_HALO, :] = ext_p[tt:tt + SHIFT_HALO, :]

    cl = _dot_exact_lhs(tril_ref[...], lw)
    tot = _dot_exact_lhs(blk_ref[...], lw)
    gam = jnp.exp(cl)
    gam_inv = jnp.exp(-cl)
    gam_end = jnp.exp(tot)
    beta = kk * a
    kd_full = kmod * gam_inv
    bd_full = beta * gam_inv
    kq_c = _chains(kk * jnp.exp(cl - lw), nc)
    rq_c = _chains(r * gam, nc)
    kd_c = _chains(kd_full, nc)
    bd_c = _chains(bd_full, nc)
    v_c = _chains(v, nc)
    kdg_c = _chains(kd_full * gam_end, nc)
    bdg_c = _chains(bd_full * gam_end, nc)
    ge_c = _chains(gam_end, nc)
    n_ch = ng * nc
    every = range(n_ch)

    m4 = bdiag_ref[0:GROUP_LANES, 0:GROUP_LANES]
    m4f = m4.astype(F32)
    row = lax.broadcasted_iota(jnp.int32, (CHUNK, GROUP_LANES), 0)
    col = lax.broadcasted_iota(jnp.int32, (CHUNK, GROUP_LANES), 1) % RWKV_HEAD
    strict = row > col
    incl = row >= col
    eye_c = jnp.where(row == col, 1.0, 0.0)
    eye_g = (lax.broadcasted_iota(jnp.int32, (GROUP_LANES, GROUP_LANES), 0)
             == lax.broadcasted_iota(jnp.int32, (GROUP_LANES, GROUP_LANES), 1))

    bd_kd = [_block_diag(kd_c[i], m4) for i in every]
    bd_bd = [_block_diag(bd_c[i], m4) for i in every]
    lhs = [jnp.concatenate([kq_c[i], rq_c[i]], axis=0) for i in every]
    pk = [_dot_nt(lhs[i], bd_kd[i]) for i in every]
    pb = [_dot_nt(lhs[i], bd_bd[i]) for i in every]
    a_k = [jnp.where(strict, pk[i][:CHUNK], 0.0) for i in every]
    p_k = [jnp.where(incl, pk[i][CHUNK:], 0.0) for i in every]
    a_b = [jnp.where(strict, pb[i][:CHUNK], 0.0) for i in every]
    p_b = [jnp.where(incl, pb[i][CHUNK:], 0.0) for i in every]
    tm = [eye_c - a_b[i] for i in every]
    apow = a_b
    bd_ap = [_block_diag(apow[i], m4) for i in every]
    n = 1
    while 2 * n < CHUNK:
        apow = [_dot(apow[i], bd_ap[i]) for i in every]
        bd_ap = [_block_diag(apow[i], m4) for i in every]
        tm = [tm[i] + _dot(tm[i], bd_ap[i]) for i in every]
        n *= 2
    bd_v = [_block_diag(v_c[i], m4) for i in every]
    t_kq = [_dot(tm[i], _block_diag(kq_c[i], m4)) for i in every]
    akv = [_dot(a_k[i], bd_v[i]) for i in every]
    t_akv = [_dot(tm[i], _block_diag(akv[i], m4)) for i in every]
    r_y = [rq_c[i] - _dot(p_b[i], _block_diag(t_kq[i], m4)) for i in every]
    y0 = [_dot(p_k[i], bd_v[i]) - _dot(p_b[i], _block_diag(t_akv[i], m4)) for i in every]
    g_m = [m4f * (jnp.where(eye_g, ge_c[i][0:1, :], 0.0) - _dot_tn(t_kq[i], bdg_c[i])) for i in every]
    h_full = [m4f * _dot_tn(jnp.concatenate([v_c[i], t_akv[i]], axis=0),
                            jnp.concatenate([kdg_c[i], -bdg_c[i]], axis=0)) for i in every]
    h_t = [sum(h_full[i][j * RWKV_HEAD:(j + 1) * RWKV_HEAD] for j in range(GROUP_LANES // RWKV_HEAD))
           for i in every]

    ys = [None] * n_ch
    for g in range(ng):
        s = s_ref[g]
        for c in range(nc):
            i = g * nc + c
            ys[i] = _dot_nt(r_y[i], _block_diag(s, m4)) + y0[i]
            s = _dot(s, g_m[i]) + h_t[i]
        s_ref[g] = s
    y = jnp.concatenate([jnp.concatenate([ys[g * nc + c] for g in range(ng)], axis=1) for c in range(nc)],
                        axis=0)

    ycat[:, POOL_WIDTH:] = _rwkv_finish(y, r, kmod, v, gate, rk_ref, lnw_ref, lnb_ref, bdiag_ref)
    o_ref[0] = x + jnp.dot(ycat[...].astype(BF16), wout_ref[...], preferred_element_type=F32)

    @pl.when(t == pl.num_programs(1) - 1)
    def _():
        pool_tail_ref[0] = ext_u[0:POOL_HALO, :]
        shift_tail_ref[0] = ext_p[0:SHIFT_HALO, :]
        for hh in range(RWKV_HEADS):
            g, j = divmod(hh, GROUP_LANES // RWKV_HEAD)
            s_out_ref[0, hh] = s_ref[g][:, j * RWKV_HEAD:(j + 1) * RWKV_HEAD]


def _mix0_params(P):
    c = RWKV_WIDTH
    row = lambda x: x.reshape(1, -1).astype(F32)
    w2 = jnp.zeros((W_RANK + A_RANK, c), F32).at[:W_RANK].set(P['rw_w2']).astype(BF16)
    a2 = jnp.zeros((W_RANK + A_RANK, c), F32).at[W_RANK:].set(P['rw_a2']).astype(BF16)
    hid = jnp.arange(c) // RWKV_HEAD
    bdiag = (hid[:, None] == hid[None, :]).astype(BF16)
    return (P['pool_w'].astype(BF16), row(P['pool_scale']), row(P['mu_shift']), row(P['rw_w0']), w2,
            row(P['rw_a0']), a2, P['rw_g2'].astype(BF16), row(P['rw_kk']), row(P['rw_ka']),
            row(P['rw_rk']), row(P['rw_lnx_w']), row(P['rw_lnx_b']), bdiag)


def _full_spec(x):
    nd = x.ndim
    return pl.BlockSpec(x.shape, lambda *_: (0,) * nd)


def mix0_seq(x, g, w_in, w_out, P):
    b, t, _ = x.shape
    tt = _row_tile(t, 512)
    assert tt % CHUNK == 0 and tt >= POOL_HALO
    prm = _mix0_params(P)
    ti = jnp.arange(tt)
    same_chunk = (ti[:, None] // CHUNK) == (ti[None, :] // CHUNK)
    tril = (same_chunk & (ti[:, None] >= ti[None, :])).astype(BF16)
    blk = same_chunk.astype(BF16)
    return pl.pallas_call(
        _mix0_seq_body,
        grid=(b, t // tt),
        in_specs=[pl.BlockSpec((1, tt, D_MODEL), lambda i, j: (i, j, 0)),
                  pl.BlockSpec((1, D_MODEL), lambda i, j: (0, 0)), _full_spec(w_in), _full_spec(w_out)]
                 + [_full_spec(p) for p in prm] + [_full_spec(tril), _full_spec(blk)],
        out_specs=[pl.BlockSpec((1, tt, D_MODEL), lambda i, j: (i, j, 0)),
                   pl.BlockSpec((1, RWKV_HEADS, RWKV_HEAD, RWKV_HEAD), lambda i, j: (i, 0, 0, 0)),
                   pl.BlockSpec((1, POOL_HALO, POOL_WIDTH), lambda i, j: (i, 0, 0)),
                   pl.BlockSpec((1, SHIFT_HALO, SHIFT_WIDTH), lambda i, j: (i, 0, 0))],
        out_shape=[jax.ShapeDtypeStruct((b, t, D_MODEL), F32),
                   jax.ShapeDtypeStruct((b, RWKV_HEADS, RWKV_HEAD, RWKV_HEAD), F32),
                   jax.ShapeDtypeStruct((b, POOL_HALO, POOL_WIDTH), F32),
                   jax.ShapeDtypeStruct((b, SHIFT_HALO, SHIFT_WIDTH), F32)],
        scratch_shapes=[pltpu.VMEM((tt + POOL_HALO, POOL_WIDTH), F32),
                        pltpu.VMEM((tt + SHIFT_HALO, SHIFT_WIDTH), F32),
                        pltpu.VMEM((RWKV_WIDTH // GROUP_LANES, RWKV_HEAD, GROUP_LANES), F32),
                        pltpu.VMEM((tt, D_MODEL), F32)],
        compiler_params=pltpu.CompilerParams(dimension_semantics=("parallel", "arbitrary")),
        name="mix0_seq",
    )(x, g.reshape(1, D_MODEL), w_in, w_out, *prm, tril, blk)


def _to_leading(x):
    n = x.shape[0]
    ii = lax.broadcasted_iota(jnp.int32, (n, n, 1), 0)
    jj = lax.broadcasted_iota(jnp.int32, (n, n, 1), 1)
    return jnp.sum(jnp.where(ii == jj, x[None, :, :], 0.0), axis=1, keepdims=True)


def _from_leading(x3):
    n = x3.shape[0]
    ii = lax.broadcasted_iota(jnp.int32, (n, n, 1), 0)
    jj = lax.broadcasted_iota(jnp.int32, (n, n, 1), 1)
    return jnp.sum(jnp.where(ii == jj, x3, 0.0), axis=0)


def _mix0_step_body(h_ref, pool_prev_ref, shift_prev_ref, s_in_ref,
                    pool_w_ref, pool_scale_ref, mu_ref, w0_ref, w2_ref, a0_ref, a2_ref, g2_ref,
                    kk_ref, ka_ref, rk_ref, lnw_ref, lnb_ref, bdiag_ref,
                    o_ref, s_out_ref,
                    r_t, w_t, k_t, v_t, kk_t, bt_t, y_t, r_s, k_s, v_s, g_s):
    hh = pl.program_id(0)
    nb = h_ref.shape[0]

    @pl.when(hh == 0)
    def _():
        for g, win in enumerate(POOL_WINDOWS):
            cs = slice(g * POOL_GC, (g + 1) * POOL_GC)
            u = h_ref[:, cs]
            acc = u
            for j in range(1, win):
                acc = acc + pool_prev_ref[POOL_BUF - j][:, cs]
            cnt = jnp.float32(min(win, PAST_LEN + 1))
            o_ref[:, cs] = _pool_group_out(acc, cnt, u, pool_w_ref, pool_scale_ref, g)
        prm = (mu_ref, w0_ref, w2_ref, a0_ref, a2_ref, g2_ref, kk_ref, ka_ref, bdiag_ref)
        r, lw, kmod, v, kk, a, gate = _rwkv_prep(h_ref[:, POOL_WIDTH:IN_AB_WIDTH], shift_prev_ref[...], prm)
        r_s[...] = r
        k_s[...] = kmod
        v_s[...] = v
        g_s[...] = gate
        r_t[...] = r.T
        w_t[...] = jnp.exp(lw).T
        k_t[...] = kmod.T
        v_t[...] = v.T
        kk_t[...] = kk.T
        bt_t[...] = (kk * a).T

    rows = pl.ds(pl.multiple_of(hh * RWKV_HEAD, RWKV_HEAD), RWKV_HEAD)
    s = s_in_ref[...].T.reshape(RWKV_HEAD, RWKV_HEAD, nb)
    kk = kk_t[rows, :][None]
    s_kk = jnp.sum(s * kk, axis=1, keepdims=True)
    v3 = _to_leading(v_t[rows, :])
    s = s * w_t[rows, :][None] - s_kk * bt_t[rows, :][None] + v3 * k_t[rows, :][None]
    y3 = jnp.sum(s * r_t[rows, :][None], axis=1, keepdims=True)
    y_t[rows, :] = _from_leading(y3)
    s_out_ref[...] = s.reshape(RWKV_HEAD * RWKV_HEAD, nb).T

    @pl.when(hh == pl.num_programs(0) - 1)
    def _():
        o_ref[:, POOL_WIDTH:] = _rwkv_finish(y_t[...].T, r_s[...], k_s[...], v_s[...], g_s[...],
                                             rk_ref, lnw_ref, lnb_ref, bdiag_ref)


def mix0_step(h, pool_prev, shift_prev, s_prev, P):
    b = h.shape[0]
    prm = _mix0_params(P)
    hw = RWKV_HEAD * RWKV_HEAD
    s2 = s_prev.reshape(b, RWKV_HEADS * hw)
    pool_t = jnp.swapaxes(pool_prev, 0, 1)
    tvec = lambda: pltpu.VMEM((RWKV_WIDTH, b), F32)
    svec = lambda: pltpu.VMEM((b, RWKV_WIDTH), F32)
    out, s_new = pl.pallas_call(
        _mix0_step_body,
        grid=(RWKV_HEADS,),
        in_specs=[_full_spec(h), _full_spec(pool_t), _full_spec(shift_prev),
                  pl.BlockSpec((b, hw), lambda i: (0, i))] + [_full_spec(x) for x in prm],
        out_specs=[pl.BlockSpec((b, D_MODEL), lambda i: (0, 0)),
                   pl.BlockSpec((b, hw), lambda i: (0, i))],
        out_shape=[jax.ShapeDtypeStruct((b, D_MODEL), F32),
                   jax.ShapeDtypeStruct((b, RWKV_HEADS * hw), F32)],
        scratch_shapes=[tvec(), tvec(), tvec(), tvec(), tvec(), tvec(), tvec(),
                        svec(), svec(), svec(), svec()],
        compiler_params=pltpu.CompilerParams(dimension_semantics=("arbitrary",)),
        name="mix0_step",
    )(h, pool_t, shift_prev, s2, *prm)
    return out, s_new.reshape(b, RWKV_HEADS, RWKV_HEAD, RWKV_HEAD)


def _conv_seq_body(x_ref, g_ref, wi_ref, cw_ref, wo_ref, o_ref, tail_ref, ext):
    t = pl.program_id(1)
    tt = o_ref.shape[1]
    c = D_MODEL

    @pl.when(t == 0)
    def _():
        ext[0:SHIFT_HALO, :] = jnp.zeros((SHIFT_HALO, c), F32)

    x = x_ref[0]
    h = jnp.dot(_rmsnorm(x, g_ref[...]).astype(BF16), wi_ref[...], preferred_element_type=F32)
    ext[SHIFT_HALO:SHIFT_HALO + tt, :] = h[:, c:2 * c] * h[:, 2 * c:3 * c]
    z = cw_ref[0:1, :] * ext[SHIFT_HALO - 2:SHIFT_HALO - 2 + tt, :]
    z = z + cw_ref[1:2, :] * ext[SHIFT_HALO - 1:SHIFT_HALO - 1 + tt, :]
    z = z + cw_ref[2:3, :] * ext[SHIFT_HALO:SHIFT_HALO + tt, :]
    gated = (h[:, 0:c] * z).astype(BF16)
    o_ref[0] = x + jnp.dot(gated, wo_ref[...], preferred_element_type=F32)
    ext[0:SHIFT_HALO, :] = ext[tt:tt + SHIFT_HALO, :]

    @pl.when(t == pl.num_programs(1) - 1)
    def _():
        tail_ref[0] = ext[SHIFT_HALO - (CONV_WIDTH - 1):SHIFT_HALO, :]


def conv_seq(x, g, w_in, conv_w, w_out):
    b, t, c = x.shape
    tt = _row_tile(t, 512)
    return pl.pallas_call(
        _conv_seq_body,
        grid=(b, t // tt),
        in_specs=[pl.BlockSpec((1, tt, c), lambda i, j: (i, j, 0)),
                  pl.BlockSpec((1, c), lambda i, j: (0, 0)),
                  _full_spec(w_in), _full_spec(conv_w), _full_spec(w_out)],
        out_specs=[pl.BlockSpec((1, tt, c), lambda i, j: (i, j, 0)),
                   pl.BlockSpec((1, CONV_WIDTH - 1, c), lambda i, j: (i, 0, 0))],
        out_shape=[jax.ShapeDtypeStruct((b, t, c), F32),
                   jax.ShapeDtypeStruct((b, CONV_WIDTH - 1, c), F32)],
        scratch_shapes=[pltpu.VMEM((tt + SHIFT_HALO, c), F32)],
        compiler_params=pltpu.CompilerParams(dimension_semantics=("parallel", "arbitrary")),
        name="conv_seq",
    )(x, g.reshape(1, c), w_in, conv_w, w_out)


def _conv_step_body(h_ref, x_ref, p0_ref, p1_ref, cw_ref, wo_ref, o_ref, e_ref):
    c = D_MODEL
    e = h_ref[:, c:2 * c] * h_ref[:, 2 * c:3 * c]
    z = cw_ref[0:1, :] * p0_ref[...] + cw_ref[1:2, :] * p1_ref[...] + cw_ref[2:3, :] * e
    gated = (h_ref[:, 0:c] * z).astype(BF16)
    o_ref[...] = x_ref[...] + jnp.dot(gated, wo_ref[...], preferred_element_type=F32)
    e_ref[...] = e


def conv_step(h, x, prev, conv_w, w_out):
    b = h.shape[0]
    c = D_MODEL
    args = (h, x, prev[:, 0, :], prev[:, 1, :], conv_w, w_out)
    return pl.pallas_call(
        _conv_step_body,
        grid=(1,),
        in_specs=[_full_spec(a) for a in args],
        out_specs=[pl.BlockSpec((b, c), lambda i: (0, 0)), pl.BlockSpec((b, c), lambda i: (0, 0))],
        out_shape=[jax.ShapeDtypeStruct((b, c), F32), jax.ShapeDtypeStruct((b, c), F32)],
        compiler_params=pltpu.CompilerParams(dimension_semantics=("arbitrary",)),
        name="conv_step",
    )(*args)


def _xattn(x2, i, mem_k, mem_v, W, seq_shape, extra_rows=0, into=None):
    last = i == W['norm_xattn'].shape[0] - 1
    if seq_shape is not None:
        b, t = seq_shape
        y = xattn_seq(x2.reshape(b, t, D_MODEL), W['norm_xattn'][i], W['w_xq'][i], W['w_xo'][i],
                      mem_k, mem_v, i, row_tiles=last, extra_rows=extra_rows)
        return y if last else y.reshape(b * t, D_MODEL)
    return xattn_step(x2, W['norm_xattn'][i], W['w_xq'][i], W['w_xo'][i], mem_k, mem_v, i,
                      row_tiles=last, into=into)


def _ffn0(x2, W):
    return ffn_dense(x2, W['norm_ffn'][0], W['ffn_gate'][0], W['ffn_up'][0], W['ffn_down'][0], FFN_TILE)


def _trunk_seq(x, mem_k, mem_v, W, extra_rows):
    b, t, d = x.shape
    P0 = {k: v[0] for k, v in W['mix0'].items()}
    x1, wkv, pool_tail, shift_tail = mix0_seq(x, W['norm_mix'][0], W['w_in_ab'][0], W['w_out_ab'][0], P0)
    pool = pool_tail[:, POOL_HALO - POOL_BUF:]
    shift = shift_tail[:, SHIFT_HALO - 1]
    x2 = _ffn0(_xattn(x1.reshape(b * t, d), 0, mem_k, mem_v, W, (b, t)), W)
    x3, conv = conv_seq(x2.reshape(b, t, d), W['norm_mix'][1], W['w_in_c'][0], W['conv_w'][0], W['w_out_c'][0])
    x8 = _xattn(x3.reshape(b * t, d), 1, mem_k, mem_v, W, (b, t), extra_rows=extra_rows)
    return x8, pool[None], shift[None], wkv[None], conv[None]


def _trunk_step(x, mem_k, mem_v, pool_prev, shift_prev, wkv_prev, conv_prev, W, into):
    b, _, d = x.shape
    x2 = x.reshape(b, d)
    P0 = {k: v[0] for k, v in W['mix0'].items()}
    h = norm_matmul(x2, W['norm_mix'][0], W['w_in_ab'][0])
    mix, wkv = mix0_step(h, pool_prev[0], shift_prev[0], wkv_prev[0], P0)
    pool = jnp.concatenate([pool_prev[0][:, 1:], h[:, None, :POOL_WIDTH]], axis=1)
    shift = h[:, POOL_WIDTH:]
    x2 = matmul_res(mix, W['w_out_ab'][0], x2)
    x2 = _ffn0(_xattn(x2, 0, mem_k, mem_v, W, None), W)
    h = norm_matmul(x2, W['norm_mix'][1], W['w_in_c'][0])
    x2, e = conv_step(h, x2, conv_prev[0], W['conv_w'][0], W['w_out_c'][0])
    conv = jnp.concatenate([conv_prev[0][:, 1:], e[:, None]], axis=1)
    x8 = _xattn(x2, 1, mem_k, mem_v, W, None, into=into)
    return x8, pool[None], shift[None], wkv[None], conv[None]


def kernel(x_prompt, x_sample, mem_prompt, cache_mem_k, cache_mem_v, state_pool, state_shift, state_wkv, state_conv, norm_mix, norm_xattn, norm_mem, norm_ffn, norm_final, w_xq, w_xk, w_xv, w_xo, w_in_ab, pool_w, pool_scale, mu_shift, rw_w0, rw_w2, rw_a0, rw_a2, rw_g2, rw_kk, rw_ka, rw_rk, rw_lnx_w, rw_lnx_b, w_out_ab, ffn_gate, ffn_up, ffn_down, w_in_c, conv_w, w_out_c, router_w, router_b, moe_gate, moe_up, moe_down):
    depth = norm_mix.shape[0]
    assert depth == 2 and w_in_ab.shape[0] == 1 and w_in_c.shape[0] == 1
    bp = x_prompt.shape[0]
    bs = x_sample.shape[0]
    d = D_MODEL
    bf = lambda w: w.astype(BF16)
    W = dict(norm_mix=norm_mix, norm_xattn=norm_xattn, norm_ffn=norm_ffn, norm_final=norm_final,
             w_xq=bf(w_xq), w_xo=bf(w_xo), w_in_ab=bf(w_in_ab), w_out_ab=bf(w_out_ab),
             ffn_gate=bf(ffn_gate), ffn_up=bf(ffn_up), ffn_down=bf(ffn_down),
             w_in_c=bf(w_in_c), conv_w=conv_w, w_out_c=bf(w_out_c),
             router_w=router_w, router_b=router_b,
             moe_gate=bf(moe_gate), moe_up=bf(moe_up), moe_down=bf(moe_down),
             mix0=dict(pool_w=pool_w, pool_scale=pool_scale, mu_shift=mu_shift, rw_w0=rw_w0, rw_w2=rw_w2,
                       rw_a0=rw_a0, rw_a2=rw_a2, rw_g2=rw_g2, rw_kk=rw_kk, rw_ka=rw_ka, rw_rk=rw_rk,
                       rw_lnx_w=rw_lnx_w, rw_lnx_b=rw_lnx_b))

    mk, mv = mem_kv(mem_prompt.reshape(bp * N_MEM, d), norm_mem, bf(w_xk), bf(w_xv))
    mk = mk.reshape(depth, bp, N_MEM, d)
    mv = mv.reshape(depth, bp, N_MEM, d)

    n_p = bp * x_prompt.shape[1]
    x8, pool_p, shift_p, wkv_p, conv_p = _trunk_seq(x_prompt, mk, mv, W, bs)
    x8, pool_s, shift_s, wkv_s, conv_s = _trunk_step(x_sample, cache_mem_k, cache_mem_v, state_pool,
                                                      state_shift, state_wkv, state_conv, W, (x8, n_p))
    y_p, y_s = moe_top2_norm(x8, [n_p, bs], norm_ffn[1], router_w[0], router_b[0], W['moe_gate'][0],
                             W['moe_up'][0], W['moe_down'][0], norm_final)
    y_p = y_p.reshape(x_prompt.shape)
    y_s = y_s.reshape(x_sample.shape)
    mem_k_p = mk.reshape(depth, bp, N_MEM, X_HEADS, X_HEAD_DIM)
    mem_v_p = mv.reshape(depth, bp, N_MEM, X_HEADS, X_HEAD_DIM)
    return (y_p, y_s, pool_p, pool_s, shift_p, shift_s, wkv_p, wkv_s, conv_p, conv_s, mem_k_p, mem_v_p)
```

```python
import functools

import jax
import jax.numpy as jnp
from jax import lax
from jax.experimental import pallas as pl
from jax.experimental.pallas import tpu as pltpu

F32 = jnp.float32
BF16 = jnp.bfloat16

D_MODEL = 1024
POOL_WIDTH = 512
POOL_GROUPS = 4
POOL_GC = 128
POOL_WINDOWS = (2, 4, 8, 16)
POOL_BUF = 15
RWKV_WIDTH = 512
RWKV_HEAD = 64
RWKV_HEADS = 8
W_RANK = 64
A_RANK = 64
G_RANK = 128
SHIFT_WIDTH = 3 * RWKV_WIDTH + W_RANK + A_RANK + G_RANK
IN_AB_WIDTH = POOL_WIDTH + SHIFT_WIDTH
LNX_EPS = 64e-5
CONV_WIDTH = 3
N_EXPERTS = 8
N_MEM = 256
X_HEADS = 4
X_HEAD_DIM = 256
RMS_EPS = 1e-6
PAST_LEN = 16384

CHUNK = 64
GROUP_LANES = 256
POOL_HALO = 16
SHIFT_HALO = 8
ROUTER_PAD = 128
TOP_K = 2
ROW_SUBLANES = 8
MOE_ROWS = 512
FFN_TILE = 1408
MOE_TILE = 1792
XATTN_STEP_VMEM = 48 * 1024 * 1024


def _rmsnorm(x, g):
    ms = jnp.mean(x * x, axis=-1, keepdims=True)
    return x * lax.rsqrt(ms + RMS_EPS) * g


def _dot(a, b):
    return jnp.dot(a.astype(BF16), b.astype(BF16), preferred_element_type=F32)


def _dot_nt(a, b):
    return lax.dot_general(a.astype(BF16), b.astype(BF16), (((1,), (1,)), ((), ())),
                           preferred_element_type=F32)


def _dot_tn(a, b):
    return lax.dot_general(a.astype(BF16), b.astype(BF16), (((0,), (0,)), ((), ())),
                           preferred_element_type=F32)


def _split3(x):
    hi = x.astype(BF16)
    r1 = x - hi.astype(F32)
    mid = r1.astype(BF16)
    lo = (r1 - mid.astype(F32)).astype(BF16)
    return hi, mid, lo


def _dot_exact_lhs(m01, x):
    hi, mid, lo = _split3(x)
    f = lambda p: jnp.dot(m01, p, preferred_element_type=F32)
    return f(hi) + f(mid) + f(lo)


def _softplus(x):
    return jnp.maximum(x, 0.0) + jnp.log1p(jnp.exp(-jnp.abs(x)))


def _sigmoid(x):
    return 1.0 / (1.0 + jnp.exp(-x))


def _row_tile(m, want):
    t = min(m, want)
    assert m % t == 0, (m, t)
    return t


def _norm_matmul_body(x_ref, g_ref, w_ref, o_ref):
    xn = _rmsnorm(x_ref[...], g_ref[...]).astype(BF16)
    o_ref[...] = jnp.dot(xn, w_ref[...], preferred_element_type=F32)


def norm_matmul(x, g, w):
    m, k = x.shape
    n = w.shape[1]
    tm = _row_tile(m, 512)
    return pl.pallas_call(
        _norm_matmul_body,
        grid=(m // tm,),
        in_specs=[pl.BlockSpec((tm, k), lambda i: (i, 0)),
                  pl.BlockSpec((1, k), lambda i: (0, 0)),
                  pl.BlockSpec((k, n), lambda i: (0, 0))],
        out_specs=pl.BlockSpec((tm, n), lambda i: (i, 0)),
        out_shape=jax.ShapeDtypeStruct((m, n), F32),
        compiler_params=pltpu.CompilerParams(dimension_semantics=("parallel",)),
        name="norm_matmul",
    )(x, g.reshape(1, k), w)


def _mem_kv_body(m_ref, g_ref, wk_ref, wv_ref, k_ref, v_ref):
    xn = _rmsnorm(m_ref[...], g_ref[0]).astype(BF16)
    k_ref[0] = jnp.dot(xn, wk_ref[0], preferred_element_type=F32)
    v_ref[0] = jnp.dot(xn, wv_ref[0], preferred_element_type=F32)


def mem_kv(mem, g, wk, wv):
    r, d = mem.shape
    nl = g.shape[0]
    tm = _row_tile(r, 512)
    out = jax.ShapeDtypeStruct((nl, r, d), F32)
    return pl.pallas_call(
        _mem_kv_body,
        grid=(nl, r // tm),
        in_specs=[pl.BlockSpec((tm, d), lambda l, j: (j, 0)),
                  pl.BlockSpec((1, 1, d), lambda l, j: (l, 0, 0)),
                  pl.BlockSpec((1, d, d), lambda l, j: (l, 0, 0)),
                  pl.BlockSpec((1, d, d), lambda l, j: (l, 0, 0))],
        out_specs=[pl.BlockSpec((1, tm, d), lambda l, j: (l, j, 0)),
                   pl.BlockSpec((1, tm, d), lambda l, j: (l, j, 0))],
        out_shape=[out, out],
        compiler_params=pltpu.CompilerParams(dimension_semantics=("parallel", "parallel")),
        name="mem_kv",
    )(mem, g.reshape(nl, 1, d), wk, wv)


def _matmul_res_body(a_ref, w_ref, r_ref, o_ref):
    o_ref[...] = r_ref[...] + jnp.dot(a_ref[...].astype(BF16), w_ref[...], preferred_element_type=F32)


def matmul_res(a, w, res):
    m, k = a.shape
    n = w.shape[1]
    tm = _row_tile(m, 512)
    return pl.pallas_call(
        _matmul_res_body,
        grid=(m // tm,),
        in_specs=[pl.BlockSpec((tm, k), lambda i: (i, 0)),
                  pl.BlockSpec((k, n), lambda i: (0, 0)),
                  pl.BlockSpec((tm, n), lambda i: (i, 0))],
        out_specs=pl.BlockSpec((tm, n), lambda i: (i, 0)),
        out_shape=jax.ShapeDtypeStruct((m, n), F32),
        compiler_params=pltpu.CompilerParams(dimension_semantics=("parallel",)),
        name="matmul_res",
    )(a, w, res)


def _ffn_body(x_ref, g_ref, wg_ref, wu_ref, wd_ref, o_ref, xn_ref, acc_ref):
    f = pl.program_id(1)

    @pl.when(f == 0)
    def _():
        xn_ref[...] = _rmsnorm(x_ref[...], g_ref[...]).astype(BF16)
        acc_ref[...] = jnp.zeros_like(acc_ref)

    xn = xn_ref[...]
    gate = jnp.dot(xn, wg_ref[...], preferred_element_type=F32)
    up = jnp.dot(xn, wu_ref[...], preferred_element_type=F32)
    hid = (gate * _sigmoid(gate) * up).astype(BF16)
    acc_ref[...] += jnp.dot(hid, wd_ref[...], preferred_element_type=F32)

    @pl.when(f == pl.num_programs(1) - 1)
    def _():
        o_ref[...] = x_ref[...] + acc_ref[...]


def ffn_dense(x, g, wg, wu, wd, tf):
    m, d = x.shape
    ff = wg.shape[1]
    tm = _row_tile(m, 512)
    return pl.pallas_call(
        _ffn_body,
        grid=(m // tm, ff // tf),
        in_specs=[pl.BlockSpec((tm, d), lambda i, f: (i, 0)),
                  pl.BlockSpec((1, d), lambda i, f: (0, 0)),
                  pl.BlockSpec((d, tf), lambda i, f: (0, f)),
                  pl.BlockSpec((d, tf), lambda i, f: (0, f)),
                  pl.BlockSpec((tf, d), lambda i, f: (f, 0))],
        out_specs=pl.BlockSpec((tm, d), lambda i, f: (i, 0)),
        out_shape=jax.ShapeDtypeStruct((m, d), F32),
        scratch_shapes=[pltpu.VMEM((tm, d), BF16), pltpu.VMEM((tm, d), F32)],
        compiler_params=pltpu.CompilerParams(dimension_semantics=("parallel", "arbitrary")),
        name="ffn_dense",
    )(x, g.reshape(1, d), wg, wu, wd)


def _router_body(x_ref, g_ref, w_ref, b_ref, oi_ref, ow_ref):
    xn = _rmsnorm(_load_row_tiles(x_ref, oi_ref.shape[0]), g_ref[...])
    logits = _dot_3pass(xn, w_ref[...]) + b_ref[...]
    lane = lax.broadcasted_iota(jnp.int32, logits.shape, 1)
    neg = jnp.float32(-jnp.inf)
    logits = jnp.where(lane < N_EXPERTS, logits, neg)
    m1 = jnp.max(logits, axis=-1, keepdims=True)
    i1 = jnp.min(jnp.where(logits == m1, lane, ROUTER_PAD), axis=-1, keepdims=True)
    rest = jnp.where(lane == i1, neg, logits)
    m2 = jnp.max(rest, axis=-1, keepdims=True)
    i2 = jnp.min(jnp.where(rest == m2, lane, ROUTER_PAD), axis=-1, keepdims=True)
    e2 = jnp.exp(m2 - m1)
    den = 1.0 + e2
    slot = lax.broadcasted_iota(jnp.int32, oi_ref.shape, 1)
    oi_ref[...] = jnp.where(slot == 0, i1, i2)
    ow_ref[...] = jnp.where(slot == 0, 1.0 / den, e2 / den)


def _dot_3pass(x, w):
    xh = x.astype(BF16)
    xm = (x - xh.astype(F32)).astype(BF16)
    wh = w.astype(BF16)
    wm = (w - wh.astype(F32)).astype(BF16)
    f = lambda p, q: jnp.dot(p, q, preferred_element_type=F32)
    return f(xh, wh) + (f(xh, wm) + f(xm, wh))


def moe_router(x8, row0, m, g, router_w, router_b):
    d = D_MODEL
    tm = _row_tile(m, 512)
    assert row0 % tm == 0
    blk0 = row0 // tm
    w = jnp.zeros((d, ROUTER_PAD), F32).at[:, :N_EXPERTS].set(router_w)
    b = jnp.zeros((1, ROUTER_PAD), F32).at[0, :N_EXPERTS].set(router_b)
    return pl.pallas_call(
        _router_body,
        grid=(m // tm,),
        in_specs=[pl.BlockSpec((tm * ROW_SUBLANES, 128), lambda i: (blk0 + i, 0)),
                  pl.BlockSpec((1, d), lambda i: (0, 0)),
                  pl.BlockSpec((d, ROUTER_PAD), lambda i: (0, 0)),
                  pl.BlockSpec((1, ROUTER_PAD), lambda i: (0, 0))],
        out_specs=[pl.BlockSpec((tm, TOP_K), lambda i: (i, 0)), pl.BlockSpec((tm, TOP_K), lambda i: (i, 0))],
        out_shape=[jax.ShapeDtypeStruct((m, TOP_K), jnp.int32), jax.ShapeDtypeStruct((m, TOP_K), F32)],
        compiler_params=pltpu.CompilerParams(dimension_semantics=("parallel",)),
        name="moe_router",
    )(x8, g.reshape(1, d), w, b)


def _row_tiles_to_2d(ref, slot, n):
    return jnp.concatenate([ref[slot, pl.ds(j, n, stride=ROW_SUBLANES), :] for j in range(ROW_SUBLANES)],
                           axis=1)


def _row_copy(src_hbm, row, dst, slot, i, sem):
    return pltpu.make_async_copy(src_hbm.at[pl.ds(row * ROW_SUBLANES, ROW_SUBLANES)],
                                 dst.at[slot, pl.ds(i * ROW_SUBLANES, ROW_SUBLANES)], sem.at[slot])


def _gather_start(src_hbm, rows_smem, dst, slot, n, sem):
    def body(i, carry):
        for k in range(2):
            r = 2 * i + k
            _row_copy(src_hbm, rows_smem[0, 0, r], dst, slot, r, sem).start(priority=k)
        return carry
    lax.fori_loop(0, n // 2, body, 0, unroll=4)


def _gather_wait(src_hbm, dst, slot, n, sem):
    pltpu.make_async_copy(src_hbm.at[pl.ds(0, n * ROW_SUBLANES)], dst.at[slot], sem.at[slot]).wait()


def _moe_group_body(te_ref, nu_ref, tok0_ref, tokn_ref, x8_ref, g_ref, wg_ref, wu_ref, wd_ref,
                    o_ref, xbuf, sem, xn_ref, acc_ref, *, nf):
    t = pl.program_id(0)
    f = pl.program_id(1)
    tm = xn_ref.shape[0]
    per_step = tm // nf
    used = t < nu_ref[0]
    slot = t % 2

    @pl.when(jnp.logical_and(t == 0, f == 0))
    def _():
        _gather_start(x8_ref, tok0_ref, xbuf, 0, tm, sem)

    @pl.when(jnp.logical_and(used, f == 0))
    def _():
        _gather_wait(x8_ref, xbuf, slot, tm, sem)
        x = _row_tiles_to_2d(xbuf, slot, tm)
        xn_ref[...] = _rmsnorm(x, g_ref[...]).astype(BF16)
        acc_ref[...] = jnp.zeros_like(acc_ref)

    @pl.when(used)
    def _():
        base = f * per_step
        for i in range(per_step):
            _row_copy(x8_ref, tokn_ref[0, 0, base + i], xbuf, 1 - slot, base + i, sem).start()
        xn = xn_ref[...]
        gate = jnp.dot(xn, wg_ref[0], preferred_element_type=F32)
        up = jnp.dot(xn, wu_ref[0], preferred_element_type=F32)
        hid = (gate * _sigmoid(gate) * up).astype(BF16)
        acc_ref[...] += jnp.dot(hid, wd_ref[0], preferred_element_type=F32)

    @pl.when(f == nf - 1)
    def _():
        y = jnp.where(used, acc_ref[...], 0.0)
        for j in range(ROW_SUBLANES):
            o_ref[pl.ds(j, tm, stride=ROW_SUBLANES), :] = y[:, j * 128:(j + 1) * 128]

    @pl.when(jnp.logical_and(t == nu_ref[0] - 1, f == nf - 1))
    def _():
        _gather_wait(x8_ref, xbuf, 1 - slot, tm, sem)


def moe_grouped(x8, g, tile_expert, n_used, row_token, wg, wu, wd, tf):
    n_tiles, _, tm = row_token.shape
    d = D_MODEL
    ne, _, ff = wg.shape
    nf = ff // tf
    assert tm % nf == 0
    live_f = lambda t, f, nu: jnp.where(t < nu[0], f, nf - 1)
    smem_rows = lambda imap: pl.BlockSpec((1, 1, tm), imap, memory_space=pltpu.SMEM)
    grid_spec = pltpu.PrefetchScalarGridSpec(
        num_scalar_prefetch=2,
        grid=(n_tiles, nf),
        in_specs=[smem_rows(lambda t, f, te, nu: (0, 0, 0)),
                  smem_rows(lambda t, f, te, nu: (jnp.minimum(t + 1, nu[0] - 1), 0, 0)),
                  pl.BlockSpec(memory_space=pl.ANY),
                  pl.BlockSpec((1, d), lambda t, f, te, nu: (0, 0)),
                  pl.BlockSpec((1, d, tf), lambda t, f, te, nu: (te[t], 0, live_f(t, f, nu))),
                  pl.BlockSpec((1, d, tf), lambda t, f, te, nu: (te[t], 0, live_f(t, f, nu))),
                  pl.BlockSpec((1, tf, d), lambda t, f, te, nu: (te[t], live_f(t, f, nu), 0))],
        out_specs=pl.BlockSpec((tm * ROW_SUBLANES, 128), lambda t, f, te, nu: (t, 0)),
        scratch_shapes=[pltpu.VMEM((2, tm * ROW_SUBLANES, 128), F32), pltpu.SemaphoreType.DMA((2,)),
                        pltpu.VMEM((tm, d), BF16), pltpu.VMEM((tm, d), F32)])
    return pl.pallas_call(
        functools.partial(_moe_group_body, nf=nf),
        grid_spec=grid_spec,
        out_shape=jax.ShapeDtypeStruct((n_tiles * tm * ROW_SUBLANES, 128), F32),
        compiler_params=pltpu.CompilerParams(dimension_semantics=("arbitrary", "arbitrary")),
        name="moe_grouped",
    )(tile_expert, n_used, row_token, row_token, x8, g.reshape(1, d), wg, wu, wd)


def _moe_combine_body(pos0_ref, posn_ref, y8_ref, x_ref, w_ref, g_ref, o_ref, ybuf, sem):
    t = pl.program_id(0)
    tc = o_ref.shape[0]
    n = TOP_K * tc

    @pl.when(t == 0)
    def _():
        _gather_start(y8_ref, pos0_ref, ybuf, 0, n, sem)

    @pl.when(t + 1 < pl.num_programs(0))
    def _():
        _gather_start(y8_ref, posn_ref, ybuf, (t + 1) % 2, n, sem)

    slot = t % 2
    _gather_wait(y8_ref, ybuf, slot, n, sem)
    y = _row_tiles_to_2d(ybuf, slot, n)
    w = w_ref[...]
    moe = w[:, 0:1] * y[0:tc] + w[:, 1:2] * y[tc:n]
    o_ref[...] = _rmsnorm(_load_row_tiles(x_ref, tc) + moe, g_ref[...])


def moe_combine_norm(y8, x8, row0, pos, topw, g):
    d = D_MODEL
    nt, _, n = pos.shape
    tc = n // TOP_K
    m = nt * tc
    assert row0 % tc == 0
    blk0 = row0 // tc
    smem_rows = lambda imap: pl.BlockSpec((1, 1, n), imap, memory_space=pltpu.SMEM)
    return pl.pallas_call(
        _moe_combine_body,
        grid=(nt,),
        in_specs=[smem_rows(lambda t: (0, 0, 0)),
                  smem_rows(lambda t: (jnp.minimum(t + 1, nt - 1), 0, 0)),
                  pl.BlockSpec(memory_space=pl.ANY),
                  pl.BlockSpec((tc * ROW_SUBLANES, 128), lambda t: (blk0 + t, 0)),
                  pl.BlockSpec((tc, TOP_K), lambda t: (t, 0)),
                  pl.BlockSpec((1, d), lambda t: (0, 0))],
        out_specs=pl.BlockSpec((tc, d), lambda t: (t, 0)),
        out_shape=jax.ShapeDtypeStruct((m, d), F32),
        scratch_shapes=[pltpu.VMEM((2, n * ROW_SUBLANES, 128), F32), pltpu.SemaphoreType.DMA((2,))],
        compiler_params=pltpu.CompilerParams(dimension_semantics=("arbitrary",)),
        name="moe_combine_norm",
    )(pos, pos, y8, x8, topw, g.reshape(1, d))


def _moe_plan(topi, tm):
    n = topi.shape[0]
    a = TOP_K * n
    n_tiles = (a + N_EXPERTS * (tm - 1) + tm - 1) // tm
    e = topi.reshape(a)
    eid = jnp.arange(N_EXPERTS, dtype=jnp.int32)
    onehot = (e[:, None] == eid[None, :]).astype(jnp.int32)
    csum = jnp.cumsum(onehot, axis=0)
    cnt = csum[-1]
    rank = jnp.sum((csum - onehot) * onehot, axis=1)
    tiles = (cnt + tm - 1) // tm
    tile_end = jnp.cumsum(tiles)
    tile_start = tile_end - tiles
    n_used = tile_end[-1]
    pos = (jnp.sum(onehot * tile_start[None, :], axis=1) * tm + rank).reshape(n, TOP_K)
    tid = jnp.minimum(jnp.arange(n_tiles, dtype=jnp.int32), n_used - 1)
    tile_expert = jnp.sum(tid[:, None] >= tile_end[None, :], axis=1).astype(jnp.int32)
    fill = tiles * tm - cnt
    j = jnp.arange(tm - 1, dtype=jnp.int32)
    dummy_key = jnp.where(j[None, :] < fill[:, None], eid[:, None], N_EXPERTS).reshape(-1)
    dummy_tok = jnp.tile(j % n, N_EXPERTS)
    spare = n_tiles * tm - a - dummy_key.shape[0]
    keys = jnp.concatenate([e, dummy_key, jnp.full((spare,), N_EXPERTS, jnp.int32)])
    toks = jnp.concatenate([jnp.arange(a, dtype=jnp.int32) // TOP_K, dummy_tok,
                            jnp.zeros((spare,), jnp.int32)])
    _, row_token = lax.sort((keys, toks), num_keys=1, is_stable=True)
    return tile_expert, n_used.reshape(1).astype(jnp.int32), row_token.reshape(n_tiles, 1, tm), pos


def moe_top2_norm(x8, groups, g, router_w, router_b, wg, wu, wd, g_final):
    starts = [sum(groups[:i]) for i in range(len(groups))]
    tops = [moe_router(x8, s, m, g, router_w, router_b) for s, m in zip(starts, groups)]
    topi = jnp.concatenate([t[0] for t in tops], axis=0)
    tile_expert, n_used, row_token, pos = _moe_plan(topi, MOE_ROWS)
    y8 = moe_grouped(x8, g, tile_expert, n_used, row_token, wg, wu, wd, MOE_TILE)
    outs = []
    for s, m, (_, topw) in zip(starts, groups, tops):
        tc = _row_tile(m, 256)
        p = pos[s:s + m].reshape(m // tc, tc, TOP_K)
        p = jnp.swapaxes(p, 1, 2).reshape(m // tc, 1, TOP_K * tc)
        outs.append(moe_combine_norm(y8, x8, s, p, topw, g_final))
    return outs


def _store_row_tiles(o_ref, y):
    n = y.shape[0]
    for j in range(ROW_SUBLANES):
        o_ref[pl.ds(j, n, stride=ROW_SUBLANES), :] = y[:, j * 128:(j + 1) * 128]


def _load_row_tiles(ref, n):
    return jnp.concatenate([ref[pl.ds(j, n, stride=ROW_SUBLANES), :] for j in range(ROW_SUBLANES)], axis=1)


def _xattn_seq_body(x_ref, g_ref, wq_ref, wo_ref, k_ref, v_ref, o_ref, *, row_tiles, fill_step):
    if fill_step is not None:
        @pl.when(pl.program_id(0) == fill_step)
        def _():
            o_ref[...] = jnp.zeros_like(o_ref)

        @pl.when(pl.program_id(0) != fill_step)
        def _():
            _xattn_seq_tile(x_ref, g_ref, wq_ref, wo_ref, k_ref, v_ref, o_ref, row_tiles)
    else:
        _xattn_seq_tile(x_ref, g_ref, wq_ref, wo_ref, k_ref, v_ref, o_ref, row_tiles)


def _xattn_seq_tile(x_ref, g_ref, wq_ref, wo_ref, k_ref, v_ref, o_ref, row_tiles):
    x = x_ref[0]
    xn = _rmsnorm(x, g_ref[...]).astype(BF16)
    q = jnp.dot(xn, wq_ref[...], preferred_element_type=F32).astype(BF16)
    kb = k_ref[0, 0].astype(BF16)
    vb = v_ref[0, 0].astype(BF16)
    heads = []
    for h in range(X_HEADS):
        sl = slice(h * X_HEAD_DIM, (h + 1) * X_HEAD_DIM)
        s = _dot_nt(q[:, sl], kb[:, sl]) * (X_HEAD_DIM ** -0.5)
        mx = jnp.max(s, axis=-1, keepdims=True)
        ex = jnp.exp(s - mx)
        pr = ex / jnp.sum(ex, axis=-1, keepdims=True)
        heads.append(jnp.dot(pr.astype(BF16), vb[:, sl], preferred_element_type=F32))
    att = jnp.concatenate(heads, axis=-1).astype(BF16)
    y = x + jnp.dot(att, wo_ref[...], preferred_element_type=F32)
    if row_tiles:
        _store_row_tiles(o_ref, y)
    else:
        o_ref[0] = y


def xattn_seq(x, g, wq, wo, mem_k, mem_v, layer, row_tiles=False, extra_rows=0):
    b, t, d = x.shape
    tq = _row_tile(t, 512)
    nq = t // tq
    assert extra_rows <= tq and (extra_rows == 0 or row_tiles)
    steps = b * nq + (1 if extra_rows else 0)
    bi = lambda s: jnp.minimum(s // nq, b - 1)
    qi = lambda s: jnp.where(s < b * nq, s % nq, 0)
    if row_tiles:
        out_spec = pl.BlockSpec((tq * ROW_SUBLANES, 128), lambda s: (s, 0))
        out_shape = jax.ShapeDtypeStruct(((b * t + extra_rows) * ROW_SUBLANES, 128), F32)
    else:
        out_spec = pl.BlockSpec((1, tq, d), lambda s: (bi(s), qi(s), 0))
        out_shape = jax.ShapeDtypeStruct((b, t, d), F32)
    return pl.pallas_call(
        functools.partial(_xattn_seq_body, row_tiles=row_tiles, fill_step=b * nq if extra_rows else None),
        grid=(steps,),
        in_specs=[pl.BlockSpec((1, tq, d), lambda s: (bi(s), qi(s), 0)),
                  pl.BlockSpec((1, d), lambda s: (0, 0)),
                  pl.BlockSpec((d, d), lambda s: (0, 0)),
                  pl.BlockSpec((d, d), lambda s: (0, 0)),
                  pl.BlockSpec((1, 1, N_MEM, d), lambda s: (layer, bi(s), 0, 0)),
                  pl.BlockSpec((1, 1, N_MEM, d), lambda s: (layer, bi(s), 0, 0))],
        out_specs=out_spec,
        out_shape=out_shape,
        compiler_params=pltpu.CompilerParams(dimension_semantics=("parallel",)),
        name="xattn_seq",
    )(x, g.reshape(1, d), wq, wo, mem_k, mem_v)


XATTN_STEP_ROWS = 8
XATTN_STEP_KV = 4


def _xattn_step_body(x_ref, g_ref, wq_ref, wo_ref, k_ref, v_ref, *rest, row_tiles, aliased):
    o_ref, q_s, att_s = rest[1:] if aliased else rest
    j = pl.program_id(1)

    @pl.when(j == 0)
    def _():
        xn = _rmsnorm(x_ref[...], g_ref[...]).astype(BF16)
        q_s[...] = jnp.dot(xn, wq_ref[...], preferred_element_type=F32) * (X_HEAD_DIM ** -0.5)

    halves = X_HEAD_DIM // 128
    packed = [(r % X_HEADS) * X_HEAD_DIM + (r // X_HEADS) * 128 for r in range(halves * X_HEADS)]
    for b in range(XATTN_STEP_KV):
        rsel = pl.ds(j * XATTN_STEP_KV + b, 1)
        q_row = q_s[rsel, :]
        q8 = jnp.concatenate([q_row[:, c0:c0 + 128] for c0 in packed], axis=0)
        part = jnp.sum(k_ref[0, b] * q8[None], axis=-1, keepdims=True)
        s = part + pltpu.roll(part, X_HEADS, axis=1)
        mx = jnp.max(s, axis=0, keepdims=True)
        ex = jnp.exp(s - mx)
        den = jnp.sum(ex, axis=0)
        o8 = jnp.sum(ex * v_ref[0, b], axis=0) / den
        order = sorted(range(len(packed)), key=lambda r: packed[r])
        att_s[rsel, :] = jnp.concatenate([o8[r:r + 1, :] for r in order], axis=1)

    @pl.when(j == pl.num_programs(1) - 1)
    def _():
        y = x_ref[...] + jnp.dot(att_s[...].astype(BF16), wo_ref[...], preferred_element_type=F32)
        if row_tiles:
            _store_row_tiles(o_ref, y)
        else:
            o_ref[...] = y


def xattn_step(x, g, wq, wo, mem_k, mem_v, layer, row_tiles=False, into=None):
    b, d = x.shape
    rows, kvb = XATTN_STEP_ROWS, XATTN_STEP_KV
    assert b % rows == 0 and rows % kvb == 0
    nj = rows // kvb
    kv_spec = pl.BlockSpec((1, kvb, N_MEM, ROW_SUBLANES, 128), lambda i, j: (layer, i * nj + j, 0, 0, 0))
    args = [x, g.reshape(1, d), wq, wo, mem_k, mem_v]
    in_specs = [pl.BlockSpec((rows, d), lambda i, j: (i, 0)),
                pl.BlockSpec((1, d), lambda i, j: (0, 0)),
                pl.BlockSpec((d, d), lambda i, j: (0, 0)),
                pl.BlockSpec((d, d), lambda i, j: (0, 0)),
                kv_spec, kv_spec]
    aliases = {}
    if into is not None:
        buf, row0 = into
        assert row_tiles and row0 % rows == 0 and buf.shape == ((row0 + b) * ROW_SUBLANES, 128)
        blk0 = row0 // rows
        out_spec = pl.BlockSpec((rows * ROW_SUBLANES, 128), lambda i, j: (blk0 + i, 0))
        out_shape = jax.ShapeDtypeStruct(buf.shape, F32)
        aliases = {len(args): 0}
        args.append(buf)
        in_specs.append(pl.BlockSpec(memory_space=pl.ANY))
    elif row_tiles:
        out_spec = pl.BlockSpec((rows * ROW_SUBLANES, 128), lambda i, j: (i, 0))
        out_shape = jax.ShapeDtypeStruct((b * ROW_SUBLANES, 128), F32)
    else:
        out_spec = pl.BlockSpec((rows, d), lambda i, j: (i, 0))
        out_shape = jax.ShapeDtypeStruct((b, d), F32)
    return pl.pallas_call(
        functools.partial(_xattn_step_body, row_tiles=row_tiles, aliased=into is not None),
        grid=(b // rows, nj),
        in_specs=in_specs,
        out_specs=out_spec,
        out_shape=out_shape,
        input_output_aliases=aliases,
        scratch_shapes=[pltpu.VMEM((rows, d), F32), pltpu.VMEM((rows, d), F32)],
        compiler_params=pltpu.CompilerParams(dimension_semantics=("parallel", "arbitrary"),
                                             vmem_limit_bytes=XATTN_STEP_VMEM),
        name="xattn_step",
    )(*args)


def _head_sum(x, bdiag_ref):
    m = bdiag_ref[0:GROUP_LANES, 0:GROUP_LANES]
    parts = []
    for g in range(RWKV_WIDTH // GROUP_LANES):
        xs = x[:, g * GROUP_LANES:(g + 1) * GROUP_LANES]
        hi = xs.astype(BF16)
        lo = (xs - hi.astype(F32)).astype(BF16)
        parts.append(jnp.dot(hi, m, preferred_element_type=F32) + jnp.dot(lo, m, preferred_element_type=F32))
    return jnp.concatenate(parts, axis=1)


def _pool_group_out(acc, cnt, u, pool_w_ref, pool_scale_ref, g):
    cs = slice(g * POOL_GC, (g + 1) * POOL_GC)
    d = acc / cnt - u
    return _dot(d, pool_w_ref[g]) * pool_scale_ref[:, cs]


def _rwkv_prep(p, shifted, prm):
    (mu_ref, w0_ref, w2_ref, a0_ref, a2_ref, g2_ref, kk_ref, ka_ref, bdiag_ref) = prm
    m = p + (shifted - p) * mu_ref[...]
    c = RWKV_WIDTH
    r = m[:, 0:c]
    k = m[:, c:2 * c]
    v = m[:, 2 * c:3 * c]
    dwa = m[:, 3 * c:3 * c + W_RANK + A_RANK]
    dg = m[:, 3 * c + W_RANK + A_RANK:]
    w_log = -_softplus(-(w0_ref[...] + _dot(jnp.tanh(dwa), w2_ref[...]))) - 0.5
    lw = -jnp.exp(w_log)
    a = _sigmoid(a0_ref[...] + _dot(dwa, a2_ref[...]))
    gate = _dot(_sigmoid(dg), g2_ref[...])
    kk = k * kk_ref[...]
    kk = kk / jnp.maximum(jnp.sqrt(_head_sum(kk * kk, bdiag_ref)), 1e-12)
    kmod = k * (1.0 + (a - 1.0) * ka_ref[...])
    return r, lw, kmod, v, kk, a, gate


def _rwkv_finish(y, r, kmod, v, gate, rk_ref, lnw_ref, lnb_ref, bdiag_ref):
    mu = _head_sum(y, bdiag_ref) * (1.0 / RWKV_HEAD)
    dlt = y - mu
    var = _head_sum(dlt * dlt, bdiag_ref) * (1.0 / RWKV_HEAD)
    yn = dlt * lax.rsqrt(var + LNX_EPS) * lnw_ref[...] + lnb_ref[...]
    bonus = _head_sum(r * kmod * rk_ref[...], bdiag_ref) * v
    return (yn + bonus) * gate


def _chains(x, nc):
    return [x[c * CHUNK:(c + 1) * CHUNK, g * GROUP_LANES:(g + 1) * GROUP_LANES]
            for g in range(RWKV_WIDTH // GROUP_LANES) for c in range(nc)]


def _block_diag(x, m4):
    xb = x.astype(BF16)
    return jnp.concatenate([xb] * (GROUP_LANES // RWKV_HEAD), axis=0) * m4


def _mix0_seq_body(x_ref, gmix_ref, win_ref, wout_ref,
                   pool_w_ref, pool_scale_ref, mu_ref, w0_ref, w2_ref, a0_ref, a2_ref, g2_ref,
                   kk_ref, ka_ref, rk_ref, lnw_ref, lnb_ref, bdiag_ref, tril_ref, blk_ref,
                   o_ref, s_out_ref, pool_tail_ref, shift_tail_ref,
                   ext_u, ext_p, s_ref, ycat):
    t = pl.program_id(1)
    tt = o_ref.shape[1]
    nc = tt // CHUNK
    ng = RWKV_WIDTH // GROUP_LANES

    @pl.when(t == 0)
    def _():
        ext_u[0:POOL_HALO, :] = jnp.zeros((POOL_HALO, POOL_WIDTH), F32)
        ext_p[0:SHIFT_HALO, :] = jnp.zeros((SHIFT_HALO, SHIFT_WIDTH), F32)
        s_ref[...] = jnp.zeros_like(s_ref)

    x = x_ref[0]
    h = jnp.dot(_rmsnorm(x, gmix_ref[...]).astype(BF16), win_ref[...], preferred_element_type=F32)
    ext_u[POOL_HALO:POOL_HALO + tt, :] = h[:, 0:POOL_WIDTH]
    ext_p[SHIFT_HALO:SHIFT_HALO + tt, :] = h[:, POOL_WIDTH:IN_AB_WIDTH]

    pos = t * tt + lax.broadcasted_iota(jnp.int32, (tt, 1), 0)
    for g, win in enumerate(POOL_WINDOWS):
        cs = slice(g * POOL_GC, (g + 1) * POOL_GC)
        u = ext_u[POOL_HALO:POOL_HALO + tt, cs]
        acc = u
        for j in range(1, win):
            acc = acc + ext_u[POOL_HALO - j:POOL_HALO - j + tt, cs]
        cnt = jnp.minimum(win, pos + 1).astype(F32)
        ycat[:, cs] = _pool_group_out(acc, cnt, u, pool_w_ref, pool_scale_ref, g)

    p = ext_p[SHIFT_HALO:SHIFT_HALO + tt, :]
    shifted = ext_p[SHIFT_HALO - 1:SHIFT_HALO - 1 + tt, :]
    prm = (mu_ref, w0_ref, w2_ref, a0_ref, a2_ref, g2_ref, kk_ref, ka_ref, bdiag_ref)
    r, lw, kmod, v, kk, a, gate = _rwkv_prep(p, shifted, prm)

    ext_u[0:POOL_HALO, :] = ext_u[tt:tt + POOL_HALO, :]
    ext_p[0:SHIFT_HALO, :] = ext_p[tt:tt + SHIFT_HALO, :]

    cl = _dot_exact_lhs(tril_ref[...], lw)
    tot = _dot_exact_lhs(blk_ref[...], lw)
    gam = jnp.exp(cl)
    gam_inv = jnp.exp(-cl)
    gam_end = jnp.exp(tot)
    beta = kk * a
    kd_full = kmod * gam_inv
    bd_full = beta * gam_inv
    kq_c = _chains(kk * jnp.exp(cl - lw), nc)
    rq_c = _chains(r * gam, nc)
    kd_c = _chains(kd_full, nc)
    bd_c = _chains(bd_full, nc)
    v_c = _chains(v, nc)
    kdg_c = _chains(kd_full * gam_end, nc)
    bdg_c = _chains(bd_full * gam_end, nc)
    ge_c = _chains(gam_end, nc)
    n_ch = ng * nc
    every = range(n_ch)

    m4 = bdiag_ref[0:GROUP_LANES, 0:GROUP_LANES]
    m4f = m4.astype(F32)
    row = lax.broadcasted_iota(jnp.int32, (CHUNK, GROUP_LANES), 0)
    col = lax.broadcasted_iota(jnp.int32, (CHUNK, GROUP_LANES), 1) % RWKV_HEAD
    strict = row > col
    incl = row >= col
    eye_c = jnp.where(row == col, 1.0, 0.0)
    eye_g = (lax.broadcasted_iota(jnp.int32, (GROUP_LANES, GROUP_LANES), 0)
             == lax.broadcasted_iota(jnp.int32, (GROUP_LANES, GROUP_LANES), 1))

    bd_kd = [_block_diag(kd_c[i], m4) for i in every]
    bd_bd = [_block_diag(bd_c[i], m4) for i in every]
    lhs = [jnp.concatenate([kq_c[i], rq_c[i]], axis=0) for i in every]
    pk = [_dot_nt(lhs[i], bd_kd[i]) for i in every]
    pb = [_dot_nt(lhs[i], bd_bd[i]) for i in every]
    a_k = [jnp.where(strict, pk[i][:CHUNK], 0.0) for i in every]
    p_k = [jnp.where(incl, pk[i][CHUNK:], 0.0) for i in every]
    a_b = [jnp.where(strict, pb[i][:CHUNK], 0.0) for i in every]
    p_b = [jnp.where(incl, pb[i][CHUNK:], 0.0) for i in every]
    tm = [eye_c - a_b[i] for i in every]
    apow = a_b
    bd_ap = [_block_diag(apow[i], m4) for i in every]
    n = 1
    while 2 * n < CHUNK:
        apow = [_dot(apow[i], bd_ap[i]) for i in every]
        bd_ap = [_block_diag(apow[i], m4) for i in every]
        tm = [tm[i] + _dot(tm[i], bd_ap[i]) for i in every]
        n *= 2
    bd_v = [_block_diag(v_c[i], m4) for i in every]
    t_kq = [_dot(tm[i], _block_diag(kq_c[i], m4)) for i in every]
    akv = [_dot(a_k[i], bd_v[i]) for i in every]
    t_akv = [_dot(tm[i], _block_diag(akv[i], m4)) for i in every]
    r_y = [rq_c[i] - _dot(p_b[i], _block_diag(t_kq[i], m4)) for i in every]
    y0 = [_dot(p_k[i], bd_v[i]) - _dot(p_b[i], _block_diag(t_akv[i], m4)) for i in every]
    g_m = [m4f * (jnp.where(eye_g, ge_c[i][0:1, :], 0.0) - _dot_tn(t_kq[i], bdg_c[i])) for i in every]
    h_full = [m4f * _dot_tn(jnp.concatenate([v_c[i], t_akv[i]], axis=0),
                            jnp.concatenate([kdg_c[i], -bdg_c[i]], axis=0)) for i in every]
    h_t = [sum(h_full[i][j * RWKV_HEAD:(j + 1) * RWKV_HEAD] for j in range(GROUP_LANES // RWKV_HEAD))
           for i in every]

    ys = [None] * n_ch
    for g in range(ng):
        s = s_ref[g]
        for c in range(nc):
            i = g * nc + c
            ys[i] = _dot_nt(r_y[i], _block_diag(s, m4)) + y0[i]
            s = _dot(s, g_m[i]) + h_t[i]
        s_ref[g] = s
    y = jnp.concatenate([jnp.concatenate([ys[g * nc + c] for g in range(ng)], axis=1) for c in range(nc)],
                        axis=0)

    ycat[:, POOL_WIDTH:] = _rwkv_finish(y, r, kmod, v, gate, rk_ref, lnw_ref, lnb_ref, bdiag_ref)
    o_ref[0] = x + jnp.dot(ycat[...].astype(BF16), wout_ref[...], preferred_element_type=F32)

    @pl.when(t == pl.num_programs(1) - 1)
    def _():
        pool_tail_ref[0] = ext_u[0:POOL_HALO, :]
        shift_tail_ref[0] = ext_p[0:SHIFT_HALO, :]
        for hh in range(RWKV_HEADS):
            g, j = divmod(hh, GROUP_LANES // RWKV_HEAD)
            s_out_ref[0, hh] = s_ref[g][:, j * RWKV_HEAD:(j + 1) * RWKV_HEAD]


def _mix0_params(P):
    c = RWKV_WIDTH
    row = lambda x: x.reshape(1, -1).astype(F32)
    w2 = jnp.zeros((W_RANK + A_RANK, c), F32).at[:W_RANK].set(P['rw_w2']).astype(BF16)
    a2 = jnp.zeros((W_RANK + A_RANK, c), F32).at[W_RANK:].set(P['rw_a2']).astype(BF16)
    hid = jnp.arange(c) // RWKV_HEAD
    bdiag = (hid[:, None] == hid[None, :]).astype(BF16)
    return (P['pool_w'].astype(BF16), row(P['pool_scale']), row(P['mu_shift']), row(P['rw_w0']), w2,
            row(P['rw_a0']), a2, P['rw_g2'].astype(BF16), row(P['rw_kk']), row(P['rw_ka']),
            row(P['rw_rk']), row(P['rw_lnx_w']), row(P['rw_lnx_b']), bdiag)


def _full_spec(x):
    nd = x.ndim
    return pl.BlockSpec(x.shape, lambda *_: (0,) * nd)


def mix0_seq(x, g, w_in, w_out, P):
    b, t, _ = x.shape
    tt = _row_tile(t, 512)
    assert tt % CHUNK == 0 and tt >= POOL_HALO
    prm = _mix0_params(P)
    ti = jnp.arange(tt)
    same_chunk = (ti[:, None] // CHUNK) == (ti[None, :] // CHUNK)
    tril = (same_chunk & (ti[:, None] >= ti[None, :])).astype(BF16)
    blk = same_chunk.astype(BF16)
    return pl.pallas_call(
        _mix0_seq_body,
        grid=(b, t // tt),
        in_specs=[pl.BlockSpec((1, tt, D_MODEL), lambda i, j: (i, j, 0)),
                  pl.BlockSpec((1, D_MODEL), lambda i, j: (0, 0)), _full_spec(w_in), _full_spec(w_out)]
                 + [_full_spec(p) for p in prm] + [_full_spec(tril), _full_spec(blk)],
        out_specs=[pl.BlockSpec((1, tt, D_MODEL), lambda i, j: (i, j, 0)),
                   pl.BlockSpec((1, RWKV_HEADS, RWKV_HEAD, RWKV_HEAD), lambda i, j: (i, 0, 0, 0)),
                   pl.BlockSpec((1, POOL_HALO, POOL_WIDTH), lambda i, j: (i, 0, 0)),
                   pl.BlockSpec((1, SHIFT_HALO, SHIFT_WIDTH), lambda i, j: (i, 0, 0))],
        out_shape=[jax.ShapeDtypeStruct((b, t, D_MODEL), F32),
                   jax.ShapeDtypeStruct((b, RWKV_HEADS, RWKV_HEAD, RWKV_HEAD), F32),
                   jax.ShapeDtypeStruct((b, POOL_HALO, POOL_WIDTH), F32),
                   jax.ShapeDtypeStruct((b, SHIFT_HALO, SHIFT_WIDTH), F32)],
        scratch_shapes=[pltpu.VMEM((tt + POOL_HALO, POOL_WIDTH), F32),
                        pltpu.VMEM((tt + SHIFT_HALO, SHIFT_WIDTH), F32),
                        pltpu.VMEM((RWKV_WIDTH // GROUP_LANES, RWKV_HEAD, GROUP_LANES), F32),
                        pltpu.VMEM((tt, D_MODEL), F32)],
        compiler_params=pltpu.CompilerParams(dimension_semantics=("parallel", "arbitrary")),
        name="mix0_seq",
    )(x, g.reshape(1, D_MODEL), w_in, w_out, *prm, tril, blk)


def _to_leading(x):
    n = x.shape[0]
    ii = lax.broadcasted_iota(jnp.int32, (n, n, 1), 0)
    jj = lax.broadcasted_iota(jnp.int32, (n, n, 1), 1)
    return jnp.sum(jnp.where(ii == jj, x[None, :, :], 0.0), axis=1, keepdims=True)


def _from_leading(x3):
    n = x3.shape[0]
    ii = lax.broadcasted_iota(jnp.int32, (n, n, 1), 0)
    jj = lax.broadcasted_iota(jnp.int32, (n, n, 1), 1)
    return jnp.sum(jnp.where(ii == jj, x3, 0.0), axis=0)


def _mix0_step_body(h_ref, pool_prev_ref, shift_prev_ref, s_in_ref,
                    pool_w_ref, pool_scale_ref, mu_ref, w0_ref, w2_ref, a0_ref, a2_ref, g2_ref,
                    kk_ref, ka_ref, rk_ref, lnw_ref, lnb_ref, bdiag_ref,
                    o_ref, s_out_ref,
                    r_t, w_t, k_t, v_t, kk_t, bt_t, y_t, r_s, k_s, v_s, g_s):
    hh = pl.program_id(0)
    nb = h_ref.shape[0]

    @pl.when(hh == 0)
    def _():
        for g, win in enumerate(POOL_WINDOWS):
            cs = slice(g * POOL_GC, (g + 1) * POOL_GC)
            u = h_ref[:, cs]
            acc = u
            for j in range(1, win):
                acc = acc + pool_prev_ref[POOL_BUF - j][:, cs]
            cnt = jnp.float32(min(win, PAST_LEN + 1))
            o_ref[:, cs] = _pool_group_out(acc, cnt, u, pool_w_ref, pool_scale_ref, g)
        prm = (mu_ref, w0_ref, w2_ref, a0_ref, a2_ref, g2_ref, kk_ref, ka_ref, bdiag_ref)
        r, lw, kmod, v, kk, a, gate = _rwkv_prep(h_ref[:, POOL_WIDTH:IN_AB_WIDTH], shift_prev_ref[...], prm)
        r_s[...] = r
        k_s[...] = kmod
        v_s[...] = v
        g_s[...] = gate
        r_t[...] = r.T
        w_t[...] = jnp.exp(lw).T
        k_t[...] = kmod.T
        v_t[...] = v.T
        kk_t[...] = kk.T
        bt_t[...] = (kk * a).T

    rows = pl.ds(pl.multiple_of(hh * RWKV_HEAD, RWKV_HEAD), RWKV_HEAD)
    s = s_in_ref[...].T.reshape(RWKV_HEAD, RWKV_HEAD, nb)
    kk = kk_t[rows, :][None]
    s_kk = jnp.sum(s * kk, axis=1, keepdims=True)
    v3 = _to_leading(v_t[rows, :])
    s = s * w_t[rows, :][None] - s_kk * bt_t[rows, :][None] + v3 * k_t[rows, :][None]
    y3 = jnp.sum(s * r_t[rows, :][None], axis=1, keepdims=True)
    y_t[rows, :] = _from_leading(y3)
    s_out_ref[...] = s.reshape(RWKV_HEAD * RWKV_HEAD, nb).T

    @pl.when(hh == pl.num_programs(0) - 1)
    def _():
        o_ref[:, POOL_WIDTH:] = _rwkv_finish(y_t[...].T, r_s[...], k_s[...], v_s[...], g_s[...],
                                             rk_ref, lnw_ref, lnb_ref, bdiag_ref)


def mix0_step(h, pool_prev, shift_prev, s_prev, P):
    b = h.shape[0]
    prm = _mix0_params(P)
    hw = RWKV_HEAD * RWKV_HEAD
    s2 = s_prev.reshape(b, RWKV_HEADS * hw)
    pool_t = jnp.swapaxes(pool_prev, 0, 1)
    tvec = lambda: pltpu.VMEM((RWKV_WIDTH, b), F32)
    svec = lambda: pltpu.VMEM((b, RWKV_WIDTH), F32)
    out, s_new = pl.pallas_call(
        _mix0_step_body,
        grid=(RWKV_HEADS,),
        in_specs=[_full_spec(h), _full_spec(pool_t), _full_spec(shift_prev),
                  pl.BlockSpec((b, hw), lambda i: (0, i))] + [_full_spec(x) for x in prm],
        out_specs=[pl.BlockSpec((b, D_MODEL), lambda i: (0, 0)),
                   pl.BlockSpec((b, hw), lambda i: (0, i))],
        out_shape=[jax.ShapeDtypeStruct((b, D_MODEL), F32),
                   jax.ShapeDtypeStruct((b, RWKV_HEADS * hw), F32)],
        scratch_shapes=[tvec(), tvec(), tvec(), tvec(), tvec(), tvec(), tvec(),
                        svec(), svec(), svec(), svec()],
        compiler_params=pltpu.CompilerParams(dimension_semantics=("arbitrary",)),
        name="mix0_step",
    )(h, pool_t, shift_prev, s2, *prm)
    return out, s_new.reshape(b, RWKV_HEADS, RWKV_HEAD, RWKV_HEAD)


def _conv_seq_body(x_ref, g_ref, wi_ref, cw_ref, wo_ref, o_ref, tail_ref, ext):
    t = pl.program_id(1)
    tt = o_ref.shape[1]
    c = D_MODEL

    @pl.when(t == 0)
    def _():
        ext[0:SHIFT_HALO, :] = jnp.zeros((SHIFT_HALO, c), F32)

    x = x_ref[0]
    h = jnp.dot(_rmsnorm(x, g_ref[...]).astype(BF16), wi_ref[...], preferred_element_type=F32)
    ext[SHIFT_HALO:SHIFT_HALO + tt, :] = h[:, c:2 * c] * h[:, 2 * c:3 * c]
    z = cw_ref[0:1, :] * ext[SHIFT_HALO - 2:SHIFT_HALO - 2 + tt, :]
    z = z + cw_ref[1:2, :] * ext[SHIFT_HALO - 1:SHIFT_HALO - 1 + tt, :]
    z = z + cw_ref[2:3, :] * ext[SHIFT_HALO:SHIFT_HALO + tt, :]
    gated = (h[:, 0:c] * z).astype(BF16)
    o_ref[0] = x + jnp.dot(gated, wo_ref[...], preferred_element_type=F32)
    ext[0:SHIFT_HALO, :] = ext[tt:tt + SHIFT_HALO, :]

    @pl.when(t == pl.num_programs(1) - 1)
    def _():
        tail_ref[0] = ext[SHIFT_HALO - (CONV_WIDTH - 1):SHIFT_HALO, :]


def conv_seq(x, g, w_in, conv_w, w_out):
    b, t, c = x.shape
    tt = _row_tile(t, 512)
    return pl.pallas_call(
        _conv_seq_body,
        grid=(b, t // tt),
        in_specs=[pl.BlockSpec((1, tt, c), lambda i, j: (i, j, 0)),
                  pl.BlockSpec((1, c), lambda i, j: (0, 0)),
                  _full_spec(w_in), _full_spec(conv_w), _full_spec(w_out)],
        out_specs=[pl.BlockSpec((1, tt, c), lambda i, j: (i, j, 0)),
                   pl.BlockSpec((1, CONV_WIDTH - 1, c), lambda i, j: (i, 0, 0))],
        out_shape=[jax.ShapeDtypeStruct((b, t, c), F32),
                   jax.ShapeDtypeStruct((b, CONV_WIDTH - 1, c), F32)],
        scratch_shapes=[pltpu.VMEM((tt + SHIFT_HALO, c), F32)],
        compiler_params=pltpu.CompilerParams(dimension_semantics=("parallel", "arbitrary")),
        name="conv_seq",
    )(x, g.reshape(1, c), w_in, conv_w, w_out)


def _conv_step_body(h_ref, x_ref, p0_ref, p1_ref, cw_ref, wo_ref, o_ref, e_ref):
    c = D_MODEL
    e = h_ref[:, c:2 * c] * h_ref[:, 2 * c:3 * c]
    z = cw_ref[0:1, :] * p0_ref[...] + cw_ref[1:2, :] * p1_ref[...] + cw_ref[2:3, :] * e
    gated = (h_ref[:, 0:c] * z).astype(BF16)
    o_ref[...] = x_ref[...] + jnp.dot(gated, wo_ref[...], preferred_element_type=F32)
    e_ref[...] = e


def conv_step(h, x, prev, conv_w, w_out):
    b = h.shape[0]
    c = D_MODEL
    args = (h, x, prev[:, 0, :], prev[:, 1, :], conv_w, w_out)
    return pl.pallas_call(
        _conv_step_body,
        grid=(1,),
        in_specs=[_full_spec(a) for a in args],
        out_specs=[pl.BlockSpec((b, c), lambda i: (0, 0)), pl.BlockSpec((b, c), lambda i: (0, 0))],
        out_shape=[jax.ShapeDtypeStruct((b, c), F32), jax.ShapeDtypeStruct((b, c), F32)],
        compiler_params=pltpu.CompilerParams(dimension_semantics=("arbitrary",)),
        name="conv_step",
    )(*args)


def _xattn(x2, i, mem_k, mem_v, W, seq_shape, extra_rows=0, into=None):
    last = i == W['norm_xattn'].shape[0] - 1
    if seq_shape is not None:
        b, t = seq_shape
        y = xattn_seq(x2.reshape(b, t, D_MODEL), W['norm_xattn'][i], W['w_xq'][i], W['w_xo'][i],
                      mem_k, mem_v, i, row_tiles=last, extra_rows=extra_rows)
        return y if last else y.reshape(b * t, D_MODEL)
    return xattn_step(x2, W['norm_xattn'][i], W['w_xq'][i], W['w_xo'][i], mem_k, mem_v, i,
                      row_tiles=last, into=into)


def _ffn0(x2, W):
    return ffn_dense(x2, W['norm_ffn'][0], W['ffn_gate'][0], W['ffn_up'][0], W['ffn_down'][0], FFN_TILE)


def _trunk_seq(x, mem_k, mem_v, W, extra_rows):
    b, t, d = x.shape
    P0 = {k: v[0] for k, v in W['mix0'].items()}
    x1, wkv, pool_tail, shift_tail = mix0_seq(x, W['norm_mix'][0], W['w_in_ab'][0], W['w_out_ab'][0], P0)
    pool = pool_tail[:, POOL_HALO - POOL_BUF:]
    shift = shift_tail[:, SHIFT_HALO - 1]
    x2 = _ffn0(_xattn(x1.reshape(b * t, d), 0, mem_k, mem_v, W, (b, t)), W)
    x3, conv = conv_seq(x2.reshape(b, t, d), W['norm_mix'][1], W['w_in_c'][0], W['conv_w'][0], W['w_out_c'][0])
    x8 = _xattn(x3.reshape(b * t, d), 1, mem_k, mem_v, W, (b, t), extra_rows=extra_rows)
    return x8, pool[None], shift[None], wkv[None], conv[None]


def _trunk_step(x, mem_k, mem_v, pool_prev, shift_prev, wkv_prev, conv_prev, W, into):
    b, _, d = x.shape
    x2 = x.reshape(b, d)
    P0 = {k: v[0] for k, v in W['mix0'].items()}
    h = norm_matmul(x2, W['norm_mix'][0], W['w_in_ab'][0])
    mix, wkv = mix0_step(h, pool_prev[0], shift_prev[0], wkv_prev[0], P0)
    pool = jnp.concatenate([pool_prev[0][:, 1:], h[:, None, :POOL_WIDTH]], axis=1)
    shift = h[:, POOL_WIDTH:]
    x2 = matmul_res(mix, W['w_out_ab'][0], x2)
    x2 = _ffn0(_xattn(x2, 0, mem_k, mem_v, W, None), W)
    h = norm_matmul(x2, W['norm_mix'][1], W['w_in_c'][0])
    x2, e = conv_step(h, x2, conv_prev[0], W['conv_w'][0], W['w_out_c'][0])
    conv = jnp.concatenate([conv_prev[0][:, 1:], e[:, None]], axis=1)
    x8 = _xattn(x2, 1, mem_k, mem_v, W, None, into=into)
    return x8, pool[None], shift[None], wkv[None], conv[None]


def _pack_heads(c):
    lead = c.shape[:-2]
    halves = X_HEAD_DIM // 128
    c = c.reshape(*lead, X_HEADS, halves, 128)
    c = jnp.swapaxes(c, -3, -2)
    return c.reshape(*lead, halves * X_HEADS, 128)


def kernel(x_prompt, x_sample, mem_prompt, cache_mem_k, cache_mem_v, state_pool, state_shift, state_wkv, state_conv, norm_mix, norm_xattn, norm_mem, norm_ffn, norm_final, w_xq, w_xk, w_xv, w_xo, w_in_ab, pool_w, pool_scale, mu_shift, rw_w0, rw_w2, rw_a0, rw_a2, rw_g2, rw_kk, rw_ka, rw_rk, rw_lnx_w, rw_lnx_b, w_out_ab, ffn_gate, ffn_up, ffn_down, w_in_c, conv_w, w_out_c, router_w, router_b, moe_gate, moe_up, moe_down):
    depth = norm_mix.shape[0]
    assert depth == 2 and w_in_ab.shape[0] == 1 and w_in_c.shape[0] == 1
    bp = x_prompt.shape[0]
    bs = x_sample.shape[0]
    d = D_MODEL
    bf = lambda w: w.astype(BF16)
    W = dict(norm_mix=norm_mix, norm_xattn=norm_xattn, norm_ffn=norm_ffn, norm_final=norm_final,
             w_xq=bf(w_xq), w_xo=bf(w_xo), w_in_ab=bf(w_in_ab), w_out_ab=bf(w_out_ab),
             ffn_gate=bf(ffn_gate), ffn_up=bf(ffn_up), ffn_down=bf(ffn_down),
             w_in_c=bf(w_in_c), conv_w=conv_w, w_out_c=bf(w_out_c),
             router_w=router_w, router_b=router_b,
             moe_gate=bf(moe_gate), moe_up=bf(moe_up), moe_down=bf(moe_down),
             mix0=dict(pool_w=pool_w, pool_scale=pool_scale, mu_shift=mu_shift, rw_w0=rw_w0, rw_w2=rw_w2,
                       rw_a0=rw_a0, rw_a2=rw_a2, rw_g2=rw_g2, rw_kk=rw_kk, rw_ka=rw_ka, rw_rk=rw_rk,
                       rw_lnx_w=rw_lnx_w, rw_lnx_b=rw_lnx_b))

    mk, mv = mem_kv(mem_prompt.reshape(bp * N_MEM, d), norm_mem, bf(w_xk), bf(w_xv))
    mk = mk.reshape(depth, bp, N_MEM, d)
    mv = mv.reshape(depth, bp, N_MEM, d)

    n_p = bp * x_prompt.shape[1]
    x8, pool_p, shift_p, wkv_p, conv_p = _trunk_seq(x_prompt, mk, mv, W, bs)
    x8, pool_s, shift_s, wkv_s, conv_s = _trunk_step(x_sample, _pack_heads(cache_mem_k),
                                                      _pack_heads(cache_mem_v), state_pool,
                                                      state_shift, state_wkv, state_conv, W, (x8, n_p))
    y_p, y_s = moe_top2_norm(x8, [n_p, bs], norm_ffn[1], router_w[0], router_b[0], W['moe_gate'][0],
                             W['moe_up'][0], W['moe_down'][0], norm_final)
    y_p = y_p.reshape(x_prompt.shape)
    y_s = y_s.reshape(x_sample.shape)
    mem_k_p = mk.reshape(depth, bp, N_MEM, X_HEADS, X_HEAD_DIM)
    mem_v_p = mv.reshape(depth, bp, N_MEM, X_HEADS, X_HEAD_DIM)
    return (y_p, y_s, pool_p, pool_s, shift_p, shift_s, wkv_p, wkv_s, conv_p, conv_s, mem_k_p, mem_v_p)
```

```python
import functools

import jax
import jax.numpy as jnp
from jax import lax
from jax.experimental import pallas as pl
from jax.experimental.pallas import tpu as pltpu

F32 = jnp.float32
BF16 = jnp.bfloat16

D_MODEL = 1024
POOL_WIDTH = 512
POOL_GROUPS = 4
POOL_GC = 128
POOL_WINDOWS = (2, 4, 8, 16)
POOL_BUF = 15
RWKV_WIDTH = 512
RWKV_HEAD = 64
RWKV_HEADS = 8
W_RANK = 64
A_RANK = 64
G_RANK = 128
SHIFT_WIDTH = 3 * RWKV_WIDTH + W_RANK + A_RANK + G_RANK
IN_AB_WIDTH = POOL_WIDTH + SHIFT_WIDTH
LNX_EPS = 64e-5
CONV_WIDTH = 3
N_EXPERTS = 8
N_MEM = 256
X_HEADS = 4
X_HEAD_DIM = 256
RMS_EPS = 1e-6
PAST_LEN = 16384

CHUNK = 64
GROUP_LANES = 256
POOL_HALO = 16
SHIFT_HALO = 8
ROUTER_PAD = 128
TOP_K = 2
ROW_SUBLANES = 8
MOE_ROWS = 512
FFN_TILE = 1408
MOE_TILE = 1792
XATTN_STEP_VMEM = 48 * 1024 * 1024


def _rmsnorm(x, g):
    ms = jnp.mean(x * x, axis=-1, keepdims=True)
    return x * lax.rsqrt(ms + RMS_EPS) * g


def _dot(a, b):
    return jnp.dot(a.astype(BF16), b.astype(BF16), preferred_element_type=F32)


def _dot_nt(a, b):
    return lax.dot_general(a.astype(BF16), b.astype(BF16), (((1,), (1,)), ((), ())),
                           preferred_element_type=F32)


def _dot_tn(a, b):
    return lax.dot_general(a.astype(BF16), b.astype(BF16), (((0,), (0,)), ((), ())),
                           preferred_element_type=F32)


def _split3(x):
    hi = x.astype(BF16)
    r1 = x - hi.astype(F32)
    mid = r1.astype(BF16)
    lo = (r1 - mid.astype(F32)).astype(BF16)
    return hi, mid, lo


def _dot_exact_lhs(m01, x):
    hi, mid, lo = _split3(x)
    f = lambda p: jnp.dot(m01, p, preferred_element_type=F32)
    return f(hi) + f(mid) + f(lo)


def _softplus(x):
    return jnp.maximum(x, 0.0) + jnp.log1p(jnp.exp(-jnp.abs(x)))


def _sigmoid(x):
    return 1.0 / (1.0 + jnp.exp(-x))


def _row_tile(m, want):
    t = min(m, want)
    assert m % t == 0, (m, t)
    return t


def _norm_matmul_body(x_ref, g_ref, w_ref, o_ref):
    xn = _rmsnorm(x_ref[...], g_ref[...]).astype(BF16)
    o_ref[...] = jnp.dot(xn, w_ref[...], preferred_element_type=F32)


def norm_matmul(x, g, w):
    m, k = x.shape
    n = w.shape[1]
    tm = _row_tile(m, 512)
    return pl.pallas_call(
        _norm_matmul_body,
        grid=(m // tm,),
        in_specs=[pl.BlockSpec((tm, k), lambda i: (i, 0)),
                  pl.BlockSpec((1, k), lambda i: (0, 0)),
                  pl.BlockSpec((k, n), lambda i: (0, 0))],
        out_specs=pl.BlockSpec((tm, n), lambda i: (i, 0)),
        out_shape=jax.ShapeDtypeStruct((m, n), F32),
        compiler_params=pltpu.CompilerParams(dimension_semantics=("parallel",)),
        name="norm_matmul",
    )(x, g.reshape(1, k), w)


PACKED_COLS = [(r % X_HEADS) * X_HEAD_DIM + (r // X_HEADS) * 128 for r in range(ROW_SUBLANES)]
PACKED_ORDER = sorted(range(ROW_SUBLANES), key=lambda r: PACKED_COLS[r])


def _store_packed(ref, y):
    n = y.shape[0]
    for r, c0 in enumerate(PACKED_COLS):
        ref[0, pl.ds(r, n, stride=ROW_SUBLANES), :] = y[:, c0:c0 + 128]


def _load_packed(ref, n):
    return jnp.concatenate([ref[0, pl.ds(r, n, stride=ROW_SUBLANES), :] for r in PACKED_ORDER], axis=1)


def _mem_kv_body(m_ref, g_ref, wk_ref, wv_ref, k_ref, v_ref):
    xn = _rmsnorm(m_ref[...], g_ref[0]).astype(BF16)
    _store_packed(k_ref, jnp.dot(xn, wk_ref[0], preferred_element_type=F32))
    _store_packed(v_ref, jnp.dot(xn, wv_ref[0], preferred_element_type=F32))


def mem_kv(mem, g, wk, wv):
    r, d = mem.shape
    nl = g.shape[0]
    tm = _row_tile(r, 512)
    out = jax.ShapeDtypeStruct((nl, r * ROW_SUBLANES, 128), F32)
    return pl.pallas_call(
        _mem_kv_body,
        grid=(nl, r // tm),
        in_specs=[pl.BlockSpec((tm, d), lambda l, j: (j, 0)),
                  pl.BlockSpec((1, 1, d), lambda l, j: (l, 0, 0)),
                  pl.BlockSpec((1, d, d), lambda l, j: (l, 0, 0)),
                  pl.BlockSpec((1, d, d), lambda l, j: (l, 0, 0))],
        out_specs=[pl.BlockSpec((1, tm * ROW_SUBLANES, 128), lambda l, j: (l, j, 0)),
                   pl.BlockSpec((1, tm * ROW_SUBLANES, 128), lambda l, j: (l, j, 0))],
        out_shape=[out, out],
        compiler_params=pltpu.CompilerParams(dimension_semantics=("parallel", "parallel")),
        name="mem_kv",
    )(mem, g.reshape(nl, 1, d), wk, wv)


def _matmul_res_body(a_ref, w_ref, r_ref, o_ref):
    o_ref[...] = r_ref[...] + jnp.dot(a_ref[...].astype(BF16), w_ref[...], preferred_element_type=F32)


def matmul_res(a, w, res):
    m, k = a.shape
    n = w.shape[1]
    tm = _row_tile(m, 512)
    return pl.pallas_call(
        _matmul_res_body,
        grid=(m // tm,),
        in_specs=[pl.BlockSpec((tm, k), lambda i: (i, 0)),
                  pl.BlockSpec((k, n), lambda i: (0, 0)),
                  pl.BlockSpec((tm, n), lambda i: (i, 0))],
        out_specs=pl.BlockSpec((tm, n), lambda i: (i, 0)),
        out_shape=jax.ShapeDtypeStruct((m, n), F32),
        compiler_params=pltpu.CompilerParams(dimension_semantics=("parallel",)),
        name="matmul_res",
    )(a, w, res)


def _ffn_body(x_ref, g_ref, wg_ref, wu_ref, wd_ref, o_ref, xn_ref, acc_ref):
    f = pl.program_id(1)

    @pl.when(f == 0)
    def _():
        xn_ref[...] = _rmsnorm(x_ref[...], g_ref[...]).astype(BF16)
        acc_ref[...] = jnp.zeros_like(acc_ref)

    xn = xn_ref[...]
    gate = jnp.dot(xn, wg_ref[...], preferred_element_type=F32)
    up = jnp.dot(xn, wu_ref[...], preferred_element_type=F32)
    hid = (gate * _sigmoid(gate) * up).astype(BF16)
    acc_ref[...] += jnp.dot(hid, wd_ref[...], preferred_element_type=F32)

    @pl.when(f == pl.num_programs(1) - 1)
    def _():
        o_ref[...] = x_ref[...] + acc_ref[...]


def ffn_dense(x, g, wg, wu, wd, tf):
    m, d = x.shape
    ff = wg.shape[1]
    tm = _row_tile(m, 512)
    return pl.pallas_call(
        _ffn_body,
        grid=(m // tm, ff // tf),
        in_specs=[pl.BlockSpec((tm, d), lambda i, f: (i, 0)),
                  pl.BlockSpec((1, d), lambda i, f: (0, 0)),
                  pl.BlockSpec((d, tf), lambda i, f: (0, f)),
                  pl.BlockSpec((d, tf), lambda i, f: (0, f)),
                  pl.BlockSpec((tf, d), lambda i, f: (f, 0))],
        out_specs=pl.BlockSpec((tm, d), lambda i, f: (i, 0)),
        out_shape=jax.ShapeDtypeStruct((m, d), F32),
        scratch_shapes=[pltpu.VMEM((tm, d), BF16), pltpu.VMEM((tm, d), F32)],
        compiler_params=pltpu.CompilerParams(dimension_semantics=("parallel", "arbitrary")),
        name="ffn_dense",
    )(x, g.reshape(1, d), wg, wu, wd)


def _router_body(x_ref, g_ref, w_ref, b_ref, oi_ref, ow_ref):
    xn = _rmsnorm(_load_row_tiles(x_ref, oi_ref.shape[0]), g_ref[...])
    logits = _dot_3pass(xn, w_ref[...]) + b_ref[...]
    lane = lax.broadcasted_iota(jnp.int32, logits.shape, 1)
    neg = jnp.float32(-jnp.inf)
    logits = jnp.where(lane < N_EXPERTS, logits, neg)
    m1 = jnp.max(logits, axis=-1, keepdims=True)
    i1 = jnp.min(jnp.where(logits == m1, lane, ROUTER_PAD), axis=-1, keepdims=True)
    rest = jnp.where(lane == i1, neg, logits)
    m2 = jnp.max(rest, axis=-1, keepdims=True)
    i2 = jnp.min(jnp.where(rest == m2, lane, ROUTER_PAD), axis=-1, keepdims=True)
    e2 = jnp.exp(m2 - m1)
    den = 1.0 + e2
    slot = lax.broadcasted_iota(jnp.int32, oi_ref.shape, 1)
    oi_ref[...] = jnp.where(slot == 0, i1, i2)
    ow_ref[...] = jnp.where(slot == 0, 1.0 / den, e2 / den)


def _dot_3pass(x, w):
    xh = x.astype(BF16)
    xm = (x - xh.astype(F32)).astype(BF16)
    wh = w.astype(BF16)
    wm = (w - wh.astype(F32)).astype(BF16)
    f = lambda p, q: jnp.dot(p, q, preferred_element_type=F32)
    return f(xh, wh) + (f(xh, wm) + f(xm, wh))


def moe_router(x8, row0, m, g, router_w, router_b):
    d = D_MODEL
    tm = _row_tile(m, 512)
    assert row0 % tm == 0
    blk0 = row0 // tm
    w = jnp.zeros((d, ROUTER_PAD), F32).at[:, :N_EXPERTS].set(router_w)
    b = jnp.zeros((1, ROUTER_PAD), F32).at[0, :N_EXPERTS].set(router_b)
    return pl.pallas_call(
        _router_body,
        grid=(m // tm,),
        in_specs=[pl.BlockSpec((tm * ROW_SUBLANES, 128), lambda i: (blk0 + i, 0)),
                  pl.BlockSpec((1, d), lambda i: (0, 0)),
                  pl.BlockSpec((d, ROUTER_PAD), lambda i: (0, 0)),
                  pl.BlockSpec((1, ROUTER_PAD), lambda i: (0, 0))],
        out_specs=[pl.BlockSpec((tm, TOP_K), lambda i: (i, 0)), pl.BlockSpec((tm, TOP_K), lambda i: (i, 0))],
        out_shape=[jax.ShapeDtypeStruct((m, TOP_K), jnp.int32), jax.ShapeDtypeStruct((m, TOP_K), F32)],
        compiler_params=pltpu.CompilerParams(dimension_semantics=("parallel",)),
        name="moe_router",
    )(x8, g.reshape(1, d), w, b)


def _row_tiles_to_2d(ref, slot, n):
    return jnp.concatenate([ref[slot, pl.ds(j, n, stride=ROW_SUBLANES), :] for j in range(ROW_SUBLANES)],
                           axis=1)


def _row_copy(src_hbm, row, dst, slot, i, sem):
    return pltpu.make_async_copy(src_hbm.at[pl.ds(row * ROW_SUBLANES, ROW_SUBLANES)],
                                 dst.at[slot, pl.ds(i * ROW_SUBLANES, ROW_SUBLANES)], sem.at[slot])


def _gather_start(src_hbm, rows_smem, dst, slot, n, sem):
    def body(i, carry):
        for k in range(2):
            r = 2 * i + k
            _row_copy(src_hbm, rows_smem[0, 0, r], dst, slot, r, sem).start(priority=k)
        return carry
    lax.fori_loop(0, n // 2, body, 0, unroll=4)


def _gather_wait(src_hbm, dst, slot, n, sem):
    pltpu.make_async_copy(src_hbm.at[pl.ds(0, n * ROW_SUBLANES)], dst.at[slot], sem.at[slot]).wait()


def _moe_group_body(te_ref, nu_ref, tok0_ref, tokn_ref, x8_ref, g_ref, wg_ref, wu_ref, wd_ref,
                    o_ref, xbuf, sem, xn_ref, acc_ref, *, nf):
    t = pl.program_id(0)
    f = pl.program_id(1)
    tm = xn_ref.shape[0]
    per_step = tm // nf
    used = t < nu_ref[0]
    slot = t % 2

    @pl.when(jnp.logical_and(t == 0, f == 0))
    def _():
        _gather_start(x8_ref, tok0_ref, xbuf, 0, tm, sem)

    @pl.when(jnp.logical_and(used, f == 0))
    def _():
        _gather_wait(x8_ref, xbuf, slot, tm, sem)
        x = _row_tiles_to_2d(xbuf, slot, tm)
        xn_ref[...] = _rmsnorm(x, g_ref[...]).astype(BF16)
        acc_ref[...] = jnp.zeros_like(acc_ref)

    @pl.when(used)
    def _():
        base = f * per_step
        for i in range(per_step):
            _row_copy(x8_ref, tokn_ref[0, 0, base + i], xbuf, 1 - slot, base + i, sem).start()
        xn = xn_ref[...]
        gate = jnp.dot(xn, wg_ref[0], preferred_element_type=F32)
        up = jnp.dot(xn, wu_ref[0], preferred_element_type=F32)
        hid = (gate * _sigmoid(gate) * up).astype(BF16)
        acc_ref[...] += jnp.dot(hid, wd_ref[0], preferred_element_type=F32)

    @pl.when(f == nf - 1)
    def _():
        y = jnp.where(used, acc_ref[...], 0.0)
        for j in range(ROW_SUBLANES):
            o_ref[pl.ds(j, tm, stride=ROW_SUBLANES), :] = y[:, j * 128:(j + 1) * 128]

    @pl.when(jnp.logical_and(t == nu_ref[0] - 1, f == nf - 1))
    def _():
        _gather_wait(x8_ref, xbuf, 1 - slot, tm, sem)


def moe_grouped(x8, g, tile_expert, n_used, row_token, wg, wu, wd, tf):
    n_tiles, _, tm = row_token.shape
    d = D_MODEL
    ne, _, ff = wg.shape
    nf = ff // tf
    assert tm % nf == 0
    live_f = lambda t, f, nu: jnp.where(t < nu[0], f, nf - 1)
    smem_rows = lambda imap: pl.BlockSpec((1, 1, tm), imap, memory_space=pltpu.SMEM)
    grid_spec = pltpu.PrefetchScalarGridSpec(
        num_scalar_prefetch=2,
        grid=(n_tiles, nf),
        in_specs=[smem_rows(lambda t, f, te, nu: (0, 0, 0)),
                  smem_rows(lambda t, f, te, nu: (jnp.minimum(t + 1, nu[0] - 1), 0, 0)),
                  pl.BlockSpec(memory_space=pl.ANY),
                  pl.BlockSpec((1, d), lambda t, f, te, nu: (0, 0)),
                  pl.BlockSpec((1, d, tf), lambda t, f, te, nu: (te[t], 0, live_f(t, f, nu))),
                  pl.BlockSpec((1, d, tf), lambda t, f, te, nu: (te[t], 0, live_f(t, f, nu))),
                  pl.BlockSpec((1, tf, d), lambda t, f, te, nu: (te[t], live_f(t, f, nu), 0))],
        out_specs=pl.BlockSpec((tm * ROW_SUBLANES, 128), lambda t, f, te, nu: (t, 0)),
        scratch_shapes=[pltpu.VMEM((2, tm * ROW_SUBLANES, 128), F32), pltpu.SemaphoreType.DMA((2,)),
                        pltpu.VMEM((tm, d), BF16), pltpu.VMEM((tm, d), F32)])
    return pl.pallas_call(
        functools.partial(_moe_group_body, nf=nf),
        grid_spec=grid_spec,
        out_shape=jax.ShapeDtypeStruct((n_tiles * tm * ROW_SUBLANES, 128), F32),
        compiler_params=pltpu.CompilerParams(dimension_semantics=("arbitrary", "arbitrary")),
        name="moe_grouped",
    )(tile_expert, n_used, row_token, row_token, x8, g.reshape(1, d), wg, wu, wd)


def _moe_combine_body(pos0_ref, posn_ref, y8_ref, x_ref, w_ref, g_ref, o_ref, ybuf, sem):
    t = pl.program_id(0)
    tc = o_ref.shape[0]
    n = TOP_K * tc

    @pl.when(t == 0)
    def _():
        _gather_start(y8_ref, pos0_ref, ybuf, 0, n, sem)

    @pl.when(t + 1 < pl.num_programs(0))
    def _():
        _gather_start(y8_ref, posn_ref, ybuf, (t + 1) % 2, n, sem)

    slot = t % 2
    _gather_wait(y8_ref, ybuf, slot, n, sem)
    y = _row_tiles_to_2d(ybuf, slot, n)
    w = w_ref[...]
    moe = w[:, 0:1] * y[0:tc] + w[:, 1:2] * y[tc:n]
    o_ref[...] = _rmsnorm(_load_row_tiles(x_ref, tc) + moe, g_ref[...])


def moe_combine_norm(y8, x8, row0, pos, topw, g):
    d = D_MODEL
    nt, _, n = pos.shape
    tc = n // TOP_K
    m = nt * tc
    assert row0 % tc == 0
    blk0 = row0 // tc
    smem_rows = lambda imap: pl.BlockSpec((1, 1, n), imap, memory_space=pltpu.SMEM)
    return pl.pallas_call(
        _moe_combine_body,
        grid=(nt,),
        in_specs=[smem_rows(lambda t: (0, 0, 0)),
                  smem_rows(lambda t: (jnp.minimum(t + 1, nt - 1), 0, 0)),
                  pl.BlockSpec(memory_space=pl.ANY),
                  pl.BlockSpec((tc * ROW_SUBLANES, 128), lambda t: (blk0 + t, 0)),
                  pl.BlockSpec((tc, TOP_K), lambda t: (t, 0)),
                  pl.BlockSpec((1, d), lambda t: (0, 0))],
        out_specs=pl.BlockSpec((tc, d), lambda t: (t, 0)),
        out_shape=jax.ShapeDtypeStruct((m, d), F32),
        scratch_shapes=[pltpu.VMEM((2, n * ROW_SUBLANES, 128), F32), pltpu.SemaphoreType.DMA((2,))],
        compiler_params=pltpu.CompilerParams(dimension_semantics=("arbitrary",)),
        name="moe_combine_norm",
    )(pos, pos, y8, x8, topw, g.reshape(1, d))


def _moe_plan(topi, tm):
    n = topi.shape[0]
    a = TOP_K * n
    n_tiles = (a + N_EXPERTS * (tm - 1) + tm - 1) // tm
    e = topi.reshape(a)
    eid = jnp.arange(N_EXPERTS, dtype=jnp.int32)
    onehot = (e[:, None] == eid[None, :]).astype(jnp.int32)
    csum = jnp.cumsum(onehot, axis=0)
    cnt = csum[-1]
    rank = jnp.sum((csum - onehot) * onehot, axis=1)
    tiles = (cnt + tm - 1) // tm
    tile_end = jnp.cumsum(tiles)
    tile_start = tile_end - tiles
    n_used = tile_end[-1]
    pos = (jnp.sum(onehot * tile_start[None, :], axis=1) * tm + rank).reshape(n, TOP_K)
    tid = jnp.minimum(jnp.arange(n_tiles, dtype=jnp.int32), n_used - 1)
    tile_expert = jnp.sum(tid[:, None] >= tile_end[None, :], axis=1).astype(jnp.int32)
    fill = tiles * tm - cnt
    j = jnp.arange(tm - 1, dtype=jnp.int32)
    dummy_key = jnp.where(j[None, :] < fill[:, None], eid[:, None], N_EXPERTS).reshape(-1)
    dummy_tok = jnp.tile(j % n, N_EXPERTS)
    spare = n_tiles * tm - a - dummy_key.shape[0]
    keys = jnp.concatenate([e, dummy_key, jnp.full((spare,), N_EXPERTS, jnp.int32)])
    toks = jnp.concatenate([jnp.arange(a, dtype=jnp.int32) // TOP_K, dummy_tok,
                            jnp.zeros((spare,), jnp.int32)])
    _, row_token = lax.sort((keys, toks), num_keys=1, is_stable=True)
    return tile_expert, n_used.reshape(1).astype(jnp.int32), row_token.reshape(n_tiles, 1, tm), pos


def moe_top2_norm(x8, groups, g, router_w, router_b, wg, wu, wd, g_final):
    starts = [sum(groups[:i]) for i in range(len(groups))]
    tops = [moe_router(x8, s, m, g, router_w, router_b) for s, m in zip(starts, groups)]
    topi = jnp.concatenate([t[0] for t in tops], axis=0)
    tile_expert, n_used, row_token, pos = _moe_plan(topi, MOE_ROWS)
    y8 = moe_grouped(x8, g, tile_expert, n_used, row_token, wg, wu, wd, MOE_TILE)
    outs = []
    for s, m, (_, topw) in zip(starts, groups, tops):
        tc = _row_tile(m, 256)
        p = pos[s:s + m].reshape(m // tc, tc, TOP_K)
        p = jnp.swapaxes(p, 1, 2).reshape(m // tc, 1, TOP_K * tc)
        outs.append(moe_combine_norm(y8, x8, s, p, topw, g_final))
    return outs


def _store_row_tiles(o_ref, y):
    n = y.shape[0]
    for j in range(ROW_SUBLANES):
        o_ref[pl.ds(j, n, stride=ROW_SUBLANES), :] = y[:, j * 128:(j + 1) * 128]


def _load_row_tiles(ref, n):
    return jnp.concatenate([ref[pl.ds(j, n, stride=ROW_SUBLANES), :] for j in range(ROW_SUBLANES)], axis=1)


def _xattn_seq_body(x_ref, g_ref, wq_ref, wo_ref, k_ref, v_ref, o_ref, *, row_tiles, fill_step):
    if fill_step is not None:
        @pl.when(pl.program_id(0) == fill_step)
        def _():
            o_ref[...] = jnp.zeros_like(o_ref)

        @pl.when(pl.program_id(0) != fill_step)
        def _():
            _xattn_seq_tile(x_ref, g_ref, wq_ref, wo_ref, k_ref, v_ref, o_ref, row_tiles)
    else:
        _xattn_seq_tile(x_ref, g_ref, wq_ref, wo_ref, k_ref, v_ref, o_ref, row_tiles)


def _xattn_seq_tile(x_ref, g_ref, wq_ref, wo_ref, k_ref, v_ref, o_ref, row_tiles):
    x = x_ref[0]
    xn = _rmsnorm(x, g_ref[...]).astype(BF16)
    q = jnp.dot(xn, wq_ref[...], preferred_element_type=F32).astype(BF16)
    kb = _load_packed(k_ref, N_MEM).astype(BF16)
    vb = _load_packed(v_ref, N_MEM).astype(BF16)
    heads = []
    for h in range(X_HEADS):
        sl = slice(h * X_HEAD_DIM, (h + 1) * X_HEAD_DIM)
        s = _dot_nt(q[:, sl], kb[:, sl]) * (X_HEAD_DIM ** -0.5)
        mx = jnp.max(s, axis=-1, keepdims=True)
        ex = jnp.exp(s - mx)
        pr = ex / jnp.sum(ex, axis=-1, keepdims=True)
        heads.append(jnp.dot(pr.astype(BF16), vb[:, sl], preferred_element_type=F32))
    att = jnp.concatenate(heads, axis=-1).astype(BF16)
    y = x + jnp.dot(att, wo_ref[...], preferred_element_type=F32)
    if row_tiles:
        _store_row_tiles(o_ref, y)
    else:
        o_ref[0] = y


def xattn_seq(x, g, wq, wo, mem_k, mem_v, layer, row_tiles=False, extra_rows=0):
    b, t, d = x.shape
    tq = _row_tile(t, 512)
    nq = t // tq
    assert extra_rows <= tq and (extra_rows == 0 or row_tiles)
    steps = b * nq + (1 if extra_rows else 0)
    bi = lambda s: jnp.minimum(s // nq, b - 1)
    qi = lambda s: jnp.where(s < b * nq, s % nq, 0)
    if row_tiles:
        out_spec = pl.BlockSpec((tq * ROW_SUBLANES, 128), lambda s: (s, 0))
        out_shape = jax.ShapeDtypeStruct(((b * t + extra_rows) * ROW_SUBLANES, 128), F32)
    else:
        out_spec = pl.BlockSpec((1, tq, d), lambda s: (bi(s), qi(s), 0))
        out_shape = jax.ShapeDtypeStruct((b, t, d), F32)
    return pl.pallas_call(
        functools.partial(_xattn_seq_body, row_tiles=row_tiles, fill_step=b * nq if extra_rows else None),
        grid=(steps,),
        in_specs=[pl.BlockSpec((1, tq, d), lambda s: (bi(s), qi(s), 0)),
                  pl.BlockSpec((1, d), lambda s: (0, 0)),
                  pl.BlockSpec((d, d), lambda s: (0, 0)),
                  pl.BlockSpec((d, d), lambda s: (0, 0)),
                  pl.BlockSpec((1, N_MEM * ROW_SUBLANES, 128), lambda s: (layer, bi(s), 0)),
                  pl.BlockSpec((1, N_MEM * ROW_SUBLANES, 128), lambda s: (layer, bi(s), 0))],
        out_specs=out_spec,
        out_shape=out_shape,
        compiler_params=pltpu.CompilerParams(dimension_semantics=("parallel",)),
        name="xattn_seq",
    )(x, g.reshape(1, d), wq, wo, mem_k, mem_v)


XATTN_STEP_ROWS = 8
XATTN_STEP_KV = 4


def _xattn_step_body(x_ref, g_ref, wq_ref, wo_ref, k_ref, v_ref, *rest, row_tiles, aliased):
    o_ref, q_s, att_s = rest[1:] if aliased else rest
    j = pl.program_id(1)

    @pl.when(j == 0)
    def _():
        xn = _rmsnorm(x_ref[...], g_ref[...]).astype(BF16)
        q_s[...] = jnp.dot(xn, wq_ref[...], preferred_element_type=F32) * (X_HEAD_DIM ** -0.5)

    halves = X_HEAD_DIM // 128
    packed = [(r % X_HEADS) * X_HEAD_DIM + (r // X_HEADS) * 128 for r in range(halves * X_HEADS)]
    for b in range(XATTN_STEP_KV):
        rsel = pl.ds(j * XATTN_STEP_KV + b, 1)
        q_row = q_s[rsel, :]
        q8 = jnp.concatenate([q_row[:, c0:c0 + 128] for c0 in packed], axis=0)
        part = jnp.sum(k_ref[0, b] * q8[None], axis=-1, keepdims=True)
        s = part + pltpu.roll(part, X_HEADS, axis=1)
        mx = jnp.max(s, axis=0, keepdims=True)
        ex = jnp.exp(s - mx)
        den = jnp.sum(ex, axis=0)
        o8 = jnp.sum(ex * v_ref[0, b], axis=0) / den
        order = sorted(range(len(packed)), key=lambda r: packed[r])
        att_s[rsel, :] = jnp.concatenate([o8[r:r + 1, :] for r in order], axis=1)

    @pl.when(j == pl.num_programs(1) - 1)
    def _():
        y = x_ref[...] + jnp.dot(att_s[...].astype(BF16), wo_ref[...], preferred_element_type=F32)
        if row_tiles:
            _store_row_tiles(o_ref, y)
        else:
            o_ref[...] = y


def xattn_step(x, g, wq, wo, mem_k, mem_v, layer, row_tiles=False, into=None):
    b, d = x.shape
    rows, kvb = XATTN_STEP_ROWS, XATTN_STEP_KV
    assert b % rows == 0 and rows % kvb == 0
    nj = rows // kvb
    kv_spec = pl.BlockSpec((1, kvb, N_MEM, ROW_SUBLANES, 128), lambda i, j: (layer, i * nj + j, 0, 0, 0))
    args = [x, g.reshape(1, d), wq, wo, mem_k, mem_v]
    in_specs = [pl.BlockSpec((rows, d), lambda i, j: (i, 0)),
                pl.BlockSpec((1, d), lambda i, j: (0, 0)),
                pl.BlockSpec((d, d), lambda i, j: (0, 0)),
                pl.BlockSpec((d, d), lambda i, j: (0, 0)),
                kv_spec, kv_spec]
    aliases = {}
    if into is not None:
        buf, row0 = into
        assert row_tiles and row0 % rows == 0 and buf.shape == ((row0 + b) * ROW_SUBLANES, 128)
        blk0 = row0 // rows
        out_spec = pl.BlockSpec((rows * ROW_SUBLANES, 128), lambda i, j: (blk0 + i, 0))
        out_shape = jax.ShapeDtypeStruct(buf.shape, F32)
        aliases = {len(args): 0}
        args.append(buf)
        in_specs.append(pl.BlockSpec(memory_space=pl.ANY))
    elif row_tiles:
        out_spec = pl.BlockSpec((rows * ROW_SUBLANES, 128), lambda i, j: (i, 0))
        out_shape = jax.ShapeDtypeStruct((b * ROW_SUBLANES, 128), F32)
    else:
        out_spec = pl.BlockSpec((rows, d), lambda i, j: (i, 0))
        out_shape = jax.ShapeDtypeStruct((b, d), F32)
    return pl.pallas_call(
        functools.partial(_xattn_step_body, row_tiles=row_tiles, aliased=into is not None),
        grid=(b // rows, nj),
        in_specs=in_specs,
        out_specs=out_spec,
        out_shape=out_shape,
        input_output_aliases=aliases,
        scratch_shapes=[pltpu.VMEM((rows, d), F32), pltpu.VMEM((rows, d), F32)],
        compiler_params=pltpu.CompilerParams(dimension_semantics=("parallel", "arbitrary"),
                                             vmem_limit_bytes=XATTN_STEP_VMEM),
        name="xattn_step",
    )(*args)


def _head_sum(x, bdiag_ref):
    m = bdiag_ref[0:GROUP_LANES, 0:GROUP_LANES]
    parts = []
    for g in range(RWKV_WIDTH // GROUP_LANES):
        xs = x[:, g * GROUP_LANES:(g + 1) * GROUP_LANES]
        hi = xs.astype(BF16)
        lo = (xs - hi.astype(F32)).astype(BF16)
        parts.append(jnp.dot(hi, m, preferred_element_type=F32) + jnp.dot(lo, m, preferred_element_type=F32))
    return jnp.concatenate(parts, axis=1)


def _pool_group_out(acc, cnt, u, pool_w_ref, pool_scale_ref, g):
    cs = slice(g * POOL_GC, (g + 1) * POOL_GC)
    d = acc / cnt - u
    return _dot(d, pool_w_ref[g]) * pool_scale_ref[:, cs]


def _rwkv_prep(p, shifted, prm):
    (mu_ref, w0_ref, w2_ref, a0_ref, a2_ref, g2_ref, kk_ref, ka_ref, bdiag_ref) = prm
    m = p + (shifted - p) * mu_ref[...]
    c = RWKV_WIDTH
    r = m[:, 0:c]
    k = m[:, c:2 * c]
    v = m[:, 2 * c:3 * c]
    dwa = m[:, 3 * c:3 * c + W_RANK + A_RANK]
    dg = m[:, 3 * c + W_RANK + A_RANK:]
    w_log = -_softplus(-(w0_ref[...] + _dot(jnp.tanh(dwa), w2_ref[...]))) - 0.5
    lw = -jnp.exp(w_log)
    a = _sigmoid(a0_ref[...] + _dot(dwa, a2_ref[...]))
    gate = _dot(_sigmoid(dg), g2_ref[...])
    kk = k * kk_ref[...]
    kk = kk / jnp.maximum(jnp.sqrt(_head_sum(kk * kk, bdiag_ref)), 1e-12)
    kmod = k * (1.0 + (a - 1.0) * ka_ref[...])
    return r, lw, kmod, v, kk, a, gate


def _rwkv_finish(y, r, kmod, v, gate, rk_ref, lnw_ref, lnb_ref, bdiag_ref):
    mu = _head_sum(y, bdiag_ref) * (1.0 / RWKV_HEAD)
    dlt = y - mu
    var = _head_sum(dlt * dlt, bdiag_ref) * (1.0 / RWKV_HEAD)
    yn = dlt * lax.rsqrt(var + LNX_EPS) * lnw_ref[...] + lnb_ref[...]
    bonus = _head_sum(r * kmod * rk_ref[...], bdiag_ref) * v
    return (yn + bonus) * gate


def _chains(x, nc):
    return [x[c * CHUNK:(c + 1) * CHUNK, g * GROUP_LANES:(g + 1) * GROUP_LANES]
            for g in range(RWKV_WIDTH // GROUP_LANES) for c in range(nc)]


def _block_diag(x, m4):
    xb = x.astype(BF16)
    return jnp.concatenate([xb] * (GROUP_LANES // RWKV_HEAD), axis=0) * m4


def _mix0_seq_body(x_ref, gmix_ref, win_ref, wout_ref,
                   pool_w_ref, pool_scale_ref, mu_ref, w0_ref, w2_ref, a0_ref, a2_ref, g2_ref,
                   kk_ref, ka_ref, rk_ref, lnw_ref, lnb_ref, bdiag_ref, tril_ref, blk_ref,
                   o_ref, s_out_ref, pool_tail_ref, shift_tail_ref,
                   ext_u, ext_p, s_ref, ycat):
    t = pl.program_id(1)
    tt = o_ref.shape[1]
    nc = tt // CHUNK
    ng = RWKV_WIDTH // GROUP_LANES

    @pl.when(t == 0)
    def _():
        ext_u[0:POOL_HALO, :] = jnp.zeros((POOL_HALO, POOL_WIDTH), F32)
        ext_p[0:SHIFT_HALO, :] = jnp.zeros((SHIFT_HALO, SHIFT_WIDTH), F32)
        s_ref[...] = jnp.zeros_like(s_ref)

    x = x_ref[0]
    h = jnp.dot(_rmsnorm(x, gmix_ref[...]).astype(BF16), win_ref[...], preferred_element_type=F32)
    ext_u[POOL_HALO:POOL_HALO + tt, :] = h[:, 0:POOL_WIDTH]
    ext_p[SHIFT_HALO:SHIFT_HALO + tt, :] = h[:, POOL_WIDTH:IN_AB_WIDTH]

    pos = t * tt + lax.broadcasted_iota(jnp.int32, (tt, 1), 0)
    for g, win in enumerate(POOL_WINDOWS):
        cs = slice(g * POOL_GC, (g + 1) * POOL_GC)
        u = ext_u[POOL_HALO:POOL_HALO + tt, cs]
        acc = u
        for j in range(1, win):
            acc = acc + ext_u[POOL_HALO - j:POOL_HALO - j + tt, cs]
        cnt = jnp.minimum(win, pos + 1).astype(F32)
        ycat[:, cs] = _pool_group_out(acc, cnt, u, pool_w_ref, pool_scale_ref, g)

    p = ext_p[SHIFT_HALO:SHIFT_HALO + tt, :]
    shifted = ext_p[SHIFT_HALO - 1:SHIFT_HALO - 1 + tt, :]
    prm = (mu_ref, w0_ref, w2_ref, a0_ref, a2_ref, g2_ref, kk_ref, ka_ref, bdiag_ref)
    r, lw, kmod, v, kk, a, gate = _rwkv_prep(p, shifted, prm)

    ext_u[0:POOL_HALO, :] = ext_u[tt:tt + POOL_HALO, :]
    ext_p[0:SHIFT_HALO, :] = ext_p[tt:tt + SHIFT_HALO, :]

    cl = _dot_exact_lhs(tril_ref[...], lw)
    tot = _dot_exact_lhs(blk_ref[...], lw)
    gam = jnp.exp(cl)
    gam_inv = jnp.exp(-cl)
    gam_end = jnp.exp(tot)
    beta = kk * a
    kd_full = kmod * gam_inv
    bd_full = beta * gam_inv
    kq_c = _chains(kk * jnp.exp(cl - lw), nc)
    rq_c = _chains(r * gam, nc)
    kd_c = _chains(kd_full, nc)
    bd_c = _chains(bd_full, nc)
    v_c = _chains(v, nc)
    kdg_c = _chains(kd_full * gam_end, nc)
    bdg_c = _chains(bd_full * gam_end, nc)
    ge_c = _chains(gam_end, nc)
    n_ch = ng * nc
    every = range(n_ch)

    m4 = bdiag_ref[0:GROUP_LANES, 0:GROUP_LANES]
    m4f = m4.astype(F32)
    row = lax.broadcasted_iota(jnp.int32, (CHUNK, GROUP_LANES), 0)
    col = lax.broadcasted_iota(jnp.int32, (CHUNK, GROUP_LANES), 1) % RWKV_HEAD
    strict = row > col
    incl = row >= col
    eye_c = jnp.where(row == col, 1.0, 0.0)
    eye_g = (lax.broadcasted_iota(jnp.int32, (GROUP_LANES, GROUP_LANES), 0)
             == lax.broadcasted_iota(jnp.int32, (GROUP_LANES, GROUP_LANES), 1))

    bd_kd = [_block_diag(kd_c[i], m4) for i in every]
    bd_bd = [_block_diag(bd_c[i], m4) for i in every]
    lhs = [jnp.concatenate([kq_c[i], rq_c[i]], axis=0) for i in every]
    pk = [_dot_nt(lhs[i], bd_kd[i]) for i in every]
    pb = [_dot_nt(lhs[i], bd_bd[i]) for i in every]
    a_k = [jnp.where(strict, pk[i][:CHUNK], 0.0) for i in every]
    p_k = [jnp.where(incl, pk[i][CHUNK:], 0.0) for i in every]
    a_b = [jnp.where(strict, pb[i][:CHUNK], 0.0) for i in every]
    p_b = [jnp.where(incl, pb[i][CHUNK:], 0.0) for i in every]
    tm = [eye_c - a_b[i] for i in every]
    apow = a_b
    bd_ap = [_block_diag(apow[i], m4) for i in every]
    n = 1
    while 2 * n < CHUNK:
        apow = [_dot(apow[i], bd_ap[i]) for i in every]
        bd_ap = [_block_diag(apow[i], m4) for i in every]
        tm = [tm[i] + _dot(tm[i], bd_ap[i]) for i in every]
        n *= 2
    bd_v = [_block_diag(v_c[i], m4) for i in every]
    t_kq = [_dot(tm[i], _block_diag(kq_c[i], m4)) for i in every]
    akv = [_dot(a_k[i], bd_v[i]) for i in every]
    t_akv = [_dot(tm[i], _block_diag(akv[i], m4)) for i in every]
    r_y = [rq_c[i] - _dot(p_b[i], _block_diag(t_kq[i], m4)) for i in every]
    y0 = [_dot(p_k[i], bd_v[i]) - _dot(p_b[i], _block_diag(t_akv[i], m4)) for i in every]
    g_m = [m4f * (jnp.where(eye_g, ge_c[i][0:1, :], 0.0) - _dot_tn(t_kq[i], bdg_c[i])) for i in every]
    h_full = [m4f * _dot_tn(jnp.concatenate([v_c[i], t_akv[i]], axis=0),
                            jnp.concatenate([kdg_c[i], -bdg_c[i]], axis=0)) for i in every]
    h_t = [sum(h_full[i][j * RWKV_HEAD:(j + 1) * RWKV_HEAD] for j in range(GROUP_LANES // RWKV_HEAD))
           for i in every]

    ys = [None] * n_ch
    for g in range(ng):
        s = s_ref[g]
        for c in range(nc):
            i = g * nc + c
            ys[i] = _dot_nt(r_y[i], _block_diag(s, m4)) + y0[i]
            s = _dot(s, g_m[i]) + h_t[i]
        s_ref[g] = s
    y = jnp.concatenate([jnp.concatenate([ys[g * nc + c] for g in range(ng)], axis=1) for c in range(nc)],
                        axis=0)

    ycat[:, POOL_WIDTH:] = _rwkv_finish(y, r, kmod, v, gate, rk_ref, lnw_ref, lnb_ref, bdiag_ref)
    o_ref[0] = x + jnp.dot(ycat[...].astype(BF16), wout_ref[...], preferred_element_type=F32)

    @pl.when(t == pl.num_programs(1) - 1)
    def _():
        pool_tail_ref[0] = ext_u[0:POOL_HALO, :]
        shift_tail_ref[0] = ext_p[0:SHIFT_HALO, :]
        for hh in range(RWKV_HEADS):
            g, j = divmod(hh, GROUP_LANES // RWKV_HEAD)
            s_out_ref[0, hh] = s_ref[g][:, j * RWKV_HEAD:(j + 1) * RWKV_HEAD]


def _mix0_params(P):
    c = RWKV_WIDTH
    row = lambda x: x.reshape(1, -1).astype(F32)
    w2 = jnp.zeros((W_RANK + A_RANK, c), F32).at[:W_RANK].set(P['rw_w2']).astype(BF16)
    a2 = jnp.zeros((W_RANK + A_RANK, c), F32).at[W_RANK:].set(P['rw_a2']).astype(BF16)
    hid = jnp.arange(c) // RWKV_HEAD
    bdiag = (hid[:, None] == hid[None, :]).astype(BF16)
    return (P['pool_w'].astype(BF16), row(P['pool_scale']), row(P['mu_shift']), row(P['rw_w0']), w2,
            row(P['rw_a0']), a2, P['rw_g2'].astype(BF16), row(P['rw_kk']), row(P['rw_ka']),
            row(P['rw_rk']), row(P['rw_lnx_w']), row(P['rw_lnx_b']), bdiag)


def _full_spec(x):
    nd = x.ndim
    return pl.BlockSpec(x.shape, lambda *_: (0,) * nd)


def mix0_seq(x, g, w_in, w_out, P):
    b, t, _ = x.shape
    tt = _row_tile(t, 512)
    assert tt % CHUNK == 0 and tt >= POOL_HALO
    prm = _mix0_params(P)
    ti = jnp.arange(tt)
    same_chunk = (ti[:, None] // CHUNK) == (ti[None, :] // CHUNK)
    tril = (same_chunk & (ti[:, None] >= ti[None, :])).astype(BF16)
    blk = same_chunk.astype(BF16)
    return pl.pallas_call(
        _mix0_seq_body,
        grid=(b, t // tt),
        in_specs=[pl.BlockSpec((1, tt, D_MODEL), lambda i, j: (i, j, 0)),
                  pl.BlockSpec((1, D_MODEL), lambda i, j: (0, 0)), _full_spec(w_in), _full_spec(w_out)]
                 + [_full_spec(p) for p in prm] + [_full_spec(tril), _full_spec(blk)],
        out_specs=[pl.BlockSpec((1, tt, D_MODEL), lambda i, j: (i, j, 0)),
                   pl.BlockSpec((1, RWKV_HEADS, RWKV_HEAD, RWKV_HEAD), lambda i, j: (i, 0, 0, 0)),
                   pl.BlockSpec((1, POOL_HALO, POOL_WIDTH), lambda i, j: (i, 0, 0)),
                   pl.BlockSpec((1, SHIFT_HALO, SHIFT_WIDTH), lambda i, j: (i, 0, 0))],
        out_shape=[jax.ShapeDtypeStruct((b, t, D_MODEL), F32),
                   jax.ShapeDtypeStruct((b, RWKV_HEADS, RWKV_HEAD, RWKV_HEAD), F32),
                   jax.ShapeDtypeStruct((b, POOL_HALO, POOL_WIDTH), F32),
                   jax.ShapeDtypeStruct((b, SHIFT_HALO, SHIFT_WIDTH), F32)],
        scratch_shapes=[pltpu.VMEM((tt + POOL_HALO, POOL_WIDTH), F32),
                        pltpu.VMEM((tt + SHIFT_HALO, SHIFT_WIDTH), F32),
                        pltpu.VMEM((RWKV_WIDTH // GROUP_LANES, RWKV_HEAD, GROUP_LANES), F32),
                        pltpu.VMEM((tt, D_MODEL), F32)],
        compiler_params=pltpu.CompilerParams(dimension_semantics=("parallel", "arbitrary")),
        name="mix0_seq",
    )(x, g.reshape(1, D_MODEL), w_in, w_out, *prm, tril, blk)


def _to_leading(x):
    n = x.shape[0]
    ii = lax.broadcasted_iota(jnp.int32, (n, n, 1), 0)
    jj = lax.broadcasted_iota(jnp.int32, (n, n, 1), 1)
    return jnp.sum(jnp.where(ii == jj, x[None, :, :], 0.0), axis=1, keepdims=True)


def _from_leading(x3):
    n = x3.shape[0]
    ii = lax.broadcasted_iota(jnp.int32, (n, n, 1), 0)
    jj = lax.broadcasted_iota(jnp.int32, (n, n, 1), 1)
    return jnp.sum(jnp.where(ii == jj, x3, 0.0), axis=0)


def _mix0_step_body(h_ref, pool_prev_ref, shift_prev_ref, s_in_ref,
                    pool_w_ref, pool_scale_ref, mu_ref, w0_ref, w2_ref, a0_ref, a2_ref, g2_ref,
                    kk_ref, ka_ref, rk_ref, lnw_ref, lnb_ref, bdiag_ref,
                    o_ref, s_out_ref,
                    r_t, w_t, k_t, v_t, kk_t, bt_t, y_t, r_s, k_s, v_s, g_s):
    hh = pl.program_id(0)
    nb = h_ref.shape[0]

    @pl.when(hh == 0)
    def _():
        for g, win in enumerate(POOL_WINDOWS):
            cs = slice(g * POOL_GC, (g + 1) * POOL_GC)
            u = h_ref[:, cs]
            acc = u
            for j in range(1, win):
                acc = acc + pool_prev_ref[POOL_BUF - j][:, cs]
            cnt = jnp.float32(min(win, PAST_LEN + 1))
            o_ref[:, cs] = _pool_group_out(acc, cnt, u, pool_w_ref, pool_scale_ref, g)
        prm = (mu_ref, w0_ref, w2_ref, a0_ref, a2_ref, g2_ref, kk_ref, ka_ref, bdiag_ref)
        r, lw, kmod, v, kk, a, gate = _rwkv_prep(h_ref[:, POOL_WIDTH:IN_AB_WIDTH], shift_prev_ref[...], prm)
        r_s[...] = r
        k_s[...] = kmod
        v_s[...] = v
        g_s[...] = gate
        r_t[...] = r.T
        w_t[...] = jnp.exp(lw).T
        k_t[...] = kmod.T
        v_t[...] = v.T
        kk_t[...] = kk.T
        bt_t[...] = (kk * a).T

    rows = pl.ds(pl.multiple_of(hh * RWKV_HEAD, RWKV_HEAD), RWKV_HEAD)
    s = s_in_ref[...].T.reshape(RWKV_HEAD, RWKV_HEAD, nb)
    kk = kk_t[rows, :][None]
    s_kk = jnp.sum(s * kk, axis=1, keepdims=True)
    v3 = _to_leading(v_t[rows, :])
    s = s * w_t[rows, :][None] - s_kk * bt_t[rows, :][None] + v3 * k_t[rows, :][None]
    y3 = jnp.sum(s * r_t[rows, :][None], axis=1, keepdims=True)
    y_t[rows, :] = _from_leading(y3)
    s_out_ref[...] = s.reshape(RWKV_HEAD * RWKV_HEAD, nb).T

    @pl.when(hh == pl.num_programs(0) - 1)
    def _():
        o_ref[:, POOL_WIDTH:] = _rwkv_finish(y_t[...].T, r_s[...], k_s[...], v_s[...], g_s[...],
                                             rk_ref, lnw_ref, lnb_ref, bdiag_ref)


def mix0_step(h, pool_prev, shift_prev, s_prev, P):
    b = h.shape[0]
    prm = _mix0_params(P)
    hw = RWKV_HEAD * RWKV_HEAD
    s2 = s_prev.reshape(b, RWKV_HEADS * hw)
    pool_t = jnp.swapaxes(pool_prev, 0, 1)
    tvec = lambda: pltpu.VMEM((RWKV_WIDTH, b), F32)
    svec = lambda: pltpu.VMEM((b, RWKV_WIDTH), F32)
    out, s_new = pl.pallas_call(
        _mix0_step_body,
        grid=(RWKV_HEADS,),
        in_specs=[_full_spec(h), _full_spec(pool_t), _full_spec(shift_prev),
                  pl.BlockSpec((b, hw), lambda i: (0, i))] + [_full_spec(x) for x in prm],
        out_specs=[pl.BlockSpec((b, D_MODEL), lambda i: (0, 0)),
                   pl.BlockSpec((b, hw), lambda i: (0, i))],
        out_shape=[jax.ShapeDtypeStruct((b, D_MODEL), F32),
                   jax.ShapeDtypeStruct((b, RWKV_HEADS * hw), F32)],
        scratch_shapes=[tvec(), tvec(), tvec(), tvec(), tvec(), tvec(), tvec(),
                        svec(), svec(), svec(), svec()],
        compiler_params=pltpu.CompilerParams(dimension_semantics=("arbitrary",)),
        name="mix0_step",
    )(h, pool_t, shift_prev, s2, *prm)
    return out, s_new.reshape(b, RWKV_HEADS, RWKV_HEAD, RWKV_HEAD)


def _conv_seq_body(x_ref, g_ref, wi_ref, cw_ref, wo_ref, o_ref, tail_ref, ext):
    t = pl.program_id(1)
    tt = o_ref.shape[1]
    c = D_MODEL

    @pl.when(t == 0)
    def _():
        ext[0:SHIFT_HALO, :] = jnp.zeros((SHIFT_HALO, c), F32)

    x = x_ref[0]
    h = jnp.dot(_rmsnorm(x, g_ref[...]).astype(BF16), wi_ref[...], preferred_element_type=F32)
    ext[SHIFT_HALO:SHIFT_HALO + tt, :] = h[:, c:2 * c] * h[:, 2 * c:3 * c]
    z = cw_ref[0:1, :] * ext[SHIFT_HALO - 2:SHIFT_HALO - 2 + tt, :]
    z = z + cw_ref[1:2, :] * ext[SHIFT_HALO - 1:SHIFT_HALO - 1 + tt, :]
    z = z + cw_ref[2:3, :] * ext[SHIFT_HALO:SHIFT_HALO + tt, :]
    gated = (h[:, 0:c] * z).astype(BF16)
    o_ref[0] = x + jnp.dot(gated, wo_ref[...], preferred_element_type=F32)
    ext[0:SHIFT_HALO, :] = ext[tt:tt + SHIFT_HALO, :]

    @pl.when(t == pl.num_programs(1) - 1)
    def _():
        tail_ref[0] = ext[SHIFT_HALO - (CONV_WIDTH - 1):SHIFT_HALO, :]


def conv_seq(x, g, w_in, conv_w, w_out):
    b, t, c = x.shape
    tt = _row_tile(t, 512)
    return pl.pallas_call(
        _conv_seq_body,
        grid=(b, t // tt),
        in_specs=[pl.BlockSpec((1, tt, c), lambda i, j: (i, j, 0)),
                  pl.BlockSpec((1, c), lambda i, j: (0, 0)),
                  _full_spec(w_in), _full_spec(conv_w), _full_spec(w_out)],
        out_specs=[pl.BlockSpec((1, tt, c), lambda i, j: (i, j, 0)),
                   pl.BlockSpec((1, CONV_WIDTH - 1, c), lambda i, j: (i, 0, 0))],
        out_shape=[jax.ShapeDtypeStruct((b, t, c), F32),
                   jax.ShapeDtypeStruct((b, CONV_WIDTH - 1, c), F32)],
        scratch_shapes=[pltpu.VMEM((tt + SHIFT_HALO, c), F32)],
        compiler_params=pltpu.CompilerParams(dimension_semantics=("parallel", "arbitrary")),
        name="conv_seq",
    )(x, g.reshape(1, c), w_in, conv_w, w_out)


def _conv_step_body(h_ref, x_ref, p0_ref, p1_ref, cw_ref, wo_ref, o_ref, e_ref):
    c = D_MODEL
    e = h_ref[:, c:2 * c] * h_ref[:, 2 * c:3 * c]
    z = cw_ref[0:1, :] * p0_ref[...] + cw_ref[1:2, :] * p1_ref[...] + cw_ref[2:3, :] * e
    gated = (h_ref[:, 0:c] * z).astype(BF16)
    o_ref[...] = x_ref[...] + jnp.dot(gated, wo_ref[...], preferred_element_type=F32)
    e_ref[...] = e


def conv_step(h, x, prev, conv_w, w_out):
    b = h.shape[0]
    c = D_MODEL
    args = (h, x, prev[:, 0, :], prev[:, 1, :], conv_w, w_out)
    return pl.pallas_call(
        _conv_step_body,
        grid=(1,),
        in_specs=[_full_spec(a) for a in args],
        out_specs=[pl.BlockSpec((b, c), lambda i: (0, 0)), pl.BlockSpec((b, c), lambda i: (0, 0))],
        out_shape=[jax.ShapeDtypeStruct((b, c), F32), jax.ShapeDtypeStruct((b, c), F32)],
        compiler_params=pltpu.CompilerParams(dimension_semantics=("arbitrary",)),
        name="conv_step",
    )(*args)


def _xattn(x2, i, mem_k, mem_v, W, seq_shape, extra_rows=0, into=None):
    last = i == W['norm_xattn'].shape[0] - 1
    if seq_shape is not None:
        b, t = seq_shape
        y = xattn_seq(x2.reshape(b, t, D_MODEL), W['norm_xattn'][i], W['w_xq'][i], W['w_xo'][i],
                      mem_k, mem_v, i, row_tiles=last, extra_rows=extra_rows)
        return y if last else y.reshape(b * t, D_MODEL)
    return xattn_step(x2, W['norm_xattn'][i], W['w_xq'][i], W['w_xo'][i], mem_k, mem_v, i,
                      row_tiles=last, into=into)


def _ffn0(x2, W):
    return ffn_dense(x2, W['norm_ffn'][0], W['ffn_gate'][0], W['ffn_up'][0], W['ffn_down'][0], FFN_TILE)


def _trunk_seq(x, mem_k, mem_v, W, extra_rows):
    b, t, d = x.shape
    P0 = {k: v[0] for k, v in W['mix0'].items()}
    x1, wkv, pool_tail, shift_tail = mix0_seq(x, W['norm_mix'][0], W['w_in_ab'][0], W['w_out_ab'][0], P0)
    pool = pool_tail[:, POOL_HALO - POOL_BUF:]
    shift = shift_tail[:, SHIFT_HALO - 1]
    x2 = _ffn0(_xattn(x1.reshape(b * t, d), 0, mem_k, mem_v, W, (b, t)), W)
    x3, conv = conv_seq(x2.reshape(b, t, d), W['norm_mix'][1], W['w_in_c'][0], W['conv_w'][0], W['w_out_c'][0])
    x8 = _xattn(x3.reshape(b * t, d), 1, mem_k, mem_v, W, (b, t), extra_rows=extra_rows)
    return x8, pool[None], shift[None], wkv[None], conv[None]


def _trunk_step(x, mem_k, mem_v, pool_prev, shift_prev, wkv_prev, conv_prev, W, into):
    b, _, d = x.shape
    x2 = x.reshape(b, d)
    P0 = {k: v[0] for k, v in W['mix0'].items()}
    h = norm_matmul(x2, W['norm_mix'][0], W['w_in_ab'][0])
    mix, wkv = mix0_step(h, pool_prev[0], shift_prev[0], wkv_prev[0], P0)
    pool = jnp.concatenate([pool_prev[0][:, 1:], h[:, None, :POOL_WIDTH]], axis=1)
    shift = h[:, POOL_WIDTH:]
    x2 = matmul_res(mix, W['w_out_ab'][0], x2)
    x2 = _ffn0(_xattn(x2, 0, mem_k, mem_v, W, None), W)
    h = norm_matmul(x2, W['norm_mix'][1], W['w_in_c'][0])
    x2, e = conv_step(h, x2, conv_prev[0], W['conv_w'][0], W['w_out_c'][0])
    conv = jnp.concatenate([conv_prev[0][:, 1:], e[:, None]], axis=1)
    x8 = _xattn(x2, 1, mem_k, mem_v, W, None, into=into)
    return x8, pool[None], shift[None], wkv[None], conv[None]


def _pack_heads(c):
    lead = c.shape[:-2]
    halves = X_HEAD_DIM // 128
    c = c.reshape(*lead, X_HEADS, halves, 128)
    c = jnp.swapaxes(c, -3, -2)
    return c.reshape(*lead, halves * X_HEADS, 128)


def _unpack_heads(c):
    lead = c.shape[:-2]
    halves = X_HEAD_DIM // 128
    c = c.reshape(*lead, halves, X_HEADS, 128)
    c = jnp.swapaxes(c, -3, -2)
    return c.reshape(*lead, X_HEADS, X_HEAD_DIM)


def kernel(x_prompt, x_sample, mem_prompt, cache_mem_k, cache_mem_v, state_pool, state_shift, state_wkv, state_conv, norm_mix, norm_xattn, norm_mem, norm_ffn, norm_final, w_xq, w_xk, w_xv, w_xo, w_in_ab, pool_w, pool_scale, mu_shift, rw_w0, rw_w2, rw_a0, rw_a2, rw_g2, rw_kk, rw_ka, rw_rk, rw_lnx_w, rw_lnx_b, w_out_ab, ffn_gate, ffn_up, ffn_down, w_in_c, conv_w, w_out_c, router_w, router_b, moe_gate, moe_up, moe_down):
    depth = norm_mix.shape[0]
    assert depth == 2 and w_in_ab.shape[0] == 1 and w_in_c.shape[0] == 1
    bp = x_prompt.shape[0]
    bs = x_sample.shape[0]
    d = D_MODEL
    bf = lambda w: w.astype(BF16)
    W = dict(norm_mix=norm_mix, norm_xattn=norm_xattn, norm_ffn=norm_ffn, norm_final=norm_final,
             w_xq=bf(w_xq), w_xo=bf(w_xo), w_in_ab=bf(w_in_ab), w_out_ab=bf(w_out_ab),
             ffn_gate=bf(ffn_gate), ffn_up=bf(ffn_up), ffn_down=bf(ffn_down),
             w_in_c=bf(w_in_c), conv_w=conv_w, w_out_c=bf(w_out_c),
             router_w=router_w, router_b=router_b,
             moe_gate=bf(moe_gate), moe_up=bf(moe_up), moe_down=bf(moe_down),
             mix0=dict(pool_w=pool_w, pool_scale=pool_scale, mu_shift=mu_shift, rw_w0=rw_w0, rw_w2=rw_w2,
                       rw_a0=rw_a0, rw_a2=rw_a2, rw_g2=rw_g2, rw_kk=rw_kk, rw_ka=rw_ka, rw_rk=rw_rk,
                       rw_lnx_w=rw_lnx_w, rw_lnx_b=rw_lnx_b))

    mk, mv = mem_kv(mem_prompt.reshape(bp * N_MEM, d), norm_mem, bf(w_xk), bf(w_xv))

    n_p = bp * x_prompt.shape[1]
    x8, pool_p, shift_p, wkv_p, conv_p = _trunk_seq(x_prompt, mk, mv, W, bs)
    x8, pool_s, shift_s, wkv_s, conv_s = _trunk_step(x_sample, _pack_heads(cache_mem_k),
                                                      _pack_heads(cache_mem_v), state_pool,
                                                      state_shift, state_wkv, state_conv, W, (x8, n_p))
    y_p, y_s = moe_top2_norm(x8, [n_p, bs], norm_ffn[1], router_w[0], router_b[0], W['moe_gate'][0],
                             W['moe_up'][0], W['moe_down'][0], norm_final)
    y_p = y_p.reshape(x_prompt.shape)
    y_s = y_s.reshape(x_sample.shape)
    mem_k_p = _unpack_heads(mk.reshape(depth, bp, N_MEM, ROW_SUBLANES, 128))
    mem_v_p = _unpack_heads(mv.reshape(depth, bp, N_MEM, ROW_SUBLANES, 128))
    return (y_p, y_s, pool_p, pool_s, shift_p, shift_s, wkv_p, wkv_s, conv_p, conv_s, mem_k_p, mem_v_p)
```

```python
import functools

import jax
import jax.numpy as jnp
from jax import lax
from jax.experimental import pallas as pl
from jax.experimental.pallas import tpu as pltpu

F32 = jnp.float32
BF16 = jnp.bfloat16

D_MODEL = 1024
POOL_WIDTH = 512
POOL_GROUPS = 4
POOL_GC = 128
POOL_WINDOWS = (2, 4, 8, 16)
POOL_BUF = 15
RWKV_WIDTH = 512
RWKV_HEAD = 64
RWKV_HEADS = 8
W_RANK = 64
A_RANK = 64
G_RANK = 128
SHIFT_WIDTH = 3 * RWKV_WIDTH + W_RANK + A_RANK + G_RANK
IN_AB_WIDTH = POOL_WIDTH + SHIFT_WIDTH
LNX_EPS = 64e-5
CONV_WIDTH = 3
N_EXPERTS = 8
N_MEM = 256
X_HEADS = 4
X_HEAD_DIM = 256
RMS_EPS = 1e-6
PAST_LEN = 16384

CHUNK = 64
GROUP_LANES = 256
POOL_HALO = 16
SHIFT_HALO = 8
ROUTER_PAD = 128
TOP_K = 2
ROW_SUBLANES = 8
MOE_ROWS = 512
FFN_TILE = 1408
MOE_TILE = 1792
XATTN_STEP_VMEM = 48 * 1024 * 1024


def _rmsnorm(x, g):
    ms = jnp.mean(x * x, axis=-1, keepdims=True)
    return x * lax.rsqrt(ms + RMS_EPS) * g


def _dot(a, b):
    return jnp.dot(a.astype(BF16), b.astype(BF16), preferred_element_type=F32)


def _dot_nt(a, b):
    return lax.dot_general(a.astype(BF16), b.astype(BF16), (((1,), (1,)), ((), ())),
                           preferred_element_type=F32)


def _dot_tn(a, b):
    return lax.dot_general(a.astype(BF16), b.astype(BF16), (((0,), (0,)), ((), ())),
                           preferred_element_type=F32)


def _split3(x):
    hi = x.astype(BF16)
    r1 = x - hi.astype(F32)
    mid = r1.astype(BF16)
    lo = (r1 - mid.astype(F32)).astype(BF16)
    return hi, mid, lo


def _dot_exact_lhs(m01, x):
    hi, mid, lo = _split3(x)
    f = lambda p: jnp.dot(m01, p, preferred_element_type=F32)
    return f(hi) + f(mid) + f(lo)


def _softplus(x):
    return jnp.maximum(x, 0.0) + jnp.log1p(jnp.exp(-jnp.abs(x)))


def _sigmoid(x):
    return 1.0 / (1.0 + jnp.exp(-x))


def _row_tile(m, want):
    t = min(m, want)
    assert m % t == 0, (m, t)
    return t


def _norm_matmul_body(x_ref, g_ref, w_ref, o_ref):
    xn = _rmsnorm(x_ref[...], g_ref[...]).astype(BF16)
    o_ref[...] = jnp.dot(xn, w_ref[...], preferred_element_type=F32)


def norm_matmul(x, g, w):
    m, k = x.shape
    n = w.shape[1]
    tm = _row_tile(m, 512)
    return pl.pallas_call(
        _norm_matmul_body,
        grid=(m // tm,),
        in_specs=[pl.BlockSpec((tm, k), lambda i: (i, 0)),
                  pl.BlockSpec((1, k), lambda i: (0, 0)),
                  pl.BlockSpec((k, n), lambda i: (0, 0))],
        out_specs=pl.BlockSpec((tm, n), lambda i: (i, 0)),
        out_shape=jax.ShapeDtypeStruct((m, n), F32),
        compiler_params=pltpu.CompilerParams(dimension_semantics=("parallel",)),
        name="norm_matmul",
    )(x, g.reshape(1, k), w)


PACKED_COLS = [(r % X_HEADS) * X_HEAD_DIM + (r // X_HEADS) * 128 for r in range(ROW_SUBLANES)]
PACKED_ORDER = sorted(range(ROW_SUBLANES), key=lambda r: PACKED_COLS[r])


def _store_packed(ref, y):
    n = y.shape[0]
    for r, c0 in enumerate(PACKED_COLS):
        ref[0, pl.ds(r, n, stride=ROW_SUBLANES), :] = y[:, c0:c0 + 128]


def _load_packed(ref, n):
    return jnp.concatenate([ref[0, pl.ds(r, n, stride=ROW_SUBLANES), :] for r in PACKED_ORDER], axis=1)


def _mem_kv_body(m_ref, g_ref, wk_ref, wv_ref, k_ref, v_ref):
    xn = _rmsnorm(m_ref[...], g_ref[0]).astype(BF16)
    _store_packed(k_ref, jnp.dot(xn, wk_ref[0], preferred_element_type=F32))
    _store_packed(v_ref, jnp.dot(xn, wv_ref[0], preferred_element_type=F32))


def mem_kv(mem, g, wk, wv):
    r, d = mem.shape
    nl = g.shape[0]
    tm = _row_tile(r, 512)
    out = jax.ShapeDtypeStruct((nl, r * ROW_SUBLANES, 128), F32)
    return pl.pallas_call(
        _mem_kv_body,
        grid=(nl, r // tm),
        in_specs=[pl.BlockSpec((tm, d), lambda l, j: (j, 0)),
                  pl.BlockSpec((1, 1, d), lambda l, j: (l, 0, 0)),
                  pl.BlockSpec((1, d, d), lambda l, j: (l, 0, 0)),
                  pl.BlockSpec((1, d, d), lambda l, j: (l, 0, 0))],
        out_specs=[pl.BlockSpec((1, tm * ROW_SUBLANES, 128), lambda l, j: (l, j, 0)),
                   pl.BlockSpec((1, tm * ROW_SUBLANES, 128), lambda l, j: (l, j, 0))],
        out_shape=[out, out],
        compiler_params=pltpu.CompilerParams(dimension_semantics=("parallel", "parallel")),
        name="mem_kv",
    )(mem, g.reshape(nl, 1, d), wk, wv)


def _matmul_res_body(a_ref, w_ref, r_ref, o_ref):
    o_ref[...] = r_ref[...] + jnp.dot(a_ref[...].astype(BF16), w_ref[...], preferred_element_type=F32)


def matmul_res(a, w, res):
    m, k = a.shape
    n = w.shape[1]
    tm = _row_tile(m, 512)
    return pl.pallas_call(
        _matmul_res_body,
        grid=(m // tm,),
        in_specs=[pl.BlockSpec((tm, k), lambda i: (i, 0)),
                  pl.BlockSpec((k, n), lambda i: (0, 0)),
                  pl.BlockSpec((tm, n), lambda i: (i, 0))],
        out_specs=pl.BlockSpec((tm, n), lambda i: (i, 0)),
        out_shape=jax.ShapeDtypeStruct((m, n), F32),
        compiler_params=pltpu.CompilerParams(dimension_semantics=("parallel",)),
        name="matmul_res",
    )(a, w, res)


def _ffn_body(x_ref, g_ref, wg_ref, wu_ref, wd_ref, o_ref, xn_ref, acc_ref):
    f = pl.program_id(1)

    @pl.when(f == 0)
    def _():
        xn_ref[...] = _rmsnorm(x_ref[...], g_ref[...]).astype(BF16)
        acc_ref[...] = jnp.zeros_like(acc_ref)

    xn = xn_ref[...]
    gate = jnp.dot(xn, wg_ref[...], preferred_element_type=F32)
    up = jnp.dot(xn, wu_ref[...], preferred_element_type=F32)
    hid = (gate * _sigmoid(gate) * up).astype(BF16)
    acc_ref[...] += jnp.dot(hid, wd_ref[...], preferred_element_type=F32)

    @pl.when(f == pl.num_programs(1) - 1)
    def _():
        o_ref[...] = x_ref[...] + acc_ref[...]


def ffn_dense(x, g, wg, wu, wd, tf):
    m, d = x.shape
    ff = wg.shape[1]
    tm = _row_tile(m, 512)
    return pl.pallas_call(
        _ffn_body,
        grid=(m // tm, ff // tf),
        in_specs=[pl.BlockSpec((tm, d), lambda i, f: (i, 0)),
                  pl.BlockSpec((1, d), lambda i, f: (0, 0)),
                  pl.BlockSpec((d, tf), lambda i, f: (0, f)),
                  pl.BlockSpec((d, tf), lambda i, f: (0, f)),
                  pl.BlockSpec((tf, d), lambda i, f: (f, 0))],
        out_specs=pl.BlockSpec((tm, d), lambda i, f: (i, 0)),
        out_shape=jax.ShapeDtypeStruct((m, d), F32),
        scratch_shapes=[pltpu.VMEM((tm, d), BF16), pltpu.VMEM((tm, d), F32)],
        compiler_params=pltpu.CompilerParams(dimension_semantics=("parallel", "arbitrary")),
        name="ffn_dense",
    )(x, g.reshape(1, d), wg, wu, wd)


def _router_body(x_ref, g_ref, w_ref, b_ref, oi_ref, ow_ref):
    xn = _rmsnorm(_load_row_tiles(x_ref, oi_ref.shape[0]), g_ref[...])
    logits = _dot_3pass(xn, w_ref[...]) + b_ref[...]
    lane = lax.broadcasted_iota(jnp.int32, logits.shape, 1)
    neg = jnp.float32(-jnp.inf)
    logits = jnp.where(lane < N_EXPERTS, logits, neg)
    m1 = jnp.max(logits, axis=-1, keepdims=True)
    i1 = jnp.min(jnp.where(logits == m1, lane, ROUTER_PAD), axis=-1, keepdims=True)
    rest = jnp.where(lane == i1, neg, logits)
    m2 = jnp.max(rest, axis=-1, keepdims=True)
    i2 = jnp.min(jnp.where(rest == m2, lane, ROUTER_PAD), axis=-1, keepdims=True)
    e2 = jnp.exp(m2 - m1)
    den = 1.0 + e2
    slot = lax.broadcasted_iota(jnp.int32, oi_ref.shape, 1)
    oi_ref[...] = jnp.where(slot == 0, i1, i2)
    ow_ref[...] = jnp.where(slot == 0, 1.0 / den, e2 / den)


def _dot_3pass(x, w):
    xh = x.astype(BF16)
    xm = (x - xh.astype(F32)).astype(BF16)
    wh = w.astype(BF16)
    wm = (w - wh.astype(F32)).astype(BF16)
    f = lambda p, q: jnp.dot(p, q, preferred_element_type=F32)
    return f(xh, wh) + (f(xh, wm) + f(xm, wh))


def moe_router(x8, row0, m, g, router_w, router_b):
    d = D_MODEL
    tm = _row_tile(m, 512)
    assert row0 % tm == 0
    blk0 = row0 // tm
    w = jnp.zeros((d, ROUTER_PAD), F32).at[:, :N_EXPERTS].set(router_w)
    b = jnp.zeros((1, ROUTER_PAD), F32).at[0, :N_EXPERTS].set(router_b)
    return pl.pallas_call(
        _router_body,
        grid=(m // tm,),
        in_specs=[pl.BlockSpec((tm * ROW_SUBLANES, 128), lambda i: (blk0 + i, 0)),
                  pl.BlockSpec((1, d), lambda i: (0, 0)),
                  pl.BlockSpec((d, ROUTER_PAD), lambda i: (0, 0)),
                  pl.BlockSpec((1, ROUTER_PAD), lambda i: (0, 0))],
        out_specs=[pl.BlockSpec((tm, TOP_K), lambda i: (i, 0)), pl.BlockSpec((tm, TOP_K), lambda i: (i, 0))],
        out_shape=[jax.ShapeDtypeStruct((m, TOP_K), jnp.int32), jax.ShapeDtypeStruct((m, TOP_K), F32)],
        compiler_params=pltpu.CompilerParams(dimension_semantics=("parallel",)),
        name="moe_router",
    )(x8, g.reshape(1, d), w, b)


def _row_tiles_to_2d(ref, slot, n):
    return jnp.concatenate([ref[slot, pl.ds(j, n, stride=ROW_SUBLANES), :] for j in range(ROW_SUBLANES)],
                           axis=1)


def _row_copy(src_hbm, row, dst, slot, i, sem):
    return pltpu.make_async_copy(src_hbm.at[pl.ds(row * ROW_SUBLANES, ROW_SUBLANES)],
                                 dst.at[slot, pl.ds(i * ROW_SUBLANES, ROW_SUBLANES)], sem.at[slot])


def _gather_start(src_hbm, rows_smem, dst, slot, n, sem):
    def body(i, carry):
        for k in range(2):
            r = 2 * i + k
            _row_copy(src_hbm, rows_smem[0, 0, r], dst, slot, r, sem).start(priority=k)
        return carry
    lax.fori_loop(0, n // 2, body, 0, unroll=4)


def _gather_wait(src_hbm, dst, slot, n, sem):
    pltpu.make_async_copy(src_hbm.at[pl.ds(0, n * ROW_SUBLANES)], dst.at[slot], sem.at[slot]).wait()


def _moe_group_body(te_ref, nu_ref, tok0_ref, tokn_ref, x8_ref, g_ref, wg_ref, wu_ref, wd_ref,
                    o_ref, xbuf, sem, xn_ref, acc_ref, *, nf):
    t = pl.program_id(0)
    f = pl.program_id(1)
    tm = xn_ref.shape[0]
    per_step = tm // nf
    used = t < nu_ref[0]
    slot = t % 2

    @pl.when(jnp.logical_and(t == 0, f == 0))
    def _():
        _gather_start(x8_ref, tok0_ref, xbuf, 0, tm, sem)

    @pl.when(jnp.logical_and(used, f == 0))
    def _():
        _gather_wait(x8_ref, xbuf, slot, tm, sem)
        x = _row_tiles_to_2d(xbuf, slot, tm)
        xn_ref[...] = _rmsnorm(x, g_ref[...]).astype(BF16)
        acc_ref[...] = jnp.zeros_like(acc_ref)

    @pl.when(used)
    def _():
        base = f * per_step
        for i in range(per_step):
            _row_copy(x8_ref, tokn_ref[0, 0, base + i], xbuf, 1 - slot, base + i, sem).start()
        xn = xn_ref[...]
        gate = jnp.dot(xn, wg_ref[0], preferred_element_type=F32)
        up = jnp.dot(xn, wu_ref[0], preferred_element_type=F32)
        hid = (gate * _sigmoid(gate) * up).astype(BF16)
        acc_ref[...] += jnp.dot(hid, wd_ref[0], preferred_element_type=F32)

    @pl.when(f == nf - 1)
    def _():
        y = jnp.where(used, acc_ref[...], 0.0)
        for j in range(ROW_SUBLANES):
            o_ref[pl.ds(j, tm, stride=ROW_SUBLANES), :] = y[:, j * 128:(j + 1) * 128]

    @pl.when(jnp.logical_and(t == nu_ref[0] - 1, f == nf - 1))
    def _():
        _gather_wait(x8_ref, xbuf, 1 - slot, tm, sem)


def moe_grouped(x8, g, tile_expert, n_used, row_token, wg, wu, wd, tf):
    n_tiles, _, tm = row_token.shape
    d = D_MODEL
    ne, _, ff = wg.shape
    nf = ff // tf
    assert tm % nf == 0
    live_f = lambda t, f, nu: jnp.where(t < nu[0], f, nf - 1)
    smem_rows = lambda imap: pl.BlockSpec((1, 1, tm), imap, memory_space=pltpu.SMEM)
    grid_spec = pltpu.PrefetchScalarGridSpec(
        num_scalar_prefetch=2,
        grid=(n_tiles, nf),
        in_specs=[smem_rows(lambda t, f, te, nu: (0, 0, 0)),
                  smem_rows(lambda t, f, te, nu: (jnp.minimum(t + 1, nu[0] - 1), 0, 0)),
                  pl.BlockSpec(memory_space=pl.ANY),
                  pl.BlockSpec((1, d), lambda t, f, te, nu: (0, 0)),
                  pl.BlockSpec((1, d, tf), lambda t, f, te, nu: (te[t], 0, live_f(t, f, nu))),
                  pl.BlockSpec((1, d, tf), lambda t, f, te, nu: (te[t], 0, live_f(t, f, nu))),
                  pl.BlockSpec((1, tf, d), lambda t, f, te, nu: (te[t], live_f(t, f, nu), 0))],
        out_specs=pl.BlockSpec((tm * ROW_SUBLANES, 128), lambda t, f, te, nu: (t, 0)),
        scratch_shapes=[pltpu.VMEM((2, tm * ROW_SUBLANES, 128), F32), pltpu.SemaphoreType.DMA((2,)),
                        pltpu.VMEM((tm, d), BF16), pltpu.VMEM((tm, d), F32)])
    return pl.pallas_call(
        functools.partial(_moe_group_body, nf=nf),
        grid_spec=grid_spec,
        out_shape=jax.ShapeDtypeStruct((n_tiles * tm * ROW_SUBLANES, 128), F32),
        compiler_params=pltpu.CompilerParams(dimension_semantics=("arbitrary", "arbitrary")),
        name="moe_grouped",
    )(tile_expert, n_used, row_token, row_token, x8, g.reshape(1, d), wg, wu, wd)


def _moe_combine_body(pos0_ref, posn_ref, y8_ref, x_ref, w_ref, g_ref, o_ref, ybuf, sem):
    t = pl.program_id(0)
    tc = o_ref.shape[0]
    n = TOP_K * tc

    @pl.when(t == 0)
    def _():
        _gather_start(y8_ref, pos0_ref, ybuf, 0, n, sem)

    @pl.when(t + 1 < pl.num_programs(0))
    def _():
        _gather_start(y8_ref, posn_ref, ybuf, (t + 1) % 2, n, sem)

    slot = t % 2
    _gather_wait(y8_ref, ybuf, slot, n, sem)
    y = _row_tiles_to_2d(ybuf, slot, n)
    w = w_ref[...]
    moe = w[:, 0:1] * y[0:tc] + w[:, 1:2] * y[tc:n]
    o_ref[...] = _rmsnorm(_load_row_tiles(x_ref, tc) + moe, g_ref[...])


def moe_combine_norm(y8, x8, row0, pos, topw, g):
    d = D_MODEL
    nt, _, n = pos.shape
    tc = n // TOP_K
    m = nt * tc
    assert row0 % tc == 0
    blk0 = row0 // tc
    smem_rows = lambda imap: pl.BlockSpec((1, 1, n), imap, memory_space=pltpu.SMEM)
    return pl.pallas_call(
        _moe_combine_body,
        grid=(nt,),
        in_specs=[smem_rows(lambda t: (0, 0, 0)),
                  smem_rows(lambda t: (jnp.minimum(t + 1, nt - 1), 0, 0)),
                  pl.BlockSpec(memory_space=pl.ANY),
                  pl.BlockSpec((tc * ROW_SUBLANES, 128), lambda t: (blk0 + t, 0)),
                  pl.BlockSpec((tc, TOP_K), lambda t: (t, 0)),
                  pl.BlockSpec((1, d), lambda t: (0, 0))],
        out_specs=pl.BlockSpec((tc, d), lambda t: (t, 0)),
        out_shape=jax.ShapeDtypeStruct((m, d), F32),
        scratch_shapes=[pltpu.VMEM((2, n * ROW_SUBLANES, 128), F32), pltpu.SemaphoreType.DMA((2,))],
        compiler_params=pltpu.CompilerParams(dimension_semantics=("arbitrary",)),
        name="moe_combine_norm",
    )(pos, pos, y8, x8, topw, g.reshape(1, d))


def _moe_plan(topi, tm):
    n = topi.shape[0]
    a = TOP_K * n
    n_tiles = (a + N_EXPERTS * (tm - 1) + tm - 1) // tm
    e = topi.reshape(a)
    eid = jnp.arange(N_EXPERTS, dtype=jnp.int32)
    onehot = (e[:, None] == eid[None, :]).astype(jnp.int32)
    csum = jnp.cumsum(onehot, axis=0)
    cnt = csum[-1]
    rank = jnp.sum((csum - onehot) * onehot, axis=1)
    tiles = (cnt + tm - 1) // tm
    tile_end = jnp.cumsum(tiles)
    tile_start = tile_end - tiles
    n_used = tile_end[-1]
    pos = (jnp.sum(onehot * tile_start[None, :], axis=1) * tm + rank).reshape(n, TOP_K)
    tid = jnp.minimum(jnp.arange(n_tiles, dtype=jnp.int32), n_used - 1)
    tile_expert = jnp.sum(tid[:, None] >= tile_end[None, :], axis=1).astype(jnp.int32)
    fill = tiles * tm - cnt
    j = jnp.arange(tm - 1, dtype=jnp.int32)
    dummy_key = jnp.where(j[None, :] < fill[:, None], eid[:, None], N_EXPERTS).reshape(-1)
    dummy_tok = jnp.tile(j % n, N_EXPERTS)
    spare = n_tiles * tm - a - dummy_key.shape[0]
    keys = jnp.concatenate([e, dummy_key, jnp.full((spare,), N_EXPERTS, jnp.int32)])
    toks = jnp.concatenate([jnp.arange(a, dtype=jnp.int32) // TOP_K, dummy_tok,
                            jnp.zeros((spare,), jnp.int32)])
    _, row_token = lax.sort((keys, toks), num_keys=1, is_stable=True)
    return tile_expert, n_used.reshape(1).astype(jnp.int32), row_token.reshape(n_tiles, 1, tm), pos


def moe_top2_norm(x8, groups, g, router_w, router_b, wg, wu, wd, g_final):
    starts = [sum(groups[:i]) for i in range(len(groups))]
    tops = [moe_router(x8, s, m, g, router_w, router_b) for s, m in zip(starts, groups)]
    topi = jnp.concatenate([t[0] for t in tops], axis=0)
    tile_expert, n_used, row_token, pos = _moe_plan(topi, MOE_ROWS)
    y8 = moe_grouped(x8, g, tile_expert, n_used, row_token, wg, wu, wd, MOE_TILE)
    outs = []
    for s, m, (_, topw) in zip(starts, groups, tops):
        tc = _row_tile(m, 256)
        p = pos[s:s + m].reshape(m // tc, tc, TOP_K)
        p = jnp.swapaxes(p, 1, 2).reshape(m // tc, 1, TOP_K * tc)
        outs.append(moe_combine_norm(y8, x8, s, p, topw, g_final))
    return outs


def _store_row_tiles(o_ref, y):
    n = y.shape[0]
    for j in range(ROW_SUBLANES):
        o_ref[pl.ds(j, n, stride=ROW_SUBLANES), :] = y[:, j * 128:(j + 1) * 128]


def _load_row_tiles(ref, n):
    return jnp.concatenate([ref[pl.ds(j, n, stride=ROW_SUBLANES), :] for j in range(ROW_SUBLANES)], axis=1)


def _xattn_seq_body(x_ref, g_ref, wq_ref, wo_ref, k_ref, v_ref, o_ref, *, row_tiles, fill_step):
    if fill_step is not None:
        @pl.when(pl.program_id(0) == fill_step)
        def _():
            o_ref[...] = jnp.zeros_like(o_ref)

        @pl.when(pl.program_id(0) != fill_step)
        def _():
            _xattn_seq_tile(x_ref, g_ref, wq_ref, wo_ref, k_ref, v_ref, o_ref, row_tiles)
    else:
        _xattn_seq_tile(x_ref, g_ref, wq_ref, wo_ref, k_ref, v_ref, o_ref, row_tiles)


def _xattn_seq_tile(x_ref, g_ref, wq_ref, wo_ref, k_ref, v_ref, o_ref, row_tiles):
    x = x_ref[0]
    xn = _rmsnorm(x, g_ref[...]).astype(BF16)
    q = jnp.dot(xn, wq_ref[...], preferred_element_type=F32).astype(BF16)
    kb = _load_packed(k_ref, N_MEM).astype(BF16)
    vb = _load_packed(v_ref, N_MEM).astype(BF16)
    heads = []
    for h in range(X_HEADS):
        sl = slice(h * X_HEAD_DIM, (h + 1) * X_HEAD_DIM)
        s = _dot_nt(q[:, sl], kb[:, sl]) * (X_HEAD_DIM ** -0.5)
        mx = jnp.max(s, axis=-1, keepdims=True)
        ex = jnp.exp(s - mx)
        pr = ex / jnp.sum(ex, axis=-1, keepdims=True)
        heads.append(jnp.dot(pr.astype(BF16), vb[:, sl], preferred_element_type=F32))
    att = jnp.concatenate(heads, axis=-1).astype(BF16)
    y = x + jnp.dot(att, wo_ref[...], preferred_element_type=F32)
    if row_tiles:
        _store_row_tiles(o_ref, y)
    else:
        o_ref[0] = y


def xattn_seq(x, g, wq, wo, mem_k, mem_v, layer, row_tiles=False, extra_rows=0):
    b, t, d = x.shape
    tq = _row_tile(t, 512)
    nq = t // tq
    assert extra_rows <= tq and (extra_rows == 0 or row_tiles)
    steps = b * nq + (1 if extra_rows else 0)
    bi = lambda s: jnp.minimum(s // nq, b - 1)
    qi = lambda s: jnp.where(s < b * nq, s % nq, 0)
    if row_tiles:
        out_spec = pl.BlockSpec((tq * ROW_SUBLANES, 128), lambda s: (s, 0))
        out_shape = jax.ShapeDtypeStruct(((b * t + extra_rows) * ROW_SUBLANES, 128), F32)
    else:
        out_spec = pl.BlockSpec((1, tq, d), lambda s: (bi(s), qi(s), 0))
        out_shape = jax.ShapeDtypeStruct((b, t, d), F32)
    return pl.pallas_call(
        functools.partial(_xattn_seq_body, row_tiles=row_tiles, fill_step=b * nq if extra_rows else None),
        grid=(steps,),
        in_specs=[pl.BlockSpec((1, tq, d), lambda s: (bi(s), qi(s), 0)),
                  pl.BlockSpec((1, d), lambda s: (0, 0)),
                  pl.BlockSpec((d, d), lambda s: (0, 0)),
                  pl.BlockSpec((d, d), lambda s: (0, 0)),
                  pl.BlockSpec((1, N_MEM * ROW_SUBLANES, 128), lambda s: (layer, bi(s), 0)),
                  pl.BlockSpec((1, N_MEM * ROW_SUBLANES, 128), lambda s: (layer, bi(s), 0))],
        out_specs=out_spec,
        out_shape=out_shape,
        compiler_params=pltpu.CompilerParams(dimension_semantics=("parallel",)),
        name="xattn_seq",
    )(x, g.reshape(1, d), wq, wo, mem_k, mem_v)


XATTN_STEP_ROWS = 8
XATTN_STEP_KV = 8


def _xattn_step_body(x_ref, g_ref, wq_ref, wo_ref, k_ref, v_ref, *rest, row_tiles, aliased):
    o_ref, q_s, att_s = rest[1:] if aliased else rest
    j = pl.program_id(1)

    @pl.when(j == 0)
    def _():
        xn = _rmsnorm(x_ref[...], g_ref[...]).astype(BF16)
        q_s[...] = jnp.dot(xn, wq_ref[...], preferred_element_type=F32) * (X_HEAD_DIM ** -0.5)

    halves = X_HEAD_DIM // 128
    packed = [(r % X_HEADS) * X_HEAD_DIM + (r // X_HEADS) * 128 for r in range(halves * X_HEADS)]
    for b in range(XATTN_STEP_KV):
        rsel = pl.ds(j * XATTN_STEP_KV + b, 1)
        q_row = q_s[rsel, :]
        q8 = jnp.concatenate([q_row[:, c0:c0 + 128] for c0 in packed], axis=0)
        part = jnp.sum(k_ref[0, b] * q8[None], axis=-1, keepdims=True)
        s = part + pltpu.roll(part, X_HEADS, axis=1)
        mx = jnp.max(s, axis=0, keepdims=True)
        ex = jnp.exp(s - mx)
        den = jnp.sum(ex, axis=0)
        o8 = jnp.sum(ex * v_ref[0, b], axis=0) / den
        order = sorted(range(len(packed)), key=lambda r: packed[r])
        att_s[rsel, :] = jnp.concatenate([o8[r:r + 1, :] for r in order], axis=1)

    @pl.when(j == pl.num_programs(1) - 1)
    def _():
        y = x_ref[...] + jnp.dot(att_s[...].astype(BF16), wo_ref[...], preferred_element_type=F32)
        if row_tiles:
            _store_row_tiles(o_ref, y)
        else:
            o_ref[...] = y


def xattn_step(x, g, wq, wo, mem_k, mem_v, layer, row_tiles=False, into=None):
    b, d = x.shape
    rows, kvb = XATTN_STEP_ROWS, XATTN_STEP_KV
    assert b % rows == 0 and rows % kvb == 0
    nj = rows // kvb
    kv_spec = pl.BlockSpec((1, kvb, N_MEM, ROW_SUBLANES, 128), lambda i, j: (layer, i * nj + j, 0, 0, 0))
    args = [x, g.reshape(1, d), wq, wo, mem_k, mem_v]
    in_specs = [pl.BlockSpec((rows, d), lambda i, j: (i, 0)),
                pl.BlockSpec((1, d), lambda i, j: (0, 0)),
                pl.BlockSpec((d, d), lambda i, j: (0, 0)),
                pl.BlockSpec((d, d), lambda i, j: (0, 0)),
                kv_spec, kv_spec]
    aliases = {}
    if into is not None:
        buf, row0 = into
        assert row_tiles and row0 % rows == 0 and buf.shape == ((row0 + b) * ROW_SUBLANES, 128)
        blk0 = row0 // rows
        out_spec = pl.BlockSpec((rows * ROW_SUBLANES, 128), lambda i, j: (blk0 + i, 0))
        out_shape = jax.ShapeDtypeStruct(buf.shape, F32)
        aliases = {len(args): 0}
        args.append(buf)
        in_specs.append(pl.BlockSpec(memory_space=pl.ANY))
    elif row_tiles:
        out_spec = pl.BlockSpec((rows * ROW_SUBLANES, 128), lambda i, j: (i, 0))
        out_shape = jax.ShapeDtypeStruct((b * ROW_SUBLANES, 128), F32)
    else:
        out_spec = pl.BlockSpec((rows, d), lambda i, j: (i, 0))
        out_shape = jax.ShapeDtypeStruct((b, d), F32)
    return pl.pallas_call(
        functools.partial(_xattn_step_body, row_tiles=row_tiles, aliased=into is not None),
        grid=(b // rows, nj),
        in_specs=in_specs,
        out_specs=out_spec,
        out_shape=out_shape,
        input_output_aliases=aliases,
        scratch_shapes=[pltpu.VMEM((rows, d), F32), pltpu.VMEM((rows, d), F32)],
        compiler_params=pltpu.CompilerParams(dimension_semantics=("parallel", "arbitrary"),
                                             vmem_limit_bytes=XATTN_STEP_VMEM),
        name="xattn_step",
    )(*args)


def _head_sum(x, bdiag_ref):
    m = bdiag_ref[0:GROUP_LANES, 0:GROUP_LANES]
    parts = []
    for g in range(RWKV_WIDTH // GROUP_LANES):
        xs = x[:, g * GROUP_LANES:(g + 1) * GROUP_LANES]
        hi = xs.astype(BF16)
        lo = (xs - hi.astype(F32)).astype(BF16)
        parts.append(jnp.dot(hi, m, preferred_element_type=F32) + jnp.dot(lo, m, preferred_element_type=F32))
    return jnp.concatenate(parts, axis=1)


def _pool_group_out(acc, cnt, u, pool_w_ref, pool_scale_ref, g):
    cs = slice(g * POOL_GC, (g + 1) * POOL_GC)
    d = acc / cnt - u
    return _dot(d, pool_w_ref[g]) * pool_scale_ref[:, cs]


def _rwkv_prep(p, shifted, prm):
    (mu_ref, w0_ref, w2_ref, a0_ref, a2_ref, g2_ref, kk_ref, ka_ref, bdiag_ref) = prm
    m = p + (shifted - p) * mu_ref[...]
    c = RWKV_WIDTH
    r = m[:, 0:c]
    k = m[:, c:2 * c]
    v = m[:, 2 * c:3 * c]
    dwa = m[:, 3 * c:3 * c + W_RANK + A_RANK]
    dg = m[:, 3 * c + W_RANK + A_RANK:]
    w_log = -_softplus(-(w0_ref[...] + _dot(jnp.tanh(dwa), w2_ref[...]))) - 0.5
    lw = -jnp.exp(w_log)
    a = _sigmoid(a0_ref[...] + _dot(dwa, a2_ref[...]))
    gate = _dot(_sigmoid(dg), g2_ref[...])
    kk = k * kk_ref[...]
    kk = kk / jnp.maximum(jnp.sqrt(_head_sum(kk * kk, bdiag_ref)), 1e-12)
    kmod = k * (1.0 + (a - 1.0) * ka_ref[...])
    return r, lw, kmod, v, kk, a, gate


def _rwkv_finish(y, r, kmod, v, gate, rk_ref, lnw_ref, lnb_ref, bdiag_ref):
    mu = _head_sum(y, bdiag_ref) * (1.0 / RWKV_HEAD)
    dlt = y - mu
    var = _head_sum(dlt * dlt, bdiag_ref) * (1.0 / RWKV_HEAD)
    yn = dlt * lax.rsqrt(var + LNX_EPS) * lnw_ref[...] + lnb_ref[...]
    bonus = _head_sum(r * kmod * rk_ref[...], bdiag_ref) * v
    return (yn + bonus) * gate


def _chains(x, nc):
    return [x[c * CHUNK:(c + 1) * CHUNK, g * GROUP_LANES:(g + 1) * GROUP_LANES]
            for g in range(RWKV_WIDTH // GROUP_LANES) for c in range(nc)]


def _block_diag(x, m4):
    xb = x.astype(BF16)
    return jnp.concatenate([xb] * (GROUP_LANES // RWKV_HEAD), axis=0) * m4


def _mix0_seq_body(x_ref, gmix_ref, win_ref, wout_ref,
                   pool_w_ref, pool_scale_ref, mu_ref, w0_ref, w2_ref, a0_ref, a2_ref, g2_ref,
                   kk_ref, ka_ref, rk_ref, lnw_ref, lnb_ref, bdiag_ref, tril_ref, blk_ref,
                   o_ref, s_out_ref, pool_tail_ref, shift_tail_ref,
                   ext_u, ext_p, s_ref, ycat):
    t = pl.program_id(1)
    tt = o_ref.shape[1]
    nc = tt // CHUNK
    ng = RWKV_WIDTH // GROUP_LANES

    @pl.when(t == 0)
    def _():
        ext_u[0:POOL_HALO, :] = jnp.zeros((POOL_HALO, POOL_WIDTH), F32)
        ext_p[0:SHIFT_HALO, :] = jnp.zeros((SHIFT_HALO, SHIFT_WIDTH), F32)
        s_ref[...] = jnp.zeros_like(s_ref)

    x = x_ref[0]
    h = jnp.dot(_rmsnorm(x, gmix_ref[...]).astype(BF16), win_ref[...], preferred_element_type=F32)
    ext_u[POOL_HALO:POOL_HALO + tt, :] = h[:, 0:POOL_WIDTH]
    ext_p[SHIFT_HALO:SHIFT_HALO + tt, :] = h[:, POOL_WIDTH:IN_AB_WIDTH]

    pos = t * tt + lax.broadcasted_iota(jnp.int32, (tt, 1), 0)
    for g, win in enumerate(POOL_WINDOWS):
        cs = slice(g * POOL_GC, (g + 1) * POOL_GC)
        u = ext_u[POOL_HALO:POOL_HALO + tt, cs]
        acc = u
        for j in range(1, win):
            acc = acc + ext_u[POOL_HALO - j:POOL_HALO - j + tt, cs]
        cnt = jnp.minimum(win, pos + 1).astype(F32)
        ycat[:, cs] = _pool_group_out(acc, cnt, u, pool_w_ref, pool_scale_ref, g)

    p = ext_p[SHIFT_HALO:SHIFT_HALO + tt, :]
    shifted = ext_p[SHIFT_HALO - 1:SHIFT_HALO - 1 + tt, :]
    prm = (mu_ref, w0_ref, w2_ref, a0_ref, a2_ref, g2_ref, kk_ref, ka_ref, bdiag_ref)
    r, lw, kmod, v, kk, a, gate = _rwkv_prep(p, shifted, prm)

    ext_u[0:POOL_HALO, :] = ext_u[tt:tt + POOL_HALO, :]
    ext_p[0:SHIFT_HALO, :] = ext_p[tt:tt + SHIFT_HALO, :]

    cl = _dot_exact_lhs(tril_ref[...], lw)
    tot = _dot_exact_lhs(blk_ref[...], lw)
    gam = jnp.exp(cl)
    gam_inv = jnp.exp(-cl)
    gam_end = jnp.exp(tot)
    beta = kk * a
    kd_full = kmod * gam_inv
    bd_full = beta * gam_inv
    kq_c = _chains(kk * jnp.exp(cl - lw), nc)
    rq_c = _chains(r * gam, nc)
    kd_c = _chains(kd_full, nc)
    bd_c = _chains(bd_full, nc)
    v_c = _chains(v, nc)
    kdg_c = _chains(kd_full * gam_end, nc)
    bdg_c = _chains(bd_full * gam_end, nc)
    ge_c = _chains(gam_end, nc)
    n_ch = ng * nc
    every = range(n_ch)

    m4 = bdiag_ref[0:GROUP_LANES, 0:GROUP_LANES]
    m4f = m4.astype(F32)
    row = lax.broadcasted_iota(jnp.int32, (CHUNK, GROUP_LANES), 0)
    col = lax.broadcasted_iota(jnp.int32, (CHUNK, GROUP_LANES), 1) % RWKV_HEAD
    strict = row > col
    incl = row >= col
    eye_c = jnp.where(row == col, 1.0, 0.0)
    eye_g = (lax.broadcasted_iota(jnp.int32, (GROUP_LANES, GROUP_LANES), 0)
             == lax.broadcasted_iota(jnp.int32, (GROUP_LANES, GROUP_LANES), 1))

    bd_kd = [_block_diag(kd_c[i], m4) for i in every]
    bd_bd = [_block_diag(bd_c[i], m4) for i in every]
    lhs = [jnp.concatenate([kq_c[i], rq_c[i]], axis=0) for i in every]
    pk = [_dot_nt(lhs[i], bd_kd[i]) for i in every]
    pb = [_dot_nt(lhs[i], bd_bd[i]) for i in every]
    a_k = [jnp.where(strict, pk[i][:CHUNK], 0.0) for i in every]
    p_k = [jnp.where(incl, pk[i][CHUNK:], 0.0) for i in every]
    a_b = [jnp.where(strict, pb[i][:CHUNK], 0.0) for i in every]
    p_b = [jnp.where(incl, pb[i][CHUNK:], 0.0) for i in every]
    tm = [eye_c - a_b[i] for i in every]
    apow = a_b
    bd_ap = [_block_diag(apow[i], m4) for i in every]
    n = 1
    while 2 * n < CHUNK:
        apow = [_dot(apow[i], bd_ap[i]) for i in every]
        bd_ap = [_block_diag(apow[i], m4) for i in every]
        tm = [tm[i] + _dot(tm[i], bd_ap[i]) for i in every]
        n *= 2
    bd_v = [_block_diag(v_c[i], m4) for i in every]
    t_kq = [_dot(tm[i], _block_diag(kq_c[i], m4)) for i in every]
    akv = [_dot(a_k[i], bd_v[i]) for i in every]
    t_akv = [_dot(tm[i], _block_diag(akv[i], m4)) for i in every]
    r_y = [rq_c[i] - _dot(p_b[i], _block_diag(t_kq[i], m4)) for i in every]
    y0 = [_dot(p_k[i], bd_v[i]) - _dot(p_b[i], _block_diag(t_akv[i], m4)) for i in every]
    g_m = [m4f * (jnp.where(eye_g, ge_c[i][0:1, :], 0.0) - _dot_tn(t_kq[i], bdg_c[i])) for i in every]
    h_full = [m4f * _dot_tn(jnp.concatenate([v_c[i], t_akv[i]], axis=0),
                            jnp.concatenate([kdg_c[i], -bdg_c[i]], axis=0)) for i in every]
    h_t = [sum(h_full[i][j * RWKV_HEAD:(j + 1) * RWKV_HEAD] for j in range(GROUP_LANES // RWKV_HEAD))
           for i in every]

    ys = [None] * n_ch
    for g in range(ng):
        s = s_ref[g]
        for c in range(nc):
            i = g * nc + c
            ys[i] = _dot_nt(r_y[i], _block_diag(s, m4)) + y0[i]
            s = _dot(s, g_m[i]) + h_t[i]
        s_ref[g] = s
    y = jnp.concatenate([jnp.concatenate([ys[g * nc + c] for g in range(ng)], axis=1) for c in range(nc)],
                        axis=0)

    ycat[:, POOL_WIDTH:] = _rwkv_finish(y, r, kmod, v, gate, rk_ref, lnw_ref, lnb_ref, bdiag_ref)
    o_ref[0] = x + jnp.dot(ycat[...].astype(BF16), wout_ref[...], preferred_element_type=F32)

    @pl.when(t == pl.num_programs(1) - 1)
    def _():
        pool_tail_ref[0] = ext_u[0:POOL_HALO, :]
        shift_tail_ref[0] = ext_p[0:SHIFT_HALO, :]
        for hh in range(RWKV_HEADS):
            g, j = divmod(hh, GROUP_LANES // RWKV_HEAD)
            s_out_ref[0, hh] = s_ref[g][:, j * RWKV_HEAD:(j + 1) * RWKV_HEAD]


def _mix0_params(P):
    c = RWKV_WIDTH
    row = lambda x: x.reshape(1, -1).astype(F32)
    w2 = jnp.zeros((W_RANK + A_RANK, c), F32).at[:W_RANK].set(P['rw_w2']).astype(BF16)
    a2 = jnp.zeros((W_RANK + A_RANK, c), F32).at[W_RANK:].set(P['rw_a2']).astype(BF16)
    hid = jnp.arange(c) // RWKV_HEAD
    bdiag = (hid[:, None] == hid[None, :]).astype(BF16)
    return (P['pool_w'].astype(BF16), row(P['pool_scale']), row(P['mu_shift']), row(P['rw_w0']), w2,
            row(P['rw_a0']), a2, P['rw_g2'].astype(BF16), row(P['rw_kk']), row(P['rw_ka']),
            row(P['rw_rk']), row(P['rw_lnx_w']), row(P['rw_lnx_b']), bdiag)


def _full_spec(x):
    nd = x.ndim
    return pl.BlockSpec(x.shape, lambda *_: (0,) * nd)


def mix0_seq(x, g, w_in, w_out, P):
    b, t, _ = x.shape
    tt = _row_tile(t, 512)
    assert tt % CHUNK == 0 and tt >= POOL_HALO
    prm = _mix0_params(P)
    ti = jnp.arange(tt)
    same_chunk = (ti[:, None] // CHUNK) == (ti[None, :] // CHUNK)
    tril = (same_chunk & (ti[:, None] >= ti[None, :])).astype(BF16)
    blk = same_chunk.astype(BF16)
    return pl.pallas_call(
        _mix0_seq_body,
        grid=(b, t // tt),
        in_specs=[pl.BlockSpec((1, tt, D_MODEL), lambda i, j: (i, j, 0)),
                  pl.BlockSpec((1, D_MODEL), lambda i, j: (0, 0)), _full_spec(w_in), _full_spec(w_out)]
                 + [_full_spec(p) for p in prm] + [_full_spec(tril), _full_spec(blk)],
        out_specs=[pl.BlockSpec((1, tt, D_MODEL), lambda i, j: (i, j, 0)),
                   pl.BlockSpec((1, RWKV_HEADS, RWKV_HEAD, RWKV_HEAD), lambda i, j: (i, 0, 0, 0)),
                   pl.BlockSpec((1, POOL_HALO, POOL_WIDTH), lambda i, j: (i, 0, 0)),
                   pl.BlockSpec((1, SHIFT_HALO, SHIFT_WIDTH), lambda i, j: (i, 0, 0))],
        out_shape=[jax.ShapeDtypeStruct((b, t, D_MODEL), F32),
                   jax.ShapeDtypeStruct((b, RWKV_HEADS, RWKV_HEAD, RWKV_HEAD), F32),
                   jax.ShapeDtypeStruct((b, POOL_HALO, POOL_WIDTH), F32),
                   jax.ShapeDtypeStruct((b, SHIFT_HALO, SHIFT_WIDTH), F32)],
        scratch_shapes=[pltpu.VMEM((tt + POOL_HALO, POOL_WIDTH), F32),
                        pltpu.VMEM((tt + SHIFT_HALO, SHIFT_WIDTH), F32),
                        pltpu.VMEM((RWKV_WIDTH // GROUP_LANES, RWKV_HEAD, GROUP_LANES), F32),
                        pltpu.VMEM((tt, D_MODEL), F32)],
        compiler_params=pltpu.CompilerParams(dimension_semantics=("parallel", "arbitrary")),
        name="mix0_seq",
    )(x, g.reshape(1, D_MODEL), w_in, w_out, *prm, tril, blk)


def _to_leading(x):
    n = x.shape[0]
    ii = lax.broadcasted_iota(jnp.int32, (n, n, 1), 0)
    jj = lax.broadcasted_iota(jnp.int32, (n, n, 1), 1)
    return jnp.sum(jnp.where(ii == jj, x[None, :, :], 0.0), axis=1, keepdims=True)


def _from_leading(x3):
    n = x3.shape[0]
    ii = lax.broadcasted_iota(jnp.int32, (n, n, 1), 0)
    jj = lax.broadcasted_iota(jnp.int32, (n, n, 1), 1)
    return jnp.sum(jnp.where(ii == jj, x3, 0.0), axis=0)


def _mix0_step_body(h_ref, pool_prev_ref, shift_prev_ref, s_in_ref,
                    pool_w_ref, pool_scale_ref, mu_ref, w0_ref, w2_ref, a0_ref, a2_ref, g2_ref,
                    kk_ref, ka_ref, rk_ref, lnw_ref, lnb_ref, bdiag_ref,
                    o_ref, s_out_ref,
                    r_t, w_t, k_t, v_t, kk_t, bt_t, y_t, r_s, k_s, v_s, g_s):
    hh = pl.program_id(0)
    nb = h_ref.shape[0]

    @pl.when(hh == 0)
    def _():
        for g, win in enumerate(POOL_WINDOWS):
            cs = slice(g * POOL_GC, (g + 1) * POOL_GC)
            u = h_ref[:, cs]
            acc = u
            for j in range(1, win):
                acc = acc + pool_prev_ref[POOL_BUF - j][:, cs]
            cnt = jnp.float32(min(win, PAST_LEN + 1))
            o_ref[:, cs] = _pool_group_out(acc, cnt, u, pool_w_ref, pool_scale_ref, g)
        prm = (mu_ref, w0_ref, w2_ref, a0_ref, a2_ref, g2_ref, kk_ref, ka_ref, bdiag_ref)
        r, lw, kmod, v, kk, a, gate = _rwkv_prep(h_ref[:, POOL_WIDTH:IN_AB_WIDTH], shift_prev_ref[...], prm)
        r_s[...] = r
        k_s[...] = kmod
        v_s[...] = v
        g_s[...] = gate
        r_t[...] = r.T
        w_t[...] = jnp.exp(lw).T
        k_t[...] = kmod.T
        v_t[...] = v.T
        kk_t[...] = kk.T
        bt_t[...] = (kk * a).T

    rows = pl.ds(pl.multiple_of(hh * RWKV_HEAD, RWKV_HEAD), RWKV_HEAD)
    s = s_in_ref[...].T.reshape(RWKV_HEAD, RWKV_HEAD, nb)
    kk = kk_t[rows, :][None]
    s_kk = jnp.sum(s * kk, axis=1, keepdims=True)
    v3 = _to_leading(v_t[rows, :])
    s = s * w_t[rows, :][None] - s_kk * bt_t[rows, :][None] + v3 * k_t[rows, :][None]
    y3 = jnp.sum(s * r_t[rows, :][None], axis=1, keepdims=True)
    y_t[rows, :] = _from_leading(y3)
    s_out_ref[...] = s.reshape(RWKV_HEAD * RWKV_HEAD, nb).T

    @pl.when(hh == pl.num_programs(0) - 1)
    def _():
        o_ref[:, POOL_WIDTH:] = _rwkv_finish(y_t[...].T, r_s[...], k_s[...], v_s[...], g_s[...],
                                             rk_ref, lnw_ref, lnb_ref, bdiag_ref)


def mix0_step(h, pool_prev, shift_prev, s_prev, P):
    b = h.shape[0]
    prm = _mix0_params(P)
    hw = RWKV_HEAD * RWKV_HEAD
    s2 = s_prev.reshape(b, RWKV_HEADS * hw)
    pool_t = jnp.swapaxes(pool_prev, 0, 1)
    tvec = lambda: pltpu.VMEM((RWKV_WIDTH, b), F32)
    svec = lambda: pltpu.VMEM((b, RWKV_WIDTH), F32)
    out, s_new = pl.pallas_call(
        _mix0_step_body,
        grid=(RWKV_HEADS,),
        in_specs=[_full_spec(h), _full_spec(pool_t), _full_spec(shift_prev),
                  pl.BlockSpec((b, hw), lambda i: (0, i))] + [_full_spec(x) for x in prm],
        out_specs=[pl.BlockSpec((b, D_MODEL), lambda i: (0, 0)),
                   pl.BlockSpec((b, hw), lambda i: (0, i))],
        out_shape=[jax.ShapeDtypeStruct((b, D_MODEL), F32),
                   jax.ShapeDtypeStruct((b, RWKV_HEADS * hw), F32)],
        scratch_shapes=[tvec(), tvec(), tvec(), tvec(), tvec(), tvec(), tvec(),
                        svec(), svec(), svec(), svec()],
        compiler_params=pltpu.CompilerParams(dimension_semantics=("arbitrary",)),
        name="mix0_step",
    )(h, pool_t, shift_prev, s2, *prm)
    return out, s_new.reshape(b, RWKV_HEADS, RWKV_HEAD, RWKV_HEAD)


def _conv_seq_body(x_ref, g_ref, wi_ref, cw_ref, wo_ref, o_ref, tail_ref, ext):
    t = pl.program_id(1)
    tt = o_ref.shape[1]
    c = D_MODEL

    @pl.when(t == 0)
    def _():
        ext[0:SHIFT_HALO, :] = jnp.zeros((SHIFT_HALO, c), F32)

    x = x_ref[0]
    h = jnp.dot(_rmsnorm(x, g_ref[...]).astype(BF16), wi_ref[...], preferred_element_type=F32)
    ext[SHIFT_HALO:SHIFT_HALO + tt, :] = h[:, c:2 * c] * h[:, 2 * c:3 * c]
    z = cw_ref[0:1, :] * ext[SHIFT_HALO - 2:SHIFT_HALO - 2 + tt, :]
    z = z + cw_ref[1:2, :] * ext[SHIFT_HALO - 1:SHIFT_HALO - 1 + tt, :]
    z = z + cw_ref[2:3, :] * ext[SHIFT_HALO:SHIFT_HALO + tt, :]
    gated = (h[:, 0:c] * z).astype(BF16)
    o_ref[0] = x + jnp.dot(gated, wo_ref[...], preferred_element_type=F32)
    ext[0:SHIFT_HALO, :] = ext[tt:tt + SHIFT_HALO, :]

    @pl.when(t == pl.num_programs(1) - 1)
    def _():
        tail_ref[0] = ext[SHIFT_HALO - (CONV_WIDTH - 1):SHIFT_HALO, :]


def conv_seq(x, g, w_in, conv_w, w_out):
    b, t, c = x.shape
    tt = _row_tile(t, 512)
    return pl.pallas_call(
        _conv_seq_body,
        grid=(b, t // tt),
        in_specs=[pl.BlockSpec((1, tt, c), lambda i, j: (i, j, 0)),
                  pl.BlockSpec((1, c), lambda i, j: (0, 0)),
                  _full_spec(w_in), _full_spec(conv_w), _full_spec(w_out)],
        out_specs=[pl.BlockSpec((1, tt, c), lambda i, j: (i, j, 0)),
                   pl.BlockSpec((1, CONV_WIDTH - 1, c), lambda i, j: (i, 0, 0))],
        out_shape=[jax.ShapeDtypeStruct((b, t, c), F32),
                   jax.ShapeDtypeStruct((b, CONV_WIDTH - 1, c), F32)],
        scratch_shapes=[pltpu.VMEM((tt + SHIFT_HALO, c), F32)],
        compiler_params=pltpu.CompilerParams(dimension_semantics=("parallel", "arbitrary")),
        name="conv_seq",
    )(x, g.reshape(1, c), w_in, conv_w, w_out)


def _conv_step_body(h_ref, x_ref, p0_ref, p1_ref, cw_ref, wo_ref, o_ref, e_ref):
    c = D_MODEL
    e = h_ref[:, c:2 * c] * h_ref[:, 2 * c:3 * c]
    z = cw_ref[0:1, :] * p0_ref[...] + cw_ref[1:2, :] * p1_ref[...] + cw_ref[2:3, :] * e
    gated = (h_ref[:, 0:c] * z).astype(BF16)
    o_ref[...] = x_ref[...] + jnp.dot(gated, wo_ref[...], preferred_element_type=F32)
    e_ref[...] = e


def conv_step(h, x, prev, conv_w, w_out):
    b = h.shape[0]
    c = D_MODEL
    args = (h, x, prev[:, 0, :], prev[:, 1, :], conv_w, w_out)
    return pl.pallas_call(
        _conv_step_body,
        grid=(1,),
        in_specs=[_full_spec(a) for a in args],
        out_specs=[pl.BlockSpec((b, c), lambda i: (0, 0)), pl.BlockSpec((b, c), lambda i: (0, 0))],
        out_shape=[jax.ShapeDtypeStruct((b, c), F32), jax.ShapeDtypeStruct((b, c), F32)],
        compiler_params=pltpu.CompilerParams(dimension_semantics=("arbitrary",)),
        name="conv_step",
    )(*args)


def _xattn(x2, i, mem_k, mem_v, W, seq_shape, extra_rows=0, into=None):
    last = i == W['norm_xattn'].shape[0] - 1
    if seq_shape is not None:
        b, t = seq_shape
        y = xattn_seq(x2.reshape(b, t, D_MODEL), W['norm_xattn'][i], W['w_xq'][i], W['w_xo'][i],
                      mem_k, mem_v, i, row_tiles=last, extra_rows=extra_rows)
        return y if last else y.reshape(b * t, D_MODEL)
    return xattn_step(x2, W['norm_xattn'][i], W['w_xq'][i], W['w_xo'][i], mem_k, mem_v, i,
                      row_tiles=last, into=into)


def _ffn0(x2, W):
    return ffn_dense(x2, W['norm_ffn'][0], W['ffn_gate'][0], W['ffn_up'][0], W['ffn_down'][0], FFN_TILE)


def _trunk_seq(x, mem_k, mem_v, W, extra_rows):
    b, t, d = x.shape
    P0 = {k: v[0] for k, v in W['mix0'].items()}
    x1, wkv, pool_tail, shift_tail = mix0_seq(x, W['norm_mix'][0], W['w_in_ab'][0], W['w_out_ab'][0], P0)
    pool = pool_tail[:, POOL_HALO - POOL_BUF:]
    shift = shift_tail[:, SHIFT_HALO - 1]
    x2 = _ffn0(_xattn(x1.reshape(b * t, d), 0, mem_k, mem_v, W, (b, t)), W)
    x3, conv = conv_seq(x2.reshape(b, t, d), W['norm_mix'][1], W['w_in_c'][0], W['conv_w'][0], W['w_out_c'][0])
    x8 = _xattn(x3.reshape(b * t, d), 1, mem_k, mem_v, W, (b, t), extra_rows=extra_rows)
    return x8, pool[None], shift[None], wkv[None], conv[None]


def _trunk_step(x, mem_k, mem_v, pool_prev, shift_prev, wkv_prev, conv_prev, W, into):
    b, _, d = x.shape
    x2 = x.reshape(b, d)
    P0 = {k: v[0] for k, v in W['mix0'].items()}
    h = norm_matmul(x2, W['norm_mix'][0], W['w_in_ab'][0])
    mix, wkv = mix0_step(h, pool_prev[0], shift_prev[0], wkv_prev[0], P0)
    pool = jnp.concatenate([pool_prev[0][:, 1:], h[:, None, :POOL_WIDTH]], axis=1)
    shift = h[:, POOL_WIDTH:]
    x2 = matmul_res(mix, W['w_out_ab'][0], x2)
    x2 = _ffn0(_xattn(x2, 0, mem_k, mem_v, W, None), W)
    h = norm_matmul(x2, W['norm_mix'][1], W['w_in_c'][0])
    x2, e = conv_step(h, x2, conv_prev[0], W['conv_w'][0], W['w_out_c'][0])
    conv = jnp.concatenate([conv_prev[0][:, 1:], e[:, None]], axis=1)
    x8 = _xattn(x2, 1, mem_k, mem_v, W, None, into=into)
    return x8, pool[None], shift[None], wkv[None], conv[None]


def _pack_heads(c):
    lead = c.shape[:-2]
    halves = X_HEAD_DIM // 128
    c = c.reshape(*lead, X_HEADS, halves, 128)
    c = jnp.swapaxes(c, -3, -2)
    return c.reshape(*lead, halves * X_HEADS, 128)


def _unpack_heads(c):
    lead = c.shape[:-2]
    halves = X_HEAD_DIM // 128
    c = c.reshape(*lead, halves, X_HEADS, 128)
    c = jnp.swapaxes(c, -3, -2)
    return c.reshape(*lead, X_HEADS, X_HEAD_DIM)


def kernel(x_prompt, x_sample, mem_prompt, cache_mem_k, cache_mem_v, state_pool, state_shift, state_wkv, state_conv, norm_mix, norm_xattn, norm_mem, norm_ffn, norm_final, w_xq, w_xk, w_xv, w_xo, w_in_ab, pool_w, pool_scale, mu_shift, rw_w0, rw_w2, rw_a0, rw_a2, rw_g2, rw_kk, rw_ka, rw_rk, rw_lnx_w, rw_lnx_b, w_out_ab, ffn_gate, ffn_up, ffn_down, w_in_c, conv_w, w_out_c, router_w, router_b, moe_gate, moe_up, moe_down):
    depth = norm_mix.shape[0]
    assert depth == 2 and w_in_ab.shape[0] == 1 and w_in_c.shape[0] == 1
    bp = x_prompt.shape[0]
    bs = x_sample.shape[0]
    d = D_MODEL
    bf = lambda w: w.astype(BF16)
    W = dict(norm_mix=norm_mix, norm_xattn=norm_xattn, norm_ffn=norm_ffn, norm_final=norm_final,
             w_xq=bf(w_xq), w_xo=bf(w_xo), w_in_ab=bf(w_in_ab), w_out_ab=bf(w_out_ab),
             ffn_gate=bf(ffn_gate), ffn_up=bf(ffn_up), ffn_down=bf(ffn_down),
             w_in_c=bf(w_in_c), conv_w=conv_w, w_out_c=bf(w_out_c),
             router_w=router_w, router_b=router_b,
             moe_gate=bf(moe_gate), moe_up=bf(moe_up), moe_down=bf(moe_down),
             mix0=dict(pool_w=pool_w, pool_scale=pool_scale, mu_shift=mu_shift, rw_w0=rw_w0, rw_w2=rw_w2,
                       rw_a0=rw_a0, rw_a2=rw_a2, rw_g2=rw_g2, rw_kk=rw_kk, rw_ka=rw_ka, rw_rk=rw_rk,
                       rw_lnx_w=rw_lnx_w, rw_lnx_b=rw_lnx_b))

    mk, mv = mem_kv(mem_prompt.reshape(bp * N_MEM, d), norm_mem, bf(w_xk), bf(w_xv))

    n_p = bp * x_prompt.shape[1]
    x8, pool_p, shift_p, wkv_p, conv_p = _trunk_seq(x_prompt, mk, mv, W, bs)
    x8, pool_s, shift_s, wkv_s, conv_s = _trunk_step(x_sample, _pack_heads(cache_mem_k),
                                                      _pack_heads(cache_mem_v), state_pool,
                                                      state_shift, state_wkv, state_conv, W, (x8, n_p))
    y_p, y_s = moe_top2_norm(x8, [n_p, bs], norm_ffn[1], router_w[0], router_b[0], W['moe_gate'][0],
                             W['moe_up'][0], W['moe_down'][0], norm_final)
    y_p = y_p.reshape(x_prompt.shape)
    y_s = y_s.reshape(x_sample.shape)
    mem_k_p = _unpack_heads(mk.reshape(depth, bp, N_MEM, ROW_SUBLANES, 128))
    mem_v_p = _unpack_heads(mv.reshape(depth, bp, N_MEM, ROW_SUBLANES, 128))
    return (y_p, y_s, pool_p, pool_s, shift_p, shift_s, wkv_p, wkv_s, conv_p, conv_s, mem_k_p, mem_v_p)
```
